```python
import jax, jax.numpy as jnp
from jax import lax
import numpy as np


D_MODEL = 2048
BATCH = 8
SEQ = 2048
DEPTH = 1

CHUNK = 64
Q_BLOCK = 128
EPS = 1e-6

H_A = 8
D_LATENT = 128
DH_A = 128
H_IDX = 8
D_IDX = 64
TOPK_MAX = 256

H_R = 8
DK_R = 128
DV_R = 128
ROPE_BASE = 10000.0

D_BRANCH = 1024
N_BRANCH = 2

N_GROUPS = 4
EXP_PER_GROUP = 8
N_EXPERTS = N_GROUPS * EXP_PER_GROUP
TOP_K_EXP = 2
D_EXPERT = 1024
EXPERT_BLOCK = 128

SPLITS = (H_A * D_LATENT,
          D_LATENT,
          H_IDX * D_IDX,
          D_IDX,
          H_IDX,
          H_R * DK_R,
          H_R * DK_R,
          H_R * DV_R,
          H_R * DV_R,
          N_BRANCH * D_MODEL)
D_IN = sum(SPLITS)

kernel_name = "hybrid_dsa_retention_hmoe_block"


def rms_norm(x, g):
    xf = x.astype(jnp.float32)
    y = xf * lax.rsqrt(jnp.mean(xf * xf, axis=-1, keepdims=True) + EPS)
    return (y * g.astype(jnp.float32)).astype(x.dtype)


def head_group_norm(o, g):
    b, s = o.shape[:2]
    of = o.astype(jnp.float32)
    mu = jnp.mean(of, axis=-1, keepdims=True)
    var = jnp.mean(jnp.square(of - mu), axis=-1, keepdims=True)
    y = ((of - mu) * lax.rsqrt(var + EPS)).reshape(b, s, -1)
    return (y * g.astype(jnp.float32)).astype(o.dtype)


def rotary(x, pos):
    half = x.shape[-1] // 2
    freq = ROPE_BASE ** (-jnp.arange(half, dtype=jnp.float32) / half)
    ang = pos.astype(jnp.float32)[:, None] * freq[None, :]
    cos = jnp.cos(ang)[None, :, None, :].astype(x.dtype)
    sin = jnp.sin(ang)[None, :, None, :].astype(x.dtype)
    x1, x2 = x[..., :half], x[..., half:]
    return jnp.concatenate([x1 * cos - x2 * sin, x1 * sin + x2 * cos], axis=-1)


def sparse_indexer_attention(q_lat, kv, q_idx, k_idx, w_idx, w_uv):
    b, s = kv.shape[:2]
    n_sel = min(TOPK_MAX, s // 4)
    nb = s // Q_BLOCK
    key_chunk = jnp.arange(s) // CHUNK
    idx_scale = (H_IDX ** -0.5) * (D_IDX ** -0.5)
    attn_scale = D_LATENT ** -0.5

    def to_blocks(t):
        return jnp.moveaxis(t.reshape((b, nb, Q_BLOCK) + t.shape[2:]), 1, 0)

    def block(args):
        q_blk, qi_blk, wi_blk, start = args
        q_chunk = (start + jnp.arange(Q_BLOCK)) // CHUNK
        dots = jnp.einsum('bqhd,bsd->bqhs', qi_blk, k_idx).astype(jnp.float32)
        score = jnp.einsum('bqh,bqhs->bqs', wi_blk.astype(jnp.float32), jax.nn.relu(dots)) * idx_scale
        admissible = key_chunk[None, :] <= q_chunk[:, None]
        score = jnp.where(admissible[None], score, -jnp.inf)
        _, sel = lax.top_k(score, n_sel)
        kv_sel = jax.vmap(lambda kv_b, sel_b: kv_b[sel_b])(kv, sel)
        valid = key_chunk[sel] <= q_chunk[None, :, None]
        logits = jnp.einsum('bqhc,bqkc->bqhk', q_blk, kv_sel).astype(jnp.float32) * attn_scale
        logits = jnp.where(valid[:, :, None, :], logits, -jnp.inf)
        p = jax.nn.softmax(logits, axis=-1).astype(kv.dtype)
        return jnp.einsum('bqhk,bqkc->bqhc', p, kv_sel)

    starts = jnp.arange(nb) * Q_BLOCK
    o = lax.map(block, (to_blocks(q_lat), to_blocks(q_idx), to_blocks(w_idx), starts))
    o = jnp.moveaxis(o, 0, 1).reshape(b, s, H_A, D_LATENT)
    return jnp.einsum('bshc,hcd->bshd', o, w_uv).reshape(b, s, H_A * DH_A)


def multiscale_retention(q, k, v):
    b, s = q.shape[:2]
    nc = s // CHUNK
    dt = q.dtype
    log_gamma = jnp.log1p(-jnp.exp2(-5.0 - jnp.arange(H_R, dtype=jnp.float32)))
    n = jnp.arange(CHUNK, dtype=jnp.float32)
    diff = n[:, None] - n[None, :]
    decay_in = jnp.where(diff >= 0, jnp.exp(log_gamma[:, None, None] * jnp.maximum(diff, 0.0)), 0.0).astype(dt)
    decay_q = jnp.exp(log_gamma[:, None] * (n + 1.0)).astype(dt)
    decay_k = jnp.exp(log_gamma[:, None] * (CHUNK - 1.0 - n)).astype(dt)
    decay_chunk = jnp.exp(log_gamma * CHUNK).astype(dt)

    def to_chunks(t):
        return t.reshape(b, nc, CHUNK, H_R, t.shape[-1]).transpose(1, 0, 3, 2, 4)

    k = k * (DK_R ** -0.5)

    def step(state, qkv):
        qc, kc, vc = qkv
        inner = jnp.einsum('bhnd,bhmd->bhnm', qc, kc) * decay_in[None]
        o = jnp.einsum('bhnm,bhme->bhne', inner, vc)
        o = o + jnp.einsum('bhnd,bhde->bhne', qc, state) * decay_q[None, :, :, None]
        state = state * decay_chunk[None, :, None, None] + jnp.einsum(
            'bhmd,bhme->bhde', kc * decay_k[None, :, :, None], vc)
        return state, o

    state0 = jnp.zeros((b, H_R, DK_R, DV_R), dt)
    _, o = lax.scan(step, state0, (to_chunks(q), to_chunks(k), to_chunks(v)))
    return o.transpose(1, 0, 3, 2, 4).reshape(b, s, H_R, DV_R)


def hierarchical_moe(x, w_rg, b_rg, w_re, b_re, w_gate, w_up, w_down):
    b, s, d = x.shape
    t = x.reshape(-1, d)
    n_tok = t.shape[0]
    g_prob = jax.nn.softmax((t @ w_rg + b_rg).astype(jnp.float32), axis=-1)
    p_grp, grp = lax.top_k(g_prob, 1)
    e_logits = (t @ w_re + b_re).astype(jnp.float32).reshape(n_tok, N_GROUPS, EXP_PER_GROUP)
    e_logits = jnp.take_along_axis(e_logits, grp[:, :, None], axis=1)[:, 0]
    p_exp, e_loc = lax.top_k(jax.nn.softmax(e_logits, axis=-1), TOP_K_EXP)
    p_exp = p_exp / jnp.sum(p_exp, axis=-1, keepdims=True)
    gate = (p_grp * p_exp).astype(x.dtype)
    expert = (grp * EXP_PER_GROUP + e_loc).astype(jnp.int32)

    n_asg = n_tok * TOP_K_EXP
    e_flat = expert.reshape(-1)
    tok_flat = jnp.repeat(jnp.arange(n_tok, dtype=jnp.int32), TOP_K_EXP)
    gate_flat = gate.reshape(-1)
    order = jnp.argsort(e_flat)
    e_sorted = e_flat[order]
    counts = jnp.bincount(e_flat, length=N_EXPERTS)
    padded = (counts + EXPERT_BLOCK - 1) // EXPERT_BLOCK * EXPERT_BLOCK
    pad_end = jnp.cumsum(padded)
    pad_start = pad_end - padded
    cnt_start = jnp.cumsum(counts) - counts
    rank = jnp.arange(n_asg) - cnt_start[e_sorted]
    dest = (pad_start[e_sorted] + rank).astype(jnp.int32)
    n_rows = -(-(n_asg + N_EXPERTS * (EXPERT_BLOCK - 1)) // EXPERT_BLOCK) * EXPERT_BLOCK
    n_blocks = n_rows // EXPERT_BLOCK
    row_tok = jnp.full((n_rows,), n_tok, jnp.int32).at[dest].set(tok_flat[order])
    row_gate = jnp.zeros((n_rows,), x.dtype).at[dest].set(gate_flat[order])
    block_expert = jnp.minimum(
        jnp.searchsorted(pad_end, jnp.arange(n_blocks) * EXPERT_BLOCK, side='right'), N_EXPERTS - 1)
    t_pad = jnp.concatenate([t, jnp.zeros((1, d), t.dtype)], axis=0)

    def block(args):
        rows, e = args
        xb = t_pad[rows]
        hmid = jax.nn.silu(xb @ w_gate[e]) * (xb @ w_up[e])
        return hmid @ w_down[e]

    y = lax.map(block, (row_tok.reshape(n_blocks, EXPERT_BLOCK), block_expert))
    y = y.reshape(n_rows, d) * row_gate[:, None]
    out = jnp.zeros((n_tok + 1, d), x.dtype).at[row_tok].add(y)[:n_tok]
    return out.reshape(b, s, d)


def setup_inputs(seed: int = 0) -> dict:
    key = jax.random.key(seed)
    ks = jax.random.split(key, 20)
    f32 = jnp.float32

    def nrm(k, shape, scale):
        return jax.random.normal(k, shape, f32) * scale

    def gain(k, shape):
        return 1.0 + 0.01 * jax.random.normal(k, shape, f32)

    return {
        'x': jax.random.normal(ks[0], (BATCH, SEQ, D_MODEL), f32),
        'g_mix_norm': gain(ks[1], (DEPTH, D_MODEL)),
        'w_in': nrm(ks[2], (DEPTH, D_MODEL, D_IN), D_MODEL ** -0.5),
        'g_kv': gain(ks[3], (DEPTH, D_LATENT)),
        'w_uv': nrm(ks[4], (DEPTH, H_A, D_LATENT, DH_A), D_LATENT ** -0.5),
        'g_ret': gain(ks[5], (DEPTH, H_R * DV_R)),
        'w_branch': nrm(ks[6], (DEPTH, N_BRANCH, D_BRANCH, D_MODEL), D_BRANCH ** -0.5),
        'w_out': nrm(ks[7], (DEPTH, D_MODEL, D_MODEL), D_MODEL ** -0.5),
        'g_ffn_norm': gain(ks[8], (DEPTH, D_MODEL)),
        'w_router_group': nrm(ks[9], (DEPTH, D_MODEL, N_GROUPS), D_MODEL ** -0.5),
        'b_router_group': nrm(ks[10], (DEPTH, N_GROUPS), 0.01),
        'w_router_expert': nrm(ks[11], (DEPTH, D_MODEL, N_EXPERTS), D_MODEL ** -0.5),
        'b_router_expert': nrm(ks[12], (DEPTH, N_EXPERTS), 0.01),
        'w_expert_gate': nrm(ks[13], (DEPTH, N_EXPERTS, D_MODEL, D_EXPERT), D_MODEL ** -0.5),
        'w_expert_up': nrm(ks[14], (DEPTH, N_EXPERTS, D_MODEL, D_EXPERT), D_MODEL ** -0.5),
        'w_expert_down': nrm(ks[15], (DEPTH, N_EXPERTS, D_EXPERT, D_MODEL), D_EXPERT ** -0.5),
        'g_final': gain(ks[16], (D_MODEL,)),
    }


def reference(x, g_mix_norm, w_in, g_kv, w_uv, g_ret, w_branch, w_out, g_ffn_norm,
              w_router_group, b_router_group, w_router_expert, b_router_expert,
              w_expert_gate, w_expert_up, w_expert_down, g_final):
    b, s, d = x.shape
    pos = jnp.arange(s)
    split_points = np.cumsum(np.array(SPLITS))[:-1].tolist()
    h = x
    for l in range(DEPTH):
        xn = rms_norm(h, g_mix_norm[l])
        proj = xn @ w_in[l]
        (q_lat, c_kv, q_idx, k_idx, w_idx, q_r, k_r, v_r, gate_r, gate_br) = jnp.split(
            proj, split_points, axis=-1)

        kv = rms_norm(c_kv, g_kv[l])
        o_a = sparse_indexer_attention(
            q_lat.reshape(b, s, H_A, D_LATENT), kv,
            q_idx.reshape(b, s, H_IDX, D_IDX), k_idx, w_idx, w_uv[l])

        q_r = rotary(q_r.reshape(b, s, H_R, DK_R), pos)
        k_r = rotary(k_r.reshape(b, s, H_R, DK_R), pos)
        ret = multiscale_retention(q_r, k_r, v_r.reshape(b, s, H_R, DV_R))
        o_b = jax.nn.silu(gate_r) * head_group_norm(ret, g_ret[l])

        branches = jnp.einsum('bsnc,ncd->bsnd', jnp.stack([o_a, o_b], axis=2), w_branch[l])
        gates = jax.nn.sigmoid(gate_br.reshape(b, s, N_BRANCH, d))
        mixed = jnp.einsum('bsnd,bsnd->bsd', gates, branches)
        h = h + mixed @ w_out[l]

        h = h + hierarchical_moe(rms_norm(h, g_ffn_norm[l]),
                                 w_router_group[l], b_router_group[l],
                                 w_router_expert[l], b_router_expert[l],
                                 w_expert_gate[l], w_expert_up[l], w_expert_down[l])
    return rms_norm(h, g_final)
```

```python
import functools

import jax
import jax.numpy as jnp
import numpy as np
from jax import lax
from jax.experimental import pallas as pl
from jax.experimental.pallas import tpu as pltpu

EPS = 1e-6
CHUNK = 64
H_A = 8
D_LATENT = 128
DH_A = 128
H_IDX = 8
D_IDX = 64
TOPK_MAX = 256
H_R = 8
DK_R = 128
DV_R = 128
ROPE_BASE = 10000.0
D_BRANCH = 1024
N_BRANCH = 2
N_GROUPS = 4
EXP_PER_GROUP = 8
N_EXPERTS = N_GROUPS * EXP_PER_GROUP
D_EXPERT = 1024

LANES = 128
KEY_TILE = 128
Q_TILE = 128
RET_CHUNK = 256
ROW_BLOCK = 256
VMEM_LIMIT = 56 * 1024 * 1024

C_QLAT = 0
C_GBR = 1024
C_QR = 5120
C_KR = 6144
C_VR = 7168
C_GR = 8192
C_QIDX = 9216
C_CKV = 9728
C_KW = 9856
D_IN_P = 9984

INT_MIN = np.int32(-2 ** 31)
NEG_BIG = -1e30

bf16 = jnp.bfloat16
f32 = jnp.float32


def _cparams(sem):
    return pltpu.CompilerParams(dimension_semantics=sem, vmem_limit_bytes=VMEM_LIMIT)


def _proj_kernel(x_ref, g_ref, w_ref, o_ref, xn_ref):
    @pl.when(pl.program_id(1) == 0)
    def _():
        x = x_ref[...]
        ms = jnp.mean(x * x, axis=-1, keepdims=True)
        xn_ref[...] = (x * lax.rsqrt(ms + EPS) * g_ref[...]).astype(bf16)

    o_ref[...] = jnp.dot(xn_ref[...], w_ref[...], preferred_element_type=f32)


def _proj(x2, g, w_p, tm, tn):
    t, d = x2.shape
    n = w_p.shape[1]
    return pl.pallas_call(
        _proj_kernel,
        grid=(t // tm, n // tn),
        in_specs=[
            pl.BlockSpec((tm, d), lambda i, j: (i, 0)),
            pl.BlockSpec((1, d), lambda i, j: (0, 0)),
            pl.BlockSpec((d, tn), lambda i, j: (0, j)),
        ],
        out_specs=pl.BlockSpec((tm, tn), lambda i, j: (i, j)),
        out_shape=jax.ShapeDtypeStruct((t, n), f32),
        scratch_shapes=[pltpu.VMEM((tm, d), bf16)],
        compiler_params=_cparams(("parallel", "arbitrary")),
        name="proj",
    )(x2, g, w_p)


def _float_key(s):
    bits = pltpu.bitcast(s, jnp.int32)
    key = bits ^ ((bits >> 31) & jnp.int32(0x7FFFFFFF))
    return jnp.where(s == 0.0, jnp.int32(0), key)


def _attn_kernel(qlat_ref, qidx_ref, kwq_ref, ckv_ref, kwk_ref, gkv_ref, wuv_ref, o_ref,
                 kv_s, kvT_s, kidx_s, key_s, bias_s, *, n_sel, n_kt):
    i = pl.program_id(1)
    idx_scale = (H_IDX ** -0.5) * (D_IDX ** -0.5)
    attn_scale = D_LATENT ** -0.5

    @pl.when(i == 0)
    def _():
        g = gkv_ref[...]
        for t in range(n_kt):
            c = ckv_ref[t * KEY_TILE:(t + 1) * KEY_TILE, :]
            ms = jnp.mean(c * c, axis=-1, keepdims=True)
            kv = c * lax.rsqrt(ms + EPS) * g
            kv_s[t] = kv.astype(bf16)
            kvT_s[t] = kv.T.astype(bf16)
            kidx_s[t] = kwk_ref[t * KEY_TILE:(t + 1) * KEY_TILE, :D_IDX].astype(bf16)

    nk = i + 1
    lane = lax.broadcasted_iota(jnp.int32, (1, Q_TILE), 1)
    sub = lax.broadcasted_iota(jnp.int32, (KEY_TILE, 1), 0)
    q_chunk = (i * Q_TILE + lane) // CHUNK

    wT = kwq_ref[...].T
    qidx = qidx_ref[...].astype(bf16)

    def score_tile(t, carry):
        ks = kidx_s[t]
        acc = jnp.zeros((KEY_TILE, Q_TILE), f32)
        for h in range(H_IDX):
            d = lax.dot_general(ks, qidx[:, h * D_IDX:(h + 1) * D_IDX],
                                (((1,), (1,)), ((), ())), preferred_element_type=f32)
            acc = acc + wT[D_IDX + h:D_IDX + h + 1, :] * jnp.maximum(d, 0.0)
        score = acc * idx_scale
        k_chunk = (t * KEY_TILE + sub) // CHUNK
        key_s[t] = jnp.where(k_chunk <= q_chunk, _float_key(score), INT_MIN)
        return carry

    lax.fori_loop(0, nk, score_tile, 0)

    def count(pred):
        def body(t, c):
            m = pred(key_s[t], t).astype(jnp.int32)
            return c + jnp.sum(m.reshape(KEY_TILE // 8, 8, Q_TILE), axis=0)
        c8 = lax.fori_loop(0, nk, body, jnp.zeros((8, Q_TILE), jnp.int32))
        return jnp.sum(c8, axis=0, keepdims=True)

    thr0 = jnp.where(count(lambda k, t: k >= 0) >= n_sel, jnp.int32(0), INT_MIN)
    thr0 = jnp.broadcast_to(thr0, (1, Q_TILE)).astype(jnp.int32)

    def bit_step(j, thr):
        cand = thr | (jnp.int32(1) << (jnp.int32(30) - j))
        return jnp.where(count(lambda k, t: k >= cand) >= n_sel, cand, thr)

    thr = lax.fori_loop(0, 31, bit_step, thr0)

    c_gt = count(lambda k, t: k > thr)
    c_ge = count(lambda k, t: k >= thr)
    need = n_sel - c_gt
    has_tie = jnp.max(jnp.where((c_ge > n_sel) & (thr > INT_MIN), 1, 0)) > 0

    def tie_limit():
        def step(j, m):
            cand = m | (jnp.int32(1) << (jnp.int32(14) - j))
            c = count(lambda k, t: (k == thr) & ((t * KEY_TILE + sub) < cand))
            return jnp.where(c < need, cand, m)
        return lax.fori_loop(0, 15, step, jnp.zeros((1, Q_TILE), jnp.int32))

    m_lim = lax.cond(has_tie, tie_limit, lambda: jnp.full((1, Q_TILE), 2 ** 30, jnp.int32))

    def bias_tile(t, carry):
        k = key_s[t]
        sel = (k > thr) | ((k == thr) & ((t * KEY_TILE + sub) <= m_lim))
        sel = sel & (k > INT_MIN)
        bias_s[t] = jnp.where(sel, 0.0, NEG_BIG).astype(f32)
        return carry

    lax.fori_loop(0, nk, bias_tile, 0)

    for h in range(H_A):
        qh = qlat_ref[:, h * D_LATENT:(h + 1) * D_LATENT].astype(bf16)

        def att_tile(t, carry):
            m_run, l_run, acc = carry
            logit = lax.dot_general(kv_s[t], qh, (((1,), (1,)), ((), ())),
                                    preferred_element_type=f32) * attn_scale + bias_s[t]
            m_new = jnp.maximum(m_run, jnp.max(logit, axis=0, keepdims=True))
            alpha = jnp.exp(m_run - m_new)
            p = jnp.exp(logit - m_new)
            l_new = alpha * l_run + jnp.sum(p, axis=0, keepdims=True)
            acc = alpha * acc + jnp.dot(kvT_s[t], p.astype(bf16), preferred_element_type=f32)
            return m_new, l_new, acc

        init = (jnp.full((1, Q_TILE), NEG_BIG, f32), jnp.zeros((1, Q_TILE), f32),
                jnp.zeros((D_LATENT, Q_TILE), f32))
        _, l_fin, acc = lax.fori_loop(0, nk, att_tile, init)
        o_lat = (acc / l_fin).T
        o_ref[:, h * DH_A:(h + 1) * DH_A] = jnp.dot(
            o_lat.astype(bf16), wuv_ref[h], preferred_element_type=f32)


def _attn(proj3, g_kv, w_uv_bf, n_sel):
    b, s, _ = proj3.shape
    n_kt = s // KEY_TILE
    kern = functools.partial(_attn_kernel, n_sel=n_sel, n_kt=n_kt)
    return pl.pallas_call(
        kern,
        grid=(b, s // Q_TILE),
        in_specs=[
            pl.BlockSpec((None, Q_TILE, H_A * D_LATENT), lambda bi, i: (bi, i, C_QLAT // 1024)),
            pl.BlockSpec((None, Q_TILE, H_IDX * D_IDX), lambda bi, i: (bi, i, C_QIDX // 512)),
            pl.BlockSpec((None, Q_TILE, LANES), lambda bi, i: (bi, i, C_KW // LANES)),
            pl.BlockSpec((None, s, LANES), lambda bi, i: (bi, 0, C_CKV // LANES)),
            pl.BlockSpec((None, s, LANES), lambda bi, i: (bi, 0, C_KW // LANES)),
            pl.BlockSpec((1, D_LATENT), lambda bi, i: (0, 0)),
            pl.BlockSpec((H_A, D_LATENT, DH_A), lambda bi, i: (0, 0, 0)),
        ],
        out_specs=pl.BlockSpec((None, Q_TILE, D_BRANCH), lambda bi, i: (bi, i, 0)),
        out_shape=jax.ShapeDtypeStruct((b, s, D_BRANCH), f32),
        scratch_shapes=[
            pltpu.VMEM((n_kt, KEY_TILE, D_LATENT), bf16),
            pltpu.VMEM((n_kt, D_LATENT, KEY_TILE), bf16),
            pltpu.VMEM((n_kt, KEY_TILE, D_IDX), bf16),
            pltpu.VMEM((n_kt, KEY_TILE, Q_TILE), jnp.int32),
            pltpu.VMEM((n_kt, KEY_TILE, Q_TILE), f32),
        ],
        compiler_params=_cparams(("parallel", "arbitrary")),
        name="attn",
    )(proj3, proj3, proj3, proj3, proj3, g_kv, w_uv_bf)


def _ret_kernel(q_ref, k_ref, v_ref, gr_ref, cos_ref, sin_ref, din_ref, dq_ref, dk_ref, dc_ref,
                gret_ref, o_ref, state_s):
    @pl.when(pl.program_id(1) == 0)
    def _():
        state_s[...] = jnp.zeros_like(state_s)

    cos = cos_ref[...]
    sin = sin_ref[...]

    def rot(x):
        return x * cos + pltpu.roll(x, DK_R // 2, axis=1) * sin

    for h in range(H_R):
        sl = slice(h * DK_R, (h + 1) * DK_R)
        q = rot(q_ref[:, sl]).astype(bf16)
        kf = rot(k_ref[:, sl]) * (DK_R ** -0.5)
        k = kf.astype(bf16)
        v = v_ref[:, sl].astype(bf16)
        inner = lax.dot_general(q, k, (((1,), (1,)), ((), ())), preferred_element_type=f32) * din_ref[h]
        o = jnp.dot(inner.astype(bf16), v, preferred_element_type=f32)
        st = state_s[h]
        o = o + jnp.dot(q, st.astype(bf16), preferred_element_type=f32) * dq_ref[h]
        kd = (kf * dk_ref[h]).astype(bf16)
        state_s[h] = st * dc_ref[h] + jnp.dot(kd.T, v, preferred_element_type=f32)
        mu = jnp.mean(o, axis=-1, keepdims=True)
        var = jnp.mean(jnp.square(o - mu), axis=-1, keepdims=True)
        y = (o - mu) * lax.rsqrt(var + EPS) * gret_ref[:, sl]
        gate = gr_ref[:, sl]
        o_ref[:, sl] = gate * jax.nn.sigmoid(gate) * y


def _ret(proj3, cos_t, sin_t, d_in, d_q, d_k, d_c, g_ret):
    b, s, _ = proj3.shape
    c = RET_CHUNK
    w = H_R * DK_R

    def col(off):
        return pl.BlockSpec((None, c, w), lambda bi, ci: (bi, ci, off // w))

    return pl.pallas_call(
        _ret_kernel,
        grid=(b, s // c),
        in_specs=[
            col(C_QR), col(C_KR), col(C_VR), col(C_GR),
            pl.BlockSpec((c, DK_R), lambda bi, ci: (ci, 0)),
            pl.BlockSpec((c, DK_R), lambda bi, ci: (ci, 0)),
            pl.BlockSpec((H_R, c, c), lambda bi, ci: (0, 0, 0)),
            pl.BlockSpec((H_R, c, DK_R), lambda bi, ci: (0, 0, 0)),
            pl.BlockSpec((H_R, c, DK_R), lambda bi, ci: (0, 0, 0)),
            pl.BlockSpec((H_R, 1, DK_R), lambda bi, ci: (0, 0, 0)),
            pl.BlockSpec((1, w), lambda bi, ci: (0, 0)),
        ],
        out_specs=pl.BlockSpec((None, c, w), lambda bi, ci: (bi, ci, 0)),
        out_shape=jax.ShapeDtypeStruct((b, s, w), f32),
        scratch_shapes=[pltpu.VMEM((H_R, DK_R, DV_R), f32)],
        compiler_params=_cparams(("parallel", "arbitrary")),
        name="ret",
    )(proj3, proj3, proj3, proj3, cos_t, sin_t, d_in, d_q, d_k, d_c, g_ret)


def _retention_tables(s):
    c = RET_CHUNK
    half = DK_R // 2
    freq = ROPE_BASE ** (-jnp.arange(half, dtype=f32) / half)
    ang = jnp.arange(s, dtype=f32)[:, None] * freq[None, :]
    cos = jnp.cos(ang)
    sin = jnp.sin(ang)
    cos_t = jnp.concatenate([cos, cos], axis=-1)
    sin_t = jnp.concatenate([-sin, sin], axis=-1)
    log_gamma = jnp.log1p(-jnp.exp2(-5.0 - jnp.arange(H_R, dtype=f32)))
    n = jnp.arange(c, dtype=f32)
    diff = n[:, None] - n[None, :]
    d_in = jnp.where(diff >= 0, jnp.exp(log_gamma[:, None, None] * jnp.maximum(diff, 0.0)), 0.0)
    d_q = jnp.broadcast_to(jnp.exp(log_gamma[:, None] * (n + 1.0))[:, :, None], (H_R, c, DK_R))
    d_k = jnp.broadcast_to(jnp.exp(log_gamma[:, None] * (c - 1.0 - n))[:, :, None], (H_R, c, DK_R))
    d_c = jnp.broadcast_to(jnp.exp(log_gamma * c)[:, None, None], (H_R, 1, DK_R))
    return cos_t, sin_t, d_in, d_q, d_k, d_c


def _mix_kernel(oa_ref, ob_ref, wb_ref, ga_ref, gb_ref, o_ref):
    a = jnp.dot(oa_ref[...].astype(bf16), wb_ref[0], preferred_element_type=f32)
    b = jnp.dot(ob_ref[...].astype(bf16), wb_ref[1], preferred_element_type=f32)
    o_ref[...] = (jax.nn.sigmoid(ga_ref[...]) * a + jax.nn.sigmoid(gb_ref[...]) * b).astype(o_ref.dtype)


def _mix(o_a, o_b, w_branch_bf, proj, tm, tn):
    t = o_a.shape[0]
    d = w_branch_bf.shape[2]
    return pl.pallas_call(
        _mix_kernel,
        grid=(t // tm, d // tn),
        in_specs=[
            pl.BlockSpec((tm, D_BRANCH), lambda i, j: (i, 0)),
            pl.BlockSpec((tm, D_BRANCH), lambda i, j: (i, 0)),
            pl.BlockSpec((N_BRANCH, D_BRANCH, tn), lambda i, j: (0, 0, j)),
            pl.BlockSpec((tm, tn), lambda i, j: (i, C_GBR // tn + j)),
            pl.BlockSpec((tm, tn), lambda i, j: (i, (C_GBR + d) // tn + j)),
        ],
        out_specs=pl.BlockSpec((tm, tn), lambda i, j: (i, j)),
        out_shape=jax.ShapeDtypeStruct((t, d), bf16),
        compiler_params=_cparams(("parallel", "parallel")),
        name="mix",
    )(o_a, o_b, w_branch_bf, proj, proj)


def _split_bf16(a):
    hi = a.astype(bf16)
    lo = (a - hi.astype(f32)).astype(bf16)
    return hi, lo


def _outproj_kernel(mixed_ref, x_ref, wo_ref, g_ref, wr_hi_ref, wr_lo_ref, br_ref,
                    h_ref, xn_ref, eid_ref, gate_ref):
    h = x_ref[...] + jnp.dot(mixed_ref[...], wo_ref[...], preferred_element_type=f32)
    h_ref[...] = h
    ms = jnp.mean(h * h, axis=-1, keepdims=True)
    xn = h * lax.rsqrt(ms + EPS) * g_ref[...]
    xn_ref[...] = xn

    x_hi, x_lo = _split_bf16(xn)
    logit = (jnp.dot(x_hi, wr_hi_ref[...], preferred_element_type=f32)
             + jnp.dot(x_hi, wr_lo_ref[...], preferred_element_type=f32)
             + jnp.dot(x_lo, wr_hi_ref[...], preferred_element_type=f32)) + br_ref[...]

    lane = lax.broadcasted_iota(jnp.int32, logit.shape, 1)
    lanef = lane.astype(f32)
    neg = -jnp.inf

    def first_argmax(v, m):
        return jnp.min(jnp.where(v == m, lanef, float(LANES)), axis=-1, keepdims=True)

    lg = jnp.where(lane < N_GROUPS, logit, neg)
    mg = jnp.max(lg, axis=-1, keepdims=True)
    p_grp = 1.0 / jnp.sum(jnp.exp(lg - mg), axis=-1, keepdims=True)
    grp = first_argmax(lg, mg).astype(jnp.int32)

    e_lane = lane - N_GROUPS
    in_grp = (e_lane >= 0) & (e_lane < N_EXPERTS) & ((e_lane // EXP_PER_GROUP) == grp)
    le = jnp.where(in_grp, logit, neg)
    m1 = jnp.max(le, axis=-1, keepdims=True)
    i1 = first_argmax(le, m1)
    le2 = jnp.where(lanef == i1, neg, le)
    m2 = jnp.max(le2, axis=-1, keepdims=True)
    i2 = first_argmax(le2, m2)
    e2 = jnp.exp(m2 - m1)
    g1 = p_grp / (1.0 + e2)
    g2 = p_grp * e2 / (1.0 + e2)

    eid = jnp.where(lane == 0, i1, jnp.where(lane == 1, i2, float(N_GROUPS))) - float(N_GROUPS)
    eid_ref[...] = eid.astype(jnp.int32)
    gate_ref[...] = jnp.where(lane == 0, g1, jnp.where(lane == 1, g2, 0.0))


def _outproj(mixed, x2, w_out_bf, g_ffn, wr_hi, wr_lo, b_r, tm):
    t, d = x2.shape
    row = lambda i: (i, 0)
    fixed = lambda i: (0, 0)
    return pl.pallas_call(
        _outproj_kernel,
        grid=(t // tm,),
        in_specs=[
            pl.BlockSpec((tm, d), row),
            pl.BlockSpec((tm, d), row),
            pl.BlockSpec((d, d), fixed),
            pl.BlockSpec((1, d), fixed),
            pl.BlockSpec((d, LANES), fixed),
            pl.BlockSpec((d, LANES), fixed),
            pl.BlockSpec((1, LANES), fixed),
        ],
        out_specs=[
            pl.BlockSpec((tm, d), row),
            pl.BlockSpec((tm, d), row),
            pl.BlockSpec((tm, LANES), row),
            pl.BlockSpec((tm, LANES), row),
        ],
        out_shape=[
            jax.ShapeDtypeStruct((t, d), f32),
            jax.ShapeDtypeStruct((t, d), f32),
            jax.ShapeDtypeStruct((t, LANES), jnp.int32),
            jax.ShapeDtypeStruct((t, LANES), f32),
        ],
        compiler_params=_cparams(("parallel",)),
        name="outproj",
    )(mixed, x2, w_out_bf, g_ffn, wr_hi, wr_lo, b_r)


def _row_copy(src_hbm, dst_vmem, sem, src_row, dst_row):
    return pltpu.make_async_copy(src_hbm.at[pl.ds(src_row, 1)], dst_vmem.at[pl.ds(dst_row, 1)], sem)


def _experts_kernel(bexp_ref, rtok_ref, nused_ref, xn_hbm, rg_ref, wg_ref, wu_ref, wd_ref, y_ref,
                    xbuf, sem):
    i = pl.program_id(0)

    @pl.when(i < nused_ref[0])
    def _():
        base = i * ROW_BLOCK

        def issue(r, c):
            _row_copy(xn_hbm, xbuf, sem.at[0], rtok_ref[base + r], r).start()
            return c

        lax.fori_loop(0, ROW_BLOCK, issue, 0)

        def drain(r, c):
            _row_copy(xn_hbm, xbuf, sem.at[0], 0, r).wait()
            return c

        lax.fori_loop(0, ROW_BLOCK, drain, 0)

        xb = xbuf[...].astype(bf16)
        g = jnp.dot(xb, wg_ref[...], preferred_element_type=f32)
        u = jnp.dot(xb, wu_ref[...], preferred_element_type=f32)
        hm = (g * jax.nn.sigmoid(g) * u).astype(bf16)
        y_ref[...] = jnp.dot(hm, wd_ref[...], preferred_element_type=f32) * rg_ref[...]

    @pl.when(i >= nused_ref[0])
    def _():
        y_ref[...] = jnp.zeros_like(y_ref)


def _experts(block_expert, row_tok, n_used, xn, row_gate, wg_bf, wu_bf, wd_bf):
    n_rows = row_tok.shape[0]
    d = xn.shape[1]
    f = wg_bf.shape[2]
    grid_spec = pltpu.PrefetchScalarGridSpec(
        num_scalar_prefetch=3,
        grid=(n_rows // ROW_BLOCK,),
        in_specs=[
            pl.BlockSpec(memory_space=pl.ANY),
            pl.BlockSpec((ROW_BLOCK, 1), lambda i, be, rt, nu: (i, 0)),
            pl.BlockSpec((None, d, f), lambda i, be, rt, nu: (be[i], 0, 0)),
            pl.BlockSpec((None, d, f), lambda i, be, rt, nu: (be[i], 0, 0)),
            pl.BlockSpec((None, f, d), lambda i, be, rt, nu: (be[i], 0, 0)),
        ],
        out_specs=pl.BlockSpec((ROW_BLOCK, d), lambda i, be, rt, nu: (i, 0)),
        scratch_shapes=[pltpu.VMEM((ROW_BLOCK, d), f32), pltpu.SemaphoreType.DMA((1,))],
    )
    return pl.pallas_call(
        _experts_kernel,
        grid_spec=grid_spec,
        out_shape=jax.ShapeDtypeStruct((n_rows, d), f32),
        compiler_params=_cparams(("arbitrary",)),
        name="experts",
    )(block_expert, row_tok, n_used, xn, row_gate, wg_bf, wu_bf, wd_bf)


def _combine_kernel(pos_ref, h_ref, y_hbm, g_ref, o_ref, ybuf, sem, *, tm):
    base = pl.program_id(0) * tm

    def issue(r, c):
        _row_copy(y_hbm, ybuf.at[0], sem.at[0], pos_ref[2 * (base + r)], r).start()
        _row_copy(y_hbm, ybuf.at[1], sem.at[0], pos_ref[2 * (base + r) + 1], r).start()
        return c

    lax.fori_loop(0, tm, issue, 0)

    def drain(r, c):
        _row_copy(y_hbm, ybuf.at[0], sem.at[0], 0, r).wait()
        _row_copy(y_hbm, ybuf.at[1], sem.at[0], 0, r).wait()
        return c

    lax.fori_loop(0, tm, drain, 0)

    hh = h_ref[...] + ybuf[0] + ybuf[1]
    ms = jnp.mean(hh * hh, axis=-1, keepdims=True)
    o_ref[...] = hh * lax.rsqrt(ms + EPS) * g_ref[...]


def _combine(pos, h, y, g_final, tm):
    t, d = h.shape
    grid_spec = pltpu.PrefetchScalarGridSpec(
        num_scalar_prefetch=1,
        grid=(t // tm,),
        in_specs=[
            pl.BlockSpec((tm, d), lambda i, p: (i, 0)),
            pl.BlockSpec(memory_space=pl.ANY),
            pl.BlockSpec((1, d), lambda i, p: (0, 0)),
        ],
        out_specs=pl.BlockSpec((tm, d), lambda i, p: (i, 0)),
        scratch_shapes=[pltpu.VMEM((2, tm, d), f32), pltpu.SemaphoreType.DMA((1,))],
    )
    return pl.pallas_call(
        functools.partial(_combine_kernel, tm=tm),
        grid_spec=grid_spec,
        out_shape=jax.ShapeDtypeStruct((t, d), f32),
        compiler_params=_cparams(("arbitrary",)),
        name="combine",
    )(pos, h, y, g_final)


def _dispatch_tables(eid, gate, n_tok):
    n_asg = n_tok * 2
    e_flat = eid.reshape(-1)
    gate_flat = gate.reshape(-1)
    tok_flat = jnp.repeat(jnp.arange(n_tok, dtype=jnp.int32), 2)
    order = jnp.argsort(e_flat)
    e_sorted = e_flat[order]
    counts = jnp.bincount(e_flat, length=N_EXPERTS)
    padded = (counts + ROW_BLOCK - 1) // ROW_BLOCK * ROW_BLOCK
    pad_end = jnp.cumsum(padded)
    pad_start = pad_end - padded
    cnt_start = jnp.cumsum(counts) - counts
    rank = jnp.arange(n_asg) - cnt_start[e_sorted]
    dest = (pad_start[e_sorted] + rank).astype(jnp.int32)
    n_rows = -(-(n_asg + N_EXPERTS * (ROW_BLOCK - 1)) // ROW_BLOCK) * ROW_BLOCK
    n_blocks = n_rows // ROW_BLOCK
    row_tok = jnp.zeros((n_rows,), jnp.int32).at[dest].set(tok_flat[order])
    row_gate = jnp.zeros((n_rows,), f32).at[dest].set(gate_flat[order])
    block_expert = jnp.minimum(
        jnp.searchsorted(pad_end, jnp.arange(n_blocks) * ROW_BLOCK, side='right'),
        N_EXPERTS - 1).astype(jnp.int32)
    pos = jnp.zeros((n_asg,), jnp.int32).at[order].set(dest)
    n_used = (pad_end[-1] // ROW_BLOCK).astype(jnp.int32).reshape(1)
    return block_expert, row_tok, row_gate.reshape(n_rows, 1), pos, n_used


def _pick(n, prefs):
    for p in prefs:
        if n % p == 0:
            return p
    return n


def kernel(x, g_mix_norm, w_in, g_kv, w_uv, g_ret, w_branch, w_out, g_ffn_norm, w_router_group,
           b_router_group, w_router_expert, b_router_expert, w_expert_gate, w_expert_up,
           w_expert_down, g_final):
    b, s, d = x.shape
    t = b * s
    depth = w_in.shape[0]
    n_sel = min(TOPK_MAX, s // 4)
    assert s % RET_CHUNK == 0 and s % Q_TILE == 0

    cos_t, sin_t, d_in, d_q, d_k, d_c = _retention_tables(s)
    h2 = x.reshape(t, d)
    for l in range(depth):
        wl = w_in[l]
        sp = np.cumsum([0, H_A * D_LATENT, D_LATENT, H_IDX * D_IDX, D_IDX, H_IDX,
                        H_R * DK_R, H_R * DK_R, H_R * DV_R, H_R * DV_R, N_BRANCH * d])
        seg = [wl[:, sp[k]:sp[k + 1]] for k in range(10)]
        kw_pad = jnp.zeros((d, LANES - D_IDX - H_IDX), wl.dtype)
        w_p = jnp.concatenate([seg[0], seg[9], seg[5], seg[6], seg[7], seg[8], seg[2], seg[1],
                               seg[3], seg[4], kw_pad], axis=1).astype(bf16)
        assert w_p.shape[1] == D_IN_P

        proj = _proj(h2, g_mix_norm[l].reshape(1, d), w_p, _pick(t, (1024, 512, 256)), 768)
        proj3 = proj.reshape(b, s, D_IN_P)

        o_a = _attn(proj3, g_kv[l].reshape(1, D_LATENT), w_uv[l].astype(bf16), n_sel)
        o_b = _ret(proj3, cos_t, sin_t, d_in, d_q, d_k, d_c, g_ret[l].reshape(1, H_R * DV_R))

        mixed = _mix(o_a.reshape(t, D_BRANCH), o_b.reshape(t, D_BRANCH), w_branch[l].astype(bf16),
                     proj, _pick(t, (512, 256)), 512)

        w_r = jnp.concatenate([w_router_group[l], w_router_expert[l],
                               jnp.zeros((d, LANES - N_GROUPS - N_EXPERTS), f32)], axis=1)
        b_r = jnp.concatenate([b_router_group[l], b_router_expert[l],
                               jnp.zeros((LANES - N_GROUPS - N_EXPERTS,), f32)]).reshape(1, LANES)
        wr_hi = w_r.astype(bf16)
        wr_lo = (w_r - wr_hi.astype(f32)).astype(bf16)
        h2, xn, eid, gate = _outproj(mixed, h2, w_out[l].astype(bf16), g_ffn_norm[l].reshape(1, d),
                                     wr_hi, wr_lo, b_r, _pick(t, (256,)))

        block_expert, row_tok, row_gate, pos, n_used = _dispatch_tables(eid[:, :2], gate[:, :2], t)
        y = _experts(block_expert, row_tok, n_used, xn, row_gate, w_expert_gate[l].astype(bf16),
                     w_expert_up[l].astype(bf16), w_expert_down[l].astype(bf16))
        last = l == depth - 1
        g_out = g_final.reshape(1, d) if last else jnp.ones((1, d), f32)
        assert last
        h2 = _combine(pos, h2, y, g_out, _pick(t, (256,)))
    return h2.reshape(b, s, d)
```

```python
import functools

import jax
import jax.numpy as jnp
import numpy as np
from jax import lax
from jax.experimental import pallas as pl
from jax.experimental.pallas import tpu as pltpu

EPS = 1e-6
CHUNK = 64
H_A = 8
D_LATENT = 128
DH_A = 128
H_IDX = 8
D_IDX = 64
TOPK_MAX = 256
H_R = 8
DK_R = 128
DV_R = 128
ROPE_BASE = 10000.0
D_BRANCH = 1024
N_BRANCH = 2
N_GROUPS = 4
EXP_PER_GROUP = 8
N_EXPERTS = N_GROUPS * EXP_PER_GROUP
D_EXPERT = 1024

LANES = 128
KEY_TILE = 256
Q_TILE = 128
RET_CHUNK = 256
ROW_BLOCK = 256
VMEM_LIMIT = 56 * 1024 * 1024

C_QLAT = 0
C_GBR = 1024
C_QR = 5120
C_KR = 6144
C_VR = 7168
C_GR = 8192
C_QIDX = 9216
C_CKV = 9728
C_KW = 9856
D_IN_P = 9984

INT_MIN = np.int32(-2 ** 31)
NEG_BIG = -1e30

bf16 = jnp.bfloat16
f32 = jnp.float32


def _cparams(sem):
    return pltpu.CompilerParams(dimension_semantics=sem, vmem_limit_bytes=VMEM_LIMIT)


def _proj_kernel(x_ref, g_ref, w_ref, o_ref, xn_ref):
    @pl.when(pl.program_id(1) == 0)
    def _():
        x = x_ref[...]
        ms = jnp.mean(x * x, axis=-1, keepdims=True)
        xn_ref[...] = (x * lax.rsqrt(ms + EPS) * g_ref[...]).astype(bf16)

    o_ref[...] = jnp.dot(xn_ref[...], w_ref[...], preferred_element_type=f32)


def _proj(x2, g, w_p, tm, tn):
    t, d = x2.shape
    n = w_p.shape[1]
    return pl.pallas_call(
        _proj_kernel,
        grid=(t // tm, n // tn),
        in_specs=[
            pl.BlockSpec((tm, d), lambda i, j: (i, 0)),
            pl.BlockSpec((1, d), lambda i, j: (0, 0)),
            pl.BlockSpec((d, tn), lambda i, j: (0, j)),
        ],
        out_specs=pl.BlockSpec((tm, tn), lambda i, j: (i, j)),
        out_shape=jax.ShapeDtypeStruct((t, n), f32),
        scratch_shapes=[pltpu.VMEM((tm, d), bf16)],
        compiler_params=_cparams(("parallel", "arbitrary")),
        name="proj",
    )(x2, g, w_p)


def _float_key(s):
    bits = pltpu.bitcast(s, jnp.int32)
    key = bits ^ ((bits >> 31) & jnp.int32(0x7FFFFFFF))
    return jnp.where(s == 0.0, jnp.int32(0), key)


def _attn_kernel(qlat_ref, qidx_ref, kwq_ref, ckv_ref, kwk_ref, gkv_ref, wuv_ref, o_ref,
                 kv_s, kvT_s, kidx_s, key_s, bias_s, qT_s, acc_s, *, n_sel, n_kt):
    i = pl.program_id(1)
    idx_scale = (H_IDX ** -0.5) * (D_IDX ** -0.5)
    attn_scale = D_LATENT ** -0.5
    hq = H_A * Q_TILE

    @pl.when(i == 0)
    def _():
        g = gkv_ref[...]
        for t in range(n_kt):
            c = ckv_ref[t * KEY_TILE:(t + 1) * KEY_TILE, :]
            ms = jnp.mean(c * c, axis=-1, keepdims=True)
            kv = c * lax.rsqrt(ms + EPS) * g
            kv_s[t] = kv.astype(bf16)
            kvT_s[t] = kv.T.astype(bf16)
            kidx_s[t] = kwk_ref[t * KEY_TILE:(t + 1) * KEY_TILE, :D_IDX].astype(bf16)

    nk = ((i + 1) * Q_TILE + KEY_TILE - 1) // KEY_TILE
    lane = lax.broadcasted_iota(jnp.int32, (1, Q_TILE), 1)
    sub = lax.broadcasted_iota(jnp.int32, (KEY_TILE, 1), 0)
    q_chunk = (i * Q_TILE + lane) // CHUNK

    wT = kwq_ref[...].T
    qidx = qidx_ref[...].astype(bf16)
    for h in range(H_A):
        qT_s[:, h * Q_TILE:(h + 1) * Q_TILE] = qlat_ref[:, h * D_LATENT:(h + 1) * D_LATENT].T.astype(bf16)

    def score_tile(t, carry):
        ks = kidx_s[t]
        acc = jnp.zeros((KEY_TILE, Q_TILE), f32)
        for h in range(H_IDX):
            d = lax.dot_general(ks, qidx[:, h * D_IDX:(h + 1) * D_IDX],
                                (((1,), (1,)), ((), ())), preferred_element_type=f32)
            acc = acc + wT[D_IDX + h:D_IDX + h + 1, :] * jnp.maximum(d, 0.0)
        score = acc * idx_scale
        k_chunk = (t * KEY_TILE + sub) // CHUNK
        key_s[t] = jnp.where(k_chunk <= q_chunk, _float_key(score), INT_MIN)
        return carry

    lax.fori_loop(0, nk, score_tile, 0)

    def count(pred):
        def body(t, c):
            m = pred(key_s[t], t).astype(jnp.int32)
            return c + jnp.sum(m.reshape(KEY_TILE // 8, 8, Q_TILE), axis=0)
        c8 = lax.fori_loop(0, nk, body, jnp.zeros((8, Q_TILE), jnp.int32))
        return jnp.sum(c8, axis=0, keepdims=True)

    thr0 = jnp.where(count(lambda k, t: k >= 0) >= n_sel, jnp.int32(0), INT_MIN)
    thr0 = jnp.broadcast_to(thr0, (1, Q_TILE)).astype(jnp.int32)

    def bit_step(j, thr):
        cand = thr | (jnp.int32(1) << (jnp.int32(30) - j))
        return jnp.where(count(lambda k, t: k >= cand) >= n_sel, cand, thr)

    thr = lax.fori_loop(0, 31, bit_step, thr0)

    c_gt = count(lambda k, t: k > thr)
    c_ge = count(lambda k, t: k >= thr)
    need = n_sel - c_gt
    has_tie = jnp.max(jnp.where((c_ge > n_sel) & (thr > INT_MIN), 1, 0)) > 0

    def tie_limit():
        def step(j, m):
            cand = m | (jnp.int32(1) << (jnp.int32(14) - j))
            c = count(lambda k, t: (k == thr) & ((t * KEY_TILE + sub) < cand))
            return jnp.where(c < need, cand, m)
        return lax.fori_loop(0, 15, step, jnp.zeros((1, Q_TILE), jnp.int32))

    m_lim = lax.cond(has_tie, tie_limit, lambda: jnp.full((1, Q_TILE), 2 ** 30, jnp.int32))

    def bias_tile(t, carry):
        k = key_s[t]
        sel = (k > thr) | ((k == thr) & ((t * KEY_TILE + sub) <= m_lim))
        sel = sel & (k > INT_MIN)
        bias_s[t] = jnp.where(sel, 0.0, NEG_BIG).astype(f32)
        return carry

    lax.fori_loop(0, nk, bias_tile, 0)

    acc_s[...] = jnp.zeros_like(acc_s)

    def att_tile(t, carry):
        m_run, l_run = carry
        logit = jnp.dot(kv_s[t], qT_s[...], preferred_element_type=f32) * attn_scale
        logit = logit + jnp.concatenate([bias_s[t]] * H_A, axis=1)
        m_new = jnp.maximum(m_run, jnp.max(logit, axis=0, keepdims=True))
        alpha = jnp.exp(m_run - m_new)
        p = jnp.exp(logit - m_new)
        l_new = alpha * l_run + jnp.sum(p, axis=0, keepdims=True)
        acc_s[...] = alpha * acc_s[...] + jnp.dot(kvT_s[t], p.astype(bf16), preferred_element_type=f32)
        return m_new, l_new

    init = (jnp.full((1, hq), NEG_BIG, f32), jnp.zeros((1, hq), f32))
    _, l_fin = lax.fori_loop(0, nk, att_tile, init)
    inv_l = 1.0 / l_fin
    for h in range(H_A):
        sl = slice(h * Q_TILE, (h + 1) * Q_TILE)
        o_lat = (acc_s[:, sl] * inv_l[:, sl]).T
        o_ref[:, h * DH_A:(h + 1) * DH_A] = jnp.dot(
            o_lat.astype(bf16), wuv_ref[h], preferred_element_type=f32)


def _attn(proj3, g_kv, w_uv_bf, n_sel):
    b, s, _ = proj3.shape
    n_kt = s // KEY_TILE
    kern = functools.partial(_attn_kernel, n_sel=n_sel, n_kt=n_kt)
    return pl.pallas_call(
        kern,
        grid=(b, s // Q_TILE),
        in_specs=[
            pl.BlockSpec((None, Q_TILE, H_A * D_LATENT), lambda bi, i: (bi, i, C_QLAT // 1024)),
            pl.BlockSpec((None, Q_TILE, H_IDX * D_IDX), lambda bi, i: (bi, i, C_QIDX // 512)),
            pl.BlockSpec((None, Q_TILE, LANES), lambda bi, i: (bi, i, C_KW // LANES)),
            pl.BlockSpec((None, s, LANES), lambda bi, i: (bi, 0, C_CKV // LANES)),
            pl.BlockSpec((None, s, LANES), lambda bi, i: (bi, 0, C_KW // LANES)),
            pl.BlockSpec((1, D_LATENT), lambda bi, i: (0, 0)),
            pl.BlockSpec((H_A, D_LATENT, DH_A), lambda bi, i: (0, 0, 0)),
        ],
        out_specs=pl.BlockSpec((None, Q_TILE, D_BRANCH), lambda bi, i: (bi, i, 0)),
        out_shape=jax.ShapeDtypeStruct((b, s, D_BRANCH), f32),
        scratch_shapes=[
            pltpu.VMEM((n_kt, KEY_TILE, D_LATENT), bf16),
            pltpu.VMEM((n_kt, D_LATENT, KEY_TILE), bf16),
            pltpu.VMEM((n_kt, KEY_TILE, D_IDX), bf16),
            pltpu.VMEM((n_kt, KEY_TILE, Q_TILE), jnp.int32),
            pltpu.VMEM((n_kt, KEY_TILE, Q_TILE), f32),
            pltpu.VMEM((D_LATENT, H_A * Q_TILE), bf16),
            pltpu.VMEM((D_LATENT, H_A * Q_TILE), f32),
        ],
        compiler_params=_cparams(("parallel", "arbitrary")),
        name="attn",
    )(proj3, proj3, proj3, proj3, proj3, g_kv, w_uv_bf)


def _ret_kernel(q_ref, k_ref, v_ref, gr_ref, cos_ref, sin_ref, din_ref, dq_ref, dk_ref, dc_ref,
                gret_ref, o_ref, state_s):
    @pl.when(pl.program_id(1) == 0)
    def _():
        state_s[...] = jnp.zeros_like(state_s)

    cos = cos_ref[...]
    sin = sin_ref[...]

    def rot(x):
        return x * cos + pltpu.roll(x, DK_R // 2, axis=1) * sin

    for h in range(H_R):
        sl = slice(h * DK_R, (h + 1) * DK_R)
        q = rot(q_ref[:, sl]).astype(bf16)
        kf = rot(k_ref[:, sl]) * (DK_R ** -0.5)
        k = kf.astype(bf16)
        v = v_ref[:, sl].astype(bf16)
        inner = lax.dot_general(q, k, (((1,), (1,)), ((), ())), preferred_element_type=f32) * din_ref[h]
        o = jnp.dot(inner.astype(bf16), v, preferred_element_type=f32)
        st = state_s[h]
        o = o + jnp.dot(q, st.astype(bf16), preferred_element_type=f32) * dq_ref[h]
        kd = (kf * dk_ref[h]).astype(bf16)
        state_s[h] = st * dc_ref[h] + jnp.dot(kd.T, v, preferred_element_type=f32)
        mu = jnp.mean(o, axis=-1, keepdims=True)
        var = jnp.mean(jnp.square(o - mu), axis=-1, keepdims=True)
        y = (o - mu) * lax.rsqrt(var + EPS) * gret_ref[:, sl]
        gate = gr_ref[:, sl]
        o_ref[:, sl] = gate * jax.nn.sigmoid(gate) * y


def _ret(proj3, cos_t, sin_t, d_in, d_q, d_k, d_c, g_ret):
    b, s, _ = proj3.shape
    c = RET_CHUNK
    w = H_R * DK_R

    def col(off):
        return pl.BlockSpec((None, c, w), lambda bi, ci: (bi, ci, off // w))

    return pl.pallas_call(
        _ret_kernel,
        grid=(b, s // c),
        in_specs=[
            col(C_QR), col(C_KR), col(C_VR), col(C_GR),
            pl.BlockSpec((c, DK_R), lambda bi, ci: (ci, 0)),
            pl.BlockSpec((c, DK_R), lambda bi, ci: (ci, 0)),
            pl.BlockSpec((H_R, c, c), lambda bi, ci: (0, 0, 0)),
            pl.BlockSpec((H_R, c, DK_R), lambda bi, ci: (0, 0, 0)),
            pl.BlockSpec((H_R, c, DK_R), lambda bi, ci: (0, 0, 0)),
            pl.BlockSpec((H_R, 1, DK_R), lambda bi, ci: (0, 0, 0)),
            pl.BlockSpec((1, w), lambda bi, ci: (0, 0)),
        ],
        out_specs=pl.BlockSpec((None, c, w), lambda bi, ci: (bi, ci, 0)),
        out_shape=jax.ShapeDtypeStruct((b, s, w), f32),
        scratch_shapes=[pltpu.VMEM((H_R, DK_R, DV_R), f32)],
        compiler_params=_cparams(("parallel", "arbitrary")),
        name="ret",
    )(proj3, proj3, proj3, proj3, cos_t, sin_t, d_in, d_q, d_k, d_c, g_ret)


def _retention_tables(s):
    c = RET_CHUNK
    half = DK_R // 2
    freq = ROPE_BASE ** (-jnp.arange(half, dtype=f32) / half)
    ang = jnp.arange(s, dtype=f32)[:, None] * freq[None, :]
    cos = jnp.cos(ang)
    sin = jnp.sin(ang)
    cos_t = jnp.concatenate([cos, cos], axis=-1)
    sin_t = jnp.concatenate([-sin, sin], axis=-1)
    log_gamma = jnp.log1p(-jnp.exp2(-5.0 - jnp.arange(H_R, dtype=f32)))
    n = jnp.arange(c, dtype=f32)
    diff = n[:, None] - n[None, :]
    d_in = jnp.where(diff >= 0, jnp.exp(log_gamma[:, None, None] * jnp.maximum(diff, 0.0)), 0.0)
    d_q = jnp.broadcast_to(jnp.exp(log_gamma[:, None] * (n + 1.0))[:, :, None], (H_R, c, DK_R))
    d_k = jnp.broadcast_to(jnp.exp(log_gamma[:, None] * (c - 1.0 - n))[:, :, None], (H_R, c, DK_R))
    d_c = jnp.broadcast_to(jnp.exp(log_gamma * c)[:, None, None], (H_R, 1, DK_R))
    return cos_t, sin_t, d_in, d_q, d_k, d_c


def _mix_kernel(oa_ref, ob_ref, wb_ref, ga_ref, gb_ref, o_ref):
    a = jnp.dot(oa_ref[...].astype(bf16), wb_ref[0], preferred_element_type=f32)
    b = jnp.dot(ob_ref[...].astype(bf16), wb_ref[1], preferred_element_type=f32)
    o_ref[...] = (jax.nn.sigmoid(ga_ref[...]) * a + jax.nn.sigmoid(gb_ref[...]) * b).astype(o_ref.dtype)


def _mix(o_a, o_b, w_branch_bf, proj, tm, tn):
    t = o_a.shape[0]
    d = w_branch_bf.shape[2]
    return pl.pallas_call(
        _mix_kernel,
        grid=(t // tm, d // tn),
        in_specs=[
            pl.BlockSpec((tm, D_BRANCH), lambda i, j: (i, 0)),
            pl.BlockSpec((tm, D_BRANCH), lambda i, j: (i, 0)),
            pl.BlockSpec((N_BRANCH, D_BRANCH, tn), lambda i, j: (0, 0, j)),
            pl.BlockSpec((tm, tn), lambda i, j: (i, C_GBR // tn + j)),
            pl.BlockSpec((tm, tn), lambda i, j: (i, (C_GBR + d) // tn + j)),
        ],
        out_specs=pl.BlockSpec((tm, tn), lambda i, j: (i, j)),
        out_shape=jax.ShapeDtypeStruct((t, d), bf16),
        compiler_params=_cparams(("parallel", "parallel")),
        name="mix",
    )(o_a, o_b, w_branch_bf, proj, proj)


def _split_bf16(a):
    hi = a.astype(bf16)
    lo = (a - hi.astype(f32)).astype(bf16)
    return hi, lo


def _outproj_kernel(mixed_ref, x_ref, wo_ref, g_ref, wr_hi_ref, wr_lo_ref, br_ref,
                    h_ref, xn_ref, eid_ref, gate_ref):
    h = x_ref[...] + jnp.dot(mixed_ref[...], wo_ref[...], preferred_element_type=f32)
    h_ref[...] = h
    ms = jnp.mean(h * h, axis=-1, keepdims=True)
    xn = h * lax.rsqrt(ms + EPS) * g_ref[...]
    xn_ref[...] = xn

    x_hi, x_lo = _split_bf16(xn)
    logit = (jnp.dot(x_hi, wr_hi_ref[...], preferred_element_type=f32)
             + jnp.dot(x_hi, wr_lo_ref[...], preferred_element_type=f32)
             + jnp.dot(x_lo, wr_hi_ref[...], preferred_element_type=f32)) + br_ref[...]

    lane = lax.broadcasted_iota(jnp.int32, logit.shape, 1)
    lanef = lane.astype(f32)
    neg = -jnp.inf

    def first_argmax(v, m):
        return jnp.min(jnp.where(v == m, lanef, float(LANES)), axis=-1, keepdims=True)

    lg = jnp.where(lane < N_GROUPS, logit, neg)
    mg = jnp.max(lg, axis=-1, keepdims=True)
    p_grp = 1.0 / jnp.sum(jnp.exp(lg - mg), axis=-1, keepdims=True)
    grp = first_argmax(lg, mg).astype(jnp.int32)

    e_lane = lane - N_GROUPS
    in_grp = (e_lane >= 0) & (e_lane < N_EXPERTS) & ((e_lane // EXP_PER_GROUP) == grp)
    le = jnp.where(in_grp, logit, neg)
    m1 = jnp.max(le, axis=-1, keepdims=True)
    i1 = first_argmax(le, m1)
    le2 = jnp.where(lanef == i1, neg, le)
    m2 = jnp.max(le2, axis=-1, keepdims=True)
    i2 = first_argmax(le2, m2)
    e2 = jnp.exp(m2 - m1)
    g1 = p_grp / (1.0 + e2)
    g2 = p_grp * e2 / (1.0 + e2)

    eid = jnp.where(lane == 0, i1, jnp.where(lane == 1, i2, float(N_GROUPS))) - float(N_GROUPS)
    eid_ref[...] = eid.astype(jnp.int32)
    gate_ref[...] = jnp.where(lane == 0, g1, jnp.where(lane == 1, g2, 0.0))


def _outproj(mixed, x2, w_out_bf, g_ffn, wr_hi, wr_lo, b_r, tm):
    t, d = x2.shape
    row = lambda i: (i, 0)
    fixed = lambda i: (0, 0)
    return pl.pallas_call(
        _outproj_kernel,
        grid=(t // tm,),
        in_specs=[
            pl.BlockSpec((tm, d), row),
            pl.BlockSpec((tm, d), row),
            pl.BlockSpec((d, d), fixed),
            pl.BlockSpec((1, d), fixed),
            pl.BlockSpec((d, LANES), fixed),
            pl.BlockSpec((d, LANES), fixed),
            pl.BlockSpec((1, LANES), fixed),
        ],
        out_specs=[
            pl.BlockSpec((tm, d), row),
            pl.BlockSpec((tm, d), row),
            pl.BlockSpec((tm, LANES), row),
            pl.BlockSpec((tm, LANES), row),
        ],
        out_shape=[
            jax.ShapeDtypeStruct((t, d), f32),
            jax.ShapeDtypeStruct((t, d), f32),
            jax.ShapeDtypeStruct((t, LANES), jnp.int32),
            jax.ShapeDtypeStruct((t, LANES), f32),
        ],
        compiler_params=_cparams(("parallel",)),
        name="outproj",
    )(mixed, x2, w_out_bf, g_ffn, wr_hi, wr_lo, b_r)


def _row_copy(src_hbm, dst_vmem, sem, src_row, dst_row):
    return pltpu.make_async_copy(src_hbm.at[pl.ds(src_row, 1)], dst_vmem.at[pl.ds(dst_row, 1)], sem)


def _experts_kernel(bexp_ref, rtok_ref, nused_ref, xn_hbm, rg_ref, wg_ref, wu_ref, wd_ref, y_ref,
                    xbuf, sem):
    i = pl.program_id(0)

    @pl.when(i < nused_ref[0])
    def _():
        base = i * ROW_BLOCK

        def issue(r, c):
            _row_copy(xn_hbm, xbuf, sem.at[0], rtok_ref[base + r], r).start()
            return c

        lax.fori_loop(0, ROW_BLOCK, issue, 0)

        def drain(r, c):
            _row_copy(xn_hbm, xbuf, sem.at[0], 0, r).wait()
            return c

        lax.fori_loop(0, ROW_BLOCK, drain, 0)

        xb = xbuf[...].astype(bf16)
        g = jnp.dot(xb, wg_ref[...], preferred_element_type=f32)
        u = jnp.dot(xb, wu_ref[...], preferred_element_type=f32)
        hm = (g * jax.nn.sigmoid(g) * u).astype(bf16)
        y_ref[...] = jnp.dot(hm, wd_ref[...], preferred_element_type=f32) * rg_ref[...]

    @pl.when(i >= nused_ref[0])
    def _():
        y_ref[...] = jnp.zeros_like(y_ref)


def _experts(block_expert, row_tok, n_used, xn, row_gate, wg_bf, wu_bf, wd_bf):
    n_rows = row_tok.shape[0]
    d = xn.shape[1]
    f = wg_bf.shape[2]
    grid_spec = pltpu.PrefetchScalarGridSpec(
        num_scalar_prefetch=3,
        grid=(n_rows // ROW_BLOCK,),
        in_specs=[
            pl.BlockSpec(memory_space=pl.ANY),
            pl.BlockSpec((ROW_BLOCK, 1), lambda i, be, rt, nu: (i, 0)),
            pl.BlockSpec((None, d, f), lambda i, be, rt, nu: (be[i], 0, 0)),
            pl.BlockSpec((None, d, f), lambda i, be, rt, nu: (be[i], 0, 0)),
            pl.BlockSpec((None, f, d), lambda i, be, rt, nu: (be[i], 0, 0)),
        ],
        out_specs=pl.BlockSpec((ROW_BLOCK, d), lambda i, be, rt, nu: (i, 0)),
        scratch_shapes=[pltpu.VMEM((ROW_BLOCK, d), f32), pltpu.SemaphoreType.DMA((1,))],
    )
    return pl.pallas_call(
        _experts_kernel,
        grid_spec=grid_spec,
        out_shape=jax.ShapeDtypeStruct((n_rows, d), f32),
        compiler_params=_cparams(("arbitrary",)),
        name="experts",
    )(block_expert, row_tok, n_used, xn, row_gate, wg_bf, wu_bf, wd_bf)


def _combine_kernel(pos_ref, h_ref, y_hbm, g_ref, o_ref, ybuf, sem, *, tm):
    base = pl.program_id(0) * tm

    def issue(r, c):
        _row_copy(y_hbm, ybuf.at[0], sem.at[0], pos_ref[2 * (base + r)], r).start()
        _row_copy(y_hbm, ybuf.at[1], sem.at[0], pos_ref[2 * (base + r) + 1], r).start()
        return c

    lax.fori_loop(0, tm, issue, 0)

    def drain(r, c):
        _row_copy(y_hbm, ybuf.at[0], sem.at[0], 0, r).wait()
        _row_copy(y_hbm, ybuf.at[1], sem.at[0], 0, r).wait()
        return c

    lax.fori_loop(0, tm, drain, 0)

    hh = h_ref[...] + ybuf[0] + ybuf[1]
    ms = jnp.mean(hh * hh, axis=-1, keepdims=True)
    o_ref[...] = hh * lax.rsqrt(ms + EPS) * g_ref[...]


def _combine(pos, h, y, g_final, tm):
    t, d = h.shape
    grid_spec = pltpu.PrefetchScalarGridSpec(
        num_scalar_prefetch=1,
        grid=(t // tm,),
        in_specs=[
            pl.BlockSpec((tm, d), lambda i, p: (i, 0)),
            pl.BlockSpec(memory_space=pl.ANY),
            pl.BlockSpec((1, d), lambda i, p: (0, 0)),
        ],
        out_specs=pl.BlockSpec((tm, d), lambda i, p: (i, 0)),
        scratch_shapes=[pltpu.VMEM((2, tm, d), f32), pltpu.SemaphoreType.DMA((1,))],
    )
    return pl.pallas_call(
        functools.partial(_combine_kernel, tm=tm),
        grid_spec=grid_spec,
        out_shape=jax.ShapeDtypeStruct((t, d), f32),
        compiler_params=_cparams(("arbitrary",)),
        name="combine",
    )(pos, h, y, g_final)


def _dispatch_tables(eid, gate, n_tok):
    n_asg = n_tok * 2
    e_flat = eid.reshape(-1)
    gate_flat = gate.reshape(-1)
    tok_flat = jnp.repeat(jnp.arange(n_tok, dtype=jnp.int32), 2)
    order = jnp.argsort(e_flat)
    e_sorted = e_flat[order]
    counts = jnp.bincount(e_flat, length=N_EXPERTS)
    padded = (counts + ROW_BLOCK - 1) // ROW_BLOCK * ROW_BLOCK
    pad_end = jnp.cumsum(padded)
    pad_start = pad_end - padded
    cnt_start = jnp.cumsum(counts) - counts
    rank = jnp.arange(n_asg) - cnt_start[e_sorted]
    dest = (pad_start[e_sorted] + rank).astype(jnp.int32)
    n_rows = -(-(n_asg + N_EXPERTS * (ROW_BLOCK - 1)) // ROW_BLOCK) * ROW_BLOCK
    n_blocks = n_rows // ROW_BLOCK
    row_tok = jnp.zeros((n_rows,), jnp.int32).at[dest].set(tok_flat[order])
    row_gate = jnp.zeros((n_rows,), f32).at[dest].set(gate_flat[order])
    block_expert = jnp.minimum(
        jnp.searchsorted(pad_end, jnp.arange(n_blocks) * ROW_BLOCK, side='right'),
        N_EXPERTS - 1).astype(jnp.int32)
    pos = jnp.zeros((n_asg,), jnp.int32).at[order].set(dest)
    n_used = (pad_end[-1] // ROW_BLOCK).astype(jnp.int32).reshape(1)
    return block_expert, row_tok, row_gate.reshape(n_rows, 1), pos, n_used


def _pick(n, prefs):
    for p in prefs:
        if n % p == 0:
            return p
    return n


def kernel(x, g_mix_norm, w_in, g_kv, w_uv, g_ret, w_branch, w_out, g_ffn_norm, w_router_group,
           b_router_group, w_router_expert, b_router_expert, w_expert_gate, w_expert_up,
           w_expert_down, g_final):
    b, s, d = x.shape
    t = b * s
    depth = w_in.shape[0]
    n_sel = min(TOPK_MAX, s // 4)
    assert s % RET_CHUNK == 0 and s % Q_TILE == 0

    cos_t, sin_t, d_in, d_q, d_k, d_c = _retention_tables(s)
    h2 = x.reshape(t, d)
    for l in range(depth):
        wl = w_in[l]
        sp = np.cumsum([0, H_A * D_LATENT, D_LATENT, H_IDX * D_IDX, D_IDX, H_IDX,
                        H_R * DK_R, H_R * DK_R, H_R * DV_R, H_R * DV_R, N_BRANCH * d])
        seg = [wl[:, sp[k]:sp[k + 1]] for k in range(10)]
        kw_pad = jnp.zeros((d, LANES - D_IDX - H_IDX), wl.dtype)
        w_p = jnp.concatenate([seg[0], seg[9], seg[5], seg[6], seg[7], seg[8], seg[2], seg[1],
                               seg[3], seg[4], kw_pad], axis=1).astype(bf16)
        assert w_p.shape[1] == D_IN_P

        proj = _proj(h2, g_mix_norm[l].reshape(1, d), w_p, _pick(t, (1024, 512, 256)), 768)
        proj3 = proj.reshape(b, s, D_IN_P)

        o_a = _attn(proj3, g_kv[l].reshape(1, D_LATENT), w_uv[l].astype(bf16), n_sel)
        o_b = _ret(proj3, cos_t, sin_t, d_in, d_q, d_k, d_c, g_ret[l].reshape(1, H_R * DV_R))

        mixed = _mix(o_a.reshape(t, D_BRANCH), o_b.reshape(t, D_BRANCH), w_branch[l].astype(bf16),
                     proj, _pick(t, (512, 256)), 512)

        w_r = jnp.concatenate([w_router_group[l], w_router_expert[l],
                               jnp.zeros((d, LANES - N_GROUPS - N_EXPERTS), f32)], axis=1)
        b_r = jnp.concatenate([b_router_group[l], b_router_expert[l],
                               jnp.zeros((LANES - N_GROUPS - N_EXPERTS,), f32)]).reshape(1, LANES)
        wr_hi = w_r.astype(bf16)
        wr_lo = (w_r - wr_hi.astype(f32)).astype(bf16)
        h2, xn, eid, gate = _outproj(mixed, h2, w_out[l].astype(bf16), g_ffn_norm[l].reshape(1, d),
                                     wr_hi, wr_lo, b_r, _pick(t, (256,)))

        block_expert, row_tok, row_gate, pos, n_used = _dispatch_tables(eid[:, :2], gate[:, :2], t)
        y = _experts(block_expert, row_tok, n_used, xn, row_gate, w_expert_gate[l].astype(bf16),
                     w_expert_up[l].astype(bf16), w_expert_down[l].astype(bf16))
        last = l == depth - 1
        g_out = g_final.reshape(1, d) if last else jnp.ones((1, d), f32)
        assert last
        h2 = _combine(pos, h2, y, g_out, _pick(t, (256,)))
    return h2.reshape(b, s, d)
```

```python
import functools

import jax
import jax.numpy as jnp
import numpy as np
from jax import lax
from jax.experimental import pallas as pl
from jax.experimental.pallas import tpu as pltpu

EPS = 1e-6
CHUNK = 64
H_A = 8
D_LATENT = 128
DH_A = 128
H_IDX = 8
D_IDX = 64
TOPK_MAX = 256
H_R = 8
DK_R = 128
DV_R = 128
ROPE_BASE = 10000.0
D_BRANCH = 1024
N_BRANCH = 2
N_GROUPS = 4
EXP_PER_GROUP = 8
N_EXPERTS = N_GROUPS * EXP_PER_GROUP
D_EXPERT = 1024

LANES = 128
KEY_TILE = 256
Q_TILE = 128
RET_CHUNK = 256
ROW_BLOCK = 256
ROUTE_HALVES = 2
ROUTE_CHAINS = 2 * ROUTE_HALVES
VMEM_LIMIT = 56 * 1024 * 1024

C_QLAT = 0
C_GBR = 1024
C_QR = 5120
C_KR = 6144
C_VR = 7168
C_GR = 8192
C_QIDX = 9216
C_CKV = 9728
C_KW = 9856
D_IN_P = 9984

INT_MIN = np.int32(-2 ** 31)
NEG_BIG = -1e30

bf16 = jnp.bfloat16
f32 = jnp.float32


def _cparams(sem):
    return pltpu.CompilerParams(dimension_semantics=sem, vmem_limit_bytes=VMEM_LIMIT)


def _proj_kernel(x_ref, g_ref, w_ref, o_ref, xn_ref):
    @pl.when(pl.program_id(1) == 0)
    def _():
        x = x_ref[...]
        ms = jnp.mean(x * x, axis=-1, keepdims=True)
        xn_ref[...] = (x * lax.rsqrt(ms + EPS) * g_ref[...]).astype(bf16)

    o_ref[...] = jnp.dot(xn_ref[...], w_ref[...], preferred_element_type=f32)


def _proj(x2, g, w_p, tm, tn):
    t, d = x2.shape
    n = w_p.shape[1]
    return pl.pallas_call(
        _proj_kernel,
        grid=(t // tm, n // tn),
        in_specs=[
            pl.BlockSpec((tm, d), lambda i, j: (i, 0)),
            pl.BlockSpec((1, d), lambda i, j: (0, 0)),
            pl.BlockSpec((d, tn), lambda i, j: (0, j)),
        ],
        out_specs=pl.BlockSpec((tm, tn), lambda i, j: (i, j)),
        out_shape=jax.ShapeDtypeStruct((t, n), f32),
        scratch_shapes=[pltpu.VMEM((tm, d), bf16)],
        compiler_params=_cparams(("parallel", "arbitrary")),
        name="proj",
    )(x2, g, w_p)


def _float_key(s):
    bits = pltpu.bitcast(s, jnp.int32)
    key = bits ^ ((bits >> 31) & jnp.int32(0x7FFFFFFF))
    return jnp.where(s == 0.0, jnp.int32(0), key)


def _attn_kernel(qlat_ref, qidx_ref, kwq_ref, ckv_ref, kwk_ref, gkv_ref, wuv_ref, o_ref,
                 kv_s, kvT_s, kidx_s, key_s, bias_s, qT_s, acc_s, *, n_sel, n_kt):
    i = pl.program_id(1)
    idx_scale = (H_IDX ** -0.5) * (D_IDX ** -0.5)
    attn_scale = D_LATENT ** -0.5
    hq = H_A * Q_TILE

    @pl.when(i == 0)
    def _():
        g = gkv_ref[...]
        for t in range(n_kt):
            c = ckv_ref[t * KEY_TILE:(t + 1) * KEY_TILE, :]
            ms = jnp.mean(c * c, axis=-1, keepdims=True)
            kv = c * lax.rsqrt(ms + EPS) * g
            kv_s[t] = kv.astype(bf16)
            kvT_s[t] = kv.T.astype(bf16)
            kidx_s[t] = kwk_ref[t * KEY_TILE:(t + 1) * KEY_TILE, :D_IDX].astype(bf16)

    nk = ((i + 1) * Q_TILE + KEY_TILE - 1) // KEY_TILE
    lane = lax.broadcasted_iota(jnp.int32, (1, Q_TILE), 1)
    sub = lax.broadcasted_iota(jnp.int32, (KEY_TILE, 1), 0)
    q_chunk = (i * Q_TILE + lane) // CHUNK

    wT = kwq_ref[...].T
    qidx = qidx_ref[...].astype(bf16)
    for h in range(H_A):
        qT_s[:, h * Q_TILE:(h + 1) * Q_TILE] = qlat_ref[:, h * D_LATENT:(h + 1) * D_LATENT].T.astype(bf16)

    def score_tile(t, carry):
        ks = kidx_s[t]
        acc = jnp.zeros((KEY_TILE, Q_TILE), f32)
        for h in range(H_IDX):
            d = lax.dot_general(ks, qidx[:, h * D_IDX:(h + 1) * D_IDX],
                                (((1,), (1,)), ((), ())), preferred_element_type=f32)
            acc = acc + wT[D_IDX + h:D_IDX + h + 1, :] * jnp.maximum(d, 0.0)
        score = acc * idx_scale
        k_chunk = (t * KEY_TILE + sub) // CHUNK
        key_s[t] = jnp.where(k_chunk <= q_chunk, _float_key(score), INT_MIN)
        return carry

    lax.fori_loop(0, nk, score_tile, 0)

    def count(pred):
        def body(t, c):
            m = pred(key_s[t], t).astype(jnp.int32)
            return c + jnp.sum(m.reshape(KEY_TILE // 8, 8, Q_TILE), axis=0)
        c8 = lax.fori_loop(0, nk, body, jnp.zeros((8, Q_TILE), jnp.int32))
        return jnp.sum(c8, axis=0, keepdims=True)

    thr0 = jnp.where(count(lambda k, t: k >= 0) >= n_sel, jnp.int32(0), INT_MIN)
    thr0 = jnp.broadcast_to(thr0, (1, Q_TILE)).astype(jnp.int32)

    def bit_step(j, thr):
        cand = thr | (jnp.int32(1) << (jnp.int32(30) - j))
        return jnp.where(count(lambda k, t: k >= cand) >= n_sel, cand, thr)

    thr = lax.fori_loop(0, 31, bit_step, thr0)

    c_gt = count(lambda k, t: k > thr)
    c_ge = count(lambda k, t: k >= thr)
    need = n_sel - c_gt
    has_tie = jnp.max(jnp.where((c_ge > n_sel) & (thr > INT_MIN), 1, 0)) > 0

    def tie_limit():
        def step(j, m):
            cand = m | (jnp.int32(1) << (jnp.int32(14) - j))
            c = count(lambda k, t: (k == thr) & ((t * KEY_TILE + sub) < cand))
            return jnp.where(c < need, cand, m)
        return lax.fori_loop(0, 15, step, jnp.zeros((1, Q_TILE), jnp.int32))

    m_lim = lax.cond(has_tie, tie_limit, lambda: jnp.full((1, Q_TILE), 2 ** 30, jnp.int32))

    def bias_tile(t, carry):
        k = key_s[t]
        sel = (k > thr) | ((k == thr) & ((t * KEY_TILE + sub) <= m_lim))
        sel = sel & (k > INT_MIN)
        bias_s[t] = jnp.where(sel, 0.0, NEG_BIG).astype(f32)
        return carry

    lax.fori_loop(0, nk, bias_tile, 0)

    acc_s[...] = jnp.zeros_like(acc_s)

    def att_tile(t, carry):
        m_run, l_run = carry
        logit = jnp.dot(kv_s[t], qT_s[...], preferred_element_type=f32) * attn_scale
        logit = logit + jnp.concatenate([bias_s[t]] * H_A, axis=1)
        m_new = jnp.maximum(m_run, jnp.max(logit, axis=0, keepdims=True))
        alpha = jnp.exp(m_run - m_new)
        p = jnp.exp(logit - m_new)
        l_new = alpha * l_run + jnp.sum(p, axis=0, keepdims=True)
        acc_s[...] = alpha * acc_s[...] + jnp.dot(kvT_s[t], p.astype(bf16), preferred_element_type=f32)
        return m_new, l_new

    init = (jnp.full((1, hq), NEG_BIG, f32), jnp.zeros((1, hq), f32))
    _, l_fin = lax.fori_loop(0, nk, att_tile, init)
    inv_l = 1.0 / l_fin
    for h in range(H_A):
        sl = slice(h * Q_TILE, (h + 1) * Q_TILE)
        o_lat = (acc_s[:, sl] * inv_l[:, sl]).T
        o_ref[:, h * DH_A:(h + 1) * DH_A] = jnp.dot(
            o_lat.astype(bf16), wuv_ref[h], preferred_element_type=f32)


def _attn(proj3, g_kv, w_uv_bf, n_sel):
    b, s, _ = proj3.shape
    n_kt = s // KEY_TILE
    kern = functools.partial(_attn_kernel, n_sel=n_sel, n_kt=n_kt)
    return pl.pallas_call(
        kern,
        grid=(b, s // Q_TILE),
        in_specs=[
            pl.BlockSpec((None, Q_TILE, H_A * D_LATENT), lambda bi, i: (bi, i, C_QLAT // 1024)),
            pl.BlockSpec((None, Q_TILE, H_IDX * D_IDX), lambda bi, i: (bi, i, C_QIDX // 512)),
            pl.BlockSpec((None, Q_TILE, LANES), lambda bi, i: (bi, i, C_KW // LANES)),
            pl.BlockSpec((None, s, LANES), lambda bi, i: (bi, 0, C_CKV // LANES)),
            pl.BlockSpec((None, s, LANES), lambda bi, i: (bi, 0, C_KW // LANES)),
            pl.BlockSpec((1, D_LATENT), lambda bi, i: (0, 0)),
            pl.BlockSpec((H_A, D_LATENT, DH_A), lambda bi, i: (0, 0, 0)),
        ],
        out_specs=pl.BlockSpec((None, Q_TILE, D_BRANCH), lambda bi, i: (bi, i, 0)),
        out_shape=jax.ShapeDtypeStruct((b, s, D_BRANCH), f32),
        scratch_shapes=[
            pltpu.VMEM((n_kt, KEY_TILE, D_LATENT), bf16),
            pltpu.VMEM((n_kt, D_LATENT, KEY_TILE), bf16),
            pltpu.VMEM((n_kt, KEY_TILE, D_IDX), bf16),
            pltpu.VMEM((n_kt, KEY_TILE, Q_TILE), jnp.int32),
            pltpu.VMEM((n_kt, KEY_TILE, Q_TILE), f32),
            pltpu.VMEM((D_LATENT, H_A * Q_TILE), bf16),
            pltpu.VMEM((D_LATENT, H_A * Q_TILE), f32),
        ],
        compiler_params=_cparams(("parallel", "arbitrary")),
        name="attn",
    )(proj3, proj3, proj3, proj3, proj3, g_kv, w_uv_bf)


def _ret_kernel(q_ref, k_ref, v_ref, gr_ref, cos_ref, sin_ref, din_ref, dq_ref, dk_ref, dc_ref,
                gret_ref, o_ref, state_s):
    @pl.when(pl.program_id(1) == 0)
    def _():
        state_s[...] = jnp.zeros_like(state_s)

    cos = cos_ref[...]
    sin = sin_ref[...]

    def rot(x):
        return x * cos + pltpu.roll(x, DK_R // 2, axis=1) * sin

    for h in range(H_R):
        sl = slice(h * DK_R, (h + 1) * DK_R)
        q = rot(q_ref[:, sl]).astype(bf16)
        kf = rot(k_ref[:, sl]) * (DK_R ** -0.5)
        k = kf.astype(bf16)
        v = v_ref[:, sl].astype(bf16)
        inner = lax.dot_general(q, k, (((1,), (1,)), ((), ())), preferred_element_type=f32) * din_ref[h]
        o = jnp.dot(inner.astype(bf16), v, preferred_element_type=f32)
        st = state_s[h]
        o = o + jnp.dot(q, st.astype(bf16), preferred_element_type=f32) * dq_ref[h]
        kd = (kf * dk_ref[h]).astype(bf16)
        state_s[h] = st * dc_ref[h] + jnp.dot(kd.T, v, preferred_element_type=f32)
        mu = jnp.mean(o, axis=-1, keepdims=True)
        var = jnp.mean(jnp.square(o - mu), axis=-1, keepdims=True)
        y = (o - mu) * lax.rsqrt(var + EPS) * gret_ref[:, sl]
        gate = gr_ref[:, sl]
        o_ref[:, sl] = gate * jax.nn.sigmoid(gate) * y


def _ret(proj3, cos_t, sin_t, d_in, d_q, d_k, d_c, g_ret):
    b, s, _ = proj3.shape
    c = RET_CHUNK
    w = H_R * DK_R

    def col(off):
        return pl.BlockSpec((None, c, w), lambda bi, ci: (bi, ci, off // w))

    return pl.pallas_call(
        _ret_kernel,
        grid=(b, s // c),
        in_specs=[
            col(C_QR), col(C_KR), col(C_VR), col(C_GR),
            pl.BlockSpec((c, DK_R), lambda bi, ci: (ci, 0)),
            pl.BlockSpec((c, DK_R), lambda bi, ci: (ci, 0)),
            pl.BlockSpec((H_R, c, c), lambda bi, ci: (0, 0, 0)),
            pl.BlockSpec((H_R, c, DK_R), lambda bi, ci: (0, 0, 0)),
            pl.BlockSpec((H_R, c, DK_R), lambda bi, ci: (0, 0, 0)),
            pl.BlockSpec((H_R, 1, DK_R), lambda bi, ci: (0, 0, 0)),
            pl.BlockSpec((1, w), lambda bi, ci: (0, 0)),
        ],
        out_specs=pl.BlockSpec((None, c, w), lambda bi, ci: (bi, ci, 0)),
        out_shape=jax.ShapeDtypeStruct((b, s, w), f32),
        scratch_shapes=[pltpu.VMEM((H_R, DK_R, DV_R), f32)],
        compiler_params=_cparams(("parallel", "arbitrary")),
        name="ret",
    )(proj3, proj3, proj3, proj3, cos_t, sin_t, d_in, d_q, d_k, d_c, g_ret)


def _retention_tables(s):
    c = RET_CHUNK
    half = DK_R // 2
    freq = ROPE_BASE ** (-jnp.arange(half, dtype=f32) / half)
    ang = jnp.arange(s, dtype=f32)[:, None] * freq[None, :]
    cos = jnp.cos(ang)
    sin = jnp.sin(ang)
    cos_t = jnp.concatenate([cos, cos], axis=-1)
    sin_t = jnp.concatenate([-sin, sin], axis=-1)
    log_gamma = jnp.log1p(-jnp.exp2(-5.0 - jnp.arange(H_R, dtype=f32)))
    n = jnp.arange(c, dtype=f32)
    diff = n[:, None] - n[None, :]
    d_in = jnp.where(diff >= 0, jnp.exp(log_gamma[:, None, None] * jnp.maximum(diff, 0.0)), 0.0)
    d_q = jnp.broadcast_to(jnp.exp(log_gamma[:, None] * (n + 1.0))[:, :, None], (H_R, c, DK_R))
    d_k = jnp.broadcast_to(jnp.exp(log_gamma[:, None] * (c - 1.0 - n))[:, :, None], (H_R, c, DK_R))
    d_c = jnp.broadcast_to(jnp.exp(log_gamma * c)[:, None, None], (H_R, 1, DK_R))
    return cos_t, sin_t, d_in, d_q, d_k, d_c


def _mix_kernel(oa_ref, ob_ref, wb_ref, ga_ref, gb_ref, o_ref):
    a = jnp.dot(oa_ref[...].astype(bf16), wb_ref[0], preferred_element_type=f32)
    b = jnp.dot(ob_ref[...].astype(bf16), wb_ref[1], preferred_element_type=f32)
    o_ref[...] = (jax.nn.sigmoid(ga_ref[...]) * a + jax.nn.sigmoid(gb_ref[...]) * b).astype(o_ref.dtype)


def _mix(o_a, o_b, w_branch_bf, proj, tm, tn):
    t = o_a.shape[0]
    d = w_branch_bf.shape[2]
    return pl.pallas_call(
        _mix_kernel,
        grid=(t // tm, d // tn),
        in_specs=[
            pl.BlockSpec((tm, D_BRANCH), lambda i, j: (i, 0)),
            pl.BlockSpec((tm, D_BRANCH), lambda i, j: (i, 0)),
            pl.BlockSpec((N_BRANCH, D_BRANCH, tn), lambda i, j: (0, 0, j)),
            pl.BlockSpec((tm, tn), lambda i, j: (i, C_GBR // tn + j)),
            pl.BlockSpec((tm, tn), lambda i, j: (i, (C_GBR + d) // tn + j)),
        ],
        out_specs=pl.BlockSpec((tm, tn), lambda i, j: (i, j)),
        out_shape=jax.ShapeDtypeStruct((t, d), bf16),
        compiler_params=_cparams(("parallel", "parallel")),
        name="mix",
    )(o_a, o_b, w_branch_bf, proj, proj)


def _split_bf16(a):
    hi = a.astype(bf16)
    lo = (a - hi.astype(f32)).astype(bf16)
    return hi, lo


def _outproj_kernel(mixed_ref, x_ref, wo_ref, g_ref, wr_hi_ref, wr_lo_ref, br_ref,
                    h_ref, xn_ref, eid_ref, gate_ref, cnt_ref):
    h = x_ref[...] + jnp.dot(mixed_ref[...], wo_ref[...], preferred_element_type=f32)
    h_ref[...] = h
    ms = jnp.mean(h * h, axis=-1, keepdims=True)
    xn = h * lax.rsqrt(ms + EPS) * g_ref[...]
    xn_ref[...] = xn

    x_hi, x_lo = _split_bf16(xn)
    logit = (jnp.dot(x_hi, wr_hi_ref[...], preferred_element_type=f32)
             + jnp.dot(x_hi, wr_lo_ref[...], preferred_element_type=f32)
             + jnp.dot(x_lo, wr_hi_ref[...], preferred_element_type=f32)) + br_ref[...]

    lane = lax.broadcasted_iota(jnp.int32, logit.shape, 1)
    lanef = lane.astype(f32)
    neg = -jnp.inf

    def first_argmax(v, m):
        return jnp.min(jnp.where(v == m, lanef, float(LANES)), axis=-1, keepdims=True)

    lg = jnp.where(lane < N_GROUPS, logit, neg)
    mg = jnp.max(lg, axis=-1, keepdims=True)
    p_grp = 1.0 / jnp.sum(jnp.exp(lg - mg), axis=-1, keepdims=True)
    grp = first_argmax(lg, mg).astype(jnp.int32)

    e_lane = lane - N_GROUPS
    in_grp = (e_lane >= 0) & (e_lane < N_EXPERTS) & ((e_lane // EXP_PER_GROUP) == grp)
    le = jnp.where(in_grp, logit, neg)
    m1 = jnp.max(le, axis=-1, keepdims=True)
    i1 = first_argmax(le, m1)
    le2 = jnp.where(lanef == i1, neg, le)
    m2 = jnp.max(le2, axis=-1, keepdims=True)
    i2 = first_argmax(le2, m2)
    e2 = jnp.exp(m2 - m1)
    g1 = p_grp / (1.0 + e2)
    g2 = p_grp * e2 / (1.0 + e2)

    eid = jnp.where(lane == 0, i1, jnp.where(lane == 1, i2, float(N_GROUPS))) - float(N_GROUPS)
    eid_ref[...] = eid.astype(jnp.int32).T[:8, :]
    gate_ref[...] = jnp.where(lane == 0, g1, jnp.where(lane == 1, g2, 0.0))

    @pl.when(pl.program_id(0) == 0)
    def _():
        cnt_ref[...] = jnp.zeros_like(cnt_ref)

    half = pl.program_id(0) // (pl.num_programs(0) // ROUTE_HALVES)
    sub8 = lax.broadcasted_iota(jnp.int32, (8, LANES), 0)
    for s, idx in enumerate((i1, i2)):
        c = jnp.sum((lanef == idx).astype(jnp.int32), axis=0, keepdims=True)
        cnt_ref[...] += jnp.where(sub8 == s * ROUTE_HALVES + half, c, 0)


def _outproj(mixed, x2, w_out_bf, g_ffn, wr_hi, wr_lo, b_r, tm):
    t, d = x2.shape
    row = lambda i: (i, 0)
    fixed = lambda i: (0, 0)
    return pl.pallas_call(
        _outproj_kernel,
        grid=(t // tm,),
        in_specs=[
            pl.BlockSpec((tm, d), row),
            pl.BlockSpec((tm, d), row),
            pl.BlockSpec((d, d), fixed),
            pl.BlockSpec((1, d), fixed),
            pl.BlockSpec((d, LANES), fixed),
            pl.BlockSpec((d, LANES), fixed),
            pl.BlockSpec((1, LANES), fixed),
        ],
        out_specs=[
            pl.BlockSpec((tm, d), row),
            pl.BlockSpec((tm, d), row),
            pl.BlockSpec((8, tm), lambda i: (0, i)),
            pl.BlockSpec((tm, LANES), row),
            pl.BlockSpec((8, LANES), fixed),
        ],
        out_shape=[
            jax.ShapeDtypeStruct((t, d), f32),
            jax.ShapeDtypeStruct((t, d), f32),
            jax.ShapeDtypeStruct((8, t), jnp.int32),
            jax.ShapeDtypeStruct((t, LANES), f32),
            jax.ShapeDtypeStruct((8, LANES), jnp.int32),
        ],
        compiler_params=_cparams(("arbitrary",)),
        name="outproj",
    )(mixed, x2, w_out_bf, g_ffn, wr_hi, wr_lo, b_r)


ISSUE_UNROLL = 8


def _experts_kernel(bexp_ref, rpack_ref, nused_ref, xn_hbm, wg_ref, wu_ref, wd_ref, yt_hbm,
                    xbuf, ybuf, gsem, ssem, *, n_tok):
    j = pl.program_id(0)
    n_used = nused_ref[0]
    slot = j % 2
    tok_bits = (n_tok - 1).bit_length()

    def rows_of(blk, fn):
        base = blk * ROW_BLOCK

        def body(k, c):
            r0 = pl.multiple_of(k * ISSUE_UNROLL, ISSUE_UNROLL)
            for u in range(ISSUE_UNROLL):
                fn(r0 + u, rpack_ref[base + r0 + u])
            return c

        lax.fori_loop(0, ROW_BLOCK // ISSUE_UNROLL, body, 0)

    def start_gathers(blk, sl):
        def one(r, packed):
            tok = packed & ((1 << tok_bits) - 1)
            pltpu.make_async_copy(xn_hbm.at[pl.ds(tok, 1)], xbuf.at[sl, pl.ds(r, 1)], gsem.at[sl]).start()
        rows_of(blk, one)

    def start_scatters(blk, sl):
        def one(r, packed):
            row = lax.shift_right_logical(packed, tok_bits)
            pltpu.make_async_copy(ybuf.at[sl, pl.ds(r, 1)], yt_hbm.at[pl.ds(row, 1)], ssem.at[sl]).start()
        rows_of(blk, one)

    def wait_gathers(sl):
        pltpu.make_async_copy(xn_hbm.at[pl.ds(0, ROW_BLOCK)], xbuf.at[sl], gsem.at[sl]).wait()

    def wait_scatters(sl):
        pltpu.make_async_copy(ybuf.at[sl], yt_hbm.at[pl.ds(0, ROW_BLOCK)], ssem.at[sl]).wait()

    @pl.when(j == 0)
    def _():
        start_gathers(0, 0)
        ybuf[...] = jnp.zeros_like(ybuf)
        for sl in range(2):
            spare = yt_hbm.at[pl.ds(2 * n_tok + sl * ROW_BLOCK, ROW_BLOCK)]
            pltpu.make_async_copy(ybuf.at[sl], spare, ssem.at[sl]).start()
        for sl in range(2):
            wait_scatters(sl)

    @pl.when(j < n_used)
    def _():
        wait_gathers(slot)

        @pl.when(j + 1 < n_used)
        def _():
            start_gathers(j + 1, 1 - slot)

        @pl.when(j >= 2)
        def _():
            wait_scatters(slot)

        xb = xbuf[slot].astype(bf16)
        g = jnp.dot(xb, wg_ref[...], preferred_element_type=f32)
        u = jnp.dot(xb, wu_ref[...], preferred_element_type=f32)
        hm = (g * jax.nn.sigmoid(g) * u).astype(bf16)
        ybuf[slot] = jnp.dot(hm, wd_ref[...], preferred_element_type=f32)
        start_scatters(j, slot)

        @pl.when(j == n_used - 1)
        def _():
            wait_scatters(slot)

            @pl.when(j >= 1)
            def _():
                wait_scatters(1 - slot)


def _experts(block_expert, row_pack, n_used, xn, wg_bf, wu_bf, wd_bf):
    n_rows = row_pack.shape[0]
    n_tok, d = xn.shape
    f = wg_bf.shape[2]
    grid_spec = pltpu.PrefetchScalarGridSpec(
        num_scalar_prefetch=3,
        grid=(n_rows // ROW_BLOCK,),
        in_specs=[
            pl.BlockSpec(memory_space=pl.ANY),
            pl.BlockSpec((None, d, f), lambda i, be, rd, nu: (be[i], 0, 0)),
            pl.BlockSpec((None, d, f), lambda i, be, rd, nu: (be[i], 0, 0)),
            pl.BlockSpec((None, f, d), lambda i, be, rd, nu: (be[i], 0, 0)),
        ],
        out_specs=pl.BlockSpec(memory_space=pl.ANY),
        scratch_shapes=[pltpu.VMEM((2, ROW_BLOCK, d), f32), pltpu.VMEM((2, ROW_BLOCK, d), f32),
                        pltpu.SemaphoreType.DMA((2,)), pltpu.SemaphoreType.DMA((2,))],
    )
    return pl.pallas_call(
        functools.partial(_experts_kernel, n_tok=n_tok),
        grid_spec=grid_spec,
        out_shape=jax.ShapeDtypeStruct((2 * n_tok + 2 * ROW_BLOCK, d), f32),
        compiler_params=_cparams(("arbitrary",)),
        name="experts",
    )(block_expert, row_pack, n_used, xn, wg_bf, wu_bf, wd_bf)


def _combine_kernel(h_ref, y0_ref, y1_ref, gate_ref, g_ref, o_ref):
    gate = gate_ref[...]
    hh = h_ref[...] + gate[:, 0:1] * y0_ref[...] + gate[:, 1:2] * y1_ref[...]
    ms = jnp.mean(hh * hh, axis=-1, keepdims=True)
    o_ref[...] = hh * lax.rsqrt(ms + EPS) * g_ref[...]


def _combine(h, yt, gate, g_final, tm):
    t, d = h.shape
    nt = t // tm
    return pl.pallas_call(
        _combine_kernel,
        grid=(nt,),
        in_specs=[
            pl.BlockSpec((tm, d), lambda i: (i, 0)),
            pl.BlockSpec((tm, d), lambda i: (i, 0)),
            pl.BlockSpec((tm, d), lambda i: (nt + i, 0)),
            pl.BlockSpec((tm, LANES), lambda i: (i, 0)),
            pl.BlockSpec((1, d), lambda i: (0, 0)),
        ],
        out_specs=pl.BlockSpec((tm, d), lambda i: (i, 0)),
        out_shape=jax.ShapeDtypeStruct((t, d), f32),
        compiler_params=_cparams(("parallel",)),
        name="combine",
    )(h, yt, yt, gate, g_final)


def _route_kernel(eid_ref, cnt_ref, rpack_ref, bexp_ref, nused_ref, *cur_refs, n_tok, n_blocks):
    tok_bits = (n_tok - 1).bit_length()
    chunk = 2 * n_tok // ROUTE_CHAINS

    def per_expert(e, blk):
        start = blk * ROW_BLOCK
        run = start
        for c in range(ROUTE_CHAINS):
            cur_refs[c][e] = run
            run = run + cnt_ref[c * N_EXPERTS + e]
        nb = (run - start + ROW_BLOCK - 1) // ROW_BLOCK

        def set_block(k, carry):
            bexp_ref[blk + k] = e
            return carry

        lax.fori_loop(0, nb, set_block, 0)

        def set_pad(r, carry):
            rpack_ref[r] = (2 * n_tok + (r & (2 * ROW_BLOCK - 1))) << tok_bits
            return carry

        lax.fori_loop(run, start + nb * ROW_BLOCK, set_pad, 0)
        return blk + nb

    n_used = lax.fori_loop(0, N_EXPERTS, per_expert, 0)
    nused_ref[0] = n_used

    def tail_block(k, carry):
        bexp_ref[k] = N_EXPERTS - 1
        return carry

    lax.fori_loop(n_used, n_blocks, tail_block, 0)

    def tail_row(r, carry):
        rpack_ref[r] = (2 * n_tok + (r & (2 * ROW_BLOCK - 1))) << tok_bits
        return carry

    lax.fori_loop(n_used * ROW_BLOCK, n_blocks * ROW_BLOCK, tail_row, 0)

    def place(i, carry):
        for c in range(ROUTE_CHAINS):
            a = c * chunk + i
            e = eid_ref[a]
            p = cur_refs[c][e]
            cur_refs[c][e] = p + 1
            rpack_ref[p] = (a << tok_bits) | (a - (c * chunk // n_tok) * n_tok)
        return carry

    lax.fori_loop(0, chunk, place, 0)


def _route(eid_flat, counts, n_tok):
    n_asg = eid_flat.shape[0]
    n_rows = -(-(n_asg + N_EXPERTS * (ROW_BLOCK - 1)) // ROW_BLOCK) * ROW_BLOCK
    n_blocks = n_rows // ROW_BLOCK
    smem = pl.BlockSpec(memory_space=pltpu.SMEM)
    return pl.pallas_call(
        functools.partial(_route_kernel, n_tok=n_tok, n_blocks=n_blocks),
        in_specs=[smem, smem],
        out_specs=[smem, smem, smem],
        out_shape=[jax.ShapeDtypeStruct((n_rows,), jnp.int32),
                   jax.ShapeDtypeStruct((n_blocks,), jnp.int32),
                   jax.ShapeDtypeStruct((1,), jnp.int32)],
        scratch_shapes=[pltpu.SMEM((N_EXPERTS,), jnp.int32)] * ROUTE_CHAINS,
        name="route",
    )(eid_flat, counts)


def _pick(n, prefs):
    for p in prefs:
        if n % p == 0:
            return p
    return n


def kernel(x, g_mix_norm, w_in, g_kv, w_uv, g_ret, w_branch, w_out, g_ffn_norm, w_router_group,
           b_router_group, w_router_expert, b_router_expert, w_expert_gate, w_expert_up,
           w_expert_down, g_final):
    b, s, d = x.shape
    t = b * s
    depth = w_in.shape[0]
    n_sel = min(TOPK_MAX, s // 4)
    assert s % RET_CHUNK == 0 and s % Q_TILE == 0

    cos_t, sin_t, d_in, d_q, d_k, d_c = _retention_tables(s)
    h2 = x.reshape(t, d)
    for l in range(depth):
        wl = w_in[l]
        sp = np.cumsum([0, H_A * D_LATENT, D_LATENT, H_IDX * D_IDX, D_IDX, H_IDX,
                        H_R * DK_R, H_R * DK_R, H_R * DV_R, H_R * DV_R, N_BRANCH * d])
        seg = [wl[:, sp[k]:sp[k + 1]] for k in range(10)]
        kw_pad = jnp.zeros((d, LANES - D_IDX - H_IDX), wl.dtype)
        w_p = jnp.concatenate([seg[0], seg[9], seg[5], seg[6], seg[7], seg[8], seg[2], seg[1],
                               seg[3], seg[4], kw_pad], axis=1).astype(bf16)
        assert w_p.shape[1] == D_IN_P

        proj = _proj(h2, g_mix_norm[l].reshape(1, d), w_p, _pick(t, (1024, 512, 256)), 768)
        proj3 = proj.reshape(b, s, D_IN_P)

        o_a = _attn(proj3, g_kv[l].reshape(1, D_LATENT), w_uv[l].astype(bf16), n_sel)
        o_b = _ret(proj3, cos_t, sin_t, d_in, d_q, d_k, d_c, g_ret[l].reshape(1, H_R * DV_R))

        mixed = _mix(o_a.reshape(t, D_BRANCH), o_b.reshape(t, D_BRANCH), w_branch[l].astype(bf16),
                     proj, _pick(t, (512, 256)), 512)

        w_r = jnp.concatenate([w_router_group[l], w_router_expert[l],
                               jnp.zeros((d, LANES - N_GROUPS - N_EXPERTS), f32)], axis=1)
        b_r = jnp.concatenate([b_router_group[l], b_router_expert[l],
                               jnp.zeros((LANES - N_GROUPS - N_EXPERTS,), f32)]).reshape(1, LANES)
        wr_hi = w_r.astype(bf16)
        wr_lo = (w_r - wr_hi.astype(f32)).astype(bf16)
        h2, xn, eid_t, gate, cnt = _outproj(mixed, h2, w_out[l].astype(bf16),
                                            g_ffn_norm[l].reshape(1, d), wr_hi, wr_lo, b_r, _pick(t, (256,)))

        row_pack, block_expert, n_used = _route(
            eid_t[:2].reshape(-1), cnt[:ROUTE_CHAINS, N_GROUPS:N_GROUPS + N_EXPERTS].reshape(-1), t)
        yt = _experts(block_expert, row_pack, n_used, xn, w_expert_gate[l].astype(bf16),
                      w_expert_up[l].astype(bf16), w_expert_down[l].astype(bf16))
        assert depth == 1
        h2 = _combine(h2, yt, gate, g_final.reshape(1, d), _pick(t, (256,)))
    return h2.reshape(b, s, d)
```

```python
import functools

import jax
import jax.numpy as jnp
import numpy as np
from jax import lax
from jax.experimental import pallas as pl
from jax.experimental.pallas import tpu as pltpu

EPS = 1e-6
CHUNK = 64
H_A = 8
D_LATENT = 128
DH_A = 128
H_IDX = 8
D_IDX = 64
TOPK_MAX = 256
H_R = 8
DK_R = 128
DV_R = 128
ROPE_BASE = 10000.0
D_BRANCH = 1024
N_BRANCH = 2
N_GROUPS = 4
EXP_PER_GROUP = 8
N_EXPERTS = N_GROUPS * EXP_PER_GROUP
D_EXPERT = 1024

LANES = 128
KEY_TILE = 256
Q_TILE = 128
RET_CHUNK = 256
ROW_BLOCK = 256
ROUTE_HALVES = 2
ROUTE_CHAINS = 2 * ROUTE_HALVES
VMEM_LIMIT = 56 * 1024 * 1024

C_QLAT = 0
C_GBR = 1024
C_QR = 5120
C_KR = 6144
C_VR = 7168
C_GR = 8192
C_QIDX = 9216
C_CKV = 9728
C_KW = 9856
D_IN_P = 9984

INT_MIN = np.int32(-2 ** 31)
NEG_BIG = -1e30

bf16 = jnp.bfloat16
f32 = jnp.float32


def _cparams(sem):
    return pltpu.CompilerParams(dimension_semantics=sem, vmem_limit_bytes=VMEM_LIMIT)


def _proj_kernel(x_ref, g_ref, w_ref, o_ref, xn_ref):
    @pl.when(pl.program_id(1) == 0)
    def _():
        x = x_ref[...]
        ms = jnp.mean(x * x, axis=-1, keepdims=True)
        xn_ref[...] = (x * lax.rsqrt(ms + EPS) * g_ref[...]).astype(bf16)

    o_ref[...] = jnp.dot(xn_ref[...], w_ref[...], preferred_element_type=f32)


def _proj(x2, g, w_p, tm, tn):
    t, d = x2.shape
    n = w_p.shape[1]
    return pl.pallas_call(
        _proj_kernel,
        grid=(t // tm, n // tn),
        in_specs=[
            pl.BlockSpec((tm, d), lambda i, j: (i, 0)),
            pl.BlockSpec((1, d), lambda i, j: (0, 0)),
            pl.BlockSpec((d, tn), lambda i, j: (0, j)),
        ],
        out_specs=pl.BlockSpec((tm, tn), lambda i, j: (i, j)),
        out_shape=jax.ShapeDtypeStruct((t, n), f32),
        scratch_shapes=[pltpu.VMEM((tm, d), bf16)],
        compiler_params=_cparams(("parallel", "arbitrary")),
        name="proj",
    )(x2, g, w_p)


def _float_key(s):
    bits = pltpu.bitcast(s, jnp.int32)
    key = bits ^ ((bits >> 31) & jnp.int32(0x7FFFFFFF))
    return jnp.where(s == 0.0, jnp.int32(0), key)


def _attn_kernel(qlat_ref, qidx_ref, kwq_ref, ckv_ref, kwk_ref, gkv_ref, wuv_ref, o_ref,
                 kv_s, kvT_s, kidx_s, key_s, bias_s, qT_s, acc_s, *, n_sel, n_kt):
    i = pl.program_id(1)
    idx_scale = (H_IDX ** -0.5) * (D_IDX ** -0.5)
    attn_scale = D_LATENT ** -0.5
    hq = H_A * Q_TILE

    @pl.when(i == 0)
    def _():
        g = gkv_ref[...]
        for t in range(n_kt):
            c = ckv_ref[t * KEY_TILE:(t + 1) * KEY_TILE, :]
            ms = jnp.mean(c * c, axis=-1, keepdims=True)
            kv = c * lax.rsqrt(ms + EPS) * g
            kv_s[t] = kv.astype(bf16)
            kvT_s[t] = kv.T.astype(bf16)
            kidx_s[t] = kwk_ref[t * KEY_TILE:(t + 1) * KEY_TILE, :D_IDX].astype(bf16)

    nk = ((i + 1) * Q_TILE + KEY_TILE - 1) // KEY_TILE
    lane = lax.broadcasted_iota(jnp.int32, (1, Q_TILE), 1)
    sub = lax.broadcasted_iota(jnp.int32, (KEY_TILE, 1), 0)
    q_chunk = (i * Q_TILE + lane) // CHUNK

    wT = kwq_ref[...].T
    qidx = qidx_ref[...].astype(bf16)
    for h in range(H_A):
        qT_s[:, h * Q_TILE:(h + 1) * Q_TILE] = qlat_ref[:, h * D_LATENT:(h + 1) * D_LATENT].T.astype(bf16)

    def score_tile(t, carry):
        ks = kidx_s[t]
        acc = jnp.zeros((KEY_TILE, Q_TILE), f32)
        for h in range(H_IDX):
            d = lax.dot_general(ks, qidx[:, h * D_IDX:(h + 1) * D_IDX],
                                (((1,), (1,)), ((), ())), preferred_element_type=f32)
            acc = acc + wT[D_IDX + h:D_IDX + h + 1, :] * jnp.maximum(d, 0.0)
        score = acc * idx_scale
        k_chunk = (t * KEY_TILE + sub) // CHUNK
        key_s[t] = jnp.where(k_chunk <= q_chunk, _float_key(score), INT_MIN)
        return carry

    lax.fori_loop(0, nk, score_tile, 0)

    def count(pred):
        def body(t, c):
            m = pred(key_s[t], t).astype(jnp.int32)
            return c + jnp.sum(m.reshape(KEY_TILE // 8, 8, Q_TILE), axis=0)
        c8 = lax.fori_loop(0, nk, body, jnp.zeros((8, Q_TILE), jnp.int32))
        return jnp.sum(c8, axis=0, keepdims=True)

    thr0 = jnp.where(count(lambda k, t: k >= 0) >= n_sel, jnp.int32(0), INT_MIN)
    thr0 = jnp.broadcast_to(thr0, (1, Q_TILE)).astype(jnp.int32)

    def bit_step(j, thr):
        cand = thr | (jnp.int32(1) << (jnp.int32(30) - j))
        return jnp.where(count(lambda k, t: k >= cand) >= n_sel, cand, thr)

    thr = lax.fori_loop(0, 31, bit_step, thr0)

    c_gt = count(lambda k, t: k > thr)
    c_ge = count(lambda k, t: k >= thr)
    need = n_sel - c_gt
    has_tie = jnp.max(jnp.where((c_ge > n_sel) & (thr > INT_MIN), 1, 0)) > 0

    def tie_limit():
        def step(j, m):
            cand = m | (jnp.int32(1) << (jnp.int32(14) - j))
            c = count(lambda k, t: (k == thr) & ((t * KEY_TILE + sub) < cand))
            return jnp.where(c < need, cand, m)
        return lax.fori_loop(0, 15, step, jnp.zeros((1, Q_TILE), jnp.int32))

    m_lim = lax.cond(has_tie, tie_limit, lambda: jnp.full((1, Q_TILE), 2 ** 30, jnp.int32))

    def bias_tile(t, carry):
        k = key_s[t]
        sel = (k > thr) | ((k == thr) & ((t * KEY_TILE + sub) <= m_lim))
        sel = sel & (k > INT_MIN)
        bias_s[t] = jnp.where(sel, 0.0, NEG_BIG).astype(f32)
        return carry

    lax.fori_loop(0, nk, bias_tile, 0)

    acc_s[...] = jnp.zeros_like(acc_s)

    def att_tile(t, carry):
        m_run, l_run = carry
        logit = jnp.dot(kv_s[t], qT_s[...], preferred_element_type=f32) * attn_scale
        logit = logit + jnp.concatenate([bias_s[t]] * H_A, axis=1)
        m_new = jnp.maximum(m_run, jnp.max(logit, axis=0, keepdims=True))
        alpha = jnp.exp(m_run - m_new)
        p = jnp.exp(logit - m_new)
        l_new = alpha * l_run + jnp.sum(p, axis=0, keepdims=True)
        acc_s[...] = alpha * acc_s[...] + jnp.dot(kvT_s[t], p.astype(bf16), preferred_element_type=f32)
        return m_new, l_new

    init = (jnp.full((1, hq), NEG_BIG, f32), jnp.zeros((1, hq), f32))
    _, l_fin = lax.fori_loop(0, nk, att_tile, init)
    inv_l = 1.0 / l_fin
    for h in range(H_A):
        sl = slice(h * Q_TILE, (h + 1) * Q_TILE)
        o_lat = (acc_s[:, sl] * inv_l[:, sl]).T
        o_ref[:, h * DH_A:(h + 1) * DH_A] = jnp.dot(
            o_lat.astype(bf16), wuv_ref[h], preferred_element_type=f32)


def _attn(proj3, g_kv, w_uv_bf, n_sel):
    b, s, _ = proj3.shape
    n_kt = s // KEY_TILE
    kern = functools.partial(_attn_kernel, n_sel=n_sel, n_kt=n_kt)
    return pl.pallas_call(
        kern,
        grid=(b, s // Q_TILE),
        in_specs=[
            pl.BlockSpec((None, Q_TILE, H_A * D_LATENT), lambda bi, i: (bi, i, C_QLAT // 1024)),
            pl.BlockSpec((None, Q_TILE, H_IDX * D_IDX), lambda bi, i: (bi, i, C_QIDX // 512)),
            pl.BlockSpec((None, Q_TILE, LANES), lambda bi, i: (bi, i, C_KW // LANES)),
            pl.BlockSpec((None, s, LANES), lambda bi, i: (bi, 0, C_CKV // LANES)),
            pl.BlockSpec((None, s, LANES), lambda bi, i: (bi, 0, C_KW // LANES)),
            pl.BlockSpec((1, D_LATENT), lambda bi, i: (0, 0)),
            pl.BlockSpec((H_A, D_LATENT, DH_A), lambda bi, i: (0, 0, 0)),
        ],
        out_specs=pl.BlockSpec((None, Q_TILE, D_BRANCH), lambda bi, i: (bi, i, 0)),
        out_shape=jax.ShapeDtypeStruct((b, s, D_BRANCH), f32),
        scratch_shapes=[
            pltpu.VMEM((n_kt, KEY_TILE, D_LATENT), bf16),
            pltpu.VMEM((n_kt, D_LATENT, KEY_TILE), bf16),
            pltpu.VMEM((n_kt, KEY_TILE, D_IDX), bf16),
            pltpu.VMEM((n_kt, KEY_TILE, Q_TILE), jnp.int32),
            pltpu.VMEM((n_kt, KEY_TILE, Q_TILE), f32),
            pltpu.VMEM((D_LATENT, H_A * Q_TILE), bf16),
            pltpu.VMEM((D_LATENT, H_A * Q_TILE), f32),
        ],
        compiler_params=_cparams(("parallel", "arbitrary")),
        name="attn",
    )(proj3, proj3, proj3, proj3, proj3, g_kv, w_uv_bf)


def _ret_kernel(q_ref, k_ref, v_ref, gr_ref, cos_ref, sin_ref, din_ref, dq_ref, dk_ref, dc_ref,
                gret_ref, o_ref, state_s):
    @pl.when(pl.program_id(1) == 0)
    def _():
        state_s[...] = jnp.zeros_like(state_s)

    cos = cos_ref[...]
    sin = sin_ref[...]

    def rot(x):
        return x * cos + pltpu.roll(x, DK_R // 2, axis=1) * sin

    for h in range(H_R):
        sl = slice(h * DK_R, (h + 1) * DK_R)
        q = rot(q_ref[:, sl]).astype(bf16)
        kf = rot(k_ref[:, sl]) * (DK_R ** -0.5)
        k = kf.astype(bf16)
        v = v_ref[:, sl].astype(bf16)
        inner = lax.dot_general(q, k, (((1,), (1,)), ((), ())), preferred_element_type=f32) * din_ref[h]
        o = jnp.dot(inner.astype(bf16), v, preferred_element_type=f32)
        st = state_s[h]
        o = o + jnp.dot(q, st.astype(bf16), preferred_element_type=f32) * dq_ref[h]
        kd = (kf * dk_ref[h]).astype(bf16)
        state_s[h] = st * dc_ref[h] + jnp.dot(kd.T, v, preferred_element_type=f32)
        mu = jnp.mean(o, axis=-1, keepdims=True)
        var = jnp.mean(jnp.square(o - mu), axis=-1, keepdims=True)
        y = (o - mu) * lax.rsqrt(var + EPS) * gret_ref[:, sl]
        gate = gr_ref[:, sl]
        o_ref[:, sl] = gate * jax.nn.sigmoid(gate) * y


def _ret(proj3, cos_t, sin_t, d_in, d_q, d_k, d_c, g_ret):
    b, s, _ = proj3.shape
    c = RET_CHUNK
    w = H_R * DK_R

    def col(off):
        return pl.BlockSpec((None, c, w), lambda bi, ci: (bi, ci, off // w))

    return pl.pallas_call(
        _ret_kernel,
        grid=(b, s // c),
        in_specs=[
            col(C_QR), col(C_KR), col(C_VR), col(C_GR),
            pl.BlockSpec((c, DK_R), lambda bi, ci: (ci, 0)),
            pl.BlockSpec((c, DK_R), lambda bi, ci: (ci, 0)),
            pl.BlockSpec((H_R, c, c), lambda bi, ci: (0, 0, 0)),
            pl.BlockSpec((H_R, c, DK_R), lambda bi, ci: (0, 0, 0)),
            pl.BlockSpec((H_R, c, DK_R), lambda bi, ci: (0, 0, 0)),
            pl.BlockSpec((H_R, 1, DK_R), lambda bi, ci: (0, 0, 0)),
            pl.BlockSpec((1, w), lambda bi, ci: (0, 0)),
        ],
        out_specs=pl.BlockSpec((None, c, w), lambda bi, ci: (bi, ci, 0)),
        out_shape=jax.ShapeDtypeStruct((b, s, w), f32),
        scratch_shapes=[pltpu.VMEM((H_R, DK_R, DV_R), f32)],
        compiler_params=_cparams(("parallel", "arbitrary")),
        name="ret",
    )(proj3, proj3, proj3, proj3, cos_t, sin_t, d_in, d_q, d_k, d_c, g_ret)


def _retention_tables(s):
    c = RET_CHUNK
    half = DK_R // 2
    freq = ROPE_BASE ** (-jnp.arange(half, dtype=f32) / half)
    ang = jnp.arange(s, dtype=f32)[:, None] * freq[None, :]
    cos = jnp.cos(ang)
    sin = jnp.sin(ang)
    cos_t = jnp.concatenate([cos, cos], axis=-1)
    sin_t = jnp.concatenate([-sin, sin], axis=-1)
    log_gamma = jnp.log1p(-jnp.exp2(-5.0 - jnp.arange(H_R, dtype=f32)))
    n = jnp.arange(c, dtype=f32)
    diff = n[:, None] - n[None, :]
    d_in = jnp.where(diff >= 0, jnp.exp(log_gamma[:, None, None] * jnp.maximum(diff, 0.0)), 0.0)
    d_q = jnp.broadcast_to(jnp.exp(log_gamma[:, None] * (n + 1.0))[:, :, None], (H_R, c, DK_R))
    d_k = jnp.broadcast_to(jnp.exp(log_gamma[:, None] * (c - 1.0 - n))[:, :, None], (H_R, c, DK_R))
    d_c = jnp.broadcast_to(jnp.exp(log_gamma * c)[:, None, None], (H_R, 1, DK_R))
    return cos_t, sin_t, d_in, d_q, d_k, d_c


def _mix_kernel(oa_ref, ob_ref, wb_ref, ga_ref, gb_ref, o_ref):
    a = jnp.dot(oa_ref[...].astype(bf16), wb_ref[0], preferred_element_type=f32)
    b = jnp.dot(ob_ref[...].astype(bf16), wb_ref[1], preferred_element_type=f32)
    o_ref[...] = (jax.nn.sigmoid(ga_ref[...]) * a + jax.nn.sigmoid(gb_ref[...]) * b).astype(o_ref.dtype)


def _mix(o_a, o_b, w_branch_bf, proj, tm, tn):
    t = o_a.shape[0]
    d = w_branch_bf.shape[2]
    return pl.pallas_call(
        _mix_kernel,
        grid=(t // tm, d // tn),
        in_specs=[
            pl.BlockSpec((tm, D_BRANCH), lambda i, j: (i, 0)),
            pl.BlockSpec((tm, D_BRANCH), lambda i, j: (i, 0)),
            pl.BlockSpec((N_BRANCH, D_BRANCH, tn), lambda i, j: (0, 0, j)),
            pl.BlockSpec((tm, tn), lambda i, j: (i, C_GBR // tn + j)),
            pl.BlockSpec((tm, tn), lambda i, j: (i, (C_GBR + d) // tn + j)),
        ],
        out_specs=pl.BlockSpec((tm, tn), lambda i, j: (i, j)),
        out_shape=jax.ShapeDtypeStruct((t, d), bf16),
        compiler_params=_cparams(("parallel", "parallel")),
        name="mix",
    )(o_a, o_b, w_branch_bf, proj, proj)


def _split_bf16(a):
    hi = a.astype(bf16)
    lo = (a - hi.astype(f32)).astype(bf16)
    return hi, lo


def _outproj_kernel(mixed_ref, x_ref, wo_ref, g_ref, wr_hi_ref, wr_lo_ref, br_ref,
                    h_ref, xn_ref, eid_ref, gate_ref, cnt_ref):
    h = x_ref[...] + jnp.dot(mixed_ref[...], wo_ref[...], preferred_element_type=f32)
    h_ref[...] = h
    ms = jnp.mean(h * h, axis=-1, keepdims=True)
    xn = h * lax.rsqrt(ms + EPS) * g_ref[...]
    xn_ref[...] = xn

    x_hi, x_lo = _split_bf16(xn)
    logit = (jnp.dot(x_hi, wr_hi_ref[...], preferred_element_type=f32)
             + jnp.dot(x_hi, wr_lo_ref[...], preferred_element_type=f32)
             + jnp.dot(x_lo, wr_hi_ref[...], preferred_element_type=f32)) + br_ref[...]

    lane = lax.broadcasted_iota(jnp.int32, logit.shape, 1)
    lanef = lane.astype(f32)
    neg = -jnp.inf

    def first_argmax(v, m):
        return jnp.min(jnp.where(v == m, lanef, float(LANES)), axis=-1, keepdims=True)

    lg = jnp.where(lane < N_GROUPS, logit, neg)
    mg = jnp.max(lg, axis=-1, keepdims=True)
    p_grp = 1.0 / jnp.sum(jnp.exp(lg - mg), axis=-1, keepdims=True)
    grp = first_argmax(lg, mg).astype(jnp.int32)

    e_lane = lane - N_GROUPS
    in_grp = (e_lane >= 0) & (e_lane < N_EXPERTS) & ((e_lane // EXP_PER_GROUP) == grp)
    le = jnp.where(in_grp, logit, neg)
    m1 = jnp.max(le, axis=-1, keepdims=True)
    i1 = first_argmax(le, m1)
    le2 = jnp.where(lanef == i1, neg, le)
    m2 = jnp.max(le2, axis=-1, keepdims=True)
    i2 = first_argmax(le2, m2)
    e2 = jnp.exp(m2 - m1)
    g1 = p_grp / (1.0 + e2)
    g2 = p_grp * e2 / (1.0 + e2)

    eid = jnp.where(lane == 0, i1, jnp.where(lane == 1, i2, float(N_GROUPS))) - float(N_GROUPS)
    eid_ref[...] = eid.astype(jnp.int32).T[:8, :]
    gate_ref[...] = jnp.where(lane == 0, g1, jnp.where(lane == 1, g2, 0.0))

    @pl.when(pl.program_id(0) == 0)
    def _():
        cnt_ref[...] = jnp.zeros_like(cnt_ref)

    half = pl.program_id(0) // (pl.num_programs(0) // ROUTE_HALVES)
    sub8 = lax.broadcasted_iota(jnp.int32, (8, LANES), 0)
    for s, idx in enumerate((i1, i2)):
        c = jnp.sum((lanef == idx).astype(jnp.int32), axis=0, keepdims=True)
        cnt_ref[...] += jnp.where(sub8 == s * ROUTE_HALVES + half, c, 0)


def _outproj(mixed, x2, w_out_bf, g_ffn, wr_hi, wr_lo, b_r, tm):
    t, d = x2.shape
    row = lambda i: (i, 0)
    fixed = lambda i: (0, 0)
    return pl.pallas_call(
        _outproj_kernel,
        grid=(t // tm,),
        in_specs=[
            pl.BlockSpec((tm, d), row),
            pl.BlockSpec((tm, d), row),
            pl.BlockSpec((d, d), fixed),
            pl.BlockSpec((1, d), fixed),
            pl.BlockSpec((d, LANES), fixed),
            pl.BlockSpec((d, LANES), fixed),
            pl.BlockSpec((1, LANES), fixed),
        ],
        out_specs=[
            pl.BlockSpec((tm, d), row),
            pl.BlockSpec((tm, d), row),
            pl.BlockSpec((8, tm), lambda i: (0, i)),
            pl.BlockSpec((tm, LANES), row),
            pl.BlockSpec((8, LANES), fixed),
        ],
        out_shape=[
            jax.ShapeDtypeStruct((t, d), f32),
            jax.ShapeDtypeStruct((t, d), f32),
            jax.ShapeDtypeStruct((8, t), jnp.int32),
            jax.ShapeDtypeStruct((t, LANES), f32),
            jax.ShapeDtypeStruct((8, LANES), jnp.int32),
        ],
        compiler_params=_cparams(("arbitrary",)),
        name="outproj",
    )(mixed, x2, w_out_bf, g_ffn, wr_hi, wr_lo, b_r)


ISSUE_UNROLL = 8
ROW_DMA_PRIORITY = 1


def _experts_kernel(bexp_ref, rpack_ref, nused_ref, xn_hbm, wg_ref, wu_ref, wd_ref, yt_hbm,
                    xbuf, ybuf, gsem, ssem, *, n_tok):
    j = pl.program_id(0)
    n_used = nused_ref[0]
    slot = j % 2
    tok_bits = (n_tok - 1).bit_length()

    def rows_of(blk, fn):
        base = blk * ROW_BLOCK

        def body(k, c):
            r0 = pl.multiple_of(k * ISSUE_UNROLL, ISSUE_UNROLL)
            for u in range(ISSUE_UNROLL):
                fn(r0 + u, rpack_ref[base + r0 + u])
            return c

        lax.fori_loop(0, ROW_BLOCK // ISSUE_UNROLL, body, 0)

    def start_gathers(blk, sl):
        def one(r, packed):
            tok = packed & ((1 << tok_bits) - 1)
            pltpu.make_async_copy(xn_hbm.at[pl.ds(tok, 1)], xbuf.at[sl, pl.ds(r, 1)],
                                  gsem.at[sl]).start(priority=ROW_DMA_PRIORITY)
        rows_of(blk, one)

    def start_scatters(blk, sl):
        def one(r, packed):
            row = lax.shift_right_logical(packed, tok_bits)
            pltpu.make_async_copy(ybuf.at[sl, pl.ds(r, 1)], yt_hbm.at[pl.ds(row, 1)],
                                  ssem.at[sl]).start(priority=ROW_DMA_PRIORITY)
        rows_of(blk, one)

    def wait_gathers(sl):
        pltpu.make_async_copy(xn_hbm.at[pl.ds(0, ROW_BLOCK)], xbuf.at[sl], gsem.at[sl]).wait()

    def wait_scatters(sl):
        pltpu.make_async_copy(ybuf.at[sl], yt_hbm.at[pl.ds(0, ROW_BLOCK)], ssem.at[sl]).wait()

    @pl.when(j == 0)
    def _():
        start_gathers(0, 0)
        ybuf[...] = jnp.zeros_like(ybuf)
        for sl in range(2):
            spare = yt_hbm.at[pl.ds(2 * n_tok + sl * ROW_BLOCK, ROW_BLOCK)]
            pltpu.make_async_copy(ybuf.at[sl], spare, ssem.at[sl]).start()
        for sl in range(2):
            wait_scatters(sl)

    @pl.when(j < n_used)
    def _():
        wait_gathers(slot)

        @pl.when(j + 1 < n_used)
        def _():
            start_gathers(j + 1, 1 - slot)

        @pl.when(j >= 2)
        def _():
            wait_scatters(slot)

        xb = xbuf[slot].astype(bf16)
        g = jnp.dot(xb, wg_ref[...], preferred_element_type=f32)
        u = jnp.dot(xb, wu_ref[...], preferred_element_type=f32)
        hm = (g * jax.nn.sigmoid(g) * u).astype(bf16)
        ybuf[slot] = jnp.dot(hm, wd_ref[...], preferred_element_type=f32)
        start_scatters(j, slot)

        @pl.when(j == n_used - 1)
        def _():
            wait_scatters(slot)

            @pl.when(j >= 1)
            def _():
                wait_scatters(1 - slot)


def _experts(block_expert, row_pack, n_used, xn, wg_bf, wu_bf, wd_bf):
    n_rows = row_pack.shape[0]
    n_tok, d = xn.shape
    f = wg_bf.shape[2]
    grid_spec = pltpu.PrefetchScalarGridSpec(
        num_scalar_prefetch=3,
        grid=(n_rows // ROW_BLOCK,),
        in_specs=[
            pl.BlockSpec(memory_space=pl.ANY),
            pl.BlockSpec((None, d, f), lambda i, be, rd, nu: (be[i], 0, 0)),
            pl.BlockSpec((None, d, f), lambda i, be, rd, nu: (be[i], 0, 0)),
            pl.BlockSpec((None, f, d), lambda i, be, rd, nu: (be[i], 0, 0)),
        ],
        out_specs=pl.BlockSpec(memory_space=pl.ANY),
        scratch_shapes=[pltpu.VMEM((2, ROW_BLOCK, d), f32), pltpu.VMEM((2, ROW_BLOCK, d), f32),
                        pltpu.SemaphoreType.DMA((2,)), pltpu.SemaphoreType.DMA((2,))],
    )
    return pl.pallas_call(
        functools.partial(_experts_kernel, n_tok=n_tok),
        grid_spec=grid_spec,
        out_shape=jax.ShapeDtypeStruct((2 * n_tok + 2 * ROW_BLOCK, d), f32),
        compiler_params=_cparams(("arbitrary",)),
        name="experts",
    )(block_expert, row_pack, n_used, xn, wg_bf, wu_bf, wd_bf)


def _combine_kernel(h_ref, y0_ref, y1_ref, gate_ref, g_ref, o_ref):
    gate = gate_ref[...]
    hh = h_ref[...] + gate[:, 0:1] * y0_ref[...] + gate[:, 1:2] * y1_ref[...]
    ms = jnp.mean(hh * hh, axis=-1, keepdims=True)
    o_ref[...] = hh * lax.rsqrt(ms + EPS) * g_ref[...]


def _combine(h, yt, gate, g_final, tm):
    t, d = h.shape
    nt = t // tm
    return pl.pallas_call(
        _combine_kernel,
        grid=(nt,),
        in_specs=[
            pl.BlockSpec((tm, d), lambda i: (i, 0)),
            pl.BlockSpec((tm, d), lambda i: (i, 0)),
            pl.BlockSpec((tm, d), lambda i: (nt + i, 0)),
            pl.BlockSpec((tm, LANES), lambda i: (i, 0)),
            pl.BlockSpec((1, d), lambda i: (0, 0)),
        ],
        out_specs=pl.BlockSpec((tm, d), lambda i: (i, 0)),
        out_shape=jax.ShapeDtypeStruct((t, d), f32),
        compiler_params=_cparams(("parallel",)),
        name="combine",
    )(h, yt, yt, gate, g_final)


def _route_kernel(eid_ref, cnt_ref, rpack_ref, bexp_ref, nused_ref, *cur_refs, n_tok, n_blocks):
    tok_bits = (n_tok - 1).bit_length()
    chunk = 2 * n_tok // ROUTE_CHAINS

    def per_expert(e, blk):
        start = blk * ROW_BLOCK
        run = start
        for c in range(ROUTE_CHAINS):
            cur_refs[c][e] = run
            run = run + cnt_ref[c * N_EXPERTS + e]
        nb = (run - start + ROW_BLOCK - 1) // ROW_BLOCK

        def set_block(k, carry):
            bexp_ref[blk + k] = e
            return carry

        lax.fori_loop(0, nb, set_block, 0)

        def set_pad(r, carry):
            rpack_ref[r] = (2 * n_tok + (r & (2 * ROW_BLOCK - 1))) << tok_bits
            return carry

        lax.fori_loop(run, start + nb * ROW_BLOCK, set_pad, 0)
        return blk + nb

    n_used = lax.fori_loop(0, N_EXPERTS, per_expert, 0)
    nused_ref[0] = n_used

    def tail_block(k, carry):
        bexp_ref[k] = N_EXPERTS - 1
        return carry

    lax.fori_loop(n_used, n_blocks, tail_block, 0)

    def tail_row(r, carry):
        rpack_ref[r] = (2 * n_tok + (r & (2 * ROW_BLOCK - 1))) << tok_bits
        return carry

    lax.fori_loop(n_used * ROW_BLOCK, n_blocks * ROW_BLOCK, tail_row, 0)

    def place(i, carry):
        for c in range(ROUTE_CHAINS):
            a = c * chunk + i
            e = eid_ref[a]
            p = cur_refs[c][e]
            cur_refs[c][e] = p + 1
            rpack_ref[p] = (a << tok_bits) | (a - (c * chunk // n_tok) * n_tok)
        return carry

    lax.fori_loop(0, chunk, place, 0)


def _route(eid_flat, counts, n_tok):
    n_asg = eid_flat.shape[0]
    n_rows = -(-(n_asg + N_EXPERTS * (ROW_BLOCK - 1)) // ROW_BLOCK) * ROW_BLOCK
    n_blocks = n_rows // ROW_BLOCK
    smem = pl.BlockSpec(memory_space=pltpu.SMEM)
    return pl.pallas_call(
        functools.partial(_route_kernel, n_tok=n_tok, n_blocks=n_blocks),
        in_specs=[smem, smem],
        out_specs=[smem, smem, smem],
        out_shape=[jax.ShapeDtypeStruct((n_rows,), jnp.int32),
                   jax.ShapeDtypeStruct((n_blocks,), jnp.int32),
                   jax.ShapeDtypeStruct((1,), jnp.int32)],
        scratch_shapes=[pltpu.SMEM((N_EXPERTS,), jnp.int32)] * ROUTE_CHAINS,
        name="route",
    )(eid_flat, counts)


def _pick(n, prefs):
    for p in prefs:
        if n % p == 0:
            return p
    return n


def kernel(x, g_mix_norm, w_in, g_kv, w_uv, g_ret, w_branch, w_out, g_ffn_norm, w_router_group,
           b_router_group, w_router_expert, b_router_expert, w_expert_gate, w_expert_up,
           w_expert_down, g_final):
    b, s, d = x.shape
    t = b * s
    depth = w_in.shape[0]
    n_sel = min(TOPK_MAX, s // 4)
    assert s % RET_CHUNK == 0 and s % Q_TILE == 0

    cos_t, sin_t, d_in, d_q, d_k, d_c = _retention_tables(s)
    h2 = x.reshape(t, d)
    for l in range(depth):
        wl = w_in[l]
        sp = np.cumsum([0, H_A * D_LATENT, D_LATENT, H_IDX * D_IDX, D_IDX, H_IDX,
                        H_R * DK_R, H_R * DK_R, H_R * DV_R, H_R * DV_R, N_BRANCH * d])
        seg = [wl[:, sp[k]:sp[k + 1]] for k in range(10)]
        kw_pad = jnp.zeros((d, LANES - D_IDX - H_IDX), wl.dtype)
        w_p = jnp.concatenate([seg[0], seg[9], seg[5], seg[6], seg[7], seg[8], seg[2], seg[1],
                               seg[3], seg[4], kw_pad], axis=1).astype(bf16)
        assert w_p.shape[1] == D_IN_P

        proj = _proj(h2, g_mix_norm[l].reshape(1, d), w_p, _pick(t, (1024, 512, 256)), 768)
        proj3 = proj.reshape(b, s, D_IN_P)

        o_a = _attn(proj3, g_kv[l].reshape(1, D_LATENT), w_uv[l].astype(bf16), n_sel)
        o_b = _ret(proj3, cos_t, sin_t, d_in, d_q, d_k, d_c, g_ret[l].reshape(1, H_R * DV_R))

        mixed = _mix(o_a.reshape(t, D_BRANCH), o_b.reshape(t, D_BRANCH), w_branch[l].astype(bf16),
                     proj, _pick(t, (512, 256)), 512)

        w_r = jnp.concatenate([w_router_group[l], w_router_expert[l],
                               jnp.zeros((d, LANES - N_GROUPS - N_EXPERTS), f32)], axis=1)
        b_r = jnp.concatenate([b_router_group[l], b_router_expert[l],
                               jnp.zeros((LANES - N_GROUPS - N_EXPERTS,), f32)]).reshape(1, LANES)
        wr_hi = w_r.astype(bf16)
        wr_lo = (w_r - wr_hi.astype(f32)).astype(bf16)
        h2, xn, eid_t, gate, cnt = _outproj(mixed, h2, w_out[l].astype(bf16),
                                            g_ffn_norm[l].reshape(1, d), wr_hi, wr_lo, b_r, _pick(t, (256,)))

        row_pack, block_expert, n_used = _route(
            eid_t[:2].reshape(-1), cnt[:ROUTE_CHAINS, N_GROUPS:N_GROUPS + N_EXPERTS].reshape(-1), t)
        yt = _experts(block_expert, row_pack, n_used, xn, w_expert_gate[l].astype(bf16),
                      w_expert_up[l].astype(bf16), w_expert_down[l].astype(bf16))
        assert depth == 1
        h2 = _combine(h2, yt, gate, g_final.reshape(1, d), _pick(t, (256,)))
    return h2.reshape(b, s, d)
```

```python
import functools

import jax
import jax.numpy as jnp
import numpy as np
from jax import lax
from jax.experimental import pallas as pl
from jax.experimental.pallas import tpu as pltpu

EPS = 1e-6
CHUNK = 64
H_A = 8
D_LATENT = 128
DH_A = 128
H_IDX = 8
D_IDX = 64
TOPK_MAX = 256
H_R = 8
DK_R = 128
DV_R = 128
ROPE_BASE = 10000.0
D_BRANCH = 1024
N_BRANCH = 2
N_GROUPS = 4
EXP_PER_GROUP = 8
N_EXPERTS = N_GROUPS * EXP_PER_GROUP
D_EXPERT = 1024

LANES = 128
KEY_TILE = 256
Q_TILE = 128
RET_CHUNK = 256
ROW_BLOCK = 256
ROUTE_HALVES = 2
ROUTE_CHAINS = 2 * ROUTE_HALVES
VMEM_LIMIT = 56 * 1024 * 1024

C_QLAT = 0
C_GBR = 1024
C_QR = 5120
C_KR = 6144
C_VR = 7168
C_GR = 8192
C_QIDX = 9216
C_CKV = 9728
C_KW = 9856
D_IN_P = 9984

INT_MIN = np.int32(-2 ** 31)
NEG_BIG = -1e30

bf16 = jnp.bfloat16
f32 = jnp.float32


def _cparams(sem):
    return pltpu.CompilerParams(dimension_semantics=sem, vmem_limit_bytes=VMEM_LIMIT)


def _proj_kernel(x_ref, g_ref, w_ref, o_ref, xn_ref):
    @pl.when(pl.program_id(1) == 0)
    def _():
        x = x_ref[...]
        ms = jnp.mean(x * x, axis=-1, keepdims=True)
        xn_ref[...] = (x * lax.rsqrt(ms + EPS) * g_ref[...]).astype(bf16)

    o_ref[...] = jnp.dot(xn_ref[...], w_ref[...], preferred_element_type=f32).astype(o_ref.dtype)


def _proj(x2, g, w_p, tm, tn):
    t, d = x2.shape
    n = w_p.shape[1]
    return pl.pallas_call(
        _proj_kernel,
        grid=(t // tm, n // tn),
        in_specs=[
            pl.BlockSpec((tm, d), lambda i, j: (i, 0)),
            pl.BlockSpec((1, d), lambda i, j: (0, 0)),
            pl.BlockSpec((d, tn), lambda i, j: (0, j)),
        ],
        out_specs=pl.BlockSpec((tm, tn), lambda i, j: (i, j)),
        out_shape=jax.ShapeDtypeStruct((t, n), bf16),
        scratch_shapes=[pltpu.VMEM((tm, d), bf16)],
        compiler_params=_cparams(("parallel", "arbitrary")),
        name="proj",
    )(x2, g, w_p)


def _float_key(s):
    bits = pltpu.bitcast(s, jnp.int32)
    key = bits ^ ((bits >> 31) & jnp.int32(0x7FFFFFFF))
    return jnp.where(s == 0.0, jnp.int32(0), key)


def _attn_kernel(qlat_ref, qidx_ref, kwq_ref, ckv_ref, kwk_ref, gkv_ref, wuv_ref, o_ref,
                 kv_s, kvT_s, kidx_s, key_s, bias_s, qT_s, acc_s, *, n_sel, n_kt):
    i = pl.program_id(1)
    idx_scale = (H_IDX ** -0.5) * (D_IDX ** -0.5)
    attn_scale = D_LATENT ** -0.5
    hq = H_A * Q_TILE

    @pl.when(i == 0)
    def _():
        g = gkv_ref[...]
        for t in range(n_kt):
            c = ckv_ref[t * KEY_TILE:(t + 1) * KEY_TILE, :].astype(f32)
            ms = jnp.mean(c * c, axis=-1, keepdims=True)
            kv = c * lax.rsqrt(ms + EPS) * g
            kv_s[t] = kv.astype(bf16)
            kvT_s[t] = kv.T.astype(bf16)
            kidx_s[t] = kwk_ref[t * KEY_TILE:(t + 1) * KEY_TILE, :D_IDX]

    nk = ((i + 1) * Q_TILE + KEY_TILE - 1) // KEY_TILE
    lane = lax.broadcasted_iota(jnp.int32, (1, Q_TILE), 1)
    sub = lax.broadcasted_iota(jnp.int32, (KEY_TILE, 1), 0)
    q_chunk = (i * Q_TILE + lane) // CHUNK

    wT = kwq_ref[...].astype(f32).T
    qidx = qidx_ref[...]
    for h in range(H_A):
        qT_s[:, h * Q_TILE:(h + 1) * Q_TILE] = qlat_ref[:, h * D_LATENT:(h + 1) * D_LATENT].astype(f32).T.astype(bf16)

    def score_tile(t, carry):
        ks = kidx_s[t]
        acc = jnp.zeros((KEY_TILE, Q_TILE), f32)
        for h in range(H_IDX):
            d = lax.dot_general(ks, qidx[:, h * D_IDX:(h + 1) * D_IDX],
                                (((1,), (1,)), ((), ())), preferred_element_type=f32)
            acc = acc + wT[D_IDX + h:D_IDX + h + 1, :] * jnp.maximum(d, 0.0)
        score = acc * idx_scale
        k_chunk = (t * KEY_TILE + sub) // CHUNK
        key_s[t] = jnp.where(k_chunk <= q_chunk, _float_key(score), INT_MIN)
        return carry

    lax.fori_loop(0, nk, score_tile, 0)

    def count(pred):
        def body(t, c):
            m = pred(key_s[t], t).astype(jnp.int32)
            return c + jnp.sum(m.reshape(KEY_TILE // 8, 8, Q_TILE), axis=0)
        c8 = lax.fori_loop(0, nk, body, jnp.zeros((8, Q_TILE), jnp.int32))
        return jnp.sum(c8, axis=0, keepdims=True)

    thr0 = jnp.where(count(lambda k, t: k >= 0) >= n_sel, jnp.int32(0), INT_MIN)
    thr0 = jnp.broadcast_to(thr0, (1, Q_TILE)).astype(jnp.int32)

    def bit_step(j, thr):
        cand = thr | (jnp.int32(1) << (jnp.int32(30) - j))
        return jnp.where(count(lambda k, t: k >= cand) >= n_sel, cand, thr)

    thr = lax.fori_loop(0, 31, bit_step, thr0)

    c_gt = count(lambda k, t: k > thr)
    c_ge = count(lambda k, t: k >= thr)
    need = n_sel - c_gt
    has_tie = jnp.max(jnp.where((c_ge > n_sel) & (thr > INT_MIN), 1, 0)) > 0

    def tie_limit():
        def step(j, m):
            cand = m | (jnp.int32(1) << (jnp.int32(14) - j))
            c = count(lambda k, t: (k == thr) & ((t * KEY_TILE + sub) < cand))
            return jnp.where(c < need, cand, m)
        return lax.fori_loop(0, 15, step, jnp.zeros((1, Q_TILE), jnp.int32))

    m_lim = lax.cond(has_tie, tie_limit, lambda: jnp.full((1, Q_TILE), 2 ** 30, jnp.int32))

    def bias_tile(t, carry):
        k = key_s[t]
        sel = (k > thr) | ((k == thr) & ((t * KEY_TILE + sub) <= m_lim))
        sel = sel & (k > INT_MIN)
        bias_s[t] = jnp.where(sel, 0.0, NEG_BIG).astype(f32)
        return carry

    lax.fori_loop(0, nk, bias_tile, 0)

    acc_s[...] = jnp.zeros_like(acc_s)

    def att_tile(t, carry):
        m_run, l_run = carry
        logit = jnp.dot(kv_s[t], qT_s[...], preferred_element_type=f32) * attn_scale
        logit = logit + jnp.concatenate([bias_s[t]] * H_A, axis=1)
        m_new = jnp.maximum(m_run, jnp.max(logit, axis=0, keepdims=True))
        alpha = jnp.exp(m_run - m_new)
        p = jnp.exp(logit - m_new)
        l_new = alpha * l_run + jnp.sum(p, axis=0, keepdims=True)
        acc_s[...] = alpha * acc_s[...] + jnp.dot(kvT_s[t], p.astype(bf16), preferred_element_type=f32)
        return m_new, l_new

    init = (jnp.full((1, hq), NEG_BIG, f32), jnp.zeros((1, hq), f32))
    _, l_fin = lax.fori_loop(0, nk, att_tile, init)
    inv_l = 1.0 / l_fin
    for h in range(H_A):
        sl = slice(h * Q_TILE, (h + 1) * Q_TILE)
        o_lat = (acc_s[:, sl] * inv_l[:, sl]).T
        o_ref[:, h * DH_A:(h + 1) * DH_A] = jnp.dot(
            o_lat.astype(bf16), wuv_ref[h], preferred_element_type=f32).astype(o_ref.dtype)


def _attn(proj3, g_kv, w_uv_bf, n_sel):
    b, s, _ = proj3.shape
    n_kt = s // KEY_TILE
    kern = functools.partial(_attn_kernel, n_sel=n_sel, n_kt=n_kt)
    return pl.pallas_call(
        kern,
        grid=(b, s // Q_TILE),
        in_specs=[
            pl.BlockSpec((None, Q_TILE, H_A * D_LATENT), lambda bi, i: (bi, i, C_QLAT // 1024)),
            pl.BlockSpec((None, Q_TILE, H_IDX * D_IDX), lambda bi, i: (bi, i, C_QIDX // 512)),
            pl.BlockSpec((None, Q_TILE, LANES), lambda bi, i: (bi, i, C_KW // LANES)),
            pl.BlockSpec((None, s, LANES), lambda bi, i: (bi, 0, C_CKV // LANES)),
            pl.BlockSpec((None, s, LANES), lambda bi, i: (bi, 0, C_KW // LANES)),
            pl.BlockSpec((1, D_LATENT), lambda bi, i: (0, 0)),
            pl.BlockSpec((H_A, D_LATENT, DH_A), lambda bi, i: (0, 0, 0)),
        ],
        out_specs=pl.BlockSpec((None, Q_TILE, D_BRANCH), lambda bi, i: (bi, i, 0)),
        out_shape=jax.ShapeDtypeStruct((b, s, D_BRANCH), bf16),
        scratch_shapes=[
            pltpu.VMEM((n_kt, KEY_TILE, D_LATENT), bf16),
            pltpu.VMEM((n_kt, D_LATENT, KEY_TILE), bf16),
            pltpu.VMEM((n_kt, KEY_TILE, D_IDX), bf16),
            pltpu.VMEM((n_kt, KEY_TILE, Q_TILE), jnp.int32),
            pltpu.VMEM((n_kt, KEY_TILE, Q_TILE), f32),
            pltpu.VMEM((D_LATENT, H_A * Q_TILE), bf16),
            pltpu.VMEM((D_LATENT, H_A * Q_TILE), f32),
        ],
        compiler_params=_cparams(("parallel", "arbitrary")),
        name="attn",
    )(proj3, proj3, proj3, proj3, proj3, g_kv, w_uv_bf)


def _ret_kernel(q_ref, k_ref, v_ref, gr_ref, cos_ref, sin_ref, din_ref, dq_ref, dk_ref, dc_ref,
                gret_ref, o_ref, state_s):
    @pl.when(pl.program_id(1) == 0)
    def _():
        state_s[...] = jnp.zeros_like(state_s)

    cos = cos_ref[...]
    sin = sin_ref[...]

    def rot(x):
        return x * cos + pltpu.roll(x, DK_R // 2, axis=1) * sin

    for h in range(H_R):
        sl = slice(h * DK_R, (h + 1) * DK_R)
        q = rot(q_ref[:, sl].astype(f32)).astype(bf16)
        kf = rot(k_ref[:, sl].astype(f32)) * (DK_R ** -0.5)
        k = kf.astype(bf16)
        v = v_ref[:, sl]
        inner = lax.dot_general(q, k, (((1,), (1,)), ((), ())), preferred_element_type=f32) * din_ref[h]
        o = jnp.dot(inner.astype(bf16), v, preferred_element_type=f32)
        st = state_s[h]
        o = o + jnp.dot(q, st.astype(bf16), preferred_element_type=f32) * dq_ref[h]
        kd = (kf * dk_ref[h]).astype(bf16)
        state_s[h] = st * dc_ref[h] + jnp.dot(kd.T, v, preferred_element_type=f32)
        mu = jnp.mean(o, axis=-1, keepdims=True)
        var = jnp.mean(jnp.square(o - mu), axis=-1, keepdims=True)
        y = (o - mu) * lax.rsqrt(var + EPS) * gret_ref[:, sl]
        gate = gr_ref[:, sl].astype(f32)
        o_ref[:, sl] = (gate * jax.nn.sigmoid(gate) * y).astype(o_ref.dtype)


def _ret(proj3, cos_t, sin_t, d_in, d_q, d_k, d_c, g_ret):
    b, s, _ = proj3.shape
    c = RET_CHUNK
    w = H_R * DK_R

    def col(off):
        return pl.BlockSpec((None, c, w), lambda bi, ci: (bi, ci, off // w))

    return pl.pallas_call(
        _ret_kernel,
        grid=(b, s // c),
        in_specs=[
            col(C_QR), col(C_KR), col(C_VR), col(C_GR),
            pl.BlockSpec((c, DK_R), lambda bi, ci: (ci, 0)),
            pl.BlockSpec((c, DK_R), lambda bi, ci: (ci, 0)),
            pl.BlockSpec((H_R, c, c), lambda bi, ci: (0, 0, 0)),
            pl.BlockSpec((H_R, c, DK_R), lambda bi, ci: (0, 0, 0)),
            pl.BlockSpec((H_R, c, DK_R), lambda bi, ci: (0, 0, 0)),
            pl.BlockSpec((H_R, 1, DK_R), lambda bi, ci: (0, 0, 0)),
            pl.BlockSpec((1, w), lambda bi, ci: (0, 0)),
        ],
        out_specs=pl.BlockSpec((None, c, w), lambda bi, ci: (bi, ci, 0)),
        out_shape=jax.ShapeDtypeStruct((b, s, w), bf16),
        scratch_shapes=[pltpu.VMEM((H_R, DK_R, DV_R), f32)],
        compiler_params=_cparams(("parallel", "arbitrary")),
        name="ret",
    )(proj3, proj3, proj3, proj3, cos_t, sin_t, d_in, d_q, d_k, d_c, g_ret)


def _retention_tables(s):
    c = RET_CHUNK
    half = DK_R // 2
    freq = ROPE_BASE ** (-jnp.arange(half, dtype=f32) / half)
    ang = jnp.arange(s, dtype=f32)[:, None] * freq[None, :]
    cos = jnp.cos(ang)
    sin = jnp.sin(ang)
    cos_t = jnp.concatenate([cos, cos], axis=-1)
    sin_t = jnp.concatenate([-sin, sin], axis=-1)
    log_gamma = jnp.log1p(-jnp.exp2(-5.0 - jnp.arange(H_R, dtype=f32)))
    n = jnp.arange(c, dtype=f32)
    diff = n[:, None] - n[None, :]
    d_in = jnp.where(diff >= 0, jnp.exp(log_gamma[:, None, None] * jnp.maximum(diff, 0.0)), 0.0)
    d_q = jnp.broadcast_to(jnp.exp(log_gamma[:, None] * (n + 1.0))[:, :, None], (H_R, c, DK_R))
    d_k = jnp.broadcast_to(jnp.exp(log_gamma[:, None] * (c - 1.0 - n))[:, :, None], (H_R, c, DK_R))
    d_c = jnp.broadcast_to(jnp.exp(log_gamma * c)[:, None, None], (H_R, 1, DK_R))
    return cos_t, sin_t, d_in, d_q, d_k, d_c


def _mix_kernel(oa_ref, ob_ref, wb_ref, ga_ref, gb_ref, o_ref):
    a = jnp.dot(oa_ref[...], wb_ref[0], preferred_element_type=f32)
    b = jnp.dot(ob_ref[...], wb_ref[1], preferred_element_type=f32)
    ga = jax.nn.sigmoid(ga_ref[...].astype(f32))
    gb = jax.nn.sigmoid(gb_ref[...].astype(f32))
    o_ref[...] = (ga * a + gb * b).astype(o_ref.dtype)


def _mix(o_a, o_b, w_branch_bf, proj, tm, tn):
    t = o_a.shape[0]
    d = w_branch_bf.shape[2]
    return pl.pallas_call(
        _mix_kernel,
        grid=(t // tm, d // tn),
        in_specs=[
            pl.BlockSpec((tm, D_BRANCH), lambda i, j: (i, 0)),
            pl.BlockSpec((tm, D_BRANCH), lambda i, j: (i, 0)),
            pl.BlockSpec((N_BRANCH, D_BRANCH, tn), lambda i, j: (0, 0, j)),
            pl.BlockSpec((tm, tn), lambda i, j: (i, C_GBR // tn + j)),
            pl.BlockSpec((tm, tn), lambda i, j: (i, (C_GBR + d) // tn + j)),
        ],
        out_specs=pl.BlockSpec((tm, tn), lambda i, j: (i, j)),
        out_shape=jax.ShapeDtypeStruct((t, d), bf16),
        compiler_params=_cparams(("parallel", "parallel")),
        name="mix",
    )(o_a, o_b, w_branch_bf, proj, proj)


def _split_bf16(a):
    hi = a.astype(bf16)
    lo = (a - hi.astype(f32)).astype(bf16)
    return hi, lo


def _outproj_kernel(mixed_ref, x_ref, wo_ref, g_ref, wr_hi_ref, wr_lo_ref, br_ref,
                    h_ref, xn_ref, eid_ref, gate_ref, cnt_ref):
    h = x_ref[...] + jnp.dot(mixed_ref[...], wo_ref[...], preferred_element_type=f32)
    h_ref[...] = h
    ms = jnp.mean(h * h, axis=-1, keepdims=True)
    xn = h * lax.rsqrt(ms + EPS) * g_ref[...]
    xn_ref[...] = xn

    x_hi, x_lo = _split_bf16(xn)
    logit = (jnp.dot(x_hi, wr_hi_ref[...], preferred_element_type=f32)
             + jnp.dot(x_hi, wr_lo_ref[...], preferred_element_type=f32)
             + jnp.dot(x_lo, wr_hi_ref[...], preferred_element_type=f32)) + br_ref[...]

    lane = lax.broadcasted_iota(jnp.int32, logit.shape, 1)
    lanef = lane.astype(f32)
    neg = -jnp.inf

    def first_argmax(v, m):
        return jnp.min(jnp.where(v == m, lanef, float(LANES)), axis=-1, keepdims=True)

    lg = jnp.where(lane < N_GROUPS, logit, neg)
    mg = jnp.max(lg, axis=-1, keepdims=True)
    p_grp = 1.0 / jnp.sum(jnp.exp(lg - mg), axis=-1, keepdims=True)
    grp = first_argmax(lg, mg).astype(jnp.int32)

    e_lane = lane - N_GROUPS
    in_grp = (e_lane >= 0) & (e_lane < N_EXPERTS) & ((e_lane // EXP_PER_GROUP) == grp)
    le = jnp.where(in_grp, logit, neg)
    m1 = jnp.max(le, axis=-1, keepdims=True)
    i1 = first_argmax(le, m1)
    le2 = jnp.where(lanef == i1, neg, le)
    m2 = jnp.max(le2, axis=-1, keepdims=True)
    i2 = first_argmax(le2, m2)
    e2 = jnp.exp(m2 - m1)
    g1 = p_grp / (1.0 + e2)
    g2 = p_grp * e2 / (1.0 + e2)

    eid = jnp.where(lane == 0, i1, jnp.where(lane == 1, i2, float(N_GROUPS))) - float(N_GROUPS)
    eid_ref[...] = eid.astype(jnp.int32).T[:8, :]
    gate_ref[...] = jnp.where(lane == 0, g1, jnp.where(lane == 1, g2, 0.0))

    @pl.when(pl.program_id(0) == 0)
    def _():
        cnt_ref[...] = jnp.zeros_like(cnt_ref)

    half = pl.program_id(0) // (pl.num_programs(0) // ROUTE_HALVES)
    sub8 = lax.broadcasted_iota(jnp.int32, (8, LANES), 0)
    for s, idx in enumerate((i1, i2)):
        c = jnp.sum((lanef == idx).astype(jnp.int32), axis=0, keepdims=True)
        cnt_ref[...] += jnp.where(sub8 == s * ROUTE_HALVES + half, c, 0)


def _outproj(mixed, x2, w_out_bf, g_ffn, wr_hi, wr_lo, b_r, tm):
    t, d = x2.shape
    row = lambda i: (i, 0)
    fixed = lambda i: (0, 0)
    return pl.pallas_call(
        _outproj_kernel,
        grid=(t // tm,),
        in_specs=[
            pl.BlockSpec((tm, d), row),
            pl.BlockSpec((tm, d), row),
            pl.BlockSpec((d, d), fixed),
            pl.BlockSpec((1, d), fixed),
            pl.BlockSpec((d, LANES), fixed),
            pl.BlockSpec((d, LANES), fixed),
            pl.BlockSpec((1, LANES), fixed),
        ],
        out_specs=[
            pl.BlockSpec((tm, d), row),
            pl.BlockSpec((tm, d), row),
            pl.BlockSpec((8, tm), lambda i: (0, i)),
            pl.BlockSpec((tm, LANES), row),
            pl.BlockSpec((8, LANES), fixed),
        ],
        out_shape=[
            jax.ShapeDtypeStruct((t, d), f32),
            jax.ShapeDtypeStruct((t, d), f32),
            jax.ShapeDtypeStruct((8, t), jnp.int32),
            jax.ShapeDtypeStruct((t, LANES), f32),
            jax.ShapeDtypeStruct((8, LANES), jnp.int32),
        ],
        compiler_params=_cparams(("arbitrary",)),
        name="outproj",
    )(mixed, x2, w_out_bf, g_ffn, wr_hi, wr_lo, b_r)


ISSUE_UNROLL = 8

def _experts_kernel(bexp_ref, rpack_ref, nused_ref, xn_hbm, wg_ref, wu_ref, wd_ref, yt_hbm,
                    xbuf, ybuf, gsem, ssem, *, n_tok):
    j = pl.program_id(0)
    n_used = nused_ref[0]
    slot = j % 2
    tok_bits = (n_tok - 1).bit_length()

    def rows_of(blk, fn):
        base = blk * ROW_BLOCK

        def body(k, c):
            r0 = pl.multiple_of(k * ISSUE_UNROLL, ISSUE_UNROLL)
            for u in range(ISSUE_UNROLL):
                fn(r0 + u, rpack_ref[base + r0 + u], u % 2)
            return c

        lax.fori_loop(0, ROW_BLOCK // ISSUE_UNROLL, body, 0)

    def start_gathers(blk, sl):
        def one(r, packed, queue):
            tok = packed & ((1 << tok_bits) - 1)
            pltpu.make_async_copy(xn_hbm.at[pl.ds(tok, 1)], xbuf.at[sl, pl.ds(r, 1)],
                                  gsem.at[sl]).start(priority=queue)
        rows_of(blk, one)

    def start_scatters(blk, sl):
        def one(r, packed, queue):
            row = lax.shift_right_logical(packed, tok_bits)
            pltpu.make_async_copy(ybuf.at[sl, pl.ds(r, 1)], yt_hbm.at[pl.ds(row, 1)],
                                  ssem.at[sl]).start(priority=queue)
        rows_of(blk, one)

    def wait_gathers(sl):
        pltpu.make_async_copy(xn_hbm.at[pl.ds(0, ROW_BLOCK)], xbuf.at[sl], gsem.at[sl]).wait()

    def wait_scatters(sl):
        pltpu.make_async_copy(ybuf.at[sl], yt_hbm.at[pl.ds(0, ROW_BLOCK)], ssem.at[sl]).wait()

    @pl.when(j == 0)
    def _():
        start_gathers(0, 0)
        ybuf[...] = jnp.zeros_like(ybuf)
        for sl in range(2):
            spare = yt_hbm.at[pl.ds(2 * n_tok + sl * ROW_BLOCK, ROW_BLOCK)]
            pltpu.make_async_copy(ybuf.at[sl], spare, ssem.at[sl]).start()
        for sl in range(2):
            wait_scatters(sl)

    @pl.when(j < n_used)
    def _():
        wait_gathers(slot)

        @pl.when(j + 1 < n_used)
        def _():
            start_gathers(j + 1, 1 - slot)

        @pl.when(j >= 2)
        def _():
            wait_scatters(slot)

        xb = xbuf[slot].astype(bf16)
        g = jnp.dot(xb, wg_ref[...], preferred_element_type=f32)
        u = jnp.dot(xb, wu_ref[...], preferred_element_type=f32)
        hm = (g * jax.nn.sigmoid(g) * u).astype(bf16)
        ybuf[slot] = jnp.dot(hm, wd_ref[...], preferred_element_type=f32)
        start_scatters(j, slot)

        @pl.when(j == n_used - 1)
        def _():
            wait_scatters(slot)

            @pl.when(j >= 1)
            def _():
                wait_scatters(1 - slot)


def _experts(block_expert, row_pack, n_used, xn, wg_bf, wu_bf, wd_bf):
    n_rows = row_pack.shape[0]
    n_tok, d = xn.shape
    f = wg_bf.shape[2]
    grid_spec = pltpu.PrefetchScalarGridSpec(
        num_scalar_prefetch=3,
        grid=(n_rows // ROW_BLOCK,),
        in_specs=[
            pl.BlockSpec(memory_space=pl.ANY),
            pl.BlockSpec((None, d, f), lambda i, be, rd, nu: (be[i], 0, 0)),
            pl.BlockSpec((None, d, f), lambda i, be, rd, nu: (be[i], 0, 0)),
            pl.BlockSpec((None, f, d), lambda i, be, rd, nu: (be[i], 0, 0)),
        ],
        out_specs=pl.BlockSpec(memory_space=pl.ANY),
        scratch_shapes=[pltpu.VMEM((2, ROW_BLOCK, d), f32), pltpu.VMEM((2, ROW_BLOCK, d), f32),
                        pltpu.SemaphoreType.DMA((2,)), pltpu.SemaphoreType.DMA((2,))],
    )
    return pl.pallas_call(
        functools.partial(_experts_kernel, n_tok=n_tok),
        grid_spec=grid_spec,
        out_shape=jax.ShapeDtypeStruct((2 * n_tok + 2 * ROW_BLOCK, d), f32),
        compiler_params=_cparams(("arbitrary",)),
        name="experts",
    )(block_expert, row_pack, n_used, xn, wg_bf, wu_bf, wd_bf)


def _combine_kernel(h_ref, y0_ref, y1_ref, gate_ref, g_ref, o_ref):
    gate = gate_ref[...]
    hh = h_ref[...] + gate[:, 0:1] * y0_ref[...] + gate[:, 1:2] * y1_ref[...]
    ms = jnp.mean(hh * hh, axis=-1, keepdims=True)
    o_ref[...] = hh * lax.rsqrt(ms + EPS) * g_ref[...]


def _combine(h, yt, gate, g_final, tm):
    t, d = h.shape
    nt = t // tm
    return pl.pallas_call(
        _combine_kernel,
        grid=(nt,),
        in_specs=[
            pl.BlockSpec((tm, d), lambda i: (i, 0)),
            pl.BlockSpec((tm, d), lambda i: (i, 0)),
            pl.BlockSpec((tm, d), lambda i: (nt + i, 0)),
            pl.BlockSpec((tm, LANES), lambda i: (i, 0)),
            pl.BlockSpec((1, d), lambda i: (0, 0)),
        ],
        out_specs=pl.BlockSpec((tm, d), lambda i: (i, 0)),
        out_shape=jax.ShapeDtypeStruct((t, d), f32),
        compiler_params=_cparams(("parallel",)),
        name="combine",
    )(h, yt, yt, gate, g_final)


def _route_kernel(eid_ref, cnt_ref, rpack_ref, bexp_ref, nused_ref, *cur_refs, n_tok, n_blocks):
    tok_bits = (n_tok - 1).bit_length()
    chunk = 2 * n_tok // ROUTE_CHAINS

    def per_expert(e, blk):
        start = blk * ROW_BLOCK
        run = start
        for c in range(ROUTE_CHAINS):
            cur_refs[c][e] = run
            run = run + cnt_ref[c * N_EXPERTS + e]
        nb = (run - start + ROW_BLOCK - 1) // ROW_BLOCK

        def set_block(k, carry):
            bexp_ref[blk + k] = e
            return carry

        lax.fori_loop(0, nb, set_block, 0)

        def set_pad(r, carry):
            rpack_ref[r] = (2 * n_tok + (r & (2 * ROW_BLOCK - 1))) << tok_bits
            return carry

        lax.fori_loop(run, start + nb * ROW_BLOCK, set_pad, 0)
        return blk + nb

    n_used = lax.fori_loop(0, N_EXPERTS, per_expert, 0)
    nused_ref[0] = n_used

    def tail_block(k, carry):
        bexp_ref[k] = N_EXPERTS - 1
        return carry

    lax.fori_loop(n_used, n_blocks, tail_block, 0)

    def tail_row(r, carry):
        rpack_ref[r] = (2 * n_tok + (r & (2 * ROW_BLOCK - 1))) << tok_bits
        return carry

    lax.fori_loop(n_used * ROW_BLOCK, n_blocks * ROW_BLOCK, tail_row, 0)

    def place(i, carry):
        for c in range(ROUTE_CHAINS):
            a = c * chunk + i
            e = eid_ref[a]
            p = cur_refs[c][e]
            cur_refs[c][e] = p + 1
            rpack_ref[p] = (a << tok_bits) | (a - (c * chunk // n_tok) * n_tok)
        return carry

    lax.fori_loop(0, chunk, place, 0)


def _route(eid_flat, counts, n_tok):
    n_asg = eid_flat.shape[0]
    n_rows = -(-(n_asg + N_EXPERTS * (ROW_BLOCK - 1)) // ROW_BLOCK) * ROW_BLOCK
    n_blocks = n_rows // ROW_BLOCK
    smem = pl.BlockSpec(memory_space=pltpu.SMEM)
    return pl.pallas_call(
        functools.partial(_route_kernel, n_tok=n_tok, n_blocks=n_blocks),
        in_specs=[smem, smem],
        out_specs=[smem, smem, smem],
        out_shape=[jax.ShapeDtypeStruct((n_rows,), jnp.int32),
                   jax.ShapeDtypeStruct((n_blocks,), jnp.int32),
                   jax.ShapeDtypeStruct((1,), jnp.int32)],
        scratch_shapes=[pltpu.SMEM((N_EXPERTS,), jnp.int32)] * ROUTE_CHAINS,
        name="route",
    )(eid_flat, counts)


def _pick(n, prefs):
    for p in prefs:
        if n % p == 0:
            return p
    return n


def kernel(x, g_mix_norm, w_in, g_kv, w_uv, g_ret, w_branch, w_out, g_ffn_norm, w_router_group,
           b_router_group, w_router_expert, b_router_expert, w_expert_gate, w_expert_up,
           w_expert_down, g_final):
    b, s, d = x.shape
    t = b * s
    depth = w_in.shape[0]
    n_sel = min(TOPK_MAX, s // 4)
    assert s % RET_CHUNK == 0 and s % Q_TILE == 0

    cos_t, sin_t, d_in, d_q, d_k, d_c = _retention_tables(s)
    h2 = x.reshape(t, d)
    for l in range(depth):
        wl = w_in[l]
        sp = np.cumsum([0, H_A * D_LATENT, D_LATENT, H_IDX * D_IDX, D_IDX, H_IDX,
                        H_R * DK_R, H_R * DK_R, H_R * DV_R, H_R * DV_R, N_BRANCH * d])
        seg = [wl[:, sp[k]:sp[k + 1]] for k in range(10)]
        kw_pad = jnp.zeros((d, LANES - D_IDX - H_IDX), wl.dtype)
        w_p = jnp.concatenate([seg[0], seg[9], seg[5], seg[6], seg[7], seg[8], seg[2], seg[1],
                               seg[3], seg[4], kw_pad], axis=1).astype(bf16)
        assert w_p.shape[1] == D_IN_P

        proj = _proj(h2, g_mix_norm[l].reshape(1, d), w_p, _pick(t, (1024, 512, 256)), 768)
        proj3 = proj.reshape(b, s, D_IN_P)

        o_a = _attn(proj3, g_kv[l].reshape(1, D_LATENT), w_uv[l].astype(bf16), n_sel)
        o_b = _ret(proj3, cos_t, sin_t, d_in, d_q, d_k, d_c, g_ret[l].reshape(1, H_R * DV_R))

        mixed = _mix(o_a.reshape(t, D_BRANCH), o_b.reshape(t, D_BRANCH), w_branch[l].astype(bf16),
                     proj, _pick(t, (1024, 512, 256)), 512)

        w_r = jnp.concatenate([w_router_group[l], w_router_expert[l],
                               jnp.zeros((d, LANES - N_GROUPS - N_EXPERTS), f32)], axis=1)
        b_r = jnp.concatenate([b_router_group[l], b_router_expert[l],
                               jnp.zeros((LANES - N_GROUPS - N_EXPERTS,), f32)]).reshape(1, LANES)
        wr_hi = w_r.astype(bf16)
        wr_lo = (w_r - wr_hi.astype(f32)).astype(bf16)
        h2, xn, eid_t, gate, cnt = _outproj(mixed, h2, w_out[l].astype(bf16),
                                            g_ffn_norm[l].reshape(1, d), wr_hi, wr_lo, b_r, _pick(t, (256,)))

        row_pack, block_expert, n_used = _route(
            eid_t[:2].reshape(-1), cnt[:ROUTE_CHAINS, N_GROUPS:N_GROUPS + N_EXPERTS].reshape(-1), t)
        yt = _experts(block_expert, row_pack, n_used, xn, w_expert_gate[l].astype(bf16),
                      w_expert_up[l].astype(bf16), w_expert_down[l].astype(bf16))
        assert depth == 1
        h2 = _combine(h2, yt, gate, g_final.reshape(1, d), _pick(t, (256,)))
    return h2.reshape(b, s, d)
```

```python
import functools

import jax
import jax.numpy as jnp
import numpy as np
from jax import lax
from jax.experimental import pallas as pl
from jax.experimental.pallas import tpu as pltpu

EPS = 1e-6
CHUNK = 64
H_A = 8
D_LATENT = 128
DH_A = 128
H_IDX = 8
D_IDX = 64
TOPK_MAX = 256
H_R = 8
DK_R = 128
DV_R = 128
ROPE_BASE = 10000.0
D_BRANCH = 1024
N_BRANCH = 2
N_GROUPS = 4
EXP_PER_GROUP = 8
N_EXPERTS = N_GROUPS * EXP_PER_GROUP
D_EXPERT = 1024

LANES = 128
KEY_TILE = 256
Q_TILE = 128
RET_CHUNK = 256
ROW_BLOCK = 256
ROUTE_HALVES = 2
ROUTE_CHAINS = 2 * ROUTE_HALVES
VMEM_LIMIT = 56 * 1024 * 1024

C_QLAT = 0
C_GBR = 1024
C_QR = 5120
C_KR = 6144
C_VR = 7168
C_GR = 8192
C_QIDX = 9216
C_CKV = 9728
C_KW = 9856
D_IN_P = 9984

INT_MIN = np.int32(-2 ** 31)
NEG_BIG = -1e30

bf16 = jnp.bfloat16
f32 = jnp.float32


def _cparams(sem):
    return pltpu.CompilerParams(dimension_semantics=sem, vmem_limit_bytes=VMEM_LIMIT)


def _proj_kernel(x_ref, g_ref, w_ref, o_ref, xn_ref):
    @pl.when(pl.program_id(1) == 0)
    def _():
        x = x_ref[...]
        ms = jnp.mean(x * x, axis=-1, keepdims=True)
        xn_ref[...] = (x * lax.rsqrt(ms + EPS) * g_ref[...]).astype(bf16)

    o_ref[...] = jnp.dot(xn_ref[...], w_ref[...], preferred_element_type=f32).astype(o_ref.dtype)


def _proj(x2, g, w_p, tm, tn):
    t, d = x2.shape
    n = w_p.shape[1]
    return pl.pallas_call(
        _proj_kernel,
        grid=(t // tm, n // tn),
        in_specs=[
            pl.BlockSpec((tm, d), lambda i, j: (i, 0)),
            pl.BlockSpec((1, d), lambda i, j: (0, 0)),
            pl.BlockSpec((d, tn), lambda i, j: (0, j)),
        ],
        out_specs=pl.BlockSpec((tm, tn), lambda i, j: (i, j)),
        out_shape=jax.ShapeDtypeStruct((t, n), bf16),
        scratch_shapes=[pltpu.VMEM((tm, d), bf16)],
        compiler_params=_cparams(("parallel", "arbitrary")),
        name="proj",
    )(x2, g, w_p)


def _float_key(s):
    bits = pltpu.bitcast(s, jnp.int32)
    key = bits ^ ((bits >> 31) & jnp.int32(0x7FFFFFFF))
    return jnp.where(s == 0.0, jnp.int32(0), key)


def _attn_kernel(qlat_ref, qidx_ref, kwq_ref, ckv_ref, kwk_ref, gkv_ref, wuv_ref, o_ref,
                 kv_s, kvT_s, kidx_s, key_s, bias_s, qT_s, acc_s, *, n_sel, n_kt):
    i = pl.program_id(1)
    idx_scale = (H_IDX ** -0.5) * (D_IDX ** -0.5)
    attn_scale = D_LATENT ** -0.5
    hq = H_A * Q_TILE

    @pl.when(i == 0)
    def _():
        g = gkv_ref[...]
        for t in range(n_kt):
            c = ckv_ref[t * KEY_TILE:(t + 1) * KEY_TILE, :].astype(f32)
            ms = jnp.mean(c * c, axis=-1, keepdims=True)
            kv = c * lax.rsqrt(ms + EPS) * g
            kv_s[t] = kv.astype(bf16)
            kvT_s[t] = kv.T.astype(bf16)
            kidx_s[t] = kwk_ref[t * KEY_TILE:(t + 1) * KEY_TILE, :D_IDX]

    nk = ((i + 1) * Q_TILE + KEY_TILE - 1) // KEY_TILE
    lane = lax.broadcasted_iota(jnp.int32, (1, Q_TILE), 1)
    sub = lax.broadcasted_iota(jnp.int32, (KEY_TILE, 1), 0)
    q_chunk = (i * Q_TILE + lane) // CHUNK

    wT = kwq_ref[...].astype(f32).T
    qidx = qidx_ref[...]
    for h in range(H_A):
        qT_s[:, h * Q_TILE:(h + 1) * Q_TILE] = qlat_ref[:, h * D_LATENT:(h + 1) * D_LATENT].astype(f32).T.astype(bf16)

    def score_tile(t, carry):
        ks = kidx_s[t]
        acc = jnp.zeros((KEY_TILE, Q_TILE), f32)
        for h in range(H_IDX):
            d = lax.dot_general(ks, qidx[:, h * D_IDX:(h + 1) * D_IDX],
                                (((1,), (1,)), ((), ())), preferred_element_type=f32)
            acc = acc + wT[D_IDX + h:D_IDX + h + 1, :] * jnp.maximum(d, 0.0)
        score = acc * idx_scale
        k_chunk = (t * KEY_TILE + sub) // CHUNK
        key_s[t] = jnp.where(k_chunk <= q_chunk, _float_key(score), INT_MIN)
        return carry

    lax.fori_loop(0, nk, score_tile, 0)

    def count(pred):
        def body(t, c):
            m = pred(key_s[t], t).astype(jnp.int32)
            return c + jnp.sum(m.reshape(KEY_TILE // 8, 8, Q_TILE), axis=0)
        c8 = lax.fori_loop(0, nk, body, jnp.zeros((8, Q_TILE), jnp.int32))
        return jnp.sum(c8, axis=0, keepdims=True)

    thr0 = jnp.where(count(lambda k, t: k >= 0) >= n_sel, jnp.int32(0), INT_MIN)
    thr0 = jnp.broadcast_to(thr0, (1, Q_TILE)).astype(jnp.int32)

    def bit_step(j, thr):
        cand = thr | (jnp.int32(1) << (jnp.int32(30) - j))
        return jnp.where(count(lambda k, t: k >= cand) >= n_sel, cand, thr)

    thr = lax.fori_loop(0, 31, bit_step, thr0)

    c_gt = count(lambda k, t: k > thr)
    c_ge = count(lambda k, t: k >= thr)
    need = n_sel - c_gt
    has_tie = jnp.max(jnp.where((c_ge > n_sel) & (thr > INT_MIN), 1, 0)) > 0

    def tie_limit():
        def step(j, m):
            cand = m | (jnp.int32(1) << (jnp.int32(14) - j))
            c = count(lambda k, t: (k == thr) & ((t * KEY_TILE + sub) < cand))
            return jnp.where(c < need, cand, m)
        return lax.fori_loop(0, 15, step, jnp.zeros((1, Q_TILE), jnp.int32))

    m_lim = lax.cond(has_tie, tie_limit, lambda: jnp.full((1, Q_TILE), 2 ** 30, jnp.int32))

    def bias_tile(t, carry):
        k = key_s[t]
        sel = (k > thr) | ((k == thr) & ((t * KEY_TILE + sub) <= m_lim))
        sel = sel & (k > INT_MIN)
        bias_s[t] = jnp.where(sel, 0.0, NEG_BIG).astype(f32)
        return carry

    lax.fori_loop(0, nk, bias_tile, 0)

    acc_s[...] = jnp.zeros_like(acc_s)

    def att_tile(t, carry):
        m_run, l_run = carry
        logit = jnp.dot(kv_s[t], qT_s[...], preferred_element_type=f32) * attn_scale
        logit = logit + jnp.concatenate([bias_s[t]] * H_A, axis=1)
        m_new = jnp.maximum(m_run, jnp.max(logit, axis=0, keepdims=True))
        alpha = jnp.exp(m_run - m_new)
        p = jnp.exp(logit - m_new)
        l_new = alpha * l_run + jnp.sum(p, axis=0, keepdims=True)
        acc_s[...] = alpha * acc_s[...] + jnp.dot(kvT_s[t], p.astype(bf16), preferred_element_type=f32)
        return m_new, l_new

    init = (jnp.full((1, hq), NEG_BIG, f32), jnp.zeros((1, hq), f32))
    _, l_fin = lax.fori_loop(0, nk, att_tile, init)
    inv_l = 1.0 / l_fin
    for h in range(H_A):
        sl = slice(h * Q_TILE, (h + 1) * Q_TILE)
        o_lat = (acc_s[:, sl] * inv_l[:, sl]).T
        o_ref[:, h * DH_A:(h + 1) * DH_A] = jnp.dot(
            o_lat.astype(bf16), wuv_ref[h], preferred_element_type=f32).astype(o_ref.dtype)


def _attn(proj3, g_kv, w_uv_bf, n_sel):
    b, s, _ = proj3.shape
    n_kt = s // KEY_TILE
    kern = functools.partial(_attn_kernel, n_sel=n_sel, n_kt=n_kt)
    return pl.pallas_call(
        kern,
        grid=(b, s // Q_TILE),
        in_specs=[
            pl.BlockSpec((None, Q_TILE, H_A * D_LATENT), lambda bi, i: (bi, i, C_QLAT // 1024)),
            pl.BlockSpec((None, Q_TILE, H_IDX * D_IDX), lambda bi, i: (bi, i, C_QIDX // 512)),
            pl.BlockSpec((None, Q_TILE, LANES), lambda bi, i: (bi, i, C_KW // LANES)),
            pl.BlockSpec((None, s, LANES), lambda bi, i: (bi, 0, C_CKV // LANES)),
            pl.BlockSpec((None, s, LANES), lambda bi, i: (bi, 0, C_KW // LANES)),
            pl.BlockSpec((1, D_LATENT), lambda bi, i: (0, 0)),
            pl.BlockSpec((H_A, D_LATENT, DH_A), lambda bi, i: (0, 0, 0)),
        ],
        out_specs=pl.BlockSpec((None, Q_TILE, D_BRANCH), lambda bi, i: (bi, i, 0)),
        out_shape=jax.ShapeDtypeStruct((b, s, D_BRANCH), bf16),
        scratch_shapes=[
            pltpu.VMEM((n_kt, KEY_TILE, D_LATENT), bf16),
            pltpu.VMEM((n_kt, D_LATENT, KEY_TILE), bf16),
            pltpu.VMEM((n_kt, KEY_TILE, D_IDX), bf16),
            pltpu.VMEM((n_kt, KEY_TILE, Q_TILE), jnp.int32),
            pltpu.VMEM((n_kt, KEY_TILE, Q_TILE), f32),
            pltpu.VMEM((D_LATENT, H_A * Q_TILE), bf16),
            pltpu.VMEM((D_LATENT, H_A * Q_TILE), f32),
        ],
        compiler_params=_cparams(("parallel", "arbitrary")),
        name="attn",
    )(proj3, proj3, proj3, proj3, proj3, g_kv, w_uv_bf)


def _ret_kernel(q_ref, k_ref, v_ref, gr_ref, cos_ref, sin_ref, din_ref, dq_ref, dk_ref, dc_ref,
                gret_ref, o_ref, state_s):
    @pl.when(pl.program_id(1) == 0)
    def _():
        state_s[...] = jnp.zeros_like(state_s)

    cos = cos_ref[...]
    sin = sin_ref[...]

    def rot(x):
        return x * cos + pltpu.roll(x, DK_R // 2, axis=1) * sin

    for h in range(H_R):
        sl = slice(h * DK_R, (h + 1) * DK_R)
        q = rot(q_ref[:, sl].astype(f32)).astype(bf16)
        kf = rot(k_ref[:, sl].astype(f32)) * (DK_R ** -0.5)
        k = kf.astype(bf16)
        v = v_ref[:, sl]
        inner = lax.dot_general(q, k, (((1,), (1,)), ((), ())), preferred_element_type=f32) * din_ref[h]
        o = jnp.dot(inner.astype(bf16), v, preferred_element_type=f32)
        st = state_s[h]
        o = o + jnp.dot(q, st.astype(bf16), preferred_element_type=f32) * dq_ref[h]
        kd = (kf * dk_ref[h]).astype(bf16)
        state_s[h] = st * dc_ref[h] + jnp.dot(kd.T, v, preferred_element_type=f32)
        mu = jnp.mean(o, axis=-1, keepdims=True)
        var = jnp.mean(jnp.square(o - mu), axis=-1, keepdims=True)
        y = (o - mu) * lax.rsqrt(var + EPS) * gret_ref[:, sl]
        gate = gr_ref[:, sl].astype(f32)
        o_ref[:, sl] = (gate * jax.nn.sigmoid(gate) * y).astype(o_ref.dtype)


def _ret(proj3, cos_t, sin_t, d_in, d_q, d_k, d_c, g_ret):
    b, s, _ = proj3.shape
    c = RET_CHUNK
    w = H_R * DK_R

    def col(off):
        return pl.BlockSpec((None, c, w), lambda bi, ci: (bi, ci, off // w))

    return pl.pallas_call(
        _ret_kernel,
        grid=(b, s // c),
        in_specs=[
            col(C_QR), col(C_KR), col(C_VR), col(C_GR),
            pl.BlockSpec((c, DK_R), lambda bi, ci: (ci, 0)),
            pl.BlockSpec((c, DK_R), lambda bi, ci: (ci, 0)),
            pl.BlockSpec((H_R, c, c), lambda bi, ci: (0, 0, 0)),
            pl.BlockSpec((H_R, c, DK_R), lambda bi, ci: (0, 0, 0)),
            pl.BlockSpec((H_R, c, DK_R), lambda bi, ci: (0, 0, 0)),
            pl.BlockSpec((H_R, 1, DK_R), lambda bi, ci: (0, 0, 0)),
            pl.BlockSpec((1, w), lambda bi, ci: (0, 0)),
        ],
        out_specs=pl.BlockSpec((None, c, w), lambda bi, ci: (bi, ci, 0)),
        out_shape=jax.ShapeDtypeStruct((b, s, w), bf16),
        scratch_shapes=[pltpu.VMEM((H_R, DK_R, DV_R), f32)],
        compiler_params=_cparams(("parallel", "arbitrary")),
        name="ret",
    )(proj3, proj3, proj3, proj3, cos_t, sin_t, d_in, d_q, d_k, d_c, g_ret)


def _retention_tables(s):
    c = RET_CHUNK
    half = DK_R // 2
    freq = ROPE_BASE ** (-jnp.arange(half, dtype=f32) / half)
    ang = jnp.arange(s, dtype=f32)[:, None] * freq[None, :]
    cos = jnp.cos(ang)
    sin = jnp.sin(ang)
    cos_t = jnp.concatenate([cos, cos], axis=-1)
    sin_t = jnp.concatenate([-sin, sin], axis=-1)
    log_gamma = jnp.log1p(-jnp.exp2(-5.0 - jnp.arange(H_R, dtype=f32)))
    n = jnp.arange(c, dtype=f32)
    diff = n[:, None] - n[None, :]
    d_in = jnp.where(diff >= 0, jnp.exp(log_gamma[:, None, None] * jnp.maximum(diff, 0.0)), 0.0)
    d_q = jnp.broadcast_to(jnp.exp(log_gamma[:, None] * (n + 1.0))[:, :, None], (H_R, c, DK_R))
    d_k = jnp.broadcast_to(jnp.exp(log_gamma[:, None] * (c - 1.0 - n))[:, :, None], (H_R, c, DK_R))
    d_c = jnp.broadcast_to(jnp.exp(log_gamma * c)[:, None, None], (H_R, 1, DK_R))
    return cos_t, sin_t, d_in, d_q, d_k, d_c


def _mix_kernel(oa_ref, ob_ref, wb_ref, ga_ref, gb_ref, o_ref):
    a = jnp.dot(oa_ref[...], wb_ref[0], preferred_element_type=f32)
    b = jnp.dot(ob_ref[...], wb_ref[1], preferred_element_type=f32)
    ga = jax.nn.sigmoid(ga_ref[...].astype(f32))
    gb = jax.nn.sigmoid(gb_ref[...].astype(f32))
    o_ref[...] = (ga * a + gb * b).astype(o_ref.dtype)


def _mix(o_a, o_b, w_branch_bf, proj, tm, tn):
    t = o_a.shape[0]
    d = w_branch_bf.shape[2]
    return pl.pallas_call(
        _mix_kernel,
        grid=(t // tm, d // tn),
        in_specs=[
            pl.BlockSpec((tm, D_BRANCH), lambda i, j: (i, 0)),
            pl.BlockSpec((tm, D_BRANCH), lambda i, j: (i, 0)),
            pl.BlockSpec((N_BRANCH, D_BRANCH, tn), lambda i, j: (0, 0, j)),
            pl.BlockSpec((tm, tn), lambda i, j: (i, C_GBR // tn + j)),
            pl.BlockSpec((tm, tn), lambda i, j: (i, (C_GBR + d) // tn + j)),
        ],
        out_specs=pl.BlockSpec((tm, tn), lambda i, j: (i, j)),
        out_shape=jax.ShapeDtypeStruct((t, d), bf16),
        compiler_params=_cparams(("parallel", "parallel")),
        name="mix",
    )(o_a, o_b, w_branch_bf, proj, proj)


def _pack_rows(v):
    n = v.shape[1] // 2
    r = pltpu.bitcast(v.astype(bf16).astype(f32), jnp.uint32)
    w = (r[:, :n] >> 16) | (r[:, n:] & jnp.uint32(0xFFFF0000))
    return pltpu.einshape("r(ab)->rab", w, b=LANES)


def _unpack_rows(p):
    w = pltpu.einshape("rab->r(ab)", p)
    lo = pltpu.bitcast(w << 16, f32)
    hi = pltpu.bitcast(w & jnp.uint32(0xFFFF0000), f32)
    return lo, hi


def _split_bf16(a):
    hi = a.astype(bf16)
    lo = (a - hi.astype(f32)).astype(bf16)
    return hi, lo


def _outproj_kernel(mixed_ref, x_ref, wo_ref, g_ref, wr_hi_ref, wr_lo_ref, br_ref,
                    h_ref, xn_ref, eid_ref, gate_ref, cnt_ref):
    h = x_ref[...] + jnp.dot(mixed_ref[...], wo_ref[...], preferred_element_type=f32)
    h_ref[...] = h
    ms = jnp.mean(h * h, axis=-1, keepdims=True)
    xn = h * lax.rsqrt(ms + EPS) * g_ref[...]
    xn_ref[...] = _pack_rows(xn)

    x_hi, x_lo = _split_bf16(xn)
    logit = (jnp.dot(x_hi, wr_hi_ref[...], preferred_element_type=f32)
             + jnp.dot(x_hi, wr_lo_ref[...], preferred_element_type=f32)
             + jnp.dot(x_lo, wr_hi_ref[...], preferred_element_type=f32)) + br_ref[...]

    lane = lax.broadcasted_iota(jnp.int32, logit.shape, 1)
    lanef = lane.astype(f32)
    neg = -jnp.inf

    def first_argmax(v, m):
        return jnp.min(jnp.where(v == m, lanef, float(LANES)), axis=-1, keepdims=True)

    lg = jnp.where(lane < N_GROUPS, logit, neg)
    mg = jnp.max(lg, axis=-1, keepdims=True)
    p_grp = 1.0 / jnp.sum(jnp.exp(lg - mg), axis=-1, keepdims=True)
    grp = first_argmax(lg, mg).astype(jnp.int32)

    e_lane = lane - N_GROUPS
    in_grp = (e_lane >= 0) & (e_lane < N_EXPERTS) & ((e_lane // EXP_PER_GROUP) == grp)
    le = jnp.where(in_grp, logit, neg)
    m1 = jnp.max(le, axis=-1, keepdims=True)
    i1 = first_argmax(le, m1)
    le2 = jnp.where(lanef == i1, neg, le)
    m2 = jnp.max(le2, axis=-1, keepdims=True)
    i2 = first_argmax(le2, m2)
    e2 = jnp.exp(m2 - m1)
    g1 = p_grp / (1.0 + e2)
    g2 = p_grp * e2 / (1.0 + e2)

    eid = jnp.where(lane == 0, i1, jnp.where(lane == 1, i2, float(N_GROUPS))) - float(N_GROUPS)
    eid_ref[...] = eid.astype(jnp.int32).T[:8, :]
    gate_ref[...] = jnp.where(lane == 0, g1, jnp.where(lane == 1, g2, 0.0))

    @pl.when(pl.program_id(0) == 0)
    def _():
        cnt_ref[...] = jnp.zeros_like(cnt_ref)

    half = pl.program_id(0) // (pl.num_programs(0) // ROUTE_HALVES)
    sub8 = lax.broadcasted_iota(jnp.int32, (8, LANES), 0)
    for s, idx in enumerate((i1, i2)):
        c = jnp.sum((lanef == idx).astype(jnp.int32), axis=0, keepdims=True)
        cnt_ref[...] += jnp.where(sub8 == s * ROUTE_HALVES + half, c, 0)


def _outproj(mixed, x2, w_out_bf, g_ffn, wr_hi, wr_lo, b_r, tm):
    t, d = x2.shape
    row = lambda i: (i, 0)
    fixed = lambda i: (0, 0)
    return pl.pallas_call(
        _outproj_kernel,
        grid=(t // tm,),
        in_specs=[
            pl.BlockSpec((tm, d), row),
            pl.BlockSpec((tm, d), row),
            pl.BlockSpec((d, d), fixed),
            pl.BlockSpec((1, d), fixed),
            pl.BlockSpec((d, LANES), fixed),
            pl.BlockSpec((d, LANES), fixed),
            pl.BlockSpec((1, LANES), fixed),
        ],
        out_specs=[
            pl.BlockSpec((tm, d), row),
            pl.BlockSpec((tm, d // (2 * LANES), LANES), lambda i: (i, 0, 0)),
            pl.BlockSpec((8, tm), lambda i: (0, i)),
            pl.BlockSpec((tm, LANES), row),
            pl.BlockSpec((8, LANES), fixed),
        ],
        out_shape=[
            jax.ShapeDtypeStruct((t, d), f32),
            jax.ShapeDtypeStruct((t, d // (2 * LANES), LANES), jnp.uint32),
            jax.ShapeDtypeStruct((8, t), jnp.int32),
            jax.ShapeDtypeStruct((t, LANES), f32),
            jax.ShapeDtypeStruct((8, LANES), jnp.int32),
        ],
        compiler_params=_cparams(("arbitrary",)),
        name="outproj",
    )(mixed, x2, w_out_bf, g_ffn, wr_hi, wr_lo, b_r)


ISSUE_UNROLL = 8

def _experts_kernel(bexp_ref, rpack_ref, nused_ref, xn_hbm, wg_ref, wu_ref, wd_ref, yt_hbm,
                    xbuf, ybuf, gsem, ssem, *, n_tok):
    j = pl.program_id(0)
    n_used = nused_ref[0]
    slot = j % 2
    tok_bits = (n_tok - 1).bit_length()

    def rows_of(blk, fn):
        base = blk * ROW_BLOCK

        def body(k, c):
            r0 = pl.multiple_of(k * ISSUE_UNROLL, ISSUE_UNROLL)
            for u in range(ISSUE_UNROLL):
                fn(r0 + u, rpack_ref[base + r0 + u], u % 2)
            return c

        lax.fori_loop(0, ROW_BLOCK // ISSUE_UNROLL, body, 0)

    def start_gathers(blk, sl):
        def one(r, packed, queue):
            tok = packed & ((1 << tok_bits) - 1)
            pltpu.make_async_copy(xn_hbm.at[tok], xbuf.at[sl, r], gsem.at[sl]).start(priority=queue)
        rows_of(blk, one)

    def start_scatters(blk, sl):
        def one(r, packed, queue):
            row = lax.shift_right_logical(packed, tok_bits)
            pltpu.make_async_copy(ybuf.at[sl, r], yt_hbm.at[row], ssem.at[sl]).start(priority=queue)
        rows_of(blk, one)

    def wait_gathers(sl):
        pltpu.make_async_copy(xn_hbm.at[pl.ds(0, ROW_BLOCK)], xbuf.at[sl], gsem.at[sl]).wait()

    def wait_scatters(sl):
        pltpu.make_async_copy(ybuf.at[sl], yt_hbm.at[pl.ds(0, ROW_BLOCK)], ssem.at[sl]).wait()

    @pl.when(j == 0)
    def _():
        start_gathers(0, 0)
        ybuf[...] = jnp.zeros_like(ybuf)
        for sl in range(2):
            spare = yt_hbm.at[pl.ds(2 * n_tok + sl * ROW_BLOCK, ROW_BLOCK)]
            pltpu.make_async_copy(ybuf.at[sl], spare, ssem.at[sl]).start()
        for sl in range(2):
            wait_scatters(sl)

    @pl.when(j < n_used)
    def _():
        wait_gathers(slot)

        @pl.when(j + 1 < n_used)
        def _():
            start_gathers(j + 1, 1 - slot)

        @pl.when(j >= 2)
        def _():
            wait_scatters(slot)

        lo, hi = _unpack_rows(xbuf[slot])
        xb = jnp.concatenate([lo, hi], axis=1).astype(bf16)
        g = jnp.dot(xb, wg_ref[...], preferred_element_type=f32)
        u = jnp.dot(xb, wu_ref[...], preferred_element_type=f32)
        hm = (g * jax.nn.sigmoid(g) * u).astype(bf16)
        ybuf[slot] = _pack_rows(jnp.dot(hm, wd_ref[...], preferred_element_type=f32))
        start_scatters(j, slot)

        @pl.when(j == n_used - 1)
        def _():
            wait_scatters(slot)

            @pl.when(j >= 1)
            def _():
                wait_scatters(1 - slot)


def _experts(block_expert, row_pack, n_used, xn_packed, wg_bf, wu_bf, wd_bf):
    n_rows = row_pack.shape[0]
    n_tok = xn_packed.shape[0]
    tile = xn_packed.shape[1:]
    _, d, f = wg_bf.shape
    grid_spec = pltpu.PrefetchScalarGridSpec(
        num_scalar_prefetch=3,
        grid=(n_rows // ROW_BLOCK,),
        in_specs=[
            pl.BlockSpec(memory_space=pl.ANY),
            pl.BlockSpec((None, d, f), lambda i, be, rd, nu: (be[i], 0, 0)),
            pl.BlockSpec((None, d, f), lambda i, be, rd, nu: (be[i], 0, 0)),
            pl.BlockSpec((None, f, d), lambda i, be, rd, nu: (be[i], 0, 0)),
        ],
        out_specs=pl.BlockSpec(memory_space=pl.ANY),
        scratch_shapes=[pltpu.VMEM((2, ROW_BLOCK) + tile, jnp.uint32),
                        pltpu.VMEM((2, ROW_BLOCK) + tile, jnp.uint32),
                        pltpu.SemaphoreType.DMA((2,)), pltpu.SemaphoreType.DMA((2,))],
    )
    return pl.pallas_call(
        functools.partial(_experts_kernel, n_tok=n_tok),
        grid_spec=grid_spec,
        out_shape=jax.ShapeDtypeStruct((2 * n_tok + 2 * ROW_BLOCK,) + tile, jnp.uint32),
        compiler_params=_cparams(("arbitrary",)),
        name="experts",
    )(block_expert, row_pack, n_used, xn_packed, wg_bf, wu_bf, wd_bf)


def _combine_kernel(h_ref, y0_ref, y1_ref, gate_ref, g_ref, o_ref):
    gate = gate_ref[...]
    y0 = jnp.concatenate(_unpack_rows(y0_ref[...]), axis=1)
    y1 = jnp.concatenate(_unpack_rows(y1_ref[...]), axis=1)
    hh = h_ref[...] + gate[:, 0:1] * y0 + gate[:, 1:2] * y1
    ms = jnp.mean(hh * hh, axis=-1, keepdims=True)
    o_ref[...] = hh * lax.rsqrt(ms + EPS) * g_ref[...]


def _combine(h, yt, gate, g_final, tm):
    t, d = h.shape
    nt = t // tm
    return pl.pallas_call(
        _combine_kernel,
        grid=(nt,),
        in_specs=[
            pl.BlockSpec((tm, d), lambda i: (i, 0)),
            pl.BlockSpec((tm,) + yt.shape[1:], lambda i: (i, 0, 0)),
            pl.BlockSpec((tm,) + yt.shape[1:], lambda i: (nt + i, 0, 0)),
            pl.BlockSpec((tm, LANES), lambda i: (i, 0)),
            pl.BlockSpec((1, d), lambda i: (0, 0)),
        ],
        out_specs=pl.BlockSpec((tm, d), lambda i: (i, 0)),
        out_shape=jax.ShapeDtypeStruct((t, d), f32),
        compiler_params=_cparams(("parallel",)),
        name="combine",
    )(h, yt, yt, gate, g_final)


def _route_kernel(eid_ref, cnt_ref, rpack_ref, bexp_ref, nused_ref, *cur_refs, n_tok, n_blocks):
    tok_bits = (n_tok - 1).bit_length()
    chunk = 2 * n_tok // ROUTE_CHAINS

    def per_expert(e, blk):
        start = blk * ROW_BLOCK
        run = start
        for c in range(ROUTE_CHAINS):
            cur_refs[c][e] = run
            run = run + cnt_ref[c * N_EXPERTS + e]
        nb = (run - start + ROW_BLOCK - 1) // ROW_BLOCK

        def set_block(k, carry):
            bexp_ref[blk + k] = e
            return carry

        lax.fori_loop(0, nb, set_block, 0)

        def set_pad(r, carry):
            rpack_ref[r] = (2 * n_tok + (r & (2 * ROW_BLOCK - 1))) << tok_bits
            return carry

        lax.fori_loop(run, start + nb * ROW_BLOCK, set_pad, 0)
        return blk + nb

    n_used = lax.fori_loop(0, N_EXPERTS, per_expert, 0)
    nused_ref[0] = n_used

    def tail_block(k, carry):
        bexp_ref[k] = N_EXPERTS - 1
        return carry

    lax.fori_loop(n_used, n_blocks, tail_block, 0)

    def tail_row(r, carry):
        rpack_ref[r] = (2 * n_tok + (r & (2 * ROW_BLOCK - 1))) << tok_bits
        return carry

    lax.fori_loop(n_used * ROW_BLOCK, n_blocks * ROW_BLOCK, tail_row, 0)

    def place(i, carry):
        for c in range(ROUTE_CHAINS):
            a = c * chunk + i
            e = eid_ref[a]
            p = cur_refs[c][e]
            cur_refs[c][e] = p + 1
            rpack_ref[p] = (a << tok_bits) | (a - (c * chunk // n_tok) * n_tok)
        return carry

    lax.fori_loop(0, chunk, place, 0)


def _route(eid_flat, counts, n_tok):
    n_asg = eid_flat.shape[0]
    n_rows = -(-(n_asg + N_EXPERTS * (ROW_BLOCK - 1)) // ROW_BLOCK) * ROW_BLOCK
    n_blocks = n_rows // ROW_BLOCK
    smem = pl.BlockSpec(memory_space=pltpu.SMEM)
    return pl.pallas_call(
        functools.partial(_route_kernel, n_tok=n_tok, n_blocks=n_blocks),
        in_specs=[smem, smem],
        out_specs=[smem, smem, smem],
        out_shape=[jax.ShapeDtypeStruct((n_rows,), jnp.int32),
                   jax.ShapeDtypeStruct((n_blocks,), jnp.int32),
                   jax.ShapeDtypeStruct((1,), jnp.int32)],
        scratch_shapes=[pltpu.SMEM((N_EXPERTS,), jnp.int32)] * ROUTE_CHAINS,
        name="route",
    )(eid_flat, counts)


def _pick(n, prefs):
    for p in prefs:
        if n % p == 0:
            return p
    return n


def kernel(x, g_mix_norm, w_in, g_kv, w_uv, g_ret, w_branch, w_out, g_ffn_norm, w_router_group,
           b_router_group, w_router_expert, b_router_expert, w_expert_gate, w_expert_up,
           w_expert_down, g_final):
    b, s, d = x.shape
    t = b * s
    depth = w_in.shape[0]
    n_sel = min(TOPK_MAX, s // 4)
    assert s % RET_CHUNK == 0 and s % Q_TILE == 0

    cos_t, sin_t, d_in, d_q, d_k, d_c = _retention_tables(s)
    h2 = x.reshape(t, d)
    for l in range(depth):
        wl = w_in[l]
        sp = np.cumsum([0, H_A * D_LATENT, D_LATENT, H_IDX * D_IDX, D_IDX, H_IDX,
                        H_R * DK_R, H_R * DK_R, H_R * DV_R, H_R * DV_R, N_BRANCH * d])
        seg = [wl[:, sp[k]:sp[k + 1]] for k in range(10)]
        kw_pad = jnp.zeros((d, LANES - D_IDX - H_IDX), wl.dtype)
        w_p = jnp.concatenate([seg[0], seg[9], seg[5], seg[6], seg[7], seg[8], seg[2], seg[1],
                               seg[3], seg[4], kw_pad], axis=1).astype(bf16)
        assert w_p.shape[1] == D_IN_P

        proj = _proj(h2, g_mix_norm[l].reshape(1, d), w_p, _pick(t, (1024, 512, 256)), 768)
        proj3 = proj.reshape(b, s, D_IN_P)

        o_a = _attn(proj3, g_kv[l].reshape(1, D_LATENT), w_uv[l].astype(bf16), n_sel)
        o_b = _ret(proj3, cos_t, sin_t, d_in, d_q, d_k, d_c, g_ret[l].reshape(1, H_R * DV_R))

        mixed = _mix(o_a.reshape(t, D_BRANCH), o_b.reshape(t, D_BRANCH), w_branch[l].astype(bf16),
                     proj, _pick(t, (1024, 512, 256)), 512)

        w_r = jnp.concatenate([w_router_group[l], w_router_expert[l],
                               jnp.zeros((d, LANES - N_GROUPS - N_EXPERTS), f32)], axis=1)
        b_r = jnp.concatenate([b_router_group[l], b_router_expert[l],
                               jnp.zeros((LANES - N_GROUPS - N_EXPERTS,), f32)]).reshape(1, LANES)
        wr_hi = w_r.astype(bf16)
        wr_lo = (w_r - wr_hi.astype(f32)).astype(bf16)
        h2, xn, eid_t, gate, cnt = _outproj(mixed, h2, w_out[l].astype(bf16),
                                            g_ffn_norm[l].reshape(1, d), wr_hi, wr_lo, b_r, _pick(t, (256,)))

        row_pack, block_expert, n_used = _route(
            eid_t[:2].reshape(-1), cnt[:ROUTE_CHAINS, N_GROUPS:N_GROUPS + N_EXPERTS].reshape(-1), t)
        yt = _experts(block_expert, row_pack, n_used, xn, w_expert_gate[l].astype(bf16),
                      w_expert_up[l].astype(bf16), w_expert_down[l].astype(bf16))
        assert depth == 1
        h2 = _combine(h2, yt, gate, g_final.reshape(1, d), _pick(t, (256,)))
    return h2.reshape(b, s, d)
```

```python
import functools

import jax
import jax.numpy as jnp
import numpy as np
from jax import lax
from jax.experimental import pallas as pl
from jax.experimental.pallas import tpu as pltpu

EPS = 1e-6
CHUNK = 64
H_A = 8
D_LATENT = 128
DH_A = 128
H_IDX = 8
D_IDX = 64
TOPK_MAX = 256
H_R = 8
DK_R = 128
DV_R = 128
ROPE_BASE = 10000.0
D_BRANCH = 1024
N_BRANCH = 2
N_GROUPS = 4
EXP_PER_GROUP = 8
N_EXPERTS = N_GROUPS * EXP_PER_GROUP
D_EXPERT = 1024

LANES = 128
KEY_TILE = 256
Q_TILE = 128
RET_CHUNK = 256
ROW_BLOCK = 256
ROUTE_HALVES = 2
ROUTE_CHAINS = 2 * ROUTE_HALVES
VMEM_LIMIT = 56 * 1024 * 1024

C_QLAT = 0
C_GBR = 1024
C_QR = 5120
C_KR = 6144
C_VR = 7168
C_GR = 8192
C_QIDX = 9216
C_CKV = 9728
C_KW = 9856
D_IN_P = 9984

INT_MIN = np.int32(-2 ** 31)
NEG_BIG = -1e30

bf16 = jnp.bfloat16
f32 = jnp.float32


def _cparams(sem):
    return pltpu.CompilerParams(dimension_semantics=sem, vmem_limit_bytes=VMEM_LIMIT)


def _proj_kernel(x_ref, g_ref, w_ref, o_ref, xn_ref):
    @pl.when(pl.program_id(1) == 0)
    def _():
        x = x_ref[...]
        ms = jnp.mean(x * x, axis=-1, keepdims=True)
        xn_ref[...] = (x * lax.rsqrt(ms + EPS) * g_ref[...]).astype(bf16)

    o_ref[...] = jnp.dot(xn_ref[...], w_ref[...], preferred_element_type=f32).astype(o_ref.dtype)


def _proj(x2, g, w_p, tm, tn):
    t, d = x2.shape
    n = w_p.shape[1]
    return pl.pallas_call(
        _proj_kernel,
        grid=(t // tm, n // tn),
        in_specs=[
            pl.BlockSpec((tm, d), lambda i, j: (i, 0)),
            pl.BlockSpec((1, d), lambda i, j: (0, 0)),
            pl.BlockSpec((d, tn), lambda i, j: (0, j)),
        ],
        out_specs=pl.BlockSpec((tm, tn), lambda i, j: (i, j)),
        out_shape=jax.ShapeDtypeStruct((t, n), bf16),
        scratch_shapes=[pltpu.VMEM((tm, d), bf16)],
        compiler_params=_cparams(("parallel", "arbitrary")),
        name="proj",
    )(x2, g, w_p)


def _float_key(s):
    bits = pltpu.bitcast(s, jnp.int32)
    key = bits ^ ((bits >> 31) & jnp.int32(0x7FFFFFFF))
    return jnp.where(s == 0.0, jnp.int32(0), key)


def _attn_kernel(qlat_ref, qidx_ref, kwq_ref, ckv_ref, kwk_ref, gkv_ref, wuv_ref, o_ref,
                 kv_s, kvT_s, kidx_s, key_s, bias_s, qT_s, acc_s, *, n_sel, n_kt):
    i = pl.program_id(1)
    idx_scale = (H_IDX ** -0.5) * (D_IDX ** -0.5)
    attn_scale = D_LATENT ** -0.5
    hq = H_A * Q_TILE

    @pl.when(i == 0)
    def _():
        g = gkv_ref[...]
        for t in range(n_kt):
            c = ckv_ref[t * KEY_TILE:(t + 1) * KEY_TILE, :].astype(f32)
            ms = jnp.mean(c * c, axis=-1, keepdims=True)
            kv = c * lax.rsqrt(ms + EPS) * g
            kv_s[t] = kv.astype(bf16)
            kvT_s[t] = kv.T.astype(bf16)
            kidx_s[t] = kwk_ref[t * KEY_TILE:(t + 1) * KEY_TILE, :D_IDX]

    nk = ((i + 1) * Q_TILE + KEY_TILE - 1) // KEY_TILE
    lane = lax.broadcasted_iota(jnp.int32, (1, Q_TILE), 1)
    sub = lax.broadcasted_iota(jnp.int32, (KEY_TILE, 1), 0)
    q_chunk = (i * Q_TILE + lane) // CHUNK

    wT = kwq_ref[...].astype(f32).T
    qidx = qidx_ref[...]
    for h in range(H_A):
        qT_s[:, h * Q_TILE:(h + 1) * Q_TILE] = qlat_ref[:, h * D_LATENT:(h + 1) * D_LATENT].astype(f32).T.astype(bf16)

    def score_tile(t, carry):
        ks = kidx_s[t]
        acc = jnp.zeros((KEY_TILE, Q_TILE), f32)
        for h in range(H_IDX):
            d = lax.dot_general(ks, qidx[:, h * D_IDX:(h + 1) * D_IDX],
                                (((1,), (1,)), ((), ())), preferred_element_type=f32)
            acc = acc + wT[D_IDX + h:D_IDX + h + 1, :] * jnp.maximum(d, 0.0)
        score = acc * idx_scale
        k_chunk = (t * KEY_TILE + sub) // CHUNK
        key_s[t] = jnp.where(k_chunk <= q_chunk, _float_key(score), INT_MIN)
        return carry

    lax.fori_loop(0, nk, score_tile, 0)

    def count(pred):
        def body(t, c):
            m = pred(key_s[t], t).astype(jnp.int32)
            return c + jnp.sum(m.reshape(KEY_TILE // 8, 8, Q_TILE), axis=0)
        c8 = lax.fori_loop(0, nk, body, jnp.zeros((8, Q_TILE), jnp.int32))
        return jnp.sum(c8, axis=0, keepdims=True)

    thr0 = jnp.where(count(lambda k, t: k >= 0) >= n_sel, jnp.int32(0), INT_MIN)
    thr0 = jnp.broadcast_to(thr0, (1, Q_TILE)).astype(jnp.int32)

    def bit_step(j, thr):
        cand = thr | (jnp.int32(1) << (jnp.int32(30) - j))
        return jnp.where(count(lambda k, t: k >= cand) >= n_sel, cand, thr)

    thr = lax.fori_loop(0, 31, bit_step, thr0)

    c_gt = count(lambda k, t: k > thr)
    c_ge = count(lambda k, t: k >= thr)
    need = n_sel - c_gt
    has_tie = jnp.max(jnp.where((c_ge > n_sel) & (thr > INT_MIN), 1, 0)) > 0

    def tie_limit():
        def step(j, m):
            cand = m | (jnp.int32(1) << (jnp.int32(14) - j))
            c = count(lambda k, t: (k == thr) & ((t * KEY_TILE + sub) < cand))
            return jnp.where(c < need, cand, m)
        return lax.fori_loop(0, 15, step, jnp.zeros((1, Q_TILE), jnp.int32))

    m_lim = lax.cond(has_tie, tie_limit, lambda: jnp.full((1, Q_TILE), 2 ** 30, jnp.int32))

    def bias_tile(t, carry):
        k = key_s[t]
        sel = (k > thr) | ((k == thr) & ((t * KEY_TILE + sub) <= m_lim))
        sel = sel & (k > INT_MIN)
        bias_s[t] = jnp.where(sel, 0.0, NEG_BIG).astype(f32)
        return carry

    lax.fori_loop(0, nk, bias_tile, 0)

    acc_s[...] = jnp.zeros_like(acc_s)

    def att_tile(t, carry):
        m_run, l_run = carry
        logit = jnp.dot(kv_s[t], qT_s[...], preferred_element_type=f32) * attn_scale
        logit = logit + jnp.concatenate([bias_s[t]] * H_A, axis=1)
        m_new = jnp.maximum(m_run, jnp.max(logit, axis=0, keepdims=True))
        alpha = jnp.exp(m_run - m_new)
        p = jnp.exp(logit - m_new)
        l_new = alpha * l_run + jnp.sum(p, axis=0, keepdims=True)
        acc_s[...] = alpha * acc_s[...] + jnp.dot(kvT_s[t], p.astype(bf16), preferred_element_type=f32)
        return m_new, l_new

    init = (jnp.full((1, hq), NEG_BIG, f32), jnp.zeros((1, hq), f32))
    _, l_fin = lax.fori_loop(0, nk, att_tile, init)
    inv_l = 1.0 / l_fin
    for h in range(H_A):
        sl = slice(h * Q_TILE, (h + 1) * Q_TILE)
        o_lat = (acc_s[:, sl] * inv_l[:, sl]).T
        o_ref[:, h * DH_A:(h + 1) * DH_A] = jnp.dot(
            o_lat.astype(bf16), wuv_ref[h], preferred_element_type=f32).astype(o_ref.dtype)


def _attn(proj3, g_kv, w_uv_bf, n_sel):
    b, s, _ = proj3.shape
    n_kt = s // KEY_TILE
    kern = functools.partial(_attn_kernel, n_sel=n_sel, n_kt=n_kt)
    return pl.pallas_call(
        kern,
        grid=(b, s // Q_TILE),
        in_specs=[
            pl.BlockSpec((None, Q_TILE, H_A * D_LATENT), lambda bi, i: (bi, i, C_QLAT // 1024)),
            pl.BlockSpec((None, Q_TILE, H_IDX * D_IDX), lambda bi, i: (bi, i, C_QIDX // 512)),
            pl.BlockSpec((None, Q_TILE, LANES), lambda bi, i: (bi, i, C_KW // LANES)),
            pl.BlockSpec((None, s, LANES), lambda bi, i: (bi, 0, C_CKV // LANES)),
            pl.BlockSpec((None, s, LANES), lambda bi, i: (bi, 0, C_KW // LANES)),
            pl.BlockSpec((1, D_LATENT), lambda bi, i: (0, 0)),
            pl.BlockSpec((H_A, D_LATENT, DH_A), lambda bi, i: (0, 0, 0)),
        ],
        out_specs=pl.BlockSpec((None, Q_TILE, D_BRANCH), lambda bi, i: (bi, i, 0)),
        out_shape=jax.ShapeDtypeStruct((b, s, D_BRANCH), bf16),
        scratch_shapes=[
            pltpu.VMEM((n_kt, KEY_TILE, D_LATENT), bf16),
            pltpu.VMEM((n_kt, D_LATENT, KEY_TILE), bf16),
            pltpu.VMEM((n_kt, KEY_TILE, D_IDX), bf16),
            pltpu.VMEM((n_kt, KEY_TILE, Q_TILE), jnp.int32),
            pltpu.VMEM((n_kt, KEY_TILE, Q_TILE), f32),
            pltpu.VMEM((D_LATENT, H_A * Q_TILE), bf16),
            pltpu.VMEM((D_LATENT, H_A * Q_TILE), f32),
        ],
        compiler_params=_cparams(("parallel", "arbitrary")),
        name="attn",
    )(proj3, proj3, proj3, proj3, proj3, g_kv, w_uv_bf)


def _ret_kernel(q_ref, k_ref, v_ref, gr_ref, cos_ref, sin_ref, din_ref, dq_ref, dk_ref, dc_ref,
                gret_ref, o_ref, state_s):
    @pl.when(pl.program_id(1) == 0)
    def _():
        state_s[...] = jnp.zeros_like(state_s)

    cos = cos_ref[...]
    sin = sin_ref[...]

    def rot(x):
        return x * cos + pltpu.roll(x, DK_R // 2, axis=1) * sin

    for h in range(H_R):
        sl = slice(h * DK_R, (h + 1) * DK_R)
        q = rot(q_ref[:, sl].astype(f32)).astype(bf16)
        kf = rot(k_ref[:, sl].astype(f32)) * (DK_R ** -0.5)
        k = kf.astype(bf16)
        v = v_ref[:, sl]
        inner = lax.dot_general(q, k, (((1,), (1,)), ((), ())), preferred_element_type=f32) * din_ref[h]
        o = jnp.dot(inner.astype(bf16), v, preferred_element_type=f32)
        st = state_s[h]
        o = o + jnp.dot(q, st.astype(bf16), preferred_element_type=f32) * dq_ref[h]
        kd = (kf * dk_ref[h]).astype(bf16)
        state_s[h] = st * dc_ref[h] + jnp.dot(kd.T, v, preferred_element_type=f32)
        mu = jnp.mean(o, axis=-1, keepdims=True)
        var = jnp.mean(jnp.square(o - mu), axis=-1, keepdims=True)
        y = (o - mu) * lax.rsqrt(var + EPS) * gret_ref[:, sl]
        gate = gr_ref[:, sl].astype(f32)
        o_ref[:, sl] = (gate * jax.nn.sigmoid(gate) * y).astype(o_ref.dtype)


def _ret(proj3, cos_t, sin_t, d_in, d_q, d_k, d_c, g_ret):
    b, s, _ = proj3.shape
    c = RET_CHUNK
    w = H_R * DK_R

    def col(off):
        return pl.BlockSpec((None, c, w), lambda bi, ci: (bi, ci, off // w))

    return pl.pallas_call(
        _ret_kernel,
        grid=(b, s // c),
        in_specs=[
            col(C_QR), col(C_KR), col(C_VR), col(C_GR),
            pl.BlockSpec((c, DK_R), lambda bi, ci: (ci, 0)),
            pl.BlockSpec((c, DK_R), lambda bi, ci: (ci, 0)),
            pl.BlockSpec((H_R, c, c), lambda bi, ci: (0, 0, 0)),
            pl.BlockSpec((H_R, c, DK_R), lambda bi, ci: (0, 0, 0)),
            pl.BlockSpec((H_R, c, DK_R), lambda bi, ci: (0, 0, 0)),
            pl.BlockSpec((H_R, 1, DK_R), lambda bi, ci: (0, 0, 0)),
            pl.BlockSpec((1, w), lambda bi, ci: (0, 0)),
        ],
        out_specs=pl.BlockSpec((None, c, w), lambda bi, ci: (bi, ci, 0)),
        out_shape=jax.ShapeDtypeStruct((b, s, w), bf16),
        scratch_shapes=[pltpu.VMEM((H_R, DK_R, DV_R), f32)],
        compiler_params=_cparams(("parallel", "arbitrary")),
        name="ret",
    )(proj3, proj3, proj3, proj3, cos_t, sin_t, d_in, d_q, d_k, d_c, g_ret)


def _retention_tables(s):
    c = RET_CHUNK
    half = DK_R // 2
    freq = ROPE_BASE ** (-jnp.arange(half, dtype=f32) / half)
    ang = jnp.arange(s, dtype=f32)[:, None] * freq[None, :]
    cos = jnp.cos(ang)
    sin = jnp.sin(ang)
    cos_t = jnp.concatenate([cos, cos], axis=-1)
    sin_t = jnp.concatenate([-sin, sin], axis=-1)
    log_gamma = jnp.log1p(-jnp.exp2(-5.0 - jnp.arange(H_R, dtype=f32)))
    n = jnp.arange(c, dtype=f32)
    diff = n[:, None] - n[None, :]
    d_in = jnp.where(diff >= 0, jnp.exp(log_gamma[:, None, None] * jnp.maximum(diff, 0.0)), 0.0)
    d_q = jnp.broadcast_to(jnp.exp(log_gamma[:, None] * (n + 1.0))[:, :, None], (H_R, c, DK_R))
    d_k = jnp.broadcast_to(jnp.exp(log_gamma[:, None] * (c - 1.0 - n))[:, :, None], (H_R, c, DK_R))
    d_c = jnp.broadcast_to(jnp.exp(log_gamma * c)[:, None, None], (H_R, 1, DK_R))
    return cos_t, sin_t, d_in, d_q, d_k, d_c


def _mix_kernel(oa_ref, ob_ref, wb_ref, ga_ref, gb_ref, o_ref):
    a = jnp.dot(oa_ref[...], wb_ref[0], preferred_element_type=f32)
    b = jnp.dot(ob_ref[...], wb_ref[1], preferred_element_type=f32)
    ga = jax.nn.sigmoid(ga_ref[...].astype(f32))
    gb = jax.nn.sigmoid(gb_ref[...].astype(f32))
    o_ref[...] = (ga * a + gb * b).astype(o_ref.dtype)


def _mix(o_a, o_b, w_branch_bf, proj, tm, tn):
    t = o_a.shape[0]
    d = w_branch_bf.shape[2]
    return pl.pallas_call(
        _mix_kernel,
        grid=(t // tm, d // tn),
        in_specs=[
            pl.BlockSpec((tm, D_BRANCH), lambda i, j: (i, 0)),
            pl.BlockSpec((tm, D_BRANCH), lambda i, j: (i, 0)),
            pl.BlockSpec((N_BRANCH, D_BRANCH, tn), lambda i, j: (0, 0, j)),
            pl.BlockSpec((tm, tn), lambda i, j: (i, C_GBR // tn + j)),
            pl.BlockSpec((tm, tn), lambda i, j: (i, (C_GBR + d) // tn + j)),
        ],
        out_specs=pl.BlockSpec((tm, tn), lambda i, j: (i, j)),
        out_shape=jax.ShapeDtypeStruct((t, d), bf16),
        compiler_params=_cparams(("parallel", "parallel")),
        name="mix",
    )(o_a, o_b, w_branch_bf, proj, proj)


def _pack_rows(v):
    n = v.shape[1] // 2
    r = pltpu.bitcast(v.astype(bf16).astype(f32), jnp.uint32)
    w = (r[:, :n] >> 16) | (r[:, n:] & jnp.uint32(0xFFFF0000))
    return pltpu.einshape("r(ab)->rab", w, b=LANES)


def _unpack_rows(p):
    w = pltpu.einshape("rab->r(ab)", p)
    lo = pltpu.bitcast(w << 16, f32)
    hi = pltpu.bitcast(w & jnp.uint32(0xFFFF0000), f32)
    return lo, hi


def _split_bf16(a):
    hi = a.astype(bf16)
    lo = (a - hi.astype(f32)).astype(bf16)
    return hi, lo


def _outproj_kernel(mixed_ref, x_ref, wo_ref, g_ref, wr_hi_ref, wr_lo_ref, br_ref,
                    h_ref, xn_ref, eid_ref, gate_ref, cnt_ref):
    h = x_ref[...] + jnp.dot(mixed_ref[...], wo_ref[...], preferred_element_type=f32)
    h_ref[...] = h
    ms = jnp.mean(h * h, axis=-1, keepdims=True)
    xn = h * lax.rsqrt(ms + EPS) * g_ref[...]
    xn_ref[...] = _pack_rows(xn)

    x_hi, x_lo = _split_bf16(xn)
    logit = (jnp.dot(x_hi, wr_hi_ref[...], preferred_element_type=f32)
             + jnp.dot(x_hi, wr_lo_ref[...], preferred_element_type=f32)
             + jnp.dot(x_lo, wr_hi_ref[...], preferred_element_type=f32)) + br_ref[...]

    lane = lax.broadcasted_iota(jnp.int32, logit.shape, 1)
    lanef = lane.astype(f32)
    neg = -jnp.inf

    def first_argmax(v, m):
        return jnp.min(jnp.where(v == m, lanef, float(LANES)), axis=-1, keepdims=True)

    lg = jnp.where(lane < N_GROUPS, logit, neg)
    mg = jnp.max(lg, axis=-1, keepdims=True)
    p_grp = 1.0 / jnp.sum(jnp.exp(lg - mg), axis=-1, keepdims=True)
    grp = first_argmax(lg, mg).astype(jnp.int32)

    e_lane = lane - N_GROUPS
    in_grp = (e_lane >= 0) & (e_lane < N_EXPERTS) & ((e_lane // EXP_PER_GROUP) == grp)
    le = jnp.where(in_grp, logit, neg)
    m1 = jnp.max(le, axis=-1, keepdims=True)
    i1 = first_argmax(le, m1)
    le2 = jnp.where(lanef == i1, neg, le)
    m2 = jnp.max(le2, axis=-1, keepdims=True)
    i2 = first_argmax(le2, m2)
    e2 = jnp.exp(m2 - m1)
    g1 = p_grp / (1.0 + e2)
    g2 = p_grp * e2 / (1.0 + e2)

    eid = jnp.where(lane == 0, i1, jnp.where(lane == 1, i2, float(N_GROUPS))) - float(N_GROUPS)
    eid_ref[...] = eid.astype(jnp.int32).T[:8, :]
    gate_ref[...] = jnp.where(lane == 0, g1, jnp.where(lane == 1, g2, 0.0))

    @pl.when(pl.program_id(0) == 0)
    def _():
        cnt_ref[...] = jnp.zeros_like(cnt_ref)

    half = pl.program_id(0) // (pl.num_programs(0) // ROUTE_HALVES)
    sub8 = lax.broadcasted_iota(jnp.int32, (8, LANES), 0)
    for s, idx in enumerate((i1, i2)):
        c = jnp.sum((lanef == idx).astype(jnp.int32), axis=0, keepdims=True)
        cnt_ref[...] += jnp.where(sub8 == s * ROUTE_HALVES + half, c, 0)


def _outproj(mixed, x2, w_out_bf, g_ffn, wr_hi, wr_lo, b_r, tm):
    t, d = x2.shape
    row = lambda i: (i, 0)
    fixed = lambda i: (0, 0)
    return pl.pallas_call(
        _outproj_kernel,
        grid=(t // tm,),
        in_specs=[
            pl.BlockSpec((tm, d), row),
            pl.BlockSpec((tm, d), row),
            pl.BlockSpec((d, d), fixed),
            pl.BlockSpec((1, d), fixed),
            pl.BlockSpec((d, LANES), fixed),
            pl.BlockSpec((d, LANES), fixed),
            pl.BlockSpec((1, LANES), fixed),
        ],
        out_specs=[
            pl.BlockSpec((tm, d), row),
            pl.BlockSpec((tm, d // (2 * LANES), LANES), lambda i: (i, 0, 0)),
            pl.BlockSpec((8, tm), lambda i: (0, i)),
            pl.BlockSpec((tm, LANES), row),
            pl.BlockSpec((8, LANES), fixed),
        ],
        out_shape=[
            jax.ShapeDtypeStruct((t, d), f32),
            jax.ShapeDtypeStruct((t, d // (2 * LANES), LANES), jnp.uint32),
            jax.ShapeDtypeStruct((8, t), jnp.int32),
            jax.ShapeDtypeStruct((t, LANES), f32),
            jax.ShapeDtypeStruct((8, LANES), jnp.int32),
        ],
        compiler_params=_cparams(("arbitrary",)),
        name="outproj",
    )(mixed, x2, w_out_bf, g_ffn, wr_hi, wr_lo, b_r)


ISSUE_UNROLL = 8
CAST_ROWS = 256


def _experts_kernel(bexp_ref, nexp_ref, rpack_ref, nused_ref, xn_hbm, wg_hbm, wu_hbm, wd_hbm, yt_hbm,
                    xbuf, ybuf, wg_st, wu_st, wd_st, wg_bf, wu_bf, wd_bf, gsem, ssem, wsem, *, n_tok):
    j = pl.program_id(0)
    n_used = nused_ref[0]
    slot = j % 2
    tok_bits = (n_tok - 1).bit_length()

    def rows_of(blk, fn):
        base = blk * ROW_BLOCK

        def body(k, c):
            r0 = pl.multiple_of(k * ISSUE_UNROLL, ISSUE_UNROLL)
            for u in range(ISSUE_UNROLL):
                fn(r0 + u, rpack_ref[base + r0 + u], u % 2)
            return c

        lax.fori_loop(0, ROW_BLOCK // ISSUE_UNROLL, body, 0)

    def start_gathers(blk, sl):
        def one(r, packed, queue):
            tok = packed & ((1 << tok_bits) - 1)
            pltpu.make_async_copy(xn_hbm.at[tok], xbuf.at[sl, r], gsem.at[sl]).start(priority=queue)
        rows_of(blk, one)

    def start_scatters(blk, sl):
        def one(r, packed, queue):
            row = lax.shift_right_logical(packed, tok_bits)
            pltpu.make_async_copy(ybuf.at[sl, r], yt_hbm.at[row], ssem.at[sl]).start(priority=queue)
        rows_of(blk, one)

    def wait_gathers(sl):
        pltpu.make_async_copy(xn_hbm.at[pl.ds(0, ROW_BLOCK)], xbuf.at[sl], gsem.at[sl]).wait()

    def wait_scatters(sl):
        pltpu.make_async_copy(ybuf.at[sl], yt_hbm.at[pl.ds(0, ROW_BLOCK)], ssem.at[sl]).wait()

    staged = ((wg_hbm, wg_st, wg_bf), (wu_hbm, wu_st, wu_bf), (wd_hbm, wd_st, wd_bf))

    def start_weights(e):
        for q, (src, st, _) in enumerate(staged):
            pltpu.make_async_copy(src.at[e], st, wsem.at[q]).start()

    def wait_and_cast_weights():
        for q, (src, st, dst) in enumerate(staged):
            pltpu.make_async_copy(src.at[0], st, wsem.at[q]).wait()

            def cast(c, carry, st=st, dst=dst):
                r = pl.multiple_of(c * CAST_ROWS, CAST_ROWS)
                dst[pl.ds(r, CAST_ROWS), :] = st[pl.ds(r, CAST_ROWS), :].astype(bf16)
                return carry

            lax.fori_loop(0, st.shape[0] // CAST_ROWS, cast, 0)

    @pl.when(j == 0)
    def _():
        start_weights(bexp_ref[0])
        start_gathers(0, 0)
        ybuf[...] = jnp.zeros_like(ybuf)
        for sl in range(2):
            spare = yt_hbm.at[pl.ds(2 * n_tok + sl * ROW_BLOCK, ROW_BLOCK)]
            pltpu.make_async_copy(ybuf.at[sl], spare, ssem.at[sl]).start()
        for sl in range(2):
            wait_scatters(sl)

    @pl.when(j < n_used)
    def _():
        e = bexp_ref[j]

        @pl.when((j == 0) | (bexp_ref[jnp.maximum(j - 1, 0)] != e))
        def _():
            wait_and_cast_weights()

            @pl.when(nexp_ref[j] >= 0)
            def _():
                start_weights(nexp_ref[j])

        wait_gathers(slot)

        @pl.when(j + 1 < n_used)
        def _():
            start_gathers(j + 1, 1 - slot)

        @pl.when(j >= 2)
        def _():
            wait_scatters(slot)

        lo, hi = _unpack_rows(xbuf[slot])
        xb = jnp.concatenate([lo, hi], axis=1).astype(bf16)
        g = jnp.dot(xb, wg_bf[...], preferred_element_type=f32)
        u = jnp.dot(xb, wu_bf[...], preferred_element_type=f32)
        hm = (g * jax.nn.sigmoid(g) * u).astype(bf16)
        ybuf[slot] = _pack_rows(jnp.dot(hm, wd_bf[...], preferred_element_type=f32))
        start_scatters(j, slot)

        @pl.when(j == n_used - 1)
        def _():
            wait_scatters(slot)

            @pl.when(j >= 1)
            def _():
                wait_scatters(1 - slot)


def _experts(block_expert, next_expert, row_pack, n_used, xn_packed, w_gate, w_up, w_down):
    n_rows = row_pack.shape[0]
    n_tok = xn_packed.shape[0]
    tile = xn_packed.shape[1:]
    _, d, f = w_gate.shape
    assert d % CAST_ROWS == 0 and f % CAST_ROWS == 0
    any_space = pl.BlockSpec(memory_space=pl.ANY)
    grid_spec = pltpu.PrefetchScalarGridSpec(
        num_scalar_prefetch=4,
        grid=(n_rows // ROW_BLOCK,),
        in_specs=[any_space, any_space, any_space, any_space],
        out_specs=any_space,
        scratch_shapes=[pltpu.VMEM((2, ROW_BLOCK) + tile, jnp.uint32),
                        pltpu.VMEM((2, ROW_BLOCK) + tile, jnp.uint32),
                        pltpu.VMEM((d, f), f32), pltpu.VMEM((d, f), f32), pltpu.VMEM((f, d), f32),
                        pltpu.VMEM((d, f), bf16), pltpu.VMEM((d, f), bf16), pltpu.VMEM((f, d), bf16),
                        pltpu.SemaphoreType.DMA((2,)), pltpu.SemaphoreType.DMA((2,)),
                        pltpu.SemaphoreType.DMA((3,))],
    )
    return pl.pallas_call(
        functools.partial(_experts_kernel, n_tok=n_tok),
        grid_spec=grid_spec,
        out_shape=jax.ShapeDtypeStruct((2 * n_tok + 2 * ROW_BLOCK,) + tile, jnp.uint32),
        compiler_params=_cparams(("arbitrary",)),
        name="experts",
    )(block_expert, next_expert, row_pack, n_used, xn_packed, w_gate, w_up, w_down)


def _combine_kernel(h_ref, y0_ref, y1_ref, gate_ref, g_ref, o_ref):
    gate = gate_ref[...]
    y0 = jnp.concatenate(_unpack_rows(y0_ref[...]), axis=1)
    y1 = jnp.concatenate(_unpack_rows(y1_ref[...]), axis=1)
    hh = h_ref[...] + gate[:, 0:1] * y0 + gate[:, 1:2] * y1
    ms = jnp.mean(hh * hh, axis=-1, keepdims=True)
    o_ref[...] = hh * lax.rsqrt(ms + EPS) * g_ref[...]


def _combine(h, yt, gate, g_final, tm):
    t, d = h.shape
    nt = t // tm
    return pl.pallas_call(
        _combine_kernel,
        grid=(nt,),
        in_specs=[
            pl.BlockSpec((tm, d), lambda i: (i, 0)),
            pl.BlockSpec((tm,) + yt.shape[1:], lambda i: (i, 0, 0)),
            pl.BlockSpec((tm,) + yt.shape[1:], lambda i: (nt + i, 0, 0)),
            pl.BlockSpec((tm, LANES), lambda i: (i, 0)),
            pl.BlockSpec((1, d), lambda i: (0, 0)),
        ],
        out_specs=pl.BlockSpec((tm, d), lambda i: (i, 0)),
        out_shape=jax.ShapeDtypeStruct((t, d), f32),
        compiler_params=_cparams(("parallel",)),
        name="combine",
    )(h, yt, yt, gate, g_final)


def _route_kernel(eid_ref, cnt_ref, rpack_ref, bexp_ref, nexp_ref, nused_ref, *cur_refs, n_tok, n_blocks):
    tok_bits = (n_tok - 1).bit_length()
    chunk = 2 * n_tok // ROUTE_CHAINS

    def no_next(k, carry):
        nexp_ref[k] = -1
        return carry

    lax.fori_loop(0, n_blocks, no_next, 0)

    def per_expert(e, carry):
        blk, prev_blk, prev_nb = carry
        start = blk * ROW_BLOCK
        run = start
        for c in range(ROUTE_CHAINS):
            cur_refs[c][e] = run
            run = run + cnt_ref[c * N_EXPERTS + e]
        nb = (run - start + ROW_BLOCK - 1) // ROW_BLOCK

        def set_block(k, c):
            bexp_ref[blk + k] = e
            return c

        lax.fori_loop(0, nb, set_block, 0)

        def set_next(k, c):
            nexp_ref[prev_blk + k] = e
            return c

        lax.fori_loop(0, jnp.where(nb > 0, prev_nb, 0), set_next, 0)

        def set_pad(r, c):
            rpack_ref[r] = (2 * n_tok + (r & (2 * ROW_BLOCK - 1))) << tok_bits
            return c

        lax.fori_loop(run, start + nb * ROW_BLOCK, set_pad, 0)
        return blk + nb, jnp.where(nb > 0, blk, prev_blk), jnp.where(nb > 0, nb, prev_nb)

    n_used, _, _ = lax.fori_loop(0, N_EXPERTS, per_expert, (0, 0, 0))
    nused_ref[0] = n_used

    def tail_block(k, carry):
        bexp_ref[k] = N_EXPERTS - 1
        return carry

    lax.fori_loop(n_used, n_blocks, tail_block, 0)

    def tail_row(r, carry):
        rpack_ref[r] = (2 * n_tok + (r & (2 * ROW_BLOCK - 1))) << tok_bits
        return carry

    lax.fori_loop(n_used * ROW_BLOCK, n_blocks * ROW_BLOCK, tail_row, 0)

    def place(i, carry):
        for c in range(ROUTE_CHAINS):
            a = c * chunk + i
            e = eid_ref[a]
            p = cur_refs[c][e]
            cur_refs[c][e] = p + 1
            rpack_ref[p] = (a << tok_bits) | (a - (c * chunk // n_tok) * n_tok)
        return carry

    lax.fori_loop(0, chunk, place, 0)


def _route(eid_flat, counts, n_tok):
    n_asg = eid_flat.shape[0]
    n_rows = -(-(n_asg + N_EXPERTS * (ROW_BLOCK - 1)) // ROW_BLOCK) * ROW_BLOCK
    n_blocks = n_rows // ROW_BLOCK
    smem = pl.BlockSpec(memory_space=pltpu.SMEM)
    return pl.pallas_call(
        functools.partial(_route_kernel, n_tok=n_tok, n_blocks=n_blocks),
        in_specs=[smem, smem],
        out_specs=[smem, smem, smem, smem],
        out_shape=[jax.ShapeDtypeStruct((n_rows,), jnp.int32),
                   jax.ShapeDtypeStruct((n_blocks,), jnp.int32),
                   jax.ShapeDtypeStruct((n_blocks,), jnp.int32),
                   jax.ShapeDtypeStruct((1,), jnp.int32)],
        scratch_shapes=[pltpu.SMEM((N_EXPERTS,), jnp.int32)] * ROUTE_CHAINS,
        name="route",
    )(eid_flat, counts)


def _pick(n, prefs):
    for p in prefs:
        if n % p == 0:
            return p
    return n


def kernel(x, g_mix_norm, w_in, g_kv, w_uv, g_ret, w_branch, w_out, g_ffn_norm, w_router_group,
           b_router_group, w_router_expert, b_router_expert, w_expert_gate, w_expert_up,
           w_expert_down, g_final):
    b, s, d = x.shape
    t = b * s
    depth = w_in.shape[0]
    n_sel = min(TOPK_MAX, s // 4)
    assert s % RET_CHUNK == 0 and s % Q_TILE == 0

    cos_t, sin_t, d_in, d_q, d_k, d_c = _retention_tables(s)
    h2 = x.reshape(t, d)
    for l in range(depth):
        wl = w_in[l]
        sp = np.cumsum([0, H_A * D_LATENT, D_LATENT, H_IDX * D_IDX, D_IDX, H_IDX,
                        H_R * DK_R, H_R * DK_R, H_R * DV_R, H_R * DV_R, N_BRANCH * d])
        seg = [wl[:, sp[k]:sp[k + 1]] for k in range(10)]
        kw_pad = jnp.zeros((d, LANES - D_IDX - H_IDX), wl.dtype)
        w_p = jnp.concatenate([seg[0], seg[9], seg[5], seg[6], seg[7], seg[8], seg[2], seg[1],
                               seg[3], seg[4], kw_pad], axis=1).astype(bf16)
        assert w_p.shape[1] == D_IN_P

        proj = _proj(h2, g_mix_norm[l].reshape(1, d), w_p, _pick(t, (1024, 512, 256)), 768)
        proj3 = proj.reshape(b, s, D_IN_P)

        o_a = _attn(proj3, g_kv[l].reshape(1, D_LATENT), w_uv[l].astype(bf16), n_sel)
        o_b = _ret(proj3, cos_t, sin_t, d_in, d_q, d_k, d_c, g_ret[l].reshape(1, H_R * DV_R))

        mixed = _mix(o_a.reshape(t, D_BRANCH), o_b.reshape(t, D_BRANCH), w_branch[l].astype(bf16),
                     proj, _pick(t, (1024, 512, 256)), 512)

        w_r = jnp.concatenate([w_router_group[l], w_router_expert[l],
                               jnp.zeros((d, LANES - N_GROUPS - N_EXPERTS), f32)], axis=1)
        b_r = jnp.concatenate([b_router_group[l], b_router_expert[l],
                               jnp.zeros((LANES - N_GROUPS - N_EXPERTS,), f32)]).reshape(1, LANES)
        wr_hi = w_r.astype(bf16)
        wr_lo = (w_r - wr_hi.astype(f32)).astype(bf16)
        h2, xn, eid_t, gate, cnt = _outproj(mixed, h2, w_out[l].astype(bf16),
                                            g_ffn_norm[l].reshape(1, d), wr_hi, wr_lo, b_r, _pick(t, (256,)))

        row_pack, block_expert, next_expert, n_used = _route(
            eid_t[:2].reshape(-1), cnt[:ROUTE_CHAINS, N_GROUPS:N_GROUPS + N_EXPERTS].reshape(-1), t)
        yt = _experts(block_expert, next_expert, row_pack, n_used, xn,
                      w_expert_gate[l], w_expert_up[l], w_expert_down[l])
        assert depth == 1
        h2 = _combine(h2, yt, gate, g_final.reshape(1, d), _pick(t, (256,)))
    return h2.reshape(b, s, d)
```

```python
import functools

import jax
import jax.numpy as jnp
import numpy as np
from jax import lax
from jax.experimental import pallas as pl
from jax.experimental.pallas import tpu as pltpu

EPS = 1e-6
CHUNK = 64
H_A = 8
D_LATENT = 128
DH_A = 128
H_IDX = 8
D_IDX = 64
TOPK_MAX = 256
H_R = 8
DK_R = 128
DV_R = 128
ROPE_BASE = 10000.0
D_BRANCH = 1024
N_BRANCH = 2
N_GROUPS = 4
EXP_PER_GROUP = 8
N_EXPERTS = N_GROUPS * EXP_PER_GROUP
D_EXPERT = 1024

LANES = 128
KEY_TILE = 256
Q_TILE = 128
RET_CHUNK = 256
ROW_BLOCK = 256
ROUTE_HALVES = 2
ROUTE_CHAINS = 2 * ROUTE_HALVES
VMEM_LIMIT = 56 * 1024 * 1024

C_QLAT = 0
C_GBR = 1024
C_QR = 5120
C_KR = 6144
C_VR = 7168
C_GR = 8192
C_QIDX = 9216
C_CKV = 9728
C_KW = 9856
D_IN_P = 9984

INT_MIN = np.int32(-2 ** 31)
NEG_BIG = -1e30

bf16 = jnp.bfloat16
f32 = jnp.float32


def _cparams(sem):
    return pltpu.CompilerParams(dimension_semantics=sem, vmem_limit_bytes=VMEM_LIMIT)


def _proj_kernel(x_ref, g_ref, w_ref, o_ref, xn_ref):
    @pl.when(pl.program_id(1) == 0)
    def _():
        x = x_ref[...]
        ms = jnp.mean(x * x, axis=-1, keepdims=True)
        xn_ref[...] = (x * lax.rsqrt(ms + EPS) * g_ref[...]).astype(bf16)

    o_ref[...] = jnp.dot(xn_ref[...], w_ref[...], preferred_element_type=f32).astype(o_ref.dtype)


def _proj(x2, g, w_p, tm, tn):
    t, d = x2.shape
    n = w_p.shape[1]
    return pl.pallas_call(
        _proj_kernel,
        grid=(t // tm, n // tn),
        in_specs=[
            pl.BlockSpec((tm, d), lambda i, j: (i, 0)),
            pl.BlockSpec((1, d), lambda i, j: (0, 0)),
            pl.BlockSpec((d, tn), lambda i, j: (0, j)),
        ],
        out_specs=pl.BlockSpec((tm, tn), lambda i, j: (i, j)),
        out_shape=jax.ShapeDtypeStruct((t, n), bf16),
        scratch_shapes=[pltpu.VMEM((tm, d), bf16)],
        compiler_params=_cparams(("parallel", "arbitrary")),
        name="proj",
    )(x2, g, w_p)


def _float_key(s):
    bits = pltpu.bitcast(s, jnp.int32)
    key = bits ^ ((bits >> 31) & jnp.int32(0x7FFFFFFF))
    return jnp.where(s == 0.0, jnp.int32(0), key)


def _attn_kernel(qlat_ref, qidx_ref, kwq_ref, ckv_ref, kwk_ref, gkv_ref, wuv_ref, o_ref,
                 kv_s, kvT_s, kidx_s, key_s, hi_s, lo_s, bias_s, qT_s, acc_s, *, n_sel, n_kt):
    i = pl.program_id(1)
    idx_scale = (H_IDX ** -0.5) * (D_IDX ** -0.5)
    attn_scale = D_LATENT ** -0.5
    hq = H_A * Q_TILE

    @pl.when(i == 0)
    def _():
        g = gkv_ref[...]
        for t in range(n_kt):
            c = ckv_ref[t * KEY_TILE:(t + 1) * KEY_TILE, :].astype(f32)
            ms = jnp.mean(c * c, axis=-1, keepdims=True)
            kv = c * lax.rsqrt(ms + EPS) * g
            kv_s[t] = kv.astype(bf16)
            kvT_s[t] = kv.T.astype(bf16)
            kidx_s[t] = kwk_ref[t * KEY_TILE:(t + 1) * KEY_TILE, :D_IDX]

    nk = ((i + 1) * Q_TILE + KEY_TILE - 1) // KEY_TILE
    lane = lax.broadcasted_iota(jnp.int32, (1, Q_TILE), 1)
    sub = lax.broadcasted_iota(jnp.int32, (KEY_TILE, 1), 0)
    q_chunk = (i * Q_TILE + lane) // CHUNK

    wT = kwq_ref[...].astype(f32).T
    qidx = qidx_ref[...]
    for h in range(H_A):
        qT_s[:, h * Q_TILE:(h + 1) * Q_TILE] = qlat_ref[:, h * D_LATENT:(h + 1) * D_LATENT].astype(f32).T.astype(bf16)

    def score_tile(t, carry):
        ks = kidx_s[t]
        acc = jnp.zeros((KEY_TILE, Q_TILE), f32)
        for h in range(H_IDX):
            d = lax.dot_general(ks, qidx[:, h * D_IDX:(h + 1) * D_IDX],
                                (((1,), (1,)), ((), ())), preferred_element_type=f32)
            acc = acc + wT[D_IDX + h:D_IDX + h + 1, :] * jnp.maximum(d, 0.0)
        score = acc * idx_scale
        k_chunk = (t * KEY_TILE + sub) // CHUNK
        key = jnp.where(k_chunk <= q_chunk, _float_key(score), INT_MIN)
        key_s[t] = key
        hi_s[t] = (key >> 16).astype(jnp.int16)
        lo_s[t] = ((key & 0xFFFF) - 2 ** 15).astype(jnp.int16)
        return carry

    lax.fori_loop(0, nk, score_tile, 0)

    def count(pred):
        def body(t, c):
            m = pred(key_s[t], t).astype(jnp.int32)
            return c + jnp.sum(m.reshape(KEY_TILE // 8, 8, Q_TILE), axis=0)
        c8 = lax.fori_loop(0, nk, body, jnp.zeros((8, Q_TILE), jnp.int32))
        return jnp.sum(c8, axis=0, keepdims=True)

    @pl.when(nk % 2 == 1)
    def _():
        low = jnp.full((KEY_TILE, Q_TILE), -2 ** 15, jnp.int16)
        hi_s[nk] = low
        lo_s[nk] = low

    def count16(ref, pred):
        def body(p, c):
            parts = []
            for t in (2 * p, 2 * p + 1):
                m = jnp.where(pred(ref[t]), jnp.int16(1), jnp.int16(0))
                parts += [m[r:r + 16, :] for r in range(0, KEY_TILE, 16)]
            while len(parts) > 1:
                parts = [parts[a] + parts[a + 1] for a in range(0, len(parts), 2)]
            return c + parts[0]
        c16 = lax.fori_loop(0, (nk + 1) // 2, body, jnp.zeros((16, Q_TILE), jnp.int16))
        return jnp.sum(c16.astype(jnp.int32), axis=0, keepdims=True)

    def search16(ref, need):
        zero = jnp.zeros((1, Q_TILE), jnp.int16)
        t0 = jnp.where(count16(ref, lambda v: v >= zero) >= need, jnp.int32(0), jnp.int32(-2 ** 15))

        def step(j, t):
            cand = t | (jnp.int32(1) << (jnp.int32(14) - j))
            cand16 = cand.astype(jnp.int16)
            return jnp.where(count16(ref, lambda v: v >= cand16) >= need, cand, t)

        return lax.fori_loop(0, 15, step, jnp.broadcast_to(t0, (1, Q_TILE)))

    thr_hi = search16(hi_s, n_sel)
    thr_hi16 = thr_hi.astype(jnp.int16)
    need_lo = n_sel - count16(hi_s, lambda v: v > thr_hi16)

    def mask_low(t, carry):
        lo_s[t] = jnp.where(hi_s[t] == thr_hi16, lo_s[t], jnp.int16(-2 ** 15))
        return carry

    lax.fori_loop(0, nk, mask_low, 0)
    thr_lo = search16(lo_s, need_lo)
    thr = (thr_hi << 16) | ((thr_lo + 2 ** 15) & 0xFFFF)

    c_gt = count(lambda k, t: k > thr)
    c_ge = count(lambda k, t: k >= thr)
    need = n_sel - c_gt
    has_tie = jnp.max(jnp.where((c_ge > n_sel) & (thr > INT_MIN), 1, 0)) > 0

    def tie_limit():
        def step(j, m):
            cand = m | (jnp.int32(1) << (jnp.int32(14) - j))
            c = count(lambda k, t: (k == thr) & ((t * KEY_TILE + sub) < cand))
            return jnp.where(c < need, cand, m)
        return lax.fori_loop(0, 15, step, jnp.zeros((1, Q_TILE), jnp.int32))

    m_lim = lax.cond(has_tie, tie_limit, lambda: jnp.full((1, Q_TILE), 2 ** 30, jnp.int32))

    def bias_tile(t, carry):
        k = key_s[t]
        sel = (k > thr) | ((k == thr) & ((t * KEY_TILE + sub) <= m_lim))
        sel = sel & (k > INT_MIN)
        bias_s[t] = jnp.where(sel, 0.0, NEG_BIG).astype(f32)
        return carry

    lax.fori_loop(0, nk, bias_tile, 0)

    acc_s[...] = jnp.zeros_like(acc_s)

    def att_tile(t, carry):
        m_run, l_run = carry
        logit = jnp.dot(kv_s[t], qT_s[...], preferred_element_type=f32) * attn_scale
        logit = logit + jnp.concatenate([bias_s[t]] * H_A, axis=1)
        m_new = jnp.maximum(m_run, jnp.max(logit, axis=0, keepdims=True))
        alpha = jnp.exp(m_run - m_new)
        p = jnp.exp(logit - m_new)
        l_new = alpha * l_run + jnp.sum(p, axis=0, keepdims=True)
        acc_s[...] = alpha * acc_s[...] + jnp.dot(kvT_s[t], p.astype(bf16), preferred_element_type=f32)
        return m_new, l_new

    init = (jnp.full((1, hq), NEG_BIG, f32), jnp.zeros((1, hq), f32))
    _, l_fin = lax.fori_loop(0, nk, att_tile, init)
    inv_l = 1.0 / l_fin
    for h in range(H_A):
        sl = slice(h * Q_TILE, (h + 1) * Q_TILE)
        o_lat = (acc_s[:, sl] * inv_l[:, sl]).T
        o_ref[:, h * DH_A:(h + 1) * DH_A] = jnp.dot(
            o_lat.astype(bf16), wuv_ref[h], preferred_element_type=f32).astype(o_ref.dtype)


def _attn(proj3, g_kv, w_uv_bf, n_sel):
    b, s, _ = proj3.shape
    n_kt = s // KEY_TILE
    kern = functools.partial(_attn_kernel, n_sel=n_sel, n_kt=n_kt)
    return pl.pallas_call(
        kern,
        grid=(b, s // Q_TILE),
        in_specs=[
            pl.BlockSpec((None, Q_TILE, H_A * D_LATENT), lambda bi, i: (bi, i, C_QLAT // 1024)),
            pl.BlockSpec((None, Q_TILE, H_IDX * D_IDX), lambda bi, i: (bi, i, C_QIDX // 512)),
            pl.BlockSpec((None, Q_TILE, LANES), lambda bi, i: (bi, i, C_KW // LANES)),
            pl.BlockSpec((None, s, LANES), lambda bi, i: (bi, 0, C_CKV // LANES)),
            pl.BlockSpec((None, s, LANES), lambda bi, i: (bi, 0, C_KW // LANES)),
            pl.BlockSpec((1, D_LATENT), lambda bi, i: (0, 0)),
            pl.BlockSpec((H_A, D_LATENT, DH_A), lambda bi, i: (0, 0, 0)),
        ],
        out_specs=pl.BlockSpec((None, Q_TILE, D_BRANCH), lambda bi, i: (bi, i, 0)),
        out_shape=jax.ShapeDtypeStruct((b, s, D_BRANCH), bf16),
        scratch_shapes=[
            pltpu.VMEM((n_kt, KEY_TILE, D_LATENT), bf16),
            pltpu.VMEM((n_kt, D_LATENT, KEY_TILE), bf16),
            pltpu.VMEM((n_kt, KEY_TILE, D_IDX), bf16),
            pltpu.VMEM((n_kt, KEY_TILE, Q_TILE), jnp.int32),
            pltpu.VMEM((n_kt, KEY_TILE, Q_TILE), jnp.int16),
            pltpu.VMEM((n_kt, KEY_TILE, Q_TILE), jnp.int16),
            pltpu.VMEM((n_kt, KEY_TILE, Q_TILE), f32),
            pltpu.VMEM((D_LATENT, H_A * Q_TILE), bf16),
            pltpu.VMEM((D_LATENT, H_A * Q_TILE), f32),
        ],
        compiler_params=_cparams(("parallel", "arbitrary")),
        name="attn",
    )(proj3, proj3, proj3, proj3, proj3, g_kv, w_uv_bf)


def _ret_kernel(q_ref, k_ref, v_ref, gr_ref, cos_ref, sin_ref, din_ref, dq_ref, dk_ref, dc_ref,
                gret_ref, o_ref, state_s):
    @pl.when(pl.program_id(1) == 0)
    def _():
        state_s[...] = jnp.zeros_like(state_s)

    cos = cos_ref[...]
    sin = sin_ref[...]

    def rot(x):
        return x * cos + pltpu.roll(x, DK_R // 2, axis=1) * sin

    for h in range(H_R):
        sl = slice(h * DK_R, (h + 1) * DK_R)
        q = rot(q_ref[:, sl].astype(f32)).astype(bf16)
        kf = rot(k_ref[:, sl].astype(f32)) * (DK_R ** -0.5)
        k = kf.astype(bf16)
        v = v_ref[:, sl]
        inner = lax.dot_general(q, k, (((1,), (1,)), ((), ())), preferred_element_type=f32) * din_ref[h]
        o = jnp.dot(inner.astype(bf16), v, preferred_element_type=f32)
        st = state_s[h]
        o = o + jnp.dot(q, st.astype(bf16), preferred_element_type=f32) * dq_ref[h]
        kd = (kf * dk_ref[h]).astype(bf16)
        state_s[h] = st * dc_ref[h] + jnp.dot(kd.T, v, preferred_element_type=f32)
        mu = jnp.mean(o, axis=-1, keepdims=True)
        var = jnp.mean(jnp.square(o - mu), axis=-1, keepdims=True)
        y = (o - mu) * lax.rsqrt(var + EPS) * gret_ref[:, sl]
        gate = gr_ref[:, sl].astype(f32)
        o_ref[:, sl] = (gate * jax.nn.sigmoid(gate) * y).astype(o_ref.dtype)


def _ret(proj3, cos_t, sin_t, d_in, d_q, d_k, d_c, g_ret):
    b, s, _ = proj3.shape
    c = RET_CHUNK
    w = H_R * DK_R

    def col(off):
        return pl.BlockSpec((None, c, w), lambda bi, ci: (bi, ci, off // w))

    return pl.pallas_call(
        _ret_kernel,
        grid=(b, s // c),
        in_specs=[
            col(C_QR), col(C_KR), col(C_VR), col(C_GR),
            pl.BlockSpec((c, DK_R), lambda bi, ci: (ci, 0)),
            pl.BlockSpec((c, DK_R), lambda bi, ci: (ci, 0)),
            pl.BlockSpec((H_R, c, c), lambda bi, ci: (0, 0, 0)),
            pl.BlockSpec((H_R, c, DK_R), lambda bi, ci: (0, 0, 0)),
            pl.BlockSpec((H_R, c, DK_R), lambda bi, ci: (0, 0, 0)),
            pl.BlockSpec((H_R, 1, DK_R), lambda bi, ci: (0, 0, 0)),
            pl.BlockSpec((1, w), lambda bi, ci: (0, 0)),
        ],
        out_specs=pl.BlockSpec((None, c, w), lambda bi, ci: (bi, ci, 0)),
        out_shape=jax.ShapeDtypeStruct((b, s, w), bf16),
        scratch_shapes=[pltpu.VMEM((H_R, DK_R, DV_R), f32)],
        compiler_params=_cparams(("parallel", "arbitrary")),
        name="ret",
    )(proj3, proj3, proj3, proj3, cos_t, sin_t, d_in, d_q, d_k, d_c, g_ret)


def _retention_tables(s):
    c = RET_CHUNK
    half = DK_R // 2
    freq = ROPE_BASE ** (-jnp.arange(half, dtype=f32) / half)
    ang = jnp.arange(s, dtype=f32)[:, None] * freq[None, :]
    cos = jnp.cos(ang)
    sin = jnp.sin(ang)
    cos_t = jnp.concatenate([cos, cos], axis=-1)
    sin_t = jnp.concatenate([-sin, sin], axis=-1)
    log_gamma = jnp.log1p(-jnp.exp2(-5.0 - jnp.arange(H_R, dtype=f32)))
    n = jnp.arange(c, dtype=f32)
    diff = n[:, None] - n[None, :]
    d_in = jnp.where(diff >= 0, jnp.exp(log_gamma[:, None, None] * jnp.maximum(diff, 0.0)), 0.0)
    d_q = jnp.broadcast_to(jnp.exp(log_gamma[:, None] * (n + 1.0))[:, :, None], (H_R, c, DK_R))
    d_k = jnp.broadcast_to(jnp.exp(log_gamma[:, None] * (c - 1.0 - n))[:, :, None], (H_R, c, DK_R))
    d_c = jnp.broadcast_to(jnp.exp(log_gamma * c)[:, None, None], (H_R, 1, DK_R))
    return cos_t, sin_t, d_in, d_q, d_k, d_c


def _mix_kernel(oa_ref, ob_ref, wb_ref, ga_ref, gb_ref, o_ref):
    a = jnp.dot(oa_ref[...], wb_ref[0], preferred_element_type=f32)
    b = jnp.dot(ob_ref[...], wb_ref[1], preferred_element_type=f32)
    ga = jax.nn.sigmoid(ga_ref[...].astype(f32))
    gb = jax.nn.sigmoid(gb_ref[...].astype(f32))
    o_ref[...] = (ga * a + gb * b).astype(o_ref.dtype)


def _mix(o_a, o_b, w_branch_bf, proj, tm, tn):
    t = o_a.shape[0]
    d = w_branch_bf.shape[2]
    return pl.pallas_call(
        _mix_kernel,
        grid=(t // tm, d // tn),
        in_specs=[
            pl.BlockSpec((tm, D_BRANCH), lambda i, j: (i, 0)),
            pl.BlockSpec((tm, D_BRANCH), lambda i, j: (i, 0)),
            pl.BlockSpec((N_BRANCH, D_BRANCH, tn), lambda i, j: (0, 0, j)),
            pl.BlockSpec((tm, tn), lambda i, j: (i, C_GBR // tn + j)),
            pl.BlockSpec((tm, tn), lambda i, j: (i, (C_GBR + d) // tn + j)),
        ],
        out_specs=pl.BlockSpec((tm, tn), lambda i, j: (i, j)),
        out_shape=jax.ShapeDtypeStruct((t, d), bf16),
        compiler_params=_cparams(("parallel", "parallel")),
        name="mix",
    )(o_a, o_b, w_branch_bf, proj, proj)


def _pack_rows(v):
    n = v.shape[1] // 2
    r = pltpu.bitcast(v.astype(bf16).astype(f32), jnp.uint32)
    w = (r[:, :n] >> 16) | (r[:, n:] & jnp.uint32(0xFFFF0000))
    return pltpu.einshape("r(ab)->rab", w, b=LANES)


def _unpack_rows(p):
    w = pltpu.einshape("rab->r(ab)", p)
    lo = pltpu.bitcast(w << 16, f32)
    hi = pltpu.bitcast(w & jnp.uint32(0xFFFF0000), f32)
    return lo, hi


def _split_bf16(a):
    hi = a.astype(bf16)
    lo = (a - hi.astype(f32)).astype(bf16)
    return hi, lo


def _outproj_kernel(mixed_ref, x_ref, wo_ref, g_ref, wr_hi_ref, wr_lo_ref, br_ref,
                    h_ref, xn_ref, eid_ref, gate_ref, cnt_ref):
    h = x_ref[...] + jnp.dot(mixed_ref[...], wo_ref[...], preferred_element_type=f32)
    h_ref[...] = h
    ms = jnp.mean(h * h, axis=-1, keepdims=True)
    xn = h * lax.rsqrt(ms + EPS) * g_ref[...]
    xn_ref[...] = _pack_rows(xn)

    x_hi, x_lo = _split_bf16(xn)
    logit = (jnp.dot(x_hi, wr_hi_ref[...], preferred_element_type=f32)
             + jnp.dot(x_hi, wr_lo_ref[...], preferred_element_type=f32)
             + jnp.dot(x_lo, wr_hi_ref[...], preferred_element_type=f32)) + br_ref[...]

    lane = lax.broadcasted_iota(jnp.int32, logit.shape, 1)
    lanef = lane.astype(f32)
    neg = -jnp.inf

    def first_argmax(v, m):
        return jnp.min(jnp.where(v == m, lanef, float(LANES)), axis=-1, keepdims=True)

    lg = jnp.where(lane < N_GROUPS, logit, neg)
    mg = jnp.max(lg, axis=-1, keepdims=True)
    p_grp = 1.0 / jnp.sum(jnp.exp(lg - mg), axis=-1, keepdims=True)
    grp = first_argmax(lg, mg).astype(jnp.int32)

    e_lane = lane - N_GROUPS
    in_grp = (e_lane >= 0) & (e_lane < N_EXPERTS) & ((e_lane // EXP_PER_GROUP) == grp)
    le = jnp.where(in_grp, logit, neg)
    m1 = jnp.max(le, axis=-1, keepdims=True)
    i1 = first_argmax(le, m1)
    le2 = jnp.where(lanef == i1, neg, le)
    m2 = jnp.max(le2, axis=-1, keepdims=True)
    i2 = first_argmax(le2, m2)
    e2 = jnp.exp(m2 - m1)
    g1 = p_grp / (1.0 + e2)
    g2 = p_grp * e2 / (1.0 + e2)

    eid = jnp.where(lane == 0, i1, jnp.where(lane == 1, i2, float(N_GROUPS))) - float(N_GROUPS)
    eid_ref[...] = eid.astype(jnp.int32).T[:8, :]
    gate_ref[...] = jnp.where(lane == 0, g1, jnp.where(lane == 1, g2, 0.0))

    @pl.when(pl.program_id(0) == 0)
    def _():
        cnt_ref[...] = jnp.zeros_like(cnt_ref)

    half = pl.program_id(0) // (pl.num_programs(0) // ROUTE_HALVES)
    sub8 = lax.broadcasted_iota(jnp.int32, (8, LANES), 0)
    for s, idx in enumerate((i1, i2)):
        c = jnp.sum((lanef == idx).astype(jnp.int32), axis=0, keepdims=True)
        cnt_ref[...] += jnp.where(sub8 == s * ROUTE_HALVES + half, c, 0)


def _outproj(mixed, x2, w_out_bf, g_ffn, wr_hi, wr_lo, b_r, tm):
    t, d = x2.shape
    row = lambda i: (i, 0)
    fixed = lambda i: (0, 0)
    return pl.pallas_call(
        _outproj_kernel,
        grid=(t // tm,),
        in_specs=[
            pl.BlockSpec((tm, d), row),
            pl.BlockSpec((tm, d), row),
            pl.BlockSpec((d, d), fixed),
            pl.BlockSpec((1, d), fixed),
            pl.BlockSpec((d, LANES), fixed),
            pl.BlockSpec((d, LANES), fixed),
            pl.BlockSpec((1, LANES), fixed),
        ],
        out_specs=[
            pl.BlockSpec((tm, d), row),
            pl.BlockSpec((tm, d // (2 * LANES), LANES), lambda i: (i, 0, 0)),
            pl.BlockSpec((8, tm), lambda i: (0, i)),
            pl.BlockSpec((tm, LANES), row),
            pl.BlockSpec((8, LANES), fixed),
        ],
        out_shape=[
            jax.ShapeDtypeStruct((t, d), f32),
            jax.ShapeDtypeStruct((t, d // (2 * LANES), LANES), jnp.uint32),
            jax.ShapeDtypeStruct((8, t), jnp.int32),
            jax.ShapeDtypeStruct((t, LANES), f32),
            jax.ShapeDtypeStruct((8, LANES), jnp.int32),
        ],
        compiler_params=_cparams(("arbitrary",)),
        name="outproj",
    )(mixed, x2, w_out_bf, g_ffn, wr_hi, wr_lo, b_r)


ISSUE_UNROLL = 8
CAST_ROWS = 256


def _experts_kernel(bexp_ref, nexp_ref, rpack_ref, nused_ref, xn_hbm, wg_hbm, wu_hbm, wd_hbm, yt_hbm,
                    xbuf, ybuf, wg_st, wu_st, wd_st, wg_bf, wu_bf, wd_bf, gsem, ssem, wsem, *, n_tok):
    j = pl.program_id(0)
    n_used = nused_ref[0]
    slot = j % 2
    tok_bits = (n_tok - 1).bit_length()

    def rows_of(blk, fn):
        base = blk * ROW_BLOCK

        def body(k, c):
            r0 = pl.multiple_of(k * ISSUE_UNROLL, ISSUE_UNROLL)
            for u in range(ISSUE_UNROLL):
                fn(r0 + u, rpack_ref[base + r0 + u], u % 2)
            return c

        lax.fori_loop(0, ROW_BLOCK // ISSUE_UNROLL, body, 0)

    def start_gathers(blk, sl):
        def one(r, packed, queue):
            tok = packed & ((1 << tok_bits) - 1)
            pltpu.make_async_copy(xn_hbm.at[tok], xbuf.at[sl, r], gsem.at[sl]).start(priority=queue)
        rows_of(blk, one)

    def start_scatters(blk, sl):
        def one(r, packed, queue):
            row = lax.shift_right_logical(packed, tok_bits)
            pltpu.make_async_copy(ybuf.at[sl, r], yt_hbm.at[row], ssem.at[sl]).start(priority=queue)
        rows_of(blk, one)

    def wait_gathers(sl):
        pltpu.make_async_copy(xn_hbm.at[pl.ds(0, ROW_BLOCK)], xbuf.at[sl], gsem.at[sl]).wait()

    def wait_scatters(sl):
        pltpu.make_async_copy(ybuf.at[sl], yt_hbm.at[pl.ds(0, ROW_BLOCK)], ssem.at[sl]).wait()

    staged = ((wg_hbm, wg_st, wg_bf), (wu_hbm, wu_st, wu_bf), (wd_hbm, wd_st, wd_bf))

    def start_weights(e):
        for q, (src, st, _) in enumerate(staged):
            pltpu.make_async_copy(src.at[e], st, wsem.at[q]).start()

    def wait_and_cast_weights():
        for q, (src, st, dst) in enumerate(staged):
            pltpu.make_async_copy(src.at[0], st, wsem.at[q]).wait()

            def cast(c, carry, st=st, dst=dst):
                r = pl.multiple_of(c * CAST_ROWS, CAST_ROWS)
                dst[pl.ds(r, CAST_ROWS), :] = st[pl.ds(r, CAST_ROWS), :].astype(bf16)
                return carry

            lax.fori_loop(0, st.shape[0] // CAST_ROWS, cast, 0)

    @pl.when(j == 0)
    def _():
        start_weights(bexp_ref[0])
        start_gathers(0, 0)
        ybuf[...] = jnp.zeros_like(ybuf)
        for sl in range(2):
            spare = yt_hbm.at[pl.ds(2 * n_tok + sl * ROW_BLOCK, ROW_BLOCK)]
            pltpu.make_async_copy(ybuf.at[sl], spare, ssem.at[sl]).start()
        for sl in range(2):
            wait_scatters(sl)

    @pl.when(j < n_used)
    def _():
        e = bexp_ref[j]

        @pl.when((j == 0) | (bexp_ref[jnp.maximum(j - 1, 0)] != e))
        def _():
            wait_and_cast_weights()

            @pl.when(nexp_ref[j] >= 0)
            def _():
                start_weights(nexp_ref[j])

        wait_gathers(slot)

        @pl.when(j + 1 < n_used)
        def _():
            start_gathers(j + 1, 1 - slot)

        @pl.when(j >= 2)
        def _():
            wait_scatters(slot)

        lo, hi = _unpack_rows(xbuf[slot])
        xb = jnp.concatenate([lo, hi], axis=1).astype(bf16)
        g = jnp.dot(xb, wg_bf[...], preferred_element_type=f32)
        u = jnp.dot(xb, wu_bf[...], preferred_element_type=f32)
        hm = (g * jax.nn.sigmoid(g) * u).astype(bf16)
        ybuf[slot] = _pack_rows(jnp.dot(hm, wd_bf[...], preferred_element_type=f32))
        start_scatters(j, slot)

        @pl.when(j == n_used - 1)
        def _():
            wait_scatters(slot)

            @pl.when(j >= 1)
            def _():
                wait_scatters(1 - slot)


def _experts(block_expert, next_expert, row_pack, n_used, xn_packed, w_gate, w_up, w_down):
    n_rows = row_pack.shape[0]
    n_tok = xn_packed.shape[0]
    tile = xn_packed.shape[1:]
    _, d, f = w_gate.shape
    assert d % CAST_ROWS == 0 and f % CAST_ROWS == 0
    any_space = pl.BlockSpec(memory_space=pl.ANY)
    grid_spec = pltpu.PrefetchScalarGridSpec(
        num_scalar_prefetch=4,
        grid=(n_rows // ROW_BLOCK,),
        in_specs=[any_space, any_space, any_space, any_space],
        out_specs=any_space,
        scratch_shapes=[pltpu.VMEM((2, ROW_BLOCK) + tile, jnp.uint32),
                        pltpu.VMEM((2, ROW_BLOCK) + tile, jnp.uint32),
                        pltpu.VMEM((d, f), f32), pltpu.VMEM((d, f), f32), pltpu.VMEM((f, d), f32),
                        pltpu.VMEM((d, f), bf16), pltpu.VMEM((d, f), bf16), pltpu.VMEM((f, d), bf16),
                        pltpu.SemaphoreType.DMA((2,)), pltpu.SemaphoreType.DMA((2,)),
                        pltpu.SemaphoreType.DMA((3,))],
    )
    return pl.pallas_call(
        functools.partial(_experts_kernel, n_tok=n_tok),
        grid_spec=grid_spec,
        out_shape=jax.ShapeDtypeStruct((2 * n_tok + 2 * ROW_BLOCK,) + tile, jnp.uint32),
        compiler_params=_cparams(("arbitrary",)),
        name="experts",
    )(block_expert, next_expert, row_pack, n_used, xn_packed, w_gate, w_up, w_down)


def _combine_kernel(h_ref, y0_ref, y1_ref, gate_ref, g_ref, o_ref):
    gate = gate_ref[...]
    y0 = jnp.concatenate(_unpack_rows(y0_ref[...]), axis=1)
    y1 = jnp.concatenate(_unpack_rows(y1_ref[...]), axis=1)
    hh = h_ref[...] + gate[:, 0:1] * y0 + gate[:, 1:2] * y1
    ms = jnp.mean(hh * hh, axis=-1, keepdims=True)
    o_ref[...] = hh * lax.rsqrt(ms + EPS) * g_ref[...]


def _combine(h, yt, gate, g_final, tm):
    t, d = h.shape
    nt = t // tm
    return pl.pallas_call(
        _combine_kernel,
        grid=(nt,),
        in_specs=[
            pl.BlockSpec((tm, d), lambda i: (i, 0)),
            pl.BlockSpec((tm,) + yt.shape[1:], lambda i: (i, 0, 0)),
            pl.BlockSpec((tm,) + yt.shape[1:], lambda i: (nt + i, 0, 0)),
            pl.BlockSpec((tm, LANES), lambda i: (i, 0)),
            pl.BlockSpec((1, d), lambda i: (0, 0)),
        ],
        out_specs=pl.BlockSpec((tm, d), lambda i: (i, 0)),
        out_shape=jax.ShapeDtypeStruct((t, d), f32),
        compiler_params=_cparams(("parallel",)),
        name="combine",
    )(h, yt, yt, gate, g_final)


def _route_kernel(eid_ref, cnt_ref, rpack_ref, bexp_ref, nexp_ref, nused_ref, *cur_refs, n_tok, n_blocks):
    tok_bits = (n_tok - 1).bit_length()
    chunk = 2 * n_tok // ROUTE_CHAINS

    def no_next(k, carry):
        nexp_ref[k] = -1
        return carry

    lax.fori_loop(0, n_blocks, no_next, 0)

    def per_expert(e, carry):
        blk, prev_blk, prev_nb = carry
        start = blk * ROW_BLOCK
        run = start
        for c in range(ROUTE_CHAINS):
            cur_refs[c][e] = run
            run = run + cnt_ref[c * N_EXPERTS + e]
        nb = (run - start + ROW_BLOCK - 1) // ROW_BLOCK

        def set_block(k, c):
            bexp_ref[blk + k] = e
            return c

        lax.fori_loop(0, nb, set_block, 0)

        def set_next(k, c):
            nexp_ref[prev_blk + k] = e
            return c

        lax.fori_loop(0, jnp.where(nb > 0, prev_nb, 0), set_next, 0)

        def set_pad(r, c):
            rpack_ref[r] = (2 * n_tok + (r & (2 * ROW_BLOCK - 1))) << tok_bits
            return c

        lax.fori_loop(run, start + nb * ROW_BLOCK, set_pad, 0)
        return blk + nb, jnp.where(nb > 0, blk, prev_blk), jnp.where(nb > 0, nb, prev_nb)

    n_used, _, _ = lax.fori_loop(0, N_EXPERTS, per_expert, (0, 0, 0))
    nused_ref[0] = n_used

    def tail_block(k, carry):
        bexp_ref[k] = N_EXPERTS - 1
        return carry

    lax.fori_loop(n_used, n_blocks, tail_block, 0)

    def tail_row(r, carry):
        rpack_ref[r] = (2 * n_tok + (r & (2 * ROW_BLOCK - 1))) << tok_bits
        return carry

    lax.fori_loop(n_used * ROW_BLOCK, n_blocks * ROW_BLOCK, tail_row, 0)

    def place(i, carry):
        for c in range(ROUTE_CHAINS):
            a = c * chunk + i
            e = eid_ref[a]
            p = cur_refs[c][e]
            cur_refs[c][e] = p + 1
            rpack_ref[p] = (a << tok_bits) | (a - (c * chunk // n_tok) * n_tok)
        return carry

    lax.fori_loop(0, chunk, place, 0)


def _route(eid_flat, counts, n_tok):
    n_asg = eid_flat.shape[0]
    n_rows = -(-(n_asg + N_EXPERTS * (ROW_BLOCK - 1)) // ROW_BLOCK) * ROW_BLOCK
    n_blocks = n_rows // ROW_BLOCK
    smem = pl.BlockSpec(memory_space=pltpu.SMEM)
    return pl.pallas_call(
        functools.partial(_route_kernel, n_tok=n_tok, n_blocks=n_blocks),
        in_specs=[smem, smem],
        out_specs=[smem, smem, smem, smem],
        out_shape=[jax.ShapeDtypeStruct((n_rows,), jnp.int32),
                   jax.ShapeDtypeStruct((n_blocks,), jnp.int32),
                   jax.ShapeDtypeStruct((n_blocks,), jnp.int32),
                   jax.ShapeDtypeStruct((1,), jnp.int32)],
        scratch_shapes=[pltpu.SMEM((N_EXPERTS,), jnp.int32)] * ROUTE_CHAINS,
        name="route",
    )(eid_flat, counts)


def _pick(n, prefs):
    for p in prefs:
        if n % p == 0:
            return p
    return n


def kernel(x, g_mix_norm, w_in, g_kv, w_uv, g_ret, w_branch, w_out, g_ffn_norm, w_router_group,
           b_router_group, w_router_expert, b_router_expert, w_expert_gate, w_expert_up,
           w_expert_down, g_final):
    b, s, d = x.shape
    t = b * s
    depth = w_in.shape[0]
    n_sel = min(TOPK_MAX, s // 4)
    assert s % RET_CHUNK == 0 and s % Q_TILE == 0

    cos_t, sin_t, d_in, d_q, d_k, d_c = _retention_tables(s)
    h2 = x.reshape(t, d)
    for l in range(depth):
        wl = w_in[l]
        sp = np.cumsum([0, H_A * D_LATENT, D_LATENT, H_IDX * D_IDX, D_IDX, H_IDX,
                        H_R * DK_R, H_R * DK_R, H_R * DV_R, H_R * DV_R, N_BRANCH * d])
        seg = [wl[:, sp[k]:sp[k + 1]].astype(bf16) for k in range(10)]
        kw_pad = jnp.zeros((d, LANES - D_IDX - H_IDX), bf16)
        w_p = jnp.concatenate([seg[0], seg[9], seg[5], seg[6], seg[7], seg[8], seg[2], seg[1],
                               seg[3], seg[4], kw_pad], axis=1)
        assert w_p.shape[1] == D_IN_P

        proj = _proj(h2, g_mix_norm[l].reshape(1, d), w_p, _pick(t, (1024, 512, 256)), 768)
        proj3 = proj.reshape(b, s, D_IN_P)

        o_a = _attn(proj3, g_kv[l].reshape(1, D_LATENT), w_uv[l].astype(bf16), n_sel)
        o_b = _ret(proj3, cos_t, sin_t, d_in, d_q, d_k, d_c, g_ret[l].reshape(1, H_R * DV_R))

        mixed = _mix(o_a.reshape(t, D_BRANCH), o_b.reshape(t, D_BRANCH), w_branch[l].astype(bf16),
                     proj, _pick(t, (1024, 512, 256)), 512)

        w_r = jnp.concatenate([w_router_group[l], w_router_expert[l],
                               jnp.zeros((d, LANES - N_GROUPS - N_EXPERTS), f32)], axis=1)
        b_r = jnp.concatenate([b_router_group[l], b_router_expert[l],
                               jnp.zeros((LANES - N_GROUPS - N_EXPERTS,), f32)]).reshape(1, LANES)
        wr_hi = w_r.astype(bf16)
        wr_lo = (w_r - wr_hi.astype(f32)).astype(bf16)
        h2, xn, eid_t, gate, cnt = _outproj(mixed, h2, w_out[l].astype(bf16),
                                            g_ffn_norm[l].reshape(1, d), wr_hi, wr_lo, b_r, _pick(t, (256,)))

        row_pack, block_expert, next_expert, n_used = _route(
            eid_t[:2].reshape(-1), cnt[:ROUTE_CHAINS, N_GROUPS:N_GROUPS + N_EXPERTS].reshape(-1), t)
        yt = _experts(block_expert, next_expert, row_pack, n_used, xn,
                      w_expert_gate[l], w_expert_up[l], w_expert_down[l])
        assert depth == 1
        h2 = _combine(h2, yt, gate, g_final.reshape(1, d), _pick(t, (256,)))
    return h2.reshape(b, s, d)
```

```python
import functools

import jax
import jax.numpy as jnp
import numpy as np
from jax import lax
from jax.experimental import pallas as pl
from jax.experimental.pallas import tpu as pltpu

EPS = 1e-6
CHUNK = 64
H_A = 8
D_LATENT = 128
DH_A = 128
H_IDX = 8
D_IDX = 64
TOPK_MAX = 256
H_R = 8
DK_R = 128
DV_R = 128
ROPE_BASE = 10000.0
D_BRANCH = 1024
N_BRANCH = 2
N_GROUPS = 4
EXP_PER_GROUP = 8
N_EXPERTS = N_GROUPS * EXP_PER_GROUP
D_EXPERT = 1024

LANES = 128
KEY_TILE = 256
Q_TILE = 128
RET_CHUNK = 256
ROW_BLOCK = 256
ROUTE_HALVES = 2
ROUTE_CHAINS = 2 * ROUTE_HALVES
VMEM_LIMIT = 56 * 1024 * 1024

C_QLAT = 0
C_GBR = 1024
C_QR = 5120
C_KR = 6144
C_VR = 7168
C_GR = 8192
C_QIDX = 9216
C_CKV = 9728
C_KW = 9856
D_IN_P = 9984

INT_MIN = np.int32(-2 ** 31)
NEG_BIG = -1e30

bf16 = jnp.bfloat16
f32 = jnp.float32


def _cparams(sem):
    return pltpu.CompilerParams(dimension_semantics=sem, vmem_limit_bytes=VMEM_LIMIT)


def _proj_kernel(x_ref, g_ref, w_ref, o_ref, xn_ref):
    @pl.when(pl.program_id(1) == 0)
    def _():
        x = x_ref[...]
        ms = jnp.mean(x * x, axis=-1, keepdims=True)
        xn_ref[...] = (x * lax.rsqrt(ms + EPS) * g_ref[...]).astype(bf16)

    o_ref[...] = jnp.dot(xn_ref[...], w_ref[...], preferred_element_type=f32).astype(o_ref.dtype)


def _proj(x2, g, w_p, tm, tn):
    t, d = x2.shape
    n = w_p.shape[1]
    return pl.pallas_call(
        _proj_kernel,
        grid=(t // tm, n // tn),
        in_specs=[
            pl.BlockSpec((tm, d), lambda i, j: (i, 0)),
            pl.BlockSpec((1, d), lambda i, j: (0, 0)),
            pl.BlockSpec((d, tn), lambda i, j: (0, j)),
        ],
        out_specs=pl.BlockSpec((tm, tn), lambda i, j: (i, j)),
        out_shape=jax.ShapeDtypeStruct((t, n), bf16),
        scratch_shapes=[pltpu.VMEM((tm, d), bf16)],
        compiler_params=_cparams(("parallel", "arbitrary")),
        name="proj",
    )(x2, g, w_p)


def _float_key(s):
    bits = pltpu.bitcast(s, jnp.int32)
    key = bits ^ ((bits >> 31) & jnp.int32(0x7FFFFFFF))
    return jnp.where(s == 0.0, jnp.int32(0), key)


def _attn_kernel(qlat_ref, qidx_ref, kwq_ref, ckv_ref, kwk_ref, gkv_ref, wuv_ref, o_ref,
                 kv_s, kvT_s, kidx_s, key_s, bias_s, qT_s, qiT_s, acc_s, *, n_sel, n_kt):
    i = pl.program_id(1)
    idx_scale = (H_IDX ** -0.5) * (D_IDX ** -0.5)
    attn_scale = D_LATENT ** -0.5
    hq = H_A * Q_TILE

    @pl.when(i == 0)
    def _():
        g = gkv_ref[...]
        for t in range(n_kt):
            c = ckv_ref[t * KEY_TILE:(t + 1) * KEY_TILE, :].astype(f32)
            ms = jnp.mean(c * c, axis=-1, keepdims=True)
            kv = c * lax.rsqrt(ms + EPS) * g
            kv_s[t] = kv.astype(bf16)
            kvT_s[t] = kv.T.astype(bf16)
            kidx_s[t] = kwk_ref[t * KEY_TILE:(t + 1) * KEY_TILE, :D_IDX]

    nk = ((i + 1) * Q_TILE + KEY_TILE - 1) // KEY_TILE
    lane = lax.broadcasted_iota(jnp.int32, (1, Q_TILE), 1)
    sub = lax.broadcasted_iota(jnp.int32, (KEY_TILE, 1), 0)
    q_chunk = (i * Q_TILE + lane) // CHUNK

    wT = kwq_ref[...].astype(f32).T
    for h in range(H_A):
        qT_s[:, h * Q_TILE:(h + 1) * Q_TILE] = qlat_ref[:, h * D_LATENT:(h + 1) * D_LATENT].astype(f32).T.astype(bf16)
    for h in range(H_IDX):
        qiT_s[:, h * Q_TILE:(h + 1) * Q_TILE] = qidx_ref[:, h * D_IDX:(h + 1) * D_IDX].astype(f32).T.astype(bf16)

    def score_tile(t, carry):
        d_all = jnp.dot(kidx_s[t], qiT_s[...], preferred_element_type=f32)
        acc = jnp.zeros((KEY_TILE, Q_TILE), f32)
        for h in range(H_IDX):
            d = d_all[:, h * Q_TILE:(h + 1) * Q_TILE]
            acc = acc + wT[D_IDX + h:D_IDX + h + 1, :] * jnp.maximum(d, 0.0)
        score = acc * idx_scale
        k_chunk = (t * KEY_TILE + sub) // CHUNK
        key_s[t] = jnp.where(k_chunk <= q_chunk, _float_key(score), INT_MIN)
        return carry

    lax.fori_loop(0, nk, score_tile, 0)

    def count(pred):
        def body(t, c):
            m = pred(key_s[t], t).astype(jnp.int32)
            return c + jnp.sum(m.reshape(KEY_TILE // 8, 8, Q_TILE), axis=0)
        c8 = lax.fori_loop(0, nk, body, jnp.zeros((8, Q_TILE), jnp.int32))
        return jnp.sum(c8, axis=0, keepdims=True)

    thr0 = jnp.where(count(lambda k, t: k >= 0) >= n_sel, jnp.int32(0), INT_MIN)
    thr0 = jnp.broadcast_to(thr0, (1, Q_TILE)).astype(jnp.int32)

    def bit_step(j, thr):
        cand = thr | (jnp.int32(1) << (jnp.int32(30) - j))
        return jnp.where(count(lambda k, t: k >= cand) >= n_sel, cand, thr)

    thr = lax.fori_loop(0, 31, bit_step, thr0)

    c_gt = count(lambda k, t: k > thr)
    c_ge = count(lambda k, t: k >= thr)
    need = n_sel - c_gt
    has_tie = jnp.max(jnp.where((c_ge > n_sel) & (thr > INT_MIN), 1, 0)) > 0

    def tie_limit():
        def step(j, m):
            cand = m | (jnp.int32(1) << (jnp.int32(14) - j))
            c = count(lambda k, t: (k == thr) & ((t * KEY_TILE + sub) < cand))
            return jnp.where(c < need, cand, m)
        return lax.fori_loop(0, 15, step, jnp.zeros((1, Q_TILE), jnp.int32))

    m_lim = lax.cond(has_tie, tie_limit, lambda: jnp.full((1, Q_TILE), 2 ** 30, jnp.int32))

    def bias_tile(t, carry):
        k = key_s[t]
        sel = (k > thr) | ((k == thr) & ((t * KEY_TILE + sub) <= m_lim))
        sel = sel & (k > INT_MIN)
        bias_s[t] = jnp.where(sel, 0.0, NEG_BIG).astype(f32)
        return carry

    lax.fori_loop(0, nk, bias_tile, 0)

    acc_s[...] = jnp.zeros_like(acc_s)

    def att_tile(t, carry):
        m_run, l_run = carry
        logit = jnp.dot(kv_s[t], qT_s[...], preferred_element_type=f32) * attn_scale
        logit = logit + jnp.concatenate([bias_s[t]] * H_A, axis=1)
        m_new = jnp.maximum(m_run, jnp.max(logit, axis=0, keepdims=True))
        alpha = jnp.exp(m_run - m_new)
        p = jnp.exp(logit - m_new)
        l_new = alpha * l_run + jnp.sum(p, axis=0, keepdims=True)
        acc_s[...] = alpha * acc_s[...] + jnp.dot(kvT_s[t], p.astype(bf16), preferred_element_type=f32)
        return m_new, l_new

    init = (jnp.full((1, hq), NEG_BIG, f32), jnp.zeros((1, hq), f32))
    _, l_fin = lax.fori_loop(0, nk, att_tile, init)
    inv_l = 1.0 / l_fin
    for h in range(H_A):
        sl = slice(h * Q_TILE, (h + 1) * Q_TILE)
        o_lat = (acc_s[:, sl] * inv_l[:, sl]).T
        o_ref[:, h * DH_A:(h + 1) * DH_A] = jnp.dot(
            o_lat.astype(bf16), wuv_ref[h], preferred_element_type=f32).astype(o_ref.dtype)


def _attn(proj3, g_kv, w_uv_bf, n_sel):
    b, s, _ = proj3.shape
    n_kt = s // KEY_TILE
    kern = functools.partial(_attn_kernel, n_sel=n_sel, n_kt=n_kt)
    return pl.pallas_call(
        kern,
        grid=(b, s // Q_TILE),
        in_specs=[
            pl.BlockSpec((None, Q_TILE, H_A * D_LATENT), lambda bi, i: (bi, i, C_QLAT // 1024)),
            pl.BlockSpec((None, Q_TILE, H_IDX * D_IDX), lambda bi, i: (bi, i, C_QIDX // 512)),
            pl.BlockSpec((None, Q_TILE, LANES), lambda bi, i: (bi, i, C_KW // LANES)),
            pl.BlockSpec((None, s, LANES), lambda bi, i: (bi, 0, C_CKV // LANES)),
            pl.BlockSpec((None, s, LANES), lambda bi, i: (bi, 0, C_KW // LANES)),
            pl.BlockSpec((1, D_LATENT), lambda bi, i: (0, 0)),
            pl.BlockSpec((H_A, D_LATENT, DH_A), lambda bi, i: (0, 0, 0)),
        ],
        out_specs=pl.BlockSpec((None, Q_TILE, D_BRANCH), lambda bi, i: (bi, i, 0)),
        out_shape=jax.ShapeDtypeStruct((b, s, D_BRANCH), bf16),
        scratch_shapes=[
            pltpu.VMEM((n_kt, KEY_TILE, D_LATENT), bf16),
            pltpu.VMEM((n_kt, D_LATENT, KEY_TILE), bf16),
            pltpu.VMEM((n_kt, KEY_TILE, D_IDX), bf16),
            pltpu.VMEM((n_kt, KEY_TILE, Q_TILE), jnp.int32),
            pltpu.VMEM((n_kt, KEY_TILE, Q_TILE), f32),
            pltpu.VMEM((D_LATENT, H_A * Q_TILE), bf16),
            pltpu.VMEM((D_IDX, H_IDX * Q_TILE), bf16),
            pltpu.VMEM((D_LATENT, H_A * Q_TILE), f32),
        ],
        compiler_params=_cparams(("parallel", "arbitrary")),
        name="attn",
    )(proj3, proj3, proj3, proj3, proj3, g_kv, w_uv_bf)


def _ret_kernel(q_ref, k_ref, v_ref, gr_ref, cos_ref, sin_ref, din_ref, dq_ref, dk_ref, dc_ref,
                gret_ref, o_ref, state_s):
    @pl.when(pl.program_id(1) == 0)
    def _():
        state_s[...] = jnp.zeros_like(state_s)

    cos = cos_ref[...]
    sin = sin_ref[...]

    def rot(x):
        return x * cos + pltpu.roll(x, DK_R // 2, axis=1) * sin

    for h in range(H_R):
        sl = slice(h * DK_R, (h + 1) * DK_R)
        q = rot(q_ref[:, sl].astype(f32)).astype(bf16)
        kf = rot(k_ref[:, sl].astype(f32)) * (DK_R ** -0.5)
        k = kf.astype(bf16)
        v = v_ref[:, sl]
        inner = lax.dot_general(q, k, (((1,), (1,)), ((), ())), preferred_element_type=f32) * din_ref[h]
        o = jnp.dot(inner.astype(bf16), v, preferred_element_type=f32)
        st = state_s[h]
        o = o + jnp.dot(q, st.astype(bf16), preferred_element_type=f32) * dq_ref[h]
        kd = (kf * dk_ref[h]).astype(bf16)
        state_s[h] = st * dc_ref[h] + jnp.dot(kd.T, v, preferred_element_type=f32)
        mu = jnp.mean(o, axis=-1, keepdims=True)
        var = jnp.mean(jnp.square(o - mu), axis=-1, keepdims=True)
        y = (o - mu) * lax.rsqrt(var + EPS) * gret_ref[:, sl]
        gate = gr_ref[:, sl].astype(f32)
        o_ref[:, sl] = (gate * jax.nn.sigmoid(gate) * y).astype(o_ref.dtype)


def _ret(proj3, cos_t, sin_t, d_in, d_q, d_k, d_c, g_ret):
    b, s, _ = proj3.shape
    c = RET_CHUNK
    w = H_R * DK_R

    def col(off):
        return pl.BlockSpec((None, c, w), lambda bi, ci: (bi, ci, off // w))

    return pl.pallas_call(
        _ret_kernel,
        grid=(b, s // c),
        in_specs=[
            col(C_QR), col(C_KR), col(C_VR), col(C_GR),
            pl.BlockSpec((c, DK_R), lambda bi, ci: (ci, 0)),
            pl.BlockSpec((c, DK_R), lambda bi, ci: (ci, 0)),
            pl.BlockSpec((H_R, c, c), lambda bi, ci: (0, 0, 0)),
            pl.BlockSpec((H_R, c, DK_R), lambda bi, ci: (0, 0, 0)),
            pl.BlockSpec((H_R, c, DK_R), lambda bi, ci: (0, 0, 0)),
            pl.BlockSpec((H_R, 1, DK_R), lambda bi, ci: (0, 0, 0)),
            pl.BlockSpec((1, w), lambda bi, ci: (0, 0)),
        ],
        out_specs=pl.BlockSpec((None, c, w), lambda bi, ci: (bi, ci, 0)),
        out_shape=jax.ShapeDtypeStruct((b, s, w), bf16),
        scratch_shapes=[pltpu.VMEM((H_R, DK_R, DV_R), f32)],
        compiler_params=_cparams(("parallel", "arbitrary")),
        name="ret",
    )(proj3, proj3, proj3, proj3, cos_t, sin_t, d_in, d_q, d_k, d_c, g_ret)


def _retention_tables(s):
    c = RET_CHUNK
    half = DK_R // 2
    freq = ROPE_BASE ** (-jnp.arange(half, dtype=f32) / half)
    ang = jnp.arange(s, dtype=f32)[:, None] * freq[None, :]
    cos = jnp.cos(ang)
    sin = jnp.sin(ang)
    cos_t = jnp.concatenate([cos, cos], axis=-1)
    sin_t = jnp.concatenate([-sin, sin], axis=-1)
    log_gamma = jnp.log1p(-jnp.exp2(-5.0 - jnp.arange(H_R, dtype=f32)))
    n = jnp.arange(c, dtype=f32)
    diff = n[:, None] - n[None, :]
    d_in = jnp.where(diff >= 0, jnp.exp(log_gamma[:, None, None] * jnp.maximum(diff, 0.0)), 0.0)
    d_q = jnp.broadcast_to(jnp.exp(log_gamma[:, None] * (n + 1.0))[:, :, None], (H_R, c, DK_R))
    d_k = jnp.broadcast_to(jnp.exp(log_gamma[:, None] * (c - 1.0 - n))[:, :, None], (H_R, c, DK_R))
    d_c = jnp.broadcast_to(jnp.exp(log_gamma * c)[:, None, None], (H_R, 1, DK_R))
    return cos_t, sin_t, d_in, d_q, d_k, d_c


def _mix_kernel(oa_ref, ob_ref, wb_ref, ga_ref, gb_ref, o_ref):
    a = jnp.dot(oa_ref[...], wb_ref[0], preferred_element_type=f32)
    b = jnp.dot(ob_ref[...], wb_ref[1], preferred_element_type=f32)
    ga = jax.nn.sigmoid(ga_ref[...].astype(f32))
    gb = jax.nn.sigmoid(gb_ref[...].astype(f32))
    o_ref[...] = (ga * a + gb * b).astype(o_ref.dtype)


def _mix(o_a, o_b, w_branch_bf, proj, tm, tn):
    t = o_a.shape[0]
    d = w_branch_bf.shape[2]
    return pl.pallas_call(
        _mix_kernel,
        grid=(t // tm, d // tn),
        in_specs=[
            pl.BlockSpec((tm, D_BRANCH), lambda i, j: (i, 0)),
            pl.BlockSpec((tm, D_BRANCH), lambda i, j: (i, 0)),
            pl.BlockSpec((N_BRANCH, D_BRANCH, tn), lambda i, j: (0, 0, j)),
            pl.BlockSpec((tm, tn), lambda i, j: (i, C_GBR // tn + j)),
            pl.BlockSpec((tm, tn), lambda i, j: (i, (C_GBR + d) // tn + j)),
        ],
        out_specs=pl.BlockSpec((tm, tn), lambda i, j: (i, j)),
        out_shape=jax.ShapeDtypeStruct((t, d), bf16),
        compiler_params=_cparams(("parallel", "parallel")),
        name="mix",
    )(o_a, o_b, w_branch_bf, proj, proj)


def _pack_rows(v):
    n = v.shape[1] // 2
    r = pltpu.bitcast(v.astype(bf16).astype(f32), jnp.uint32)
    w = (r[:, :n] >> 16) | (r[:, n:] & jnp.uint32(0xFFFF0000))
    return pltpu.einshape("r(ab)->rab", w, b=LANES)


def _unpack_rows(p):
    w = pltpu.einshape("rab->r(ab)", p)
    lo = pltpu.bitcast(w << 16, f32)
    hi = pltpu.bitcast(w & jnp.uint32(0xFFFF0000), f32)
    return lo, hi


def _split_bf16(a):
    hi = a.astype(bf16)
    lo = (a - hi.astype(f32)).astype(bf16)
    return hi, lo


def _outproj_kernel(mixed_ref, x_ref, wo_ref, g_ref, wr_hi_ref, wr_lo_ref, br_ref,
                    h_ref, xn_ref, eid_ref, gate_ref, cnt_ref):
    h = x_ref[...] + jnp.dot(mixed_ref[...], wo_ref[...], preferred_element_type=f32)
    h_ref[...] = h
    ms = jnp.mean(h * h, axis=-1, keepdims=True)
    xn = h * lax.rsqrt(ms + EPS) * g_ref[...]
    xn_ref[...] = _pack_rows(xn)

    x_hi, x_lo = _split_bf16(xn)
    logit = (jnp.dot(x_hi, wr_hi_ref[...], preferred_element_type=f32)
             + jnp.dot(x_hi, wr_lo_ref[...], preferred_element_type=f32)
             + jnp.dot(x_lo, wr_hi_ref[...], preferred_element_type=f32)) + br_ref[...]

    lane = lax.broadcasted_iota(jnp.int32, logit.shape, 1)
    lanef = lane.astype(f32)
    neg = -jnp.inf

    def first_argmax(v, m):
        return jnp.min(jnp.where(v == m, lanef, float(LANES)), axis=-1, keepdims=True)

    lg = jnp.where(lane < N_GROUPS, logit, neg)
    mg = jnp.max(lg, axis=-1, keepdims=True)
    p_grp = 1.0 / jnp.sum(jnp.exp(lg - mg), axis=-1, keepdims=True)
    grp = first_argmax(lg, mg).astype(jnp.int32)

    e_lane = lane - N_GROUPS
    in_grp = (e_lane >= 0) & (e_lane < N_EXPERTS) & ((e_lane // EXP_PER_GROUP) == grp)
    le = jnp.where(in_grp, logit, neg)
    m1 = jnp.max(le, axis=-1, keepdims=True)
    i1 = first_argmax(le, m1)
    le2 = jnp.where(lanef == i1, neg, le)
    m2 = jnp.max(le2, axis=-1, keepdims=True)
    i2 = first_argmax(le2, m2)
    e2 = jnp.exp(m2 - m1)
    g1 = p_grp / (1.0 + e2)
    g2 = p_grp * e2 / (1.0 + e2)

    eid = jnp.where(lane == 0, i1, jnp.where(lane == 1, i2, float(N_GROUPS))) - float(N_GROUPS)
    eid_ref[...] = eid.astype(jnp.int32).T[:8, :]
    gate_ref[...] = jnp.where(lane == 0, g1, jnp.where(lane == 1, g2, 0.0))

    @pl.when(pl.program_id(0) == 0)
    def _():
        cnt_ref[...] = jnp.zeros_like(cnt_ref)

    half = pl.program_id(0) // (pl.num_programs(0) // ROUTE_HALVES)
    sub8 = lax.broadcasted_iota(jnp.int32, (8, LANES), 0)
    for s, idx in enumerate((i1, i2)):
        c = jnp.sum((lanef == idx).astype(jnp.int32), axis=0, keepdims=True)
        cnt_ref[...] += jnp.where(sub8 == s * ROUTE_HALVES + half, c, 0)


def _outproj(mixed, x2, w_out_bf, g_ffn, wr_hi, wr_lo, b_r, tm):
    t, d = x2.shape
    row = lambda i: (i, 0)
    fixed = lambda i: (0, 0)
    return pl.pallas_call(
        _outproj_kernel,
        grid=(t // tm,),
        in_specs=[
            pl.BlockSpec((tm, d), row),
            pl.BlockSpec((tm, d), row),
            pl.BlockSpec((d, d), fixed),
            pl.BlockSpec((1, d), fixed),
            pl.BlockSpec((d, LANES), fixed),
            pl.BlockSpec((d, LANES), fixed),
            pl.BlockSpec((1, LANES), fixed),
        ],
        out_specs=[
            pl.BlockSpec((tm, d), row),
            pl.BlockSpec((tm, d // (2 * LANES), LANES), lambda i: (i, 0, 0)),
            pl.BlockSpec((8, tm), lambda i: (0, i)),
            pl.BlockSpec((tm, LANES), row),
            pl.BlockSpec((8, LANES), fixed),
        ],
        out_shape=[
            jax.ShapeDtypeStruct((t, d), f32),
            jax.ShapeDtypeStruct((t, d // (2 * LANES), LANES), jnp.uint32),
            jax.ShapeDtypeStruct((8, t), jnp.int32),
            jax.ShapeDtypeStruct((t, LANES), f32),
            jax.ShapeDtypeStruct((8, LANES), jnp.int32),
        ],
        compiler_params=_cparams(("arbitrary",)),
        name="outproj",
    )(mixed, x2, w_out_bf, g_ffn, wr_hi, wr_lo, b_r)


ISSUE_UNROLL = 8
CAST_ROWS = 256


def _experts_kernel(bexp_ref, nexp_ref, rpack_ref, nused_ref, xn_hbm, wg_hbm, wu_hbm, wd_hbm, yt_hbm,
                    xbuf, ybuf, wg_st, wu_st, wd_st, wg_bf, wu_bf, wd_bf, gsem, ssem, wsem, *, n_tok):
    j = pl.program_id(0)
    n_used = nused_ref[0]
    slot = j % 2
    tok_bits = (n_tok - 1).bit_length()

    def rows_of(blk, fn):
        base = blk * ROW_BLOCK

        def body(k, c):
            r0 = pl.multiple_of(k * ISSUE_UNROLL, ISSUE_UNROLL)
            for u in range(ISSUE_UNROLL):
                fn(r0 + u, rpack_ref[base + r0 + u], 1)
            return c

        lax.fori_loop(0, ROW_BLOCK // ISSUE_UNROLL, body, 0)

    def start_gathers(blk, sl):
        def one(r, packed, queue):
            tok = packed & ((1 << tok_bits) - 1)
            pltpu.make_async_copy(xn_hbm.at[tok], xbuf.at[sl, r], gsem.at[sl]).start(priority=queue)
        rows_of(blk, one)

    def start_scatters(blk, sl):
        def one(r, packed, queue):
            row = lax.shift_right_logical(packed, tok_bits)
            pltpu.make_async_copy(ybuf.at[sl, r], yt_hbm.at[row], ssem.at[sl]).start(priority=queue)
        rows_of(blk, one)

    def wait_gathers(sl):
        pltpu.make_async_copy(xn_hbm.at[pl.ds(0, ROW_BLOCK)], xbuf.at[sl], gsem.at[sl]).wait()

    def wait_scatters(sl):
        pltpu.make_async_copy(ybuf.at[sl], yt_hbm.at[pl.ds(0, ROW_BLOCK)], ssem.at[sl]).wait()

    staged = ((wg_hbm, wg_st, wg_bf), (wu_hbm, wu_st, wu_bf), (wd_hbm, wd_st, wd_bf))

    def start_weights(e):
        for q, (src, st, _) in enumerate(staged):
            pltpu.make_async_copy(src.at[e], st, wsem.at[q]).start()

    def wait_and_cast_weights():
        for q, (src, st, dst) in enumerate(staged):
            pltpu.make_async_copy(src.at[0], st, wsem.at[q]).wait()

            def cast(c, carry, st=st, dst=dst):
                r = pl.multiple_of(c * CAST_ROWS, CAST_ROWS)
                dst[pl.ds(r, CAST_ROWS), :] = st[pl.ds(r, CAST_ROWS), :].astype(bf16)
                return carry

            lax.fori_loop(0, st.shape[0] // CAST_ROWS, cast, 0)

    @pl.when(j == 0)
    def _():
        start_weights(bexp_ref[0])
        start_gathers(0, 0)
        ybuf[...] = jnp.zeros_like(ybuf)
        for sl in range(2):
            spare = yt_hbm.at[pl.ds(2 * n_tok + sl * ROW_BLOCK, ROW_BLOCK)]
            pltpu.make_async_copy(ybuf.at[sl], spare, ssem.at[sl]).start()
        for sl in range(2):
            wait_scatters(sl)

    @pl.when(j < n_used)
    def _():
        e = bexp_ref[j]

        @pl.when((j == 0) | (bexp_ref[jnp.maximum(j - 1, 0)] != e))
        def _():
            wait_and_cast_weights()

            @pl.when(nexp_ref[j] >= 0)
            def _():
                start_weights(nexp_ref[j])

        wait_gathers(slot)

        @pl.when(j + 1 < n_used)
        def _():
            start_gathers(j + 1, 1 - slot)

        @pl.when(j >= 2)
        def _():
            wait_scatters(slot)

        lo, hi = _unpack_rows(xbuf[slot])
        xb = jnp.concatenate([lo, hi], axis=1).astype(bf16)
        g = jnp.dot(xb, wg_bf[...], preferred_element_type=f32)
        u = jnp.dot(xb, wu_bf[...], preferred_element_type=f32)
        hm = (g * jax.nn.sigmoid(g) * u).astype(bf16)
        ybuf[slot] = _pack_rows(jnp.dot(hm, wd_bf[...], preferred_element_type=f32))
        start_scatters(j, slot)

        @pl.when(j == n_used - 1)
        def _():
            wait_scatters(slot)

            @pl.when(j >= 1)
            def _():
                wait_scatters(1 - slot)


def _experts(block_expert, next_expert, row_pack, n_used, xn_packed, w_gate, w_up, w_down):
    n_rows = row_pack.shape[0]
    n_tok = xn_packed.shape[0]
    tile = xn_packed.shape[1:]
    _, d, f = w_gate.shape
    assert d % CAST_ROWS == 0 and f % CAST_ROWS == 0
    any_space = pl.BlockSpec(memory_space=pl.ANY)
    grid_spec = pltpu.PrefetchScalarGridSpec(
        num_scalar_prefetch=4,
        grid=(n_rows // ROW_BLOCK,),
        in_specs=[any_space, any_space, any_space, any_space],
        out_specs=any_space,
        scratch_shapes=[pltpu.VMEM((2, ROW_BLOCK) + tile, jnp.uint32),
                        pltpu.VMEM((2, ROW_BLOCK) + tile, jnp.uint32),
                        pltpu.VMEM((d, f), f32), pltpu.VMEM((d, f), f32), pltpu.VMEM((f, d), f32),
                        pltpu.VMEM((d, f), bf16), pltpu.VMEM((d, f), bf16), pltpu.VMEM((f, d), bf16),
                        pltpu.SemaphoreType.DMA((2,)), pltpu.SemaphoreType.DMA((2,)),
                        pltpu.SemaphoreType.DMA((3,))],
    )
    return pl.pallas_call(
        functools.partial(_experts_kernel, n_tok=n_tok),
        grid_spec=grid_spec,
        out_shape=jax.ShapeDtypeStruct((2 * n_tok + 2 * ROW_BLOCK,) + tile, jnp.uint32),
        compiler_params=_cparams(("arbitrary",)),
        name="experts",
    )(block_expert, next_expert, row_pack, n_used, xn_packed, w_gate, w_up, w_down)


def _combine_kernel(h_ref, y0_ref, y1_ref, gate_ref, g_ref, o_ref):
    gate = gate_ref[...]
    y0 = jnp.concatenate(_unpack_rows(y0_ref[...]), axis=1)
    y1 = jnp.concatenate(_unpack_rows(y1_ref[...]), axis=1)
    hh = h_ref[...] + gate[:, 0:1] * y0 + gate[:, 1:2] * y1
    ms = jnp.mean(hh * hh, axis=-1, keepdims=True)
    o_ref[...] = hh * lax.rsqrt(ms + EPS) * g_ref[...]


def _combine(h, yt, gate, g_final, tm):
    t, d = h.shape
    nt = t // tm
    return pl.pallas_call(
        _combine_kernel,
        grid=(nt,),
        in_specs=[
            pl.BlockSpec((tm, d), lambda i: (i, 0)),
            pl.BlockSpec((tm,) + yt.shape[1:], lambda i: (i, 0, 0)),
            pl.BlockSpec((tm,) + yt.shape[1:], lambda i: (nt + i, 0, 0)),
            pl.BlockSpec((tm, LANES), lambda i: (i, 0)),
            pl.BlockSpec((1, d), lambda i: (0, 0)),
        ],
        out_specs=pl.BlockSpec((tm, d), lambda i: (i, 0)),
        out_shape=jax.ShapeDtypeStruct((t, d), f32),
        compiler_params=_cparams(("parallel",)),
        name="combine",
    )(h, yt, yt, gate, g_final)


def _route_kernel(eid_ref, cnt_ref, rpack_ref, bexp_ref, nexp_ref, nused_ref, *cur_refs, n_tok, n_blocks):
    tok_bits = (n_tok - 1).bit_length()
    chunk = 2 * n_tok // ROUTE_CHAINS

    def no_next(k, carry):
        nexp_ref[k] = -1
        return carry

    lax.fori_loop(0, n_blocks, no_next, 0)

    def per_expert(e, carry):
        blk, prev_blk, prev_nb = carry
        start = blk * ROW_BLOCK
        run = start
        for c in range(ROUTE_CHAINS):
            cur_refs[c][e] = run
            run = run + cnt_ref[c * N_EXPERTS + e]
        nb = (run - start + ROW_BLOCK - 1) // ROW_BLOCK

        def set_block(k, c):
            bexp_ref[blk + k] = e
            return c

        lax.fori_loop(0, nb, set_block, 0)

        def set_next(k, c):
            nexp_ref[prev_blk + k] = e
            return c

        lax.fori_loop(0, jnp.where(nb > 0, prev_nb, 0), set_next, 0)

        def set_pad(r, c):
            rpack_ref[r] = (2 * n_tok + (r & (2 * ROW_BLOCK - 1))) << tok_bits
            return c

        lax.fori_loop(run, start + nb * ROW_BLOCK, set_pad, 0)
        return blk + nb, jnp.where(nb > 0, blk, prev_blk), jnp.where(nb > 0, nb, prev_nb)

    n_used, _, _ = lax.fori_loop(0, N_EXPERTS, per_expert, (0, 0, 0))
    nused_ref[0] = n_used

    def tail_block(k, carry):
        bexp_ref[k] = N_EXPERTS - 1
        return carry

    lax.fori_loop(n_used, n_blocks, tail_block, 0)

    def tail_row(r, carry):
        rpack_ref[r] = (2 * n_tok + (r & (2 * ROW_BLOCK - 1))) << tok_bits
        return carry

    lax.fori_loop(n_used * ROW_BLOCK, n_blocks * ROW_BLOCK, tail_row, 0)

    def place(i, carry):
        for c in range(ROUTE_CHAINS):
            a = c * chunk + i
            e = eid_ref[a]
            p = cur_refs[c][e]
            cur_refs[c][e] = p + 1
            rpack_ref[p] = (a << tok_bits) | (a - (c * chunk // n_tok) * n_tok)
        return carry

    lax.fori_loop(0, chunk, place, 0)


def _route(eid_flat, counts, n_tok):
    n_asg = eid_flat.shape[0]
    n_rows = -(-(n_asg + N_EXPERTS * (ROW_BLOCK - 1)) // ROW_BLOCK) * ROW_BLOCK
    n_blocks = n_rows // ROW_BLOCK
    smem = pl.BlockSpec(memory_space=pltpu.SMEM)
    return pl.pallas_call(
        functools.partial(_route_kernel, n_tok=n_tok, n_blocks=n_blocks),
        in_specs=[smem, smem],
        out_specs=[smem, smem, smem, smem],
        out_shape=[jax.ShapeDtypeStruct((n_rows,), jnp.int32),
                   jax.ShapeDtypeStruct((n_blocks,), jnp.int32),
                   jax.ShapeDtypeStruct((n_blocks,), jnp.int32),
                   jax.ShapeDtypeStruct((1,), jnp.int32)],
        scratch_shapes=[pltpu.SMEM((N_EXPERTS,), jnp.int32)] * ROUTE_CHAINS,
        name="route",
    )(eid_flat, counts)


def _pick(n, prefs):
    for p in prefs:
        if n % p == 0:
            return p
    return n


def kernel(x, g_mix_norm, w_in, g_kv, w_uv, g_ret, w_branch, w_out, g_ffn_norm, w_router_group,
           b_router_group, w_router_expert, b_router_expert, w_expert_gate, w_expert_up,
           w_expert_down, g_final):
    b, s, d = x.shape
    t = b * s
    depth = w_in.shape[0]
    n_sel = min(TOPK_MAX, s // 4)
    assert s % RET_CHUNK == 0 and s % Q_TILE == 0

    cos_t, sin_t, d_in, d_q, d_k, d_c = _retention_tables(s)
    h2 = x.reshape(t, d)
    for l in range(depth):
        wl = w_in[l]
        sp = np.cumsum([0, H_A * D_LATENT, D_LATENT, H_IDX * D_IDX, D_IDX, H_IDX,
                        H_R * DK_R, H_R * DK_R, H_R * DV_R, H_R * DV_R, N_BRANCH * d])
        seg = [wl[:, sp[k]:sp[k + 1]].astype(bf16) for k in range(10)]
        kw_pad = jnp.zeros((d, LANES - D_IDX - H_IDX), bf16)
        w_p = jnp.concatenate([seg[0], seg[9], seg[5], seg[6], seg[7], seg[8], seg[2], seg[1],
                               seg[3], seg[4], kw_pad], axis=1)
        assert w_p.shape[1] == D_IN_P

        proj = _proj(h2, g_mix_norm[l].reshape(1, d), w_p, _pick(t, (1024, 512, 256)), 1664)
        proj3 = proj.reshape(b, s, D_IN_P)

        o_a = _attn(proj3, g_kv[l].reshape(1, D_LATENT), w_uv[l].astype(bf16), n_sel)
        o_b = _ret(proj3, cos_t, sin_t, d_in, d_q, d_k, d_c, g_ret[l].reshape(1, H_R * DV_R))

        mixed = _mix(o_a.reshape(t, D_BRANCH), o_b.reshape(t, D_BRANCH), w_branch[l].astype(bf16),
                     proj, _pick(t, (1024, 512, 256)), 512)

        w_r = jnp.concatenate([w_router_group[l], w_router_expert[l],
                               jnp.zeros((d, LANES - N_GROUPS - N_EXPERTS), f32)], axis=1)
        b_r = jnp.concatenate([b_router_group[l], b_router_expert[l],
                               jnp.zeros((LANES - N_GROUPS - N_EXPERTS,), f32)]).reshape(1, LANES)
        wr_hi = w_r.astype(bf16)
        wr_lo = (w_r - wr_hi.astype(f32)).astype(bf16)
        h2, xn, eid_t, gate, cnt = _outproj(mixed, h2, w_out[l].astype(bf16),
                                            g_ffn_norm[l].reshape(1, d), wr_hi, wr_lo, b_r, _pick(t, (256,)))

        row_pack, block_expert, next_expert, n_used = _route(
            eid_t[:2].reshape(-1), cnt[:ROUTE_CHAINS, N_GROUPS:N_GROUPS + N_EXPERTS].reshape(-1), t)
        yt = _experts(block_expert, next_expert, row_pack, n_used, xn,
                      w_expert_gate[l], w_expert_up[l], w_expert_down[l])
        assert depth == 1
        h2 = _combine(h2, yt, gate, g_final.reshape(1, d), _pick(t, (256,)))
    return h2.reshape(b, s, d)
```

```python
import functools

import jax
import jax.numpy as jnp
import numpy as np
from jax import lax
from jax.experimental import pallas as pl
from jax.experimental.pallas import tpu as pltpu

EPS = 1e-6
CHUNK = 64
H_A = 8
D_LATENT = 128
DH_A = 128
H_IDX = 8
D_IDX = 64
TOPK_MAX = 256
H_R = 8
DK_R = 128
DV_R = 128
ROPE_BASE = 10000.0
D_BRANCH = 1024
N_BRANCH = 2
N_GROUPS = 4
EXP_PER_GROUP = 8
N_EXPERTS = N_GROUPS * EXP_PER_GROUP
D_EXPERT = 1024

LANES = 128
KEY_TILE = 256
Q_TILE = 128
RET_CHUNK = 256
ROW_BLOCK = 256
ROUTE_HALVES = 2
ROUTE_CHAINS = 2 * ROUTE_HALVES
VMEM_LIMIT = 56 * 1024 * 1024

C_QLAT = 0
C_GBR = 1024
C_QR = 5120
C_KR = 6144
C_VR = 7168
C_GR = 8192
C_QIDX = 9216
C_CKV = 9728
C_KW = 9856
D_IN_P = 9984

INT_MIN = np.int32(-2 ** 31)
NEG_BIG = -1e30

bf16 = jnp.bfloat16
f32 = jnp.float32


def _cparams(sem):
    return pltpu.CompilerParams(dimension_semantics=sem, vmem_limit_bytes=VMEM_LIMIT)


def _proj_kernel(x_ref, g_ref, w_ref, o_ref, xn_ref):
    @pl.when(pl.program_id(1) == 0)
    def _():
        x = x_ref[...]
        ms = jnp.mean(x * x, axis=-1, keepdims=True)
        xn_ref[...] = (x * lax.rsqrt(ms + EPS) * g_ref[...]).astype(bf16)

    o_ref[...] = jnp.dot(xn_ref[...], w_ref[...], preferred_element_type=f32).astype(o_ref.dtype)


def _proj(x2, g, w_p, tm, tn):
    t, d = x2.shape
    n = w_p.shape[1]
    return pl.pallas_call(
        _proj_kernel,
        grid=(t // tm, n // tn),
        in_specs=[
            pl.BlockSpec((tm, d), lambda i, j: (i, 0)),
            pl.BlockSpec((1, d), lambda i, j: (0, 0)),
            pl.BlockSpec((d, tn), lambda i, j: (0, j)),
        ],
        out_specs=pl.BlockSpec((tm, tn), lambda i, j: (i, j)),
        out_shape=jax.ShapeDtypeStruct((t, n), bf16),
        scratch_shapes=[pltpu.VMEM((tm, d), bf16)],
        compiler_params=_cparams(("parallel", "arbitrary")),
        name="proj",
    )(x2, g, w_p)


def _float_key(s):
    bits = pltpu.bitcast(s, jnp.int32)
    key = bits ^ ((bits >> 31) & jnp.int32(0x7FFFFFFF))
    return jnp.where(s == 0.0, jnp.int32(0), key)


def _attn_kernel(qlat_ref, qidx_ref, kwq_ref, ckv_ref, kwk_ref, gkv_ref, wuv_ref, o_ref,
                 kv_s, kvT_s, kidx_s, key_s, bias_s, qT_s, qiT_s, acc_s, *, n_sel, n_kt):
    i = pl.program_id(1)
    idx_scale = (H_IDX ** -0.5) * (D_IDX ** -0.5)
    attn_scale = D_LATENT ** -0.5
    hq = H_A * Q_TILE

    @pl.when(i == 0)
    def _():
        g = gkv_ref[...]
        for t in range(n_kt):
            c = ckv_ref[t * KEY_TILE:(t + 1) * KEY_TILE, :].astype(f32)
            ms = jnp.mean(c * c, axis=-1, keepdims=True)
            kv = c * lax.rsqrt(ms + EPS) * g
            kv_s[t] = kv.astype(bf16)
            kvT_s[t] = kv.T.astype(bf16)
            kidx_s[t] = kwk_ref[t * KEY_TILE:(t + 1) * KEY_TILE, :D_IDX]

    nk = ((i + 1) * Q_TILE + KEY_TILE - 1) // KEY_TILE
    lane = lax.broadcasted_iota(jnp.int32, (1, Q_TILE), 1)
    sub = lax.broadcasted_iota(jnp.int32, (KEY_TILE, 1), 0)
    q_chunk = (i * Q_TILE + lane) // CHUNK

    wT = kwq_ref[...].astype(f32).T
    for h in range(H_A):
        qT_s[:, h * Q_TILE:(h + 1) * Q_TILE] = qlat_ref[:, h * D_LATENT:(h + 1) * D_LATENT].astype(f32).T.astype(bf16)
    for h in range(H_IDX):
        qiT_s[:, h * Q_TILE:(h + 1) * Q_TILE] = qidx_ref[:, h * D_IDX:(h + 1) * D_IDX].astype(f32).T.astype(bf16)

    def score_tile(t, carry):
        d_all = jnp.dot(kidx_s[t], qiT_s[...], preferred_element_type=f32)
        acc = jnp.zeros((KEY_TILE, Q_TILE), f32)
        for h in range(H_IDX):
            d = d_all[:, h * Q_TILE:(h + 1) * Q_TILE]
            acc = acc + wT[D_IDX + h:D_IDX + h + 1, :] * jnp.maximum(d, 0.0)
        score = acc * idx_scale
        k_chunk = (t * KEY_TILE + sub) // CHUNK
        key_s[t] = jnp.where(k_chunk <= q_chunk, _float_key(score), INT_MIN)
        return carry

    lax.fori_loop(0, nk, score_tile, 0)

    @pl.when(nk % 2 == 1)
    def _():
        key_s[nk] = jnp.full((KEY_TILE, Q_TILE), INT_MIN, jnp.int32)

    n_pairs = (nk + 1) // 2

    def count(pred):
        def body(p, c):
            for t in (2 * p, 2 * p + 1):
                m = pred(key_s[t], t).astype(jnp.int32)
                c = c + jnp.sum(m.reshape(KEY_TILE // 8, 8, Q_TILE), axis=0)
            return c
        c8 = lax.fori_loop(0, n_pairs, body, jnp.zeros((8, Q_TILE), jnp.int32))
        return jnp.sum(c8, axis=0, keepdims=True)

    thr0 = jnp.where(count(lambda k, t: k >= 0) >= n_sel, jnp.int32(0), INT_MIN)
    thr0 = jnp.broadcast_to(thr0, (1, Q_TILE)).astype(jnp.int32)

    def bit_step(j, thr):
        cand = thr | (jnp.int32(1) << (jnp.int32(30) - j))
        return jnp.where(count(lambda k, t: k >= cand) >= n_sel, cand, thr)

    thr = lax.fori_loop(0, 31, bit_step, thr0)

    c_gt = count(lambda k, t: k > thr)
    c_ge = count(lambda k, t: k >= thr)
    need = n_sel - c_gt
    has_tie = jnp.max(jnp.where((c_ge > n_sel) & (thr > INT_MIN), 1, 0)) > 0

    def tie_limit():
        def step(j, m):
            cand = m | (jnp.int32(1) << (jnp.int32(14) - j))
            c = count(lambda k, t: (k == thr) & ((t * KEY_TILE + sub) < cand))
            return jnp.where(c < need, cand, m)
        return lax.fori_loop(0, 15, step, jnp.zeros((1, Q_TILE), jnp.int32))

    m_lim = lax.cond(has_tie, tie_limit, lambda: jnp.full((1, Q_TILE), 2 ** 30, jnp.int32))

    def bias_pair(p, carry):
        for t in (2 * p, 2 * p + 1):
            k = key_s[t]
            sel = (k > thr) | ((k == thr) & ((t * KEY_TILE + sub) <= m_lim))
            sel = sel & (k > INT_MIN)
            bias_s[t] = jnp.where(sel, 0.0, NEG_BIG).astype(f32)
        return carry

    lax.fori_loop(0, n_pairs, bias_pair, 0)

    acc_s[...] = jnp.zeros_like(acc_s)

    def att_pair(p, carry):
        m_run, l_run = carry
        kv2 = jnp.concatenate([kv_s[2 * p], kv_s[2 * p + 1]], axis=0)
        kvT2 = jnp.concatenate([kvT_s[2 * p], kvT_s[2 * p + 1]], axis=1)
        bias2 = jnp.concatenate([bias_s[2 * p], bias_s[2 * p + 1]], axis=0)
        logit = jnp.dot(kv2, qT_s[...], preferred_element_type=f32) * attn_scale
        logit = logit + jnp.concatenate([bias2] * H_A, axis=1)
        m_new = jnp.maximum(m_run, jnp.max(logit, axis=0, keepdims=True))
        alpha = jnp.exp(m_run - m_new)
        pr = jnp.exp(logit - m_new)
        l_new = alpha * l_run + jnp.sum(pr, axis=0, keepdims=True)
        acc_s[...] = alpha * acc_s[...] + jnp.dot(kvT2, pr.astype(bf16), preferred_element_type=f32)
        return m_new, l_new

    init = (jnp.full((1, hq), NEG_BIG, f32), jnp.zeros((1, hq), f32))
    _, l_fin = lax.fori_loop(0, n_pairs, att_pair, init)
    inv_l = 1.0 / l_fin
    for h in range(H_A):
        sl = slice(h * Q_TILE, (h + 1) * Q_TILE)
        o_lat = (acc_s[:, sl] * inv_l[:, sl]).T
        o_ref[:, h * DH_A:(h + 1) * DH_A] = jnp.dot(
            o_lat.astype(bf16), wuv_ref[h], preferred_element_type=f32).astype(o_ref.dtype)


def _attn(proj3, g_kv, w_uv_bf, n_sel):
    b, s, _ = proj3.shape
    n_kt = s // KEY_TILE
    kern = functools.partial(_attn_kernel, n_sel=n_sel, n_kt=n_kt)
    return pl.pallas_call(
        kern,
        grid=(b, s // Q_TILE),
        in_specs=[
            pl.BlockSpec((None, Q_TILE, H_A * D_LATENT), lambda bi, i: (bi, i, C_QLAT // 1024)),
            pl.BlockSpec((None, Q_TILE, H_IDX * D_IDX), lambda bi, i: (bi, i, C_QIDX // 512)),
            pl.BlockSpec((None, Q_TILE, LANES), lambda bi, i: (bi, i, C_KW // LANES)),
            pl.BlockSpec((None, s, LANES), lambda bi, i: (bi, 0, C_CKV // LANES)),
            pl.BlockSpec((None, s, LANES), lambda bi, i: (bi, 0, C_KW // LANES)),
            pl.BlockSpec((1, D_LATENT), lambda bi, i: (0, 0)),
            pl.BlockSpec((H_A, D_LATENT, DH_A), lambda bi, i: (0, 0, 0)),
        ],
        out_specs=pl.BlockSpec((None, Q_TILE, D_BRANCH), lambda bi, i: (bi, i, 0)),
        out_shape=jax.ShapeDtypeStruct((b, s, D_BRANCH), bf16),
        scratch_shapes=[
            pltpu.VMEM((n_kt, KEY_TILE, D_LATENT), bf16),
            pltpu.VMEM((n_kt, D_LATENT, KEY_TILE), bf16),
            pltpu.VMEM((n_kt, KEY_TILE, D_IDX), bf16),
            pltpu.VMEM((n_kt, KEY_TILE, Q_TILE), jnp.int32),
            pltpu.VMEM((n_kt, KEY_TILE, Q_TILE), f32),
            pltpu.VMEM((D_LATENT, H_A * Q_TILE), bf16),
            pltpu.VMEM((D_IDX, H_IDX * Q_TILE), bf16),
            pltpu.VMEM((D_LATENT, H_A * Q_TILE), f32),
        ],
        compiler_params=_cparams(("parallel", "arbitrary")),
        name="attn",
    )(proj3, proj3, proj3, proj3, proj3, g_kv, w_uv_bf)


def _ret_kernel(q_ref, k_ref, v_ref, gr_ref, cos_ref, sin_ref, din_ref, dq_ref, dk_ref, dc_ref,
                gret_ref, o_ref, state_s):
    @pl.when(pl.program_id(1) == 0)
    def _():
        state_s[...] = jnp.zeros_like(state_s)

    cos = cos_ref[...]
    sin = sin_ref[...]

    def rot(x):
        return x * cos + pltpu.roll(x, DK_R // 2, axis=1) * sin

    for h in range(H_R):
        sl = slice(h * DK_R, (h + 1) * DK_R)
        q = rot(q_ref[:, sl].astype(f32)).astype(bf16)
        kf = rot(k_ref[:, sl].astype(f32)) * (DK_R ** -0.5)
        k = kf.astype(bf16)
        v = v_ref[:, sl]
        inner = lax.dot_general(q, k, (((1,), (1,)), ((), ())), preferred_element_type=f32) * din_ref[h]
        o = jnp.dot(inner.astype(bf16), v, preferred_element_type=f32)
        st = state_s[h]
        o = o + jnp.dot(q, st.astype(bf16), preferred_element_type=f32) * dq_ref[h]
        kd = (kf * dk_ref[h]).astype(bf16)
        state_s[h] = st * dc_ref[h] + jnp.dot(kd.T, v, preferred_element_type=f32)
        mu = jnp.mean(o, axis=-1, keepdims=True)
        var = jnp.mean(jnp.square(o - mu), axis=-1, keepdims=True)
        y = (o - mu) * lax.rsqrt(var + EPS) * gret_ref[:, sl]
        gate = gr_ref[:, sl].astype(f32)
        o_ref[:, sl] = (gate * jax.nn.sigmoid(gate) * y).astype(o_ref.dtype)


def _ret(proj3, cos_t, sin_t, d_in, d_q, d_k, d_c, g_ret):
    b, s, _ = proj3.shape
    c = RET_CHUNK
    w = H_R * DK_R

    def col(off):
        return pl.BlockSpec((None, c, w), lambda bi, ci: (bi, ci, off // w))

    return pl.pallas_call(
        _ret_kernel,
        grid=(b, s // c),
        in_specs=[
            col(C_QR), col(C_KR), col(C_VR), col(C_GR),
            pl.BlockSpec((c, DK_R), lambda bi, ci: (ci, 0)),
            pl.BlockSpec((c, DK_R), lambda bi, ci: (ci, 0)),
            pl.BlockSpec((H_R, c, c), lambda bi, ci: (0, 0, 0)),
            pl.BlockSpec((H_R, c, DK_R), lambda bi, ci: (0, 0, 0)),
            pl.BlockSpec((H_R, c, DK_R), lambda bi, ci: (0, 0, 0)),
            pl.BlockSpec((H_R, 1, DK_R), lambda bi, ci: (0, 0, 0)),
            pl.BlockSpec((1, w), lambda bi, ci: (0, 0)),
        ],
        out_specs=pl.BlockSpec((None, c, w), lambda bi, ci: (bi, ci, 0)),
        out_shape=jax.ShapeDtypeStruct((b, s, w), bf16),
        scratch_shapes=[pltpu.VMEM((H_R, DK_R, DV_R), f32)],
        compiler_params=_cparams(("parallel", "arbitrary")),
        name="ret",
    )(proj3, proj3, proj3, proj3, cos_t, sin_t, d_in, d_q, d_k, d_c, g_ret)


def _retention_tables(s):
    c = RET_CHUNK
    half = DK_R // 2
    freq = ROPE_BASE ** (-jnp.arange(half, dtype=f32) / half)
    ang = jnp.arange(s, dtype=f32)[:, None] * freq[None, :]
    cos = jnp.cos(ang)
    sin = jnp.sin(ang)
    cos_t = jnp.concatenate([cos, cos], axis=-1)
    sin_t = jnp.concatenate([-sin, sin], axis=-1)
    log_gamma = jnp.log1p(-jnp.exp2(-5.0 - jnp.arange(H_R, dtype=f32)))
    n = jnp.arange(c, dtype=f32)
    diff = n[:, None] - n[None, :]
    d_in = jnp.where(diff >= 0, jnp.exp(log_gamma[:, None, None] * jnp.maximum(diff, 0.0)), 0.0)
    d_q = jnp.broadcast_to(jnp.exp(log_gamma[:, None] * (n + 1.0))[:, :, None], (H_R, c, DK_R))
    d_k = jnp.broadcast_to(jnp.exp(log_gamma[:, None] * (c - 1.0 - n))[:, :, None], (H_R, c, DK_R))
    d_c = jnp.broadcast_to(jnp.exp(log_gamma * c)[:, None, None], (H_R, 1, DK_R))
    return cos_t, sin_t, d_in, d_q, d_k, d_c


def _mix_kernel(oa_ref, ob_ref, wb_ref, ga_ref, gb_ref, o_ref):
    a = jnp.dot(oa_ref[...], wb_ref[0], preferred_element_type=f32)
    b = jnp.dot(ob_ref[...], wb_ref[1], preferred_element_type=f32)
    ga = jax.nn.sigmoid(ga_ref[...].astype(f32))
    gb = jax.nn.sigmoid(gb_ref[...].astype(f32))
    o_ref[...] = (ga * a + gb * b).astype(o_ref.dtype)


def _mix(o_a, o_b, w_branch_bf, proj, tm, tn):
    t = o_a.shape[0]
    d = w_branch_bf.shape[2]
    return pl.pallas_call(
        _mix_kernel,
        grid=(t // tm, d // tn),
        in_specs=[
            pl.BlockSpec((tm, D_BRANCH), lambda i, j: (i, 0)),
            pl.BlockSpec((tm, D_BRANCH), lambda i, j: (i, 0)),
            pl.BlockSpec((N_BRANCH, D_BRANCH, tn), lambda i, j: (0, 0, j)),
            pl.BlockSpec((tm, tn), lambda i, j: (i, C_GBR // tn + j)),
            pl.BlockSpec((tm, tn), lambda i, j: (i, (C_GBR + d) // tn + j)),
        ],
        out_specs=pl.BlockSpec((tm, tn), lambda i, j: (i, j)),
        out_shape=jax.ShapeDtypeStruct((t, d), bf16),
        compiler_params=_cparams(("parallel", "parallel")),
        name="mix",
    )(o_a, o_b, w_branch_bf, proj, proj)


def _pack_rows(v):
    n = v.shape[1] // 2
    r = pltpu.bitcast(v.astype(bf16).astype(f32), jnp.uint32)
    w = (r[:, :n] >> 16) | (r[:, n:] & jnp.uint32(0xFFFF0000))
    return pltpu.einshape("r(ab)->rab", w, b=LANES)


def _unpack_rows(p):
    w = pltpu.einshape("rab->r(ab)", p)
    lo = pltpu.bitcast(w << 16, f32)
    hi = pltpu.bitcast(w & jnp.uint32(0xFFFF0000), f32)
    return lo, hi


def _split_bf16(a):
    hi = a.astype(bf16)
    lo = (a - hi.astype(f32)).astype(bf16)
    return hi, lo


def _outproj_kernel(mixed_ref, x_ref, wo_ref, g_ref, wr_hi_ref, wr_lo_ref, br_ref,
                    h_ref, xn_ref, eid_ref, gate_ref, cnt_ref):
    h = x_ref[...] + jnp.dot(mixed_ref[...], wo_ref[...], preferred_element_type=f32)
    h_ref[...] = h
    ms = jnp.mean(h * h, axis=-1, keepdims=True)
    xn = h * lax.rsqrt(ms + EPS) * g_ref[...]
    xn_ref[...] = _pack_rows(xn)

    x_hi, x_lo = _split_bf16(xn)
    logit = (jnp.dot(x_hi, wr_hi_ref[...], preferred_element_type=f32)
             + jnp.dot(x_hi, wr_lo_ref[...], preferred_element_type=f32)
             + jnp.dot(x_lo, wr_hi_ref[...], preferred_element_type=f32)) + br_ref[...]

    lane = lax.broadcasted_iota(jnp.int32, logit.shape, 1)
    lanef = lane.astype(f32)
    neg = -jnp.inf

    def first_argmax(v, m):
        return jnp.min(jnp.where(v == m, lanef, float(LANES)), axis=-1, keepdims=True)

    lg = jnp.where(lane < N_GROUPS, logit, neg)
    mg = jnp.max(lg, axis=-1, keepdims=True)
    p_grp = 1.0 / jnp.sum(jnp.exp(lg - mg), axis=-1, keepdims=True)
    grp = first_argmax(lg, mg).astype(jnp.int32)

    e_lane = lane - N_GROUPS
    in_grp = (e_lane >= 0) & (e_lane < N_EXPERTS) & ((e_lane // EXP_PER_GROUP) == grp)
    le = jnp.where(in_grp, logit, neg)
    m1 = jnp.max(le, axis=-1, keepdims=True)
    i1 = first_argmax(le, m1)
    le2 = jnp.where(lanef == i1, neg, le)
    m2 = jnp.max(le2, axis=-1, keepdims=True)
    i2 = first_argmax(le2, m2)
    e2 = jnp.exp(m2 - m1)
    g1 = p_grp / (1.0 + e2)
    g2 = p_grp * e2 / (1.0 + e2)

    eid = jnp.where(lane == 0, i1, jnp.where(lane == 1, i2, float(N_GROUPS))) - float(N_GROUPS)
    eid_ref[...] = eid.astype(jnp.int32).T[:8, :]
    gate_ref[...] = jnp.where(lane == 0, g1, jnp.where(lane == 1, g2, 0.0))

    @pl.when(pl.program_id(0) == 0)
    def _():
        cnt_ref[...] = jnp.zeros_like(cnt_ref)

    half = pl.program_id(0) // (pl.num_programs(0) // ROUTE_HALVES)
    sub8 = lax.broadcasted_iota(jnp.int32, (8, LANES), 0)
    for s, idx in enumerate((i1, i2)):
        c = jnp.sum((lanef == idx).astype(jnp.int32), axis=0, keepdims=True)
        cnt_ref[...] += jnp.where(sub8 == s * ROUTE_HALVES + half, c, 0)


def _outproj(mixed, x2, w_out_bf, g_ffn, wr_hi, wr_lo, b_r, tm):
    t, d = x2.shape
    row = lambda i: (i, 0)
    fixed = lambda i: (0, 0)
    return pl.pallas_call(
        _outproj_kernel,
        grid=(t // tm,),
        in_specs=[
            pl.BlockSpec((tm, d), row),
            pl.BlockSpec((tm, d), row),
            pl.BlockSpec((d, d), fixed),
            pl.BlockSpec((1, d), fixed),
            pl.BlockSpec((d, LANES), fixed),
            pl.BlockSpec((d, LANES), fixed),
            pl.BlockSpec((1, LANES), fixed),
        ],
        out_specs=[
            pl.BlockSpec((tm, d), row),
            pl.BlockSpec((tm, d // (2 * LANES), LANES), lambda i: (i, 0, 0)),
            pl.BlockSpec((8, tm), lambda i: (0, i)),
            pl.BlockSpec((tm, LANES), row),
            pl.BlockSpec((8, LANES), fixed),
        ],
        out_shape=[
            jax.ShapeDtypeStruct((t, d), f32),
            jax.ShapeDtypeStruct((t, d // (2 * LANES), LANES), jnp.uint32),
            jax.ShapeDtypeStruct((8, t), jnp.int32),
            jax.ShapeDtypeStruct((t, LANES), f32),
            jax.ShapeDtypeStruct((8, LANES), jnp.int32),
        ],
        compiler_params=_cparams(("arbitrary",)),
        name="outproj",
    )(mixed, x2, w_out_bf, g_ffn, wr_hi, wr_lo, b_r)


ISSUE_UNROLL = 8
CAST_ROWS = 256


def _experts_kernel(bexp_ref, nexp_ref, rpack_ref, nused_ref, xn_hbm, wg_hbm, wu_hbm, wd_hbm, yt_hbm,
                    xbuf, ybuf, wg_st, wu_st, wd_st, wg_bf, wu_bf, wd_bf, gsem, ssem, wsem, *, n_tok):
    j = pl.program_id(0)
    n_used = nused_ref[0]
    slot = j % 2
    tok_bits = (n_tok - 1).bit_length()

    def rows_of(blk, fn):
        base = blk * ROW_BLOCK

        def body(k, c):
            r0 = pl.multiple_of(k * ISSUE_UNROLL, ISSUE_UNROLL)
            for u in range(ISSUE_UNROLL):
                fn(r0 + u, rpack_ref[base + r0 + u], 1)
            return c

        lax.fori_loop(0, ROW_BLOCK // ISSUE_UNROLL, body, 0)

    def start_gathers(blk, sl):
        def one(r, packed, queue):
            tok = packed & ((1 << tok_bits) - 1)
            pltpu.make_async_copy(xn_hbm.at[tok], xbuf.at[sl, r], gsem.at[sl]).start(priority=queue)
        rows_of(blk, one)

    def start_scatters(blk, sl):
        def one(r, packed, queue):
            row = lax.shift_right_logical(packed, tok_bits)
            pltpu.make_async_copy(ybuf.at[sl, r], yt_hbm.at[row], ssem.at[sl]).start(priority=queue)
        rows_of(blk, one)

    def wait_gathers(sl):
        pltpu.make_async_copy(xn_hbm.at[pl.ds(0, ROW_BLOCK)], xbuf.at[sl], gsem.at[sl]).wait()

    def wait_scatters(sl):
        pltpu.make_async_copy(ybuf.at[sl], yt_hbm.at[pl.ds(0, ROW_BLOCK)], ssem.at[sl]).wait()

    staged = ((wg_hbm, wg_st, wg_bf), (wu_hbm, wu_st, wu_bf), (wd_hbm, wd_st, wd_bf))

    def start_weights(e):
        for q, (src, st, _) in enumerate(staged):
            pltpu.make_async_copy(src.at[e], st, wsem.at[q]).start()

    def wait_and_cast_weights():
        for q, (src, st, dst) in enumerate(staged):
            pltpu.make_async_copy(src.at[0], st, wsem.at[q]).wait()

            def cast(c, carry, st=st, dst=dst):
                r = pl.multiple_of(c * CAST_ROWS, CAST_ROWS)
                dst[pl.ds(r, CAST_ROWS), :] = st[pl.ds(r, CAST_ROWS), :].astype(bf16)
                return carry

            lax.fori_loop(0, st.shape[0] // CAST_ROWS, cast, 0)

    @pl.when(j == 0)
    def _():
        start_weights(bexp_ref[0])
        start_gathers(0, 0)
        ybuf[...] = jnp.zeros_like(ybuf)
        for sl in range(2):
            spare = yt_hbm.at[pl.ds(2 * n_tok + sl * ROW_BLOCK, ROW_BLOCK)]
            pltpu.make_async_copy(ybuf.at[sl], spare, ssem.at[sl]).start()
        for sl in range(2):
            wait_scatters(sl)

    @pl.when(j < n_used)
    def _():
        e = bexp_ref[j]

        @pl.when((j == 0) | (bexp_ref[jnp.maximum(j - 1, 0)] != e))
        def _():
            wait_and_cast_weights()

            @pl.when(nexp_ref[j] >= 0)
            def _():
                start_weights(nexp_ref[j])

        wait_gathers(slot)

        @pl.when(j + 1 < n_used)
        def _():
            start_gathers(j + 1, 1 - slot)

        @pl.when(j >= 2)
        def _():
            wait_scatters(slot)

        lo, hi = _unpack_rows(xbuf[slot])
        xb = jnp.concatenate([lo, hi], axis=1).astype(bf16)
        g = jnp.dot(xb, wg_bf[...], preferred_element_type=f32)
        u = jnp.dot(xb, wu_bf[...], preferred_element_type=f32)
        hm = (g * jax.nn.sigmoid(g) * u).astype(bf16)
        ybuf[slot] = _pack_rows(jnp.dot(hm, wd_bf[...], preferred_element_type=f32))
        start_scatters(j, slot)

        @pl.when(j == n_used - 1)
        def _():
            wait_scatters(slot)

            @pl.when(j >= 1)
            def _():
                wait_scatters(1 - slot)


def _experts(block_expert, next_expert, row_pack, n_used, xn_packed, w_gate, w_up, w_down):
    n_rows = row_pack.shape[0]
    n_tok = xn_packed.shape[0]
    tile = xn_packed.shape[1:]
    _, d, f = w_gate.shape
    assert d % CAST_ROWS == 0 and f % CAST_ROWS == 0
    any_space = pl.BlockSpec(memory_space=pl.ANY)
    grid_spec = pltpu.PrefetchScalarGridSpec(
        num_scalar_prefetch=4,
        grid=(n_rows // ROW_BLOCK,),
        in_specs=[any_space, any_space, any_space, any_space],
        out_specs=any_space,
        scratch_shapes=[pltpu.VMEM((2, ROW_BLOCK) + tile, jnp.uint32),
                        pltpu.VMEM((2, ROW_BLOCK) + tile, jnp.uint32),
                        pltpu.VMEM((d, f), f32), pltpu.VMEM((d, f), f32), pltpu.VMEM((f, d), f32),
                        pltpu.VMEM((d, f), bf16), pltpu.VMEM((d, f), bf16), pltpu.VMEM((f, d), bf16),
                        pltpu.SemaphoreType.DMA((2,)), pltpu.SemaphoreType.DMA((2,)),
                        pltpu.SemaphoreType.DMA((3,))],
    )
    return pl.pallas_call(
        functools.partial(_experts_kernel, n_tok=n_tok),
        grid_spec=grid_spec,
        out_shape=jax.ShapeDtypeStruct((2 * n_tok + 2 * ROW_BLOCK,) + tile, jnp.uint32),
        compiler_params=_cparams(("arbitrary",)),
        name="experts",
    )(block_expert, next_expert, row_pack, n_used, xn_packed, w_gate, w_up, w_down)


def _combine_kernel(h_ref, y0_ref, y1_ref, gate_ref, g_ref, o_ref):
    gate = gate_ref[...]
    y0 = jnp.concatenate(_unpack_rows(y0_ref[...]), axis=1)
    y1 = jnp.concatenate(_unpack_rows(y1_ref[...]), axis=1)
    hh = h_ref[...] + gate[:, 0:1] * y0 + gate[:, 1:2] * y1
    ms = jnp.mean(hh * hh, axis=-1, keepdims=True)
    o_ref[...] = hh * lax.rsqrt(ms + EPS) * g_ref[...]


def _combine(h, yt, gate, g_final, tm):
    t, d = h.shape
    nt = t // tm
    return pl.pallas_call(
        _combine_kernel,
        grid=(nt,),
        in_specs=[
            pl.BlockSpec((tm, d), lambda i: (i, 0)),
            pl.BlockSpec((tm,) + yt.shape[1:], lambda i: (i, 0, 0)),
            pl.BlockSpec((tm,) + yt.shape[1:], lambda i: (nt + i, 0, 0)),
            pl.BlockSpec((tm, LANES), lambda i: (i, 0)),
            pl.BlockSpec((1, d), lambda i: (0, 0)),
        ],
        out_specs=pl.BlockSpec((tm, d), lambda i: (i, 0)),
        out_shape=jax.ShapeDtypeStruct((t, d), f32),
        compiler_params=_cparams(("parallel",)),
        name="combine",
    )(h, yt, yt, gate, g_final)


def _route_kernel(eid_ref, cnt_ref, rpack_ref, bexp_ref, nexp_ref, nused_ref, *cur_refs, n_tok, n_blocks):
    tok_bits = (n_tok - 1).bit_length()
    chunk = 2 * n_tok // ROUTE_CHAINS

    def no_next(k, carry):
        nexp_ref[k] = -1
        return carry

    lax.fori_loop(0, n_blocks, no_next, 0)

    def per_expert(e, carry):
        blk, prev_blk, prev_nb = carry
        start = blk * ROW_BLOCK
        run = start
        for c in range(ROUTE_CHAINS):
            cur_refs[c][e] = run
            run = run + cnt_ref[c * N_EXPERTS + e]
        nb = (run - start + ROW_BLOCK - 1) // ROW_BLOCK

        def set_block(k, c):
            bexp_ref[blk + k] = e
            return c

        lax.fori_loop(0, nb, set_block, 0)

        def set_next(k, c):
            nexp_ref[prev_blk + k] = e
            return c

        lax.fori_loop(0, jnp.where(nb > 0, prev_nb, 0), set_next, 0)

        def set_pad(r, c):
            rpack_ref[r] = (2 * n_tok + (r & (2 * ROW_BLOCK - 1))) << tok_bits
            return c

        lax.fori_loop(run, start + nb * ROW_BLOCK, set_pad, 0)
        return blk + nb, jnp.where(nb > 0, blk, prev_blk), jnp.where(nb > 0, nb, prev_nb)

    n_used, _, _ = lax.fori_loop(0, N_EXPERTS, per_expert, (0, 0, 0))
    nused_ref[0] = n_used

    def tail_block(k, carry):
        bexp_ref[k] = N_EXPERTS - 1
        return carry

    lax.fori_loop(n_used, n_blocks, tail_block, 0)

    def tail_row(r, carry):
        rpack_ref[r] = (2 * n_tok + (r & (2 * ROW_BLOCK - 1))) << tok_bits
        return carry

    lax.fori_loop(n_used * ROW_BLOCK, n_blocks * ROW_BLOCK, tail_row, 0)

    def place(i, carry):
        for c in range(ROUTE_CHAINS):
            a = c * chunk + i
            e = eid_ref[a]
            p = cur_refs[c][e]
            cur_refs[c][e] = p + 1
            rpack_ref[p] = (a << tok_bits) | (a - (c * chunk // n_tok) * n_tok)
        return carry

    lax.fori_loop(0, chunk, place, 0)


def _route(eid_flat, counts, n_tok):
    n_asg = eid_flat.shape[0]
    n_rows = -(-(n_asg + N_EXPERTS * (ROW_BLOCK - 1)) // ROW_BLOCK) * ROW_BLOCK
    n_blocks = n_rows // ROW_BLOCK
    smem = pl.BlockSpec(memory_space=pltpu.SMEM)
    return pl.pallas_call(
        functools.partial(_route_kernel, n_tok=n_tok, n_blocks=n_blocks),
        in_specs=[smem, smem],
        out_specs=[smem, smem, smem, smem],
        out_shape=[jax.ShapeDtypeStruct((n_rows,), jnp.int32),
                   jax.ShapeDtypeStruct((n_blocks,), jnp.int32),
                   jax.ShapeDtypeStruct((n_blocks,), jnp.int32),
                   jax.ShapeDtypeStruct((1,), jnp.int32)],
        scratch_shapes=[pltpu.SMEM((N_EXPERTS,), jnp.int32)] * ROUTE_CHAINS,
        name="route",
    )(eid_flat, counts)


def _pick(n, prefs):
    for p in prefs:
        if n % p == 0:
            return p
    return n


def kernel(x, g_mix_norm, w_in, g_kv, w_uv, g_ret, w_branch, w_out, g_ffn_norm, w_router_group,
           b_router_group, w_router_expert, b_router_expert, w_expert_gate, w_expert_up,
           w_expert_down, g_final):
    b, s, d = x.shape
    t = b * s
    depth = w_in.shape[0]
    n_sel = min(TOPK_MAX, s // 4)
    assert s % RET_CHUNK == 0 and s % Q_TILE == 0

    cos_t, sin_t, d_in, d_q, d_k, d_c = _retention_tables(s)
    h2 = x.reshape(t, d)
    for l in range(depth):
        wl = w_in[l]
        sp = np.cumsum([0, H_A * D_LATENT, D_LATENT, H_IDX * D_IDX, D_IDX, H_IDX,
                        H_R * DK_R, H_R * DK_R, H_R * DV_R, H_R * DV_R, N_BRANCH * d])
        seg = [wl[:, sp[k]:sp[k + 1]].astype(bf16) for k in range(10)]
        kw_pad = jnp.zeros((d, LANES - D_IDX - H_IDX), bf16)
        w_p = jnp.concatenate([seg[0], seg[9], seg[5], seg[6], seg[7], seg[8], seg[2], seg[1],
                               seg[3], seg[4], kw_pad], axis=1)
        assert w_p.shape[1] == D_IN_P

        proj = _proj(h2, g_mix_norm[l].reshape(1, d), w_p, _pick(t, (1024, 512, 256)), 1664)
        proj3 = proj.reshape(b, s, D_IN_P)

        o_a = _attn(proj3, g_kv[l].reshape(1, D_LATENT), w_uv[l].astype(bf16), n_sel)
        o_b = _ret(proj3, cos_t, sin_t, d_in, d_q, d_k, d_c, g_ret[l].reshape(1, H_R * DV_R))

        mixed = _mix(o_a.reshape(t, D_BRANCH), o_b.reshape(t, D_BRANCH), w_branch[l].astype(bf16),
                     proj, _pick(t, (1024, 512, 256)), 512)

        w_r = jnp.concatenate([w_router_group[l], w_router_expert[l],
                               jnp.zeros((d, LANES - N_GROUPS - N_EXPERTS), f32)], axis=1)
        b_r = jnp.concatenate([b_router_group[l], b_router_expert[l],
                               jnp.zeros((LANES - N_GROUPS - N_EXPERTS,), f32)]).reshape(1, LANES)
        wr_hi = w_r.astype(bf16)
        wr_lo = (w_r - wr_hi.astype(f32)).astype(bf16)
        h2, xn, eid_t, gate, cnt = _outproj(mixed, h2, w_out[l].astype(bf16),
                                            g_ffn_norm[l].reshape(1, d), wr_hi, wr_lo, b_r, _pick(t, (256,)))

        row_pack, block_expert, next_expert, n_used = _route(
            eid_t[:2].reshape(-1), cnt[:ROUTE_CHAINS, N_GROUPS:N_GROUPS + N_EXPERTS].reshape(-1), t)
        yt = _experts(block_expert, next_expert, row_pack, n_used, xn,
                      w_expert_gate[l], w_expert_up[l], w_expert_down[l])
        assert depth == 1
        h2 = _combine(h2, yt, gate, g_final.reshape(1, d), _pick(t, (256,)))
    return h2.reshape(b, s, d)
```

```python
import functools

import jax
import jax.numpy as jnp
import numpy as np
from jax import lax
from jax.experimental import pallas as pl
from jax.experimental.pallas import tpu as pltpu

EPS = 1e-6
CHUNK = 64
H_A = 8
D_LATENT = 128
DH_A = 128
H_IDX = 8
D_IDX = 64
TOPK_MAX = 256
H_R = 8
DK_R = 128
DV_R = 128
ROPE_BASE = 10000.0
D_BRANCH = 1024
N_BRANCH = 2
N_GROUPS = 4
EXP_PER_GROUP = 8
N_EXPERTS = N_GROUPS * EXP_PER_GROUP
D_EXPERT = 1024

LANES = 128
KEY_TILE = 256
Q_TILE = 128
RET_CHUNK = 256
ROW_BLOCK = 256
ROUTE_HALVES = 4
ROUTE_CHAINS = 2 * ROUTE_HALVES
VMEM_LIMIT = 56 * 1024 * 1024

C_QLAT = 0
C_GBR = 1024
C_QR = 5120
C_KR = 6144
C_VR = 7168
C_GR = 8192
C_QIDX = 9216
C_CKV = 9728
C_KW = 9856
D_IN_P = 9984

INT_MIN = np.int32(-2 ** 31)
NEG_BIG = -1e30

bf16 = jnp.bfloat16
f32 = jnp.float32


def _cparams(sem):
    return pltpu.CompilerParams(dimension_semantics=sem, vmem_limit_bytes=VMEM_LIMIT)


def _proj_kernel(x_ref, g_ref, w_ref, o_ref, xn_ref):
    @pl.when(pl.program_id(1) == 0)
    def _():
        x = x_ref[...]
        ms = jnp.mean(x * x, axis=-1, keepdims=True)
        xn_ref[...] = (x * lax.rsqrt(ms + EPS) * g_ref[...]).astype(bf16)

    o_ref[...] = jnp.dot(xn_ref[...], w_ref[...], preferred_element_type=f32).astype(o_ref.dtype)


def _proj(x2, g, w_p, tm, tn):
    t, d = x2.shape
    n = w_p.shape[1]
    return pl.pallas_call(
        _proj_kernel,
        grid=(t // tm, n // tn),
        in_specs=[
            pl.BlockSpec((tm, d), lambda i, j: (i, 0)),
            pl.BlockSpec((1, d), lambda i, j: (0, 0)),
            pl.BlockSpec((d, tn), lambda i, j: (0, j)),
        ],
        out_specs=pl.BlockSpec((tm, tn), lambda i, j: (i, j)),
        out_shape=jax.ShapeDtypeStruct((t, n), bf16),
        scratch_shapes=[pltpu.VMEM((tm, d), bf16)],
        compiler_params=_cparams(("parallel", "arbitrary")),
        name="proj",
    )(x2, g, w_p)


def _float_key(s):
    bits = pltpu.bitcast(s, jnp.int32)
    key = bits ^ ((bits >> 31) & jnp.int32(0x7FFFFFFF))
    return jnp.where(s == 0.0, jnp.int32(0), key)


def _attn_kernel(qlat_ref, qidx_ref, kwq_ref, ckv_ref, kwk_ref, gkv_ref, wuv_ref, o_ref,
                 kv_s, kvT_s, kidx_s, key_s, bias_s, qT_s, qiT_s, acc_s, *, n_sel, n_kt):
    i = pl.program_id(1)
    idx_scale = (H_IDX ** -0.5) * (D_IDX ** -0.5)
    attn_scale = D_LATENT ** -0.5
    hq = H_A * Q_TILE

    @pl.when(i == 0)
    def _():
        g = gkv_ref[...]
        for t in range(n_kt):
            c = ckv_ref[t * KEY_TILE:(t + 1) * KEY_TILE, :].astype(f32)
            ms = jnp.mean(c * c, axis=-1, keepdims=True)
            kv = c * lax.rsqrt(ms + EPS) * g
            kv_s[t] = kv.astype(bf16)
            kvT_s[t] = kv.T.astype(bf16)
            kidx_s[t] = kwk_ref[t * KEY_TILE:(t + 1) * KEY_TILE, :D_IDX]

    nk = ((i + 1) * Q_TILE + KEY_TILE - 1) // KEY_TILE
    lane = lax.broadcasted_iota(jnp.int32, (1, Q_TILE), 1)
    sub = lax.broadcasted_iota(jnp.int32, (KEY_TILE, 1), 0)
    q_chunk = (i * Q_TILE + lane) // CHUNK

    wT = kwq_ref[...].astype(f32).T
    for h in range(H_A):
        qT_s[:, h * Q_TILE:(h + 1) * Q_TILE] = qlat_ref[:, h * D_LATENT:(h + 1) * D_LATENT].astype(f32).T.astype(bf16)
    for h in range(H_IDX):
        qiT_s[:, h * Q_TILE:(h + 1) * Q_TILE] = qidx_ref[:, h * D_IDX:(h + 1) * D_IDX].astype(f32).T.astype(bf16)

    def score_tile(t, carry):
        d_all = jnp.dot(kidx_s[t], qiT_s[...], preferred_element_type=f32)
        acc = jnp.zeros((KEY_TILE, Q_TILE), f32)
        for h in range(H_IDX):
            d = d_all[:, h * Q_TILE:(h + 1) * Q_TILE]
            acc = acc + wT[D_IDX + h:D_IDX + h + 1, :] * jnp.maximum(d, 0.0)
        score = acc * idx_scale
        k_chunk = (t * KEY_TILE + sub) // CHUNK
        key_s[t] = jnp.where(k_chunk <= q_chunk, _float_key(score), INT_MIN)
        return carry

    lax.fori_loop(0, nk, score_tile, 0)

    @pl.when(nk % 2 == 1)
    def _():
        key_s[nk] = jnp.full((KEY_TILE, Q_TILE), INT_MIN, jnp.int32)

    n_pairs = (nk + 1) // 2

    def count(pred):
        def body(p, c):
            for t in (2 * p, 2 * p + 1):
                m = pred(key_s[t], t).astype(jnp.int32)
                c = c + jnp.sum(m.reshape(KEY_TILE // 8, 8, Q_TILE), axis=0)
            return c
        c8 = lax.fori_loop(0, n_pairs, body, jnp.zeros((8, Q_TILE), jnp.int32))
        return jnp.sum(c8, axis=0, keepdims=True)

    thr0 = jnp.where(count(lambda k, t: k >= 0) >= n_sel, jnp.int32(0), INT_MIN)
    thr0 = jnp.broadcast_to(thr0, (1, Q_TILE)).astype(jnp.int32)

    def bit_step(j, thr):
        cand = thr | (jnp.int32(1) << (jnp.int32(30) - j))
        return jnp.where(count(lambda k, t: k >= cand) >= n_sel, cand, thr)

    thr = lax.fori_loop(0, 31, bit_step, thr0)

    c_gt = count(lambda k, t: k > thr)
    c_ge = count(lambda k, t: k >= thr)
    need = n_sel - c_gt
    has_tie = jnp.max(jnp.where((c_ge > n_sel) & (thr > INT_MIN), 1, 0)) > 0

    def tie_limit():
        def step(j, m):
            cand = m | (jnp.int32(1) << (jnp.int32(14) - j))
            c = count(lambda k, t: (k == thr) & ((t * KEY_TILE + sub) < cand))
            return jnp.where(c < need, cand, m)
        return lax.fori_loop(0, 15, step, jnp.zeros((1, Q_TILE), jnp.int32))

    m_lim = lax.cond(has_tie, tie_limit, lambda: jnp.full((1, Q_TILE), 2 ** 30, jnp.int32))

    def bias_pair(p, carry):
        for t in (2 * p, 2 * p + 1):
            k = key_s[t]
            sel = (k > thr) | ((k == thr) & ((t * KEY_TILE + sub) <= m_lim))
            sel = sel & (k > INT_MIN)
            bias_s[t] = jnp.where(sel, 0.0, NEG_BIG).astype(f32)
        return carry

    lax.fori_loop(0, n_pairs, bias_pair, 0)

    acc_s[...] = jnp.zeros_like(acc_s)

    def att_pair(p, carry):
        m_run, l_run = carry
        kv2 = jnp.concatenate([kv_s[2 * p], kv_s[2 * p + 1]], axis=0)
        kvT2 = jnp.concatenate([kvT_s[2 * p], kvT_s[2 * p + 1]], axis=1)
        bias2 = jnp.concatenate([bias_s[2 * p], bias_s[2 * p + 1]], axis=0)
        logit = jnp.dot(kv2, qT_s[...], preferred_element_type=f32) * attn_scale
        logit = logit + jnp.concatenate([bias2] * H_A, axis=1)
        m_new = jnp.maximum(m_run, jnp.max(logit, axis=0, keepdims=True))
        alpha = jnp.exp(m_run - m_new)
        pr = jnp.exp(logit - m_new)
        l_new = alpha * l_run + jnp.sum(pr, axis=0, keepdims=True)
        acc_s[...] = alpha * acc_s[...] + jnp.dot(kvT2, pr.astype(bf16), preferred_element_type=f32)
        return m_new, l_new

    init = (jnp.full((1, hq), NEG_BIG, f32), jnp.zeros((1, hq), f32))
    _, l_fin = lax.fori_loop(0, n_pairs, att_pair, init)
    inv_l = 1.0 / l_fin
    for h in range(H_A):
        sl = slice(h * Q_TILE, (h + 1) * Q_TILE)
        o_lat = (acc_s[:, sl] * inv_l[:, sl]).T
        o_ref[:, h * DH_A:(h + 1) * DH_A] = jnp.dot(
            o_lat.astype(bf16), wuv_ref[h], preferred_element_type=f32).astype(o_ref.dtype)


def _attn(proj3, g_kv, w_uv_bf, n_sel):
    b, s, _ = proj3.shape
    n_kt = s // KEY_TILE
    kern = functools.partial(_attn_kernel, n_sel=n_sel, n_kt=n_kt)
    return pl.pallas_call(
        kern,
        grid=(b, s // Q_TILE),
        in_specs=[
            pl.BlockSpec((None, Q_TILE, H_A * D_LATENT), lambda bi, i: (bi, i, C_QLAT // 1024)),
            pl.BlockSpec((None, Q_TILE, H_IDX * D_IDX), lambda bi, i: (bi, i, C_QIDX // 512)),
            pl.BlockSpec((None, Q_TILE, LANES), lambda bi, i: (bi, i, C_KW // LANES)),
            pl.BlockSpec((None, s, LANES), lambda bi, i: (bi, 0, C_CKV // LANES)),
            pl.BlockSpec((None, s, LANES), lambda bi, i: (bi, 0, C_KW // LANES)),
            pl.BlockSpec((1, D_LATENT), lambda bi, i: (0, 0)),
            pl.BlockSpec((H_A, D_LATENT, DH_A), lambda bi, i: (0, 0, 0)),
        ],
        out_specs=pl.BlockSpec((None, Q_TILE, D_BRANCH), lambda bi, i: (bi, i, 0)),
        out_shape=jax.ShapeDtypeStruct((b, s, D_BRANCH), bf16),
        scratch_shapes=[
            pltpu.VMEM((n_kt, KEY_TILE, D_LATENT), bf16),
            pltpu.VMEM((n_kt, D_LATENT, KEY_TILE), bf16),
            pltpu.VMEM((n_kt, KEY_TILE, D_IDX), bf16),
            pltpu.VMEM((n_kt, KEY_TILE, Q_TILE), jnp.int32),
            pltpu.VMEM((n_kt, KEY_TILE, Q_TILE), f32),
            pltpu.VMEM((D_LATENT, H_A * Q_TILE), bf16),
            pltpu.VMEM((D_IDX, H_IDX * Q_TILE), bf16),
            pltpu.VMEM((D_LATENT, H_A * Q_TILE), f32),
        ],
        compiler_params=_cparams(("parallel", "arbitrary")),
        name="attn",
    )(proj3, proj3, proj3, proj3, proj3, g_kv, w_uv_bf)


def _ret_kernel(q_ref, k_ref, v_ref, gr_ref, cos_ref, sin_ref, din_ref, dq_ref, dk_ref, dc_ref,
                gret_ref, o_ref, state_s):
    @pl.when(pl.program_id(1) == 0)
    def _():
        state_s[...] = jnp.zeros_like(state_s)

    cos = cos_ref[...]
    sin = sin_ref[...]

    def rot(x):
        return x * cos + pltpu.roll(x, DK_R // 2, axis=1) * sin

    for h in range(H_R):
        sl = slice(h * DK_R, (h + 1) * DK_R)
        q = rot(q_ref[:, sl].astype(f32)).astype(bf16)
        kf = rot(k_ref[:, sl].astype(f32)) * (DK_R ** -0.5)
        k = kf.astype(bf16)
        v = v_ref[:, sl]
        inner = lax.dot_general(q, k, (((1,), (1,)), ((), ())), preferred_element_type=f32) * din_ref[h]
        o = jnp.dot(inner.astype(bf16), v, preferred_element_type=f32)
        st = state_s[h]
        o = o + jnp.dot(q, st.astype(bf16), preferred_element_type=f32) * dq_ref[h]
        kd = (kf * dk_ref[h]).astype(bf16)
        state_s[h] = st * dc_ref[h] + jnp.dot(kd.T, v, preferred_element_type=f32)
        mu = jnp.mean(o, axis=-1, keepdims=True)
        var = jnp.mean(jnp.square(o - mu), axis=-1, keepdims=True)
        y = (o - mu) * lax.rsqrt(var + EPS) * gret_ref[:, sl]
        gate = gr_ref[:, sl].astype(f32)
        o_ref[:, sl] = (gate * jax.nn.sigmoid(gate) * y).astype(o_ref.dtype)


def _ret(proj3, cos_t, sin_t, d_in, d_q, d_k, d_c, g_ret):
    b, s, _ = proj3.shape
    c = RET_CHUNK
    w = H_R * DK_R

    def col(off):
        return pl.BlockSpec((None, c, w), lambda bi, ci: (bi, ci, off // w))

    return pl.pallas_call(
        _ret_kernel,
        grid=(b, s // c),
        in_specs=[
            col(C_QR), col(C_KR), col(C_VR), col(C_GR),
            pl.BlockSpec((c, DK_R), lambda bi, ci: (ci, 0)),
            pl.BlockSpec((c, DK_R), lambda bi, ci: (ci, 0)),
            pl.BlockSpec((H_R, c, c), lambda bi, ci: (0, 0, 0)),
            pl.BlockSpec((H_R, c, DK_R), lambda bi, ci: (0, 0, 0)),
            pl.BlockSpec((H_R, c, DK_R), lambda bi, ci: (0, 0, 0)),
            pl.BlockSpec((H_R, 1, DK_R), lambda bi, ci: (0, 0, 0)),
            pl.BlockSpec((1, w), lambda bi, ci: (0, 0)),
        ],
        out_specs=pl.BlockSpec((None, c, w), lambda bi, ci: (bi, ci, 0)),
        out_shape=jax.ShapeDtypeStruct((b, s, w), bf16),
        scratch_shapes=[pltpu.VMEM((H_R, DK_R, DV_R), f32)],
        compiler_params=_cparams(("parallel", "arbitrary")),
        name="ret",
    )(proj3, proj3, proj3, proj3, cos_t, sin_t, d_in, d_q, d_k, d_c, g_ret)


def _retention_tables(s):
    c = RET_CHUNK
    half = DK_R // 2
    freq = ROPE_BASE ** (-jnp.arange(half, dtype=f32) / half)
    ang = jnp.arange(s, dtype=f32)[:, None] * freq[None, :]
    cos = jnp.cos(ang)
    sin = jnp.sin(ang)
    cos_t = jnp.concatenate([cos, cos], axis=-1)
    sin_t = jnp.concatenate([-sin, sin], axis=-1)
    log_gamma = jnp.log1p(-jnp.exp2(-5.0 - jnp.arange(H_R, dtype=f32)))
    n = jnp.arange(c, dtype=f32)
    diff = n[:, None] - n[None, :]
    d_in = jnp.where(diff >= 0, jnp.exp(log_gamma[:, None, None] * jnp.maximum(diff, 0.0)), 0.0)
    d_q = jnp.broadcast_to(jnp.exp(log_gamma[:, None] * (n + 1.0))[:, :, None], (H_R, c, DK_R))
    d_k = jnp.broadcast_to(jnp.exp(log_gamma[:, None] * (c - 1.0 - n))[:, :, None], (H_R, c, DK_R))
    d_c = jnp.broadcast_to(jnp.exp(log_gamma * c)[:, None, None], (H_R, 1, DK_R))
    return cos_t, sin_t, d_in, d_q, d_k, d_c


def _mix_kernel(oa_ref, ob_ref, wb_ref, ga_ref, gb_ref, o_ref):
    a = jnp.dot(oa_ref[...], wb_ref[0], preferred_element_type=f32)
    b = jnp.dot(ob_ref[...], wb_ref[1], preferred_element_type=f32)
    ga = jax.nn.sigmoid(ga_ref[...].astype(f32))
    gb = jax.nn.sigmoid(gb_ref[...].astype(f32))
    o_ref[...] = (ga * a + gb * b).astype(o_ref.dtype)


def _mix(o_a, o_b, w_branch_bf, proj, tm, tn):
    t = o_a.shape[0]
    d = w_branch_bf.shape[2]
    return pl.pallas_call(
        _mix_kernel,
        grid=(t // tm, d // tn),
        in_specs=[
            pl.BlockSpec((tm, D_BRANCH), lambda i, j: (i, 0)),
            pl.BlockSpec((tm, D_BRANCH), lambda i, j: (i, 0)),
            pl.BlockSpec((N_BRANCH, D_BRANCH, tn), lambda i, j: (0, 0, j)),
            pl.BlockSpec((tm, tn), lambda i, j: (i, C_GBR // tn + j)),
            pl.BlockSpec((tm, tn), lambda i, j: (i, (C_GBR + d) // tn + j)),
        ],
        out_specs=pl.BlockSpec((tm, tn), lambda i, j: (i, j)),
        out_shape=jax.ShapeDtypeStruct((t, d), bf16),
        compiler_params=_cparams(("parallel", "parallel")),
        name="mix",
    )(o_a, o_b, w_branch_bf, proj, proj)


def _pack_rows(v):
    n = v.shape[1] // 2
    r = pltpu.bitcast(v.astype(bf16).astype(f32), jnp.uint32)
    w = (r[:, :n] >> 16) | (r[:, n:] & jnp.uint32(0xFFFF0000))
    return pltpu.einshape("r(ab)->rab", w, b=LANES)


def _unpack_rows(p):
    w = pltpu.einshape("rab->r(ab)", p)
    lo = pltpu.bitcast(w << 16, f32)
    hi = pltpu.bitcast(w & jnp.uint32(0xFFFF0000), f32)
    return lo, hi


def _split_bf16(a):
    hi = a.astype(bf16)
    lo = (a - hi.astype(f32)).astype(bf16)
    return hi, lo


def _outproj_kernel(mixed_ref, x_ref, wo_ref, g_ref, wr_ref, br_ref,
                    h_ref, xn_ref, eid_ref, gate_ref, cnt_ref):
    h = x_ref[...] + jnp.dot(mixed_ref[...], wo_ref[...], preferred_element_type=f32)
    h_ref[...] = h
    ms = jnp.mean(h * h, axis=-1, keepdims=True)
    xn = h * lax.rsqrt(ms + EPS) * g_ref[...]
    xn_ref[...] = _pack_rows(xn)

    x_hi, x_lo = _split_bf16(xn)
    hh_hl = jnp.dot(x_hi, wr_ref[...], preferred_element_type=f32)
    logit = (hh_hl[:, :LANES] + hh_hl[:, LANES:]
             + jnp.dot(x_lo, wr_ref[:, :LANES], preferred_element_type=f32)) + br_ref[...]

    lane = lax.broadcasted_iota(jnp.int32, logit.shape, 1)
    lanef = lane.astype(f32)
    neg = -jnp.inf

    def first_argmax(v, m):
        return jnp.min(jnp.where(v == m, lanef, float(LANES)), axis=-1, keepdims=True)

    lg = jnp.where(lane < N_GROUPS, logit, neg)
    mg = jnp.max(lg, axis=-1, keepdims=True)
    p_grp = 1.0 / jnp.sum(jnp.exp(lg - mg), axis=-1, keepdims=True)
    grp = first_argmax(lg, mg).astype(jnp.int32)

    e_lane = lane - N_GROUPS
    in_grp = (e_lane >= 0) & (e_lane < N_EXPERTS) & ((e_lane // EXP_PER_GROUP) == grp)
    le = jnp.where(in_grp, logit, neg)
    m1 = jnp.max(le, axis=-1, keepdims=True)
    i1 = first_argmax(le, m1)
    le2 = jnp.where(lanef == i1, neg, le)
    m2 = jnp.max(le2, axis=-1, keepdims=True)
    i2 = first_argmax(le2, m2)
    e2 = jnp.exp(m2 - m1)
    g1 = p_grp / (1.0 + e2)
    g2 = p_grp * e2 / (1.0 + e2)

    eid = jnp.where(lane == 0, i1, jnp.where(lane == 1, i2, float(N_GROUPS))) - float(N_GROUPS)
    eid_ref[...] = eid.astype(jnp.int32).T[:8, :]
    gate_ref[...] = jnp.where(lane == 0, g1, jnp.where(lane == 1, g2, 0.0))

    @pl.when(pl.program_id(0) == 0)
    def _():
        cnt_ref[...] = jnp.zeros_like(cnt_ref)

    half = pl.program_id(0) // (pl.num_programs(0) // ROUTE_HALVES)
    sub8 = lax.broadcasted_iota(jnp.int32, (8, LANES), 0)
    for s, idx in enumerate((i1, i2)):
        c = jnp.sum((lanef == idx).astype(jnp.int32), axis=0, keepdims=True)
        cnt_ref[...] += jnp.where(sub8 == s * ROUTE_HALVES + half, c, 0)


def _outproj(mixed, x2, w_out_bf, g_ffn, wr_hi_lo, b_r, tm):
    t, d = x2.shape
    row = lambda i: (i, 0)
    fixed = lambda i: (0, 0)
    return pl.pallas_call(
        _outproj_kernel,
        grid=(t // tm,),
        in_specs=[
            pl.BlockSpec((tm, d), row),
            pl.BlockSpec((tm, d), row),
            pl.BlockSpec((d, d), fixed),
            pl.BlockSpec((1, d), fixed),
            pl.BlockSpec((d, 2 * LANES), fixed),
            pl.BlockSpec((1, LANES), fixed),
        ],
        out_specs=[
            pl.BlockSpec((tm, d), row),
            pl.BlockSpec((tm, d // (2 * LANES), LANES), lambda i: (i, 0, 0)),
            pl.BlockSpec((8, tm), lambda i: (0, i)),
            pl.BlockSpec((tm, LANES), row),
            pl.BlockSpec((8, LANES), fixed),
        ],
        out_shape=[
            jax.ShapeDtypeStruct((t, d), f32),
            jax.ShapeDtypeStruct((t, d // (2 * LANES), LANES), jnp.uint32),
            jax.ShapeDtypeStruct((8, t), jnp.int32),
            jax.ShapeDtypeStruct((t, LANES), f32),
            jax.ShapeDtypeStruct((8, LANES), jnp.int32),
        ],
        compiler_params=_cparams(("arbitrary",)),
        name="outproj",
    )(mixed, x2, w_out_bf, g_ffn, wr_hi_lo, b_r)


ISSUE_UNROLL = 8
CAST_ROWS = 256


def _experts_kernel(bexp_ref, nexp_ref, rpack_ref, nused_ref, xn_hbm, wg_hbm, wu_hbm, wd_hbm, yt_hbm,
                    xbuf, ybuf, wg_st, wu_st, wd_st, wg_bf, wu_bf, wd_bf, gsem, ssem, wsem, *, n_tok):
    j = pl.program_id(0)
    n_used = nused_ref[0]
    slot = j % 2
    tok_bits = (n_tok - 1).bit_length()

    def rows_of(blk, fn):
        base = blk * ROW_BLOCK

        def body(k, c):
            r0 = pl.multiple_of(k * ISSUE_UNROLL, ISSUE_UNROLL)
            for u in range(ISSUE_UNROLL):
                fn(r0 + u, rpack_ref[base + r0 + u], 1)
            return c

        lax.fori_loop(0, ROW_BLOCK // ISSUE_UNROLL, body, 0)

    def start_gathers(blk, sl):
        def one(r, packed, queue):
            tok = packed & ((1 << tok_bits) - 1)
            pltpu.make_async_copy(xn_hbm.at[tok], xbuf.at[sl, r], gsem.at[sl]).start(priority=queue)
        rows_of(blk, one)

    def start_scatters(blk, sl):
        def one(r, packed, queue):
            row = lax.shift_right_logical(packed, tok_bits)
            pltpu.make_async_copy(ybuf.at[sl, r], yt_hbm.at[row], ssem.at[sl]).start(priority=queue)
        rows_of(blk, one)

    def wait_gathers(sl):
        pltpu.make_async_copy(xn_hbm.at[pl.ds(0, ROW_BLOCK)], xbuf.at[sl], gsem.at[sl]).wait()

    def wait_scatters(sl):
        pltpu.make_async_copy(ybuf.at[sl], yt_hbm.at[pl.ds(0, ROW_BLOCK)], ssem.at[sl]).wait()

    staged = ((wg_hbm, wg_st, wg_bf), (wu_hbm, wu_st, wu_bf), (wd_hbm, wd_st, wd_bf))

    def start_weights(e):
        for q, (src, st, _) in enumerate(staged):
            pltpu.make_async_copy(src.at[e], st, wsem.at[q]).start()

    def wait_and_cast_weights():
        for q, (src, st, dst) in enumerate(staged):
            pltpu.make_async_copy(src.at[0], st, wsem.at[q]).wait()

            def cast(c, carry, st=st, dst=dst):
                r = pl.multiple_of(c * CAST_ROWS, CAST_ROWS)
                dst[pl.ds(r, CAST_ROWS), :] = st[pl.ds(r, CAST_ROWS), :].astype(bf16)
                return carry

            lax.fori_loop(0, st.shape[0] // CAST_ROWS, cast, 0)

    @pl.when(j == 0)
    def _():
        start_weights(bexp_ref[0])
        start_gathers(0, 0)
        ybuf[...] = jnp.zeros_like(ybuf)
        for sl in range(2):
            spare = yt_hbm.at[pl.ds(2 * n_tok + sl * ROW_BLOCK, ROW_BLOCK)]
            pltpu.make_async_copy(ybuf.at[sl], spare, ssem.at[sl]).start()
        for sl in range(2):
            wait_scatters(sl)

    @pl.when(j < n_used)
    def _():
        e = bexp_ref[j]

        @pl.when((j == 0) | (bexp_ref[jnp.maximum(j - 1, 0)] != e))
        def _():
            wait_and_cast_weights()

            @pl.when(nexp_ref[j] >= 0)
            def _():
                start_weights(nexp_ref[j])

        wait_gathers(slot)

        @pl.when(j + 1 < n_used)
        def _():
            start_gathers(j + 1, 1 - slot)

        @pl.when(j >= 2)
        def _():
            wait_scatters(slot)

        lo, hi = _unpack_rows(xbuf[slot])
        xb = jnp.concatenate([lo, hi], axis=1).astype(bf16)
        g = jnp.dot(xb, wg_bf[...], preferred_element_type=f32)
        u = jnp.dot(xb, wu_bf[...], preferred_element_type=f32)
        hm = (g * jax.nn.sigmoid(g) * u).astype(bf16)
        ybuf[slot] = _pack_rows(jnp.dot(hm, wd_bf[...], preferred_element_type=f32))
        start_scatters(j, slot)

        @pl.when(j == n_used - 1)
        def _():
            wait_scatters(slot)

            @pl.when(j >= 1)
            def _():
                wait_scatters(1 - slot)


def _experts(block_expert, next_expert, row_pack, n_used, xn_packed, w_gate, w_up, w_down):
    n_rows = row_pack.shape[0]
    n_tok = xn_packed.shape[0]
    tile = xn_packed.shape[1:]
    _, d, f = w_gate.shape
    assert d % CAST_ROWS == 0 and f % CAST_ROWS == 0
    any_space = pl.BlockSpec(memory_space=pl.ANY)
    grid_spec = pltpu.PrefetchScalarGridSpec(
        num_scalar_prefetch=4,
        grid=(n_rows // ROW_BLOCK,),
        in_specs=[any_space, any_space, any_space, any_space],
        out_specs=any_space,
        scratch_shapes=[pltpu.VMEM((2, ROW_BLOCK) + tile, jnp.uint32),
                        pltpu.VMEM((2, ROW_BLOCK) + tile, jnp.uint32),
                        pltpu.VMEM((d, f), f32), pltpu.VMEM((d, f), f32), pltpu.VMEM((f, d), f32),
                        pltpu.VMEM((d, f), bf16), pltpu.VMEM((d, f), bf16), pltpu.VMEM((f, d), bf16),
                        pltpu.SemaphoreType.DMA((2,)), pltpu.SemaphoreType.DMA((2,)),
                        pltpu.SemaphoreType.DMA((3,))],
    )
    return pl.pallas_call(
        functools.partial(_experts_kernel, n_tok=n_tok),
        grid_spec=grid_spec,
        out_shape=jax.ShapeDtypeStruct((2 * n_tok + 2 * ROW_BLOCK,) + tile, jnp.uint32),
        compiler_params=_cparams(("arbitrary",)),
        name="experts",
    )(block_expert, next_expert, row_pack, n_used, xn_packed, w_gate, w_up, w_down)


def _combine_kernel(h_ref, y0_ref, y1_ref, gate_ref, g_ref, o_ref):
    gate = gate_ref[...]
    y0 = jnp.concatenate(_unpack_rows(y0_ref[...]), axis=1)
    y1 = jnp.concatenate(_unpack_rows(y1_ref[...]), axis=1)
    hh = h_ref[...] + gate[:, 0:1] * y0 + gate[:, 1:2] * y1
    ms = jnp.mean(hh * hh, axis=-1, keepdims=True)
    o_ref[...] = hh * lax.rsqrt(ms + EPS) * g_ref[...]


def _combine(h, yt, gate, g_final, tm):
    t, d = h.shape
    nt = t // tm
    return pl.pallas_call(
        _combine_kernel,
        grid=(nt,),
        in_specs=[
            pl.BlockSpec((tm, d), lambda i: (i, 0)),
            pl.BlockSpec((tm,) + yt.shape[1:], lambda i: (i, 0, 0)),
            pl.BlockSpec((tm,) + yt.shape[1:], lambda i: (nt + i, 0, 0)),
            pl.BlockSpec((tm, LANES), lambda i: (i, 0)),
            pl.BlockSpec((1, d), lambda i: (0, 0)),
        ],
        out_specs=pl.BlockSpec((tm, d), lambda i: (i, 0)),
        out_shape=jax.ShapeDtypeStruct((t, d), f32),
        compiler_params=_cparams(("parallel",)),
        name="combine",
    )(h, yt, yt, gate, g_final)


def _route_kernel(eid_ref, cnt_ref, rpack_ref, bexp_ref, nexp_ref, nused_ref, *cur_refs, n_tok, n_blocks):
    tok_bits = (n_tok - 1).bit_length()
    chunk = 2 * n_tok // ROUTE_CHAINS

    def no_next(k, carry):
        nexp_ref[k] = -1
        return carry

    lax.fori_loop(0, n_blocks, no_next, 0)

    def per_expert(e, carry):
        blk, prev_blk, prev_nb = carry
        start = blk * ROW_BLOCK
        run = start
        for c in range(ROUTE_CHAINS):
            cur_refs[c][e] = run
            run = run + cnt_ref[c * N_EXPERTS + e]
        nb = (run - start + ROW_BLOCK - 1) // ROW_BLOCK

        def set_block(k, c):
            bexp_ref[blk + k] = e
            return c

        lax.fori_loop(0, nb, set_block, 0)

        def set_next(k, c):
            nexp_ref[prev_blk + k] = e
            return c

        lax.fori_loop(0, jnp.where(nb > 0, prev_nb, 0), set_next, 0)

        def set_pad(r, c):
            rpack_ref[r] = (2 * n_tok + (r & (2 * ROW_BLOCK - 1))) << tok_bits
            return c

        lax.fori_loop(run, start + nb * ROW_BLOCK, set_pad, 0)
        return blk + nb, jnp.where(nb > 0, blk, prev_blk), jnp.where(nb > 0, nb, prev_nb)

    n_used, _, _ = lax.fori_loop(0, N_EXPERTS, per_expert, (0, 0, 0))
    nused_ref[0] = n_used

    def tail_block(k, carry):
        bexp_ref[k] = N_EXPERTS - 1
        return carry

    lax.fori_loop(n_used, n_blocks, tail_block, 0)

    def tail_row(r, carry):
        rpack_ref[r] = (2 * n_tok + (r & (2 * ROW_BLOCK - 1))) << tok_bits
        return carry

    lax.fori_loop(n_used * ROW_BLOCK, n_blocks * ROW_BLOCK, tail_row, 0)

    def place(i, carry):
        for c in range(ROUTE_CHAINS):
            a = c * chunk + i
            e = eid_ref[a]
            p = cur_refs[c][e]
            cur_refs[c][e] = p + 1
            rpack_ref[p] = (a << tok_bits) | (a - (c * chunk // n_tok) * n_tok)
        return carry

    lax.fori_loop(0, chunk, place, 0)


def _route(eid_flat, counts, n_tok):
    n_asg = eid_flat.shape[0]
    n_rows = -(-(n_asg + N_EXPERTS * (ROW_BLOCK - 1)) // ROW_BLOCK) * ROW_BLOCK
    n_blocks = n_rows // ROW_BLOCK
    smem = pl.BlockSpec(memory_space=pltpu.SMEM)
    return pl.pallas_call(
        functools.partial(_route_kernel, n_tok=n_tok, n_blocks=n_blocks),
        in_specs=[smem, smem],
        out_specs=[smem, smem, smem, smem],
        out_shape=[jax.ShapeDtypeStruct((n_rows,), jnp.int32),
                   jax.ShapeDtypeStruct((n_blocks,), jnp.int32),
                   jax.ShapeDtypeStruct((n_blocks,), jnp.int32),
                   jax.ShapeDtypeStruct((1,), jnp.int32)],
        scratch_shapes=[pltpu.SMEM((N_EXPERTS,), jnp.int32)] * ROUTE_CHAINS,
        name="route",
    )(eid_flat, counts)


def _pick(n, prefs):
    for p in prefs:
        if n % p == 0:
            return p
    return n


def kernel(x, g_mix_norm, w_in, g_kv, w_uv, g_ret, w_branch, w_out, g_ffn_norm, w_router_group,
           b_router_group, w_router_expert, b_router_expert, w_expert_gate, w_expert_up,
           w_expert_down, g_final):
    b, s, d = x.shape
    t = b * s
    depth = w_in.shape[0]
    n_sel = min(TOPK_MAX, s // 4)
    assert s % RET_CHUNK == 0 and s % Q_TILE == 0

    cos_t, sin_t, d_in, d_q, d_k, d_c = _retention_tables(s)
    h2 = x.reshape(t, d)
    for l in range(depth):
        wl = w_in[l]
        sp = np.cumsum([0, H_A * D_LATENT, D_LATENT, H_IDX * D_IDX, D_IDX, H_IDX,
                        H_R * DK_R, H_R * DK_R, H_R * DV_R, H_R * DV_R, N_BRANCH * d])
        seg = [wl[:, sp[k]:sp[k + 1]].astype(bf16) for k in range(10)]
        kw_pad = jnp.zeros((d, LANES - D_IDX - H_IDX), bf16)
        w_p = jnp.concatenate([seg[0], seg[9], seg[5], seg[6], seg[7], seg[8], seg[2], seg[1],
                               seg[3], seg[4], kw_pad], axis=1)
        assert w_p.shape[1] == D_IN_P

        proj = _proj(h2, g_mix_norm[l].reshape(1, d), w_p, _pick(t, (1024, 512, 256)), 1664)
        proj3 = proj.reshape(b, s, D_IN_P)

        o_a = _attn(proj3, g_kv[l].reshape(1, D_LATENT), w_uv[l].astype(bf16), n_sel)
        o_b = _ret(proj3, cos_t, sin_t, d_in, d_q, d_k, d_c, g_ret[l].reshape(1, H_R * DV_R))

        mixed = _mix(o_a.reshape(t, D_BRANCH), o_b.reshape(t, D_BRANCH), w_branch[l].astype(bf16),
                     proj, _pick(t, (1024, 512, 256)), 512)

        w_r = jnp.concatenate([w_router_group[l], w_router_expert[l],
                               jnp.zeros((d, LANES - N_GROUPS - N_EXPERTS), f32)], axis=1)
        b_r = jnp.concatenate([b_router_group[l], b_router_expert[l],
                               jnp.zeros((LANES - N_GROUPS - N_EXPERTS,), f32)]).reshape(1, LANES)
        wr_hi = w_r.astype(bf16)
        wr_lo = (w_r - wr_hi.astype(f32)).astype(bf16)
        h2, xn, eid_t, gate, cnt = _outproj(mixed, h2, w_out[l].astype(bf16), g_ffn_norm[l].reshape(1, d),
                                            jnp.concatenate([wr_hi, wr_lo], axis=1), b_r, _pick(t, (256,)))

        row_pack, block_expert, next_expert, n_used = _route(
            eid_t[:2].reshape(-1), cnt[:ROUTE_CHAINS, N_GROUPS:N_GROUPS + N_EXPERTS].reshape(-1), t)
        yt = _experts(block_expert, next_expert, row_pack, n_used, xn,
                      w_expert_gate[l], w_expert_up[l], w_expert_down[l])
        assert depth == 1
        h2 = _combine(h2, yt, gate, g_final.reshape(1, d), _pick(t, (256,)))
    return h2.reshape(b, s, d)
```

```python
import functools

import jax
import jax.numpy as jnp
import numpy as np
from jax import lax
from jax.experimental import pallas as pl
from jax.experimental.pallas import tpu as pltpu

EPS = 1e-6
CHUNK = 64
H_A = 8
D_LATENT = 128
DH_A = 128
H_IDX = 8
D_IDX = 64
TOPK_MAX = 256
H_R = 8
DK_R = 128
DV_R = 128
ROPE_BASE = 10000.0
D_BRANCH = 1024
N_BRANCH = 2
N_GROUPS = 4
EXP_PER_GROUP = 8
N_EXPERTS = N_GROUPS * EXP_PER_GROUP
D_EXPERT = 1024

LANES = 128
KEY_TILE = 256
Q_TILE = 256
RET_CHUNK = 256
ROW_BLOCK = 256
ROUTE_HALVES = 4
ROUTE_CHAINS = 2 * ROUTE_HALVES
VMEM_LIMIT = 56 * 1024 * 1024

C_QLAT = 0
C_GBR = 1024
C_QR = 5120
C_KR = 6144
C_VR = 7168
C_GR = 8192
C_QIDX = 9216
C_CKV = 9728
C_KW = 9856
D_IN_P = 9984

INT_MIN = np.int32(-2 ** 31)
NEG_BIG = -1e30

bf16 = jnp.bfloat16
f32 = jnp.float32


def _cparams(sem):
    return pltpu.CompilerParams(dimension_semantics=sem, vmem_limit_bytes=VMEM_LIMIT)


def _proj_kernel(x_ref, g_ref, w_ref, o_ref, xn_ref):
    @pl.when(pl.program_id(1) == 0)
    def _():
        x = x_ref[...]
        ms = jnp.mean(x * x, axis=-1, keepdims=True)
        xn_ref[...] = (x * lax.rsqrt(ms + EPS) * g_ref[...]).astype(bf16)

    o_ref[...] = jnp.dot(xn_ref[...], w_ref[...], preferred_element_type=f32).astype(o_ref.dtype)


def _proj(x2, g, w_p, tm, tn):
    t, d = x2.shape
    n = w_p.shape[1]
    return pl.pallas_call(
        _proj_kernel,
        grid=(t // tm, n // tn),
        in_specs=[
            pl.BlockSpec((tm, d), lambda i, j: (i, 0)),
            pl.BlockSpec((1, d), lambda i, j: (0, 0)),
            pl.BlockSpec((d, tn), lambda i, j: (0, j)),
        ],
        out_specs=pl.BlockSpec((tm, tn), lambda i, j: (i, j)),
        out_shape=jax.ShapeDtypeStruct((t, n), bf16),
        scratch_shapes=[pltpu.VMEM((tm, d), bf16)],
        compiler_params=_cparams(("parallel", "arbitrary")),
        name="proj",
    )(x2, g, w_p)


def _float_key(s):
    bits = pltpu.bitcast(s, jnp.int32)
    key = bits ^ ((bits >> 31) & jnp.int32(0x7FFFFFFF))
    return jnp.where(s == 0.0, jnp.int32(0), key)


def _attn_kernel(qlat_ref, qidx_ref, kwq_ref, ckv_ref, kwk_ref, gkv_ref, wuv_ref, o_ref,
                 kv_s, kvT_s, kidx_s, key_s, bias_s, qT_s, qiT_s, acc_s, *, n_sel, n_kt):
    i = pl.program_id(1)
    idx_scale = (H_IDX ** -0.5) * (D_IDX ** -0.5)
    attn_scale = D_LATENT ** -0.5
    hq = H_A * Q_TILE

    @pl.when(i == 0)
    def _():
        g = gkv_ref[...]
        for t in range(n_kt):
            c = ckv_ref[t * KEY_TILE:(t + 1) * KEY_TILE, :].astype(f32)
            ms = jnp.mean(c * c, axis=-1, keepdims=True)
            kv = c * lax.rsqrt(ms + EPS) * g
            kv_s[t] = kv.astype(bf16)
            kvT_s[t] = kv.T.astype(bf16)
            kidx_s[t] = kwk_ref[t * KEY_TILE:(t + 1) * KEY_TILE, :D_IDX]

    nk = ((i + 1) * Q_TILE + KEY_TILE - 1) // KEY_TILE
    lane = lax.broadcasted_iota(jnp.int32, (1, Q_TILE), 1)
    sub = lax.broadcasted_iota(jnp.int32, (KEY_TILE, 1), 0)
    q_chunk = (i * Q_TILE + lane) // CHUNK

    wT = kwq_ref[...].astype(f32).T
    for h in range(H_A):
        qT_s[:, h * Q_TILE:(h + 1) * Q_TILE] = qlat_ref[:, h * D_LATENT:(h + 1) * D_LATENT].astype(f32).T.astype(bf16)
    for h in range(H_IDX):
        qiT_s[:, h * Q_TILE:(h + 1) * Q_TILE] = qidx_ref[:, h * D_IDX:(h + 1) * D_IDX].astype(f32).T.astype(bf16)

    def score_tile(t, carry):
        d_all = jnp.dot(kidx_s[t], qiT_s[...], preferred_element_type=f32)
        acc = jnp.zeros((KEY_TILE, Q_TILE), f32)
        for h in range(H_IDX):
            d = d_all[:, h * Q_TILE:(h + 1) * Q_TILE]
            acc = acc + wT[D_IDX + h:D_IDX + h + 1, :] * jnp.maximum(d, 0.0)
        score = acc * idx_scale
        k_chunk = (t * KEY_TILE + sub) // CHUNK
        key_s[t] = jnp.where(k_chunk <= q_chunk, _float_key(score), INT_MIN)
        return carry

    lax.fori_loop(0, nk, score_tile, 0)

    @pl.when(nk % 2 == 1)
    def _():
        key_s[nk] = jnp.full((KEY_TILE, Q_TILE), INT_MIN, jnp.int32)

    n_pairs = (nk + 1) // 2

    def count(pred):
        def body(p, c):
            for t in (2 * p, 2 * p + 1):
                m = pred(key_s[t], t).astype(jnp.int32)
                c = c + jnp.sum(m.reshape(KEY_TILE // 8, 8, Q_TILE), axis=0)
            return c
        c8 = lax.fori_loop(0, n_pairs, body, jnp.zeros((8, Q_TILE), jnp.int32))
        return jnp.sum(c8, axis=0, keepdims=True)

    thr0 = jnp.where(count(lambda k, t: k >= 0) >= n_sel, jnp.int32(0), INT_MIN)
    thr0 = jnp.broadcast_to(thr0, (1, Q_TILE)).astype(jnp.int32)

    def bit_step(j, thr):
        cand = thr | (jnp.int32(1) << (jnp.int32(30) - j))
        return jnp.where(count(lambda k, t: k >= cand) >= n_sel, cand, thr)

    thr = lax.fori_loop(0, 31, bit_step, thr0)

    c_gt = count(lambda k, t: k > thr)
    c_ge = count(lambda k, t: k >= thr)
    need = n_sel - c_gt
    has_tie = jnp.max(jnp.where((c_ge > n_sel) & (thr > INT_MIN), 1, 0)) > 0

    def tie_limit():
        def step(j, m):
            cand = m | (jnp.int32(1) << (jnp.int32(14) - j))
            c = count(lambda k, t: (k == thr) & ((t * KEY_TILE + sub) < cand))
            return jnp.where(c < need, cand, m)
        return lax.fori_loop(0, 15, step, jnp.zeros((1, Q_TILE), jnp.int32))

    m_lim = lax.cond(has_tie, tie_limit, lambda: jnp.full((1, Q_TILE), 2 ** 30, jnp.int32))

    def bias_pair(p, carry):
        for t in (2 * p, 2 * p + 1):
            k = key_s[t]
            sel = (k > thr) | ((k == thr) & ((t * KEY_TILE + sub) <= m_lim))
            sel = sel & (k > INT_MIN)
            bias_s[t] = jnp.where(sel, 0.0, NEG_BIG).astype(f32)
        return carry

    lax.fori_loop(0, n_pairs, bias_pair, 0)

    acc_s[...] = jnp.zeros_like(acc_s)

    def att_pair(p, carry):
        m_run, l_run = carry
        kv2 = jnp.concatenate([kv_s[2 * p], kv_s[2 * p + 1]], axis=0)
        kvT2 = jnp.concatenate([kvT_s[2 * p], kvT_s[2 * p + 1]], axis=1)
        bias2 = jnp.concatenate([bias_s[2 * p], bias_s[2 * p + 1]], axis=0)
        logit = jnp.dot(kv2, qT_s[...], preferred_element_type=f32) * attn_scale
        logit = logit + jnp.concatenate([bias2] * H_A, axis=1)
        m_new = jnp.maximum(m_run, jnp.max(logit, axis=0, keepdims=True))
        alpha = jnp.exp(m_run - m_new)
        pr = jnp.exp(logit - m_new)
        l_new = alpha * l_run + jnp.sum(pr, axis=0, keepdims=True)
        acc_s[...] = alpha * acc_s[...] + jnp.dot(kvT2, pr.astype(bf16), preferred_element_type=f32)
        return m_new, l_new

    init = (jnp.full((1, hq), NEG_BIG, f32), jnp.zeros((1, hq), f32))
    _, l_fin = lax.fori_loop(0, n_pairs, att_pair, init)
    inv_l = 1.0 / l_fin
    for h in range(H_A):
        sl = slice(h * Q_TILE, (h + 1) * Q_TILE)
        o_lat = (acc_s[:, sl] * inv_l[:, sl]).T
        o_ref[:, h * DH_A:(h + 1) * DH_A] = jnp.dot(
            o_lat.astype(bf16), wuv_ref[h], preferred_element_type=f32).astype(o_ref.dtype)


def _attn(proj3, g_kv, w_uv_bf, n_sel):
    b, s, _ = proj3.shape
    n_kt = s // KEY_TILE
    kern = functools.partial(_attn_kernel, n_sel=n_sel, n_kt=n_kt)
    return pl.pallas_call(
        kern,
        grid=(b, s // Q_TILE),
        in_specs=[
            pl.BlockSpec((None, Q_TILE, H_A * D_LATENT), lambda bi, i: (bi, i, C_QLAT // 1024)),
            pl.BlockSpec((None, Q_TILE, H_IDX * D_IDX), lambda bi, i: (bi, i, C_QIDX // 512)),
            pl.BlockSpec((None, Q_TILE, LANES), lambda bi, i: (bi, i, C_KW // LANES)),
            pl.BlockSpec((None, s, LANES), lambda bi, i: (bi, 0, C_CKV // LANES)),
            pl.BlockSpec((None, s, LANES), lambda bi, i: (bi, 0, C_KW // LANES)),
            pl.BlockSpec((1, D_LATENT), lambda bi, i: (0, 0)),
            pl.BlockSpec((H_A, D_LATENT, DH_A), lambda bi, i: (0, 0, 0)),
        ],
        out_specs=pl.BlockSpec((None, Q_TILE, D_BRANCH), lambda bi, i: (bi, i, 0)),
        out_shape=jax.ShapeDtypeStruct((b, s, D_BRANCH), bf16),
        scratch_shapes=[
            pltpu.VMEM((n_kt, KEY_TILE, D_LATENT), bf16),
            pltpu.VMEM((n_kt, D_LATENT, KEY_TILE), bf16),
            pltpu.VMEM((n_kt, KEY_TILE, D_IDX), bf16),
            pltpu.VMEM((n_kt, KEY_TILE, Q_TILE), jnp.int32),
            pltpu.VMEM((n_kt, KEY_TILE, Q_TILE), f32),
            pltpu.VMEM((D_LATENT, H_A * Q_TILE), bf16),
            pltpu.VMEM((D_IDX, H_IDX * Q_TILE), bf16),
            pltpu.VMEM((D_LATENT, H_A * Q_TILE), f32),
        ],
        compiler_params=_cparams(("parallel", "arbitrary")),
        name="attn",
    )(proj3, proj3, proj3, proj3, proj3, g_kv, w_uv_bf)


def _ret_kernel(q_ref, k_ref, v_ref, gr_ref, cos_ref, sin_ref, din_ref, dq_ref, dk_ref, dc_ref,
                gret_ref, o_ref, state_s):
    @pl.when(pl.program_id(1) == 0)
    def _():
        state_s[...] = jnp.zeros_like(state_s)

    cos = cos_ref[...]
    sin = sin_ref[...]

    def rot(x):
        return x * cos + pltpu.roll(x, DK_R // 2, axis=1) * sin

    for h in range(H_R):
        sl = slice(h * DK_R, (h + 1) * DK_R)
        q = rot(q_ref[:, sl].astype(f32)).astype(bf16)
        kf = rot(k_ref[:, sl].astype(f32)) * (DK_R ** -0.5)
        k = kf.astype(bf16)
        v = v_ref[:, sl]
        inner = lax.dot_general(q, k, (((1,), (1,)), ((), ())), preferred_element_type=f32) * din_ref[h]
        o = jnp.dot(inner.astype(bf16), v, preferred_element_type=f32)
        st = state_s[h]
        o = o + jnp.dot(q, st.astype(bf16), preferred_element_type=f32) * dq_ref[h]
        kd = (kf * dk_ref[h]).astype(bf16)
        state_s[h] = st * dc_ref[h] + jnp.dot(kd.T, v, preferred_element_type=f32)
        mu = jnp.mean(o, axis=-1, keepdims=True)
        var = jnp.mean(jnp.square(o - mu), axis=-1, keepdims=True)
        y = (o - mu) * lax.rsqrt(var + EPS) * gret_ref[:, sl]
        gate = gr_ref[:, sl].astype(f32)
        o_ref[:, sl] = (gate * jax.nn.sigmoid(gate) * y).astype(o_ref.dtype)


def _ret(proj3, cos_t, sin_t, d_in, d_q, d_k, d_c, g_ret):
    b, s, _ = proj3.shape
    c = RET_CHUNK
    w = H_R * DK_R

    def col(off):
        return pl.BlockSpec((None, c, w), lambda bi, ci: (bi, ci, off // w))

    return pl.pallas_call(
        _ret_kernel,
        grid=(b, s // c),
        in_specs=[
            col(C_QR), col(C_KR), col(C_VR), col(C_GR),
            pl.BlockSpec((c, DK_R), lambda bi, ci: (ci, 0)),
            pl.BlockSpec((c, DK_R), lambda bi, ci: (ci, 0)),
            pl.BlockSpec((H_R, c, c), lambda bi, ci: (0, 0, 0)),
            pl.BlockSpec((H_R, c, DK_R), lambda bi, ci: (0, 0, 0)),
            pl.BlockSpec((H_R, c, DK_R), lambda bi, ci: (0, 0, 0)),
            pl.BlockSpec((H_R, 1, DK_R), lambda bi, ci: (0, 0, 0)),
            pl.BlockSpec((1, w), lambda bi, ci: (0, 0)),
        ],
        out_specs=pl.BlockSpec((None, c, w), lambda bi, ci: (bi, ci, 0)),
        out_shape=jax.ShapeDtypeStruct((b, s, w), bf16),
        scratch_shapes=[pltpu.VMEM((H_R, DK_R, DV_R), f32)],
        compiler_params=_cparams(("parallel", "arbitrary")),
        name="ret",
    )(proj3, proj3, proj3, proj3, cos_t, sin_t, d_in, d_q, d_k, d_c, g_ret)


def _retention_tables(s):
    c = RET_CHUNK
    half = DK_R // 2
    freq = ROPE_BASE ** (-jnp.arange(half, dtype=f32) / half)
    ang = jnp.arange(s, dtype=f32)[:, None] * freq[None, :]
    cos = jnp.cos(ang)
    sin = jnp.sin(ang)
    cos_t = jnp.concatenate([cos, cos], axis=-1)
    sin_t = jnp.concatenate([-sin, sin], axis=-1)
    log_gamma = jnp.log1p(-jnp.exp2(-5.0 - jnp.arange(H_R, dtype=f32)))
    n = jnp.arange(c, dtype=f32)
    diff = n[:, None] - n[None, :]
    d_in = jnp.where(diff >= 0, jnp.exp(log_gamma[:, None, None] * jnp.maximum(diff, 0.0)), 0.0)
    d_q = jnp.broadcast_to(jnp.exp(log_gamma[:, None] * (n + 1.0))[:, :, None], (H_R, c, DK_R))
    d_k = jnp.broadcast_to(jnp.exp(log_gamma[:, None] * (c - 1.0 - n))[:, :, None], (H_R, c, DK_R))
    d_c = jnp.broadcast_to(jnp.exp(log_gamma * c)[:, None, None], (H_R, 1, DK_R))
    return cos_t, sin_t, d_in, d_q, d_k, d_c


def _mix_kernel(oa_ref, ob_ref, wb_ref, ga_ref, gb_ref, o_ref):
    a = jnp.dot(oa_ref[...], wb_ref[0], preferred_element_type=f32)
    b = jnp.dot(ob_ref[...], wb_ref[1], preferred_element_type=f32)
    ga = jax.nn.sigmoid(ga_ref[...].astype(f32))
    gb = jax.nn.sigmoid(gb_ref[...].astype(f32))
    o_ref[...] = (ga * a + gb * b).astype(o_ref.dtype)


def _mix(o_a, o_b, w_branch_bf, proj, tm, tn):
    t = o_a.shape[0]
    d = w_branch_bf.shape[2]
    return pl.pallas_call(
        _mix_kernel,
        grid=(t // tm, d // tn),
        in_specs=[
            pl.BlockSpec((tm, D_BRANCH), lambda i, j: (i, 0)),
            pl.BlockSpec((tm, D_BRANCH), lambda i, j: (i, 0)),
            pl.BlockSpec((N_BRANCH, D_BRANCH, tn), lambda i, j: (0, 0, j)),
            pl.BlockSpec((tm, tn), lambda i, j: (i, C_GBR // tn + j)),
            pl.BlockSpec((tm, tn), lambda i, j: (i, (C_GBR + d) // tn + j)),
        ],
        out_specs=pl.BlockSpec((tm, tn), lambda i, j: (i, j)),
        out_shape=jax.ShapeDtypeStruct((t, d), bf16),
        compiler_params=_cparams(("parallel", "parallel")),
        name="mix",
    )(o_a, o_b, w_branch_bf, proj, proj)


def _pack_rows(v):
    n = v.shape[1] // 2
    r = pltpu.bitcast(v.astype(bf16).astype(f32), jnp.uint32)
    w = (r[:, :n] >> 16) | (r[:, n:] & jnp.uint32(0xFFFF0000))
    return pltpu.einshape("r(ab)->rab", w, b=LANES)


def _unpack_rows(p):
    w = pltpu.einshape("rab->r(ab)", p)
    lo = pltpu.bitcast(w << 16, f32)
    hi = pltpu.bitcast(w & jnp.uint32(0xFFFF0000), f32)
    return lo, hi


def _split_bf16(a):
    hi = a.astype(bf16)
    lo = (a - hi.astype(f32)).astype(bf16)
    return hi, lo


def _outproj_kernel(mixed_ref, x_ref, wo_ref, g_ref, wr_ref, br_ref,
                    h_ref, xn_ref, eid_ref, gate_ref, cnt_ref):
    h = x_ref[...] + jnp.dot(mixed_ref[...], wo_ref[...], preferred_element_type=f32)
    h_ref[...] = h
    ms = jnp.mean(h * h, axis=-1, keepdims=True)
    xn = h * lax.rsqrt(ms + EPS) * g_ref[...]
    xn_ref[...] = _pack_rows(xn)

    x_hi, x_lo = _split_bf16(xn)
    hh_hl = jnp.dot(x_hi, wr_ref[...], preferred_element_type=f32)
    logit = (hh_hl[:, :LANES] + hh_hl[:, LANES:]
             + jnp.dot(x_lo, wr_ref[:, :LANES], preferred_element_type=f32)) + br_ref[...]

    lane = lax.broadcasted_iota(jnp.int32, logit.shape, 1)
    lanef = lane.astype(f32)
    neg = -jnp.inf

    def first_argmax(v, m):
        return jnp.min(jnp.where(v == m, lanef, float(LANES)), axis=-1, keepdims=True)

    lg = jnp.where(lane < N_GROUPS, logit, neg)
    mg = jnp.max(lg, axis=-1, keepdims=True)
    p_grp = 1.0 / jnp.sum(jnp.exp(lg - mg), axis=-1, keepdims=True)
    grp = first_argmax(lg, mg).astype(jnp.int32)

    e_lane = lane - N_GROUPS
    in_grp = (e_lane >= 0) & (e_lane < N_EXPERTS) & ((e_lane // EXP_PER_GROUP) == grp)
    le = jnp.where(in_grp, logit, neg)
    m1 = jnp.max(le, axis=-1, keepdims=True)
    i1 = first_argmax(le, m1)
    le2 = jnp.where(lanef == i1, neg, le)
    m2 = jnp.max(le2, axis=-1, keepdims=True)
    i2 = first_argmax(le2, m2)
    e2 = jnp.exp(m2 - m1)
    g1 = p_grp / (1.0 + e2)
    g2 = p_grp * e2 / (1.0 + e2)

    eid = jnp.where(lane == 0, i1, jnp.where(lane == 1, i2, float(N_GROUPS))) - float(N_GROUPS)
    eid_ref[...] = eid.astype(jnp.int32).T[:8, :]
    gate_ref[...] = jnp.where(lane == 0, g1, jnp.where(lane == 1, g2, 0.0))

    @pl.when(pl.program_id(0) == 0)
    def _():
        cnt_ref[...] = jnp.zeros_like(cnt_ref)

    half = pl.program_id(0) // (pl.num_programs(0) // ROUTE_HALVES)
    sub8 = lax.broadcasted_iota(jnp.int32, (8, LANES), 0)
    for s, idx in enumerate((i1, i2)):
        c = jnp.sum((lanef == idx).astype(jnp.int32), axis=0, keepdims=True)
        cnt_ref[...] += jnp.where(sub8 == s * ROUTE_HALVES + half, c, 0)


def _outproj(mixed, x2, w_out_bf, g_ffn, wr_hi_lo, b_r, tm):
    t, d = x2.shape
    row = lambda i: (i, 0)
    fixed = lambda i: (0, 0)
    return pl.pallas_call(
        _outproj_kernel,
        grid=(t // tm,),
        in_specs=[
            pl.BlockSpec((tm, d), row),
            pl.BlockSpec((tm, d), row),
            pl.BlockSpec((d, d), fixed),
            pl.BlockSpec((1, d), fixed),
            pl.BlockSpec((d, 2 * LANES), fixed),
            pl.BlockSpec((1, LANES), fixed),
        ],
        out_specs=[
            pl.BlockSpec((tm, d), row),
            pl.BlockSpec((tm, d // (2 * LANES), LANES), lambda i: (i, 0, 0)),
            pl.BlockSpec((8, tm), lambda i: (0, i)),
            pl.BlockSpec((tm, LANES), row),
            pl.BlockSpec((8, LANES), fixed),
        ],
        out_shape=[
            jax.ShapeDtypeStruct((t, d), f32),
            jax.ShapeDtypeStruct((t, d // (2 * LANES), LANES), jnp.uint32),
            jax.ShapeDtypeStruct((8, t), jnp.int32),
            jax.ShapeDtypeStruct((t, LANES), f32),
            jax.ShapeDtypeStruct((8, LANES), jnp.int32),
        ],
        compiler_params=_cparams(("arbitrary",)),
        name="outproj",
    )(mixed, x2, w_out_bf, g_ffn, wr_hi_lo, b_r)


ISSUE_UNROLL = 8
CAST_ROWS = 256


def _experts_kernel(bexp_ref, nexp_ref, rpack_ref, nused_ref, xn_hbm, wg_hbm, wu_hbm, wd_hbm, yt_hbm,
                    xbuf, ybuf, wg_st, wu_st, wd_st, wg_bf, wu_bf, wd_bf, gsem, ssem, wsem, *, n_tok):
    j = pl.program_id(0)
    n_used = nused_ref[0]
    slot = j % 2
    tok_bits = (n_tok - 1).bit_length()

    def rows_of(blk, fn):
        base = blk * ROW_BLOCK

        def body(k, c):
            r0 = pl.multiple_of(k * ISSUE_UNROLL, ISSUE_UNROLL)
            for u in range(ISSUE_UNROLL):
                fn(r0 + u, rpack_ref[base + r0 + u], 1)
            return c

        lax.fori_loop(0, ROW_BLOCK // ISSUE_UNROLL, body, 0)

    def start_gathers(blk, sl):
        def one(r, packed, queue):
            tok = packed & ((1 << tok_bits) - 1)
            pltpu.make_async_copy(xn_hbm.at[tok], xbuf.at[sl, r], gsem.at[sl]).start(priority=queue)
        rows_of(blk, one)

    def start_scatters(blk, sl):
        def one(r, packed, queue):
            row = lax.shift_right_logical(packed, tok_bits)
            pltpu.make_async_copy(ybuf.at[sl, r], yt_hbm.at[row], ssem.at[sl]).start(priority=queue)
        rows_of(blk, one)

    def wait_gathers(sl):
        pltpu.make_async_copy(xn_hbm.at[pl.ds(0, ROW_BLOCK)], xbuf.at[sl], gsem.at[sl]).wait()

    def wait_scatters(sl):
        pltpu.make_async_copy(ybuf.at[sl], yt_hbm.at[pl.ds(0, ROW_BLOCK)], ssem.at[sl]).wait()

    staged = ((wg_hbm, wg_st, wg_bf), (wu_hbm, wu_st, wu_bf), (wd_hbm, wd_st, wd_bf))

    def start_weights(e):
        for q, (src, st, _) in enumerate(staged):
            pltpu.make_async_copy(src.at[e], st, wsem.at[q]).start()

    def wait_and_cast_weights():
        for q, (src, st, dst) in enumerate(staged):
            pltpu.make_async_copy(src.at[0], st, wsem.at[q]).wait()

            def cast(c, carry, st=st, dst=dst):
                r = pl.multiple_of(c * CAST_ROWS, CAST_ROWS)
                dst[pl.ds(r, CAST_ROWS), :] = st[pl.ds(r, CAST_ROWS), :].astype(bf16)
                return carry

            lax.fori_loop(0, st.shape[0] // CAST_ROWS, cast, 0)

    @pl.when(j == 0)
    def _():
        start_weights(bexp_ref[0])
        start_gathers(0, 0)
        ybuf[...] = jnp.zeros_like(ybuf)
        for sl in range(2):
            spare = yt_hbm.at[pl.ds(2 * n_tok + sl * ROW_BLOCK, ROW_BLOCK)]
            pltpu.make_async_copy(ybuf.at[sl], spare, ssem.at[sl]).start()
        for sl in range(2):
            wait_scatters(sl)

    @pl.when(j < n_used)
    def _():
        e = bexp_ref[j]

        @pl.when((j == 0) | (bexp_ref[jnp.maximum(j - 1, 0)] != e))
        def _():
            wait_and_cast_weights()

            @pl.when(nexp_ref[j] >= 0)
            def _():
                start_weights(nexp_ref[j])

        wait_gathers(slot)

        @pl.when(j + 1 < n_used)
        def _():
            start_gathers(j + 1, 1 - slot)

        @pl.when(j >= 2)
        def _():
            wait_scatters(slot)

        lo, hi = _unpack_rows(xbuf[slot])
        xb = jnp.concatenate([lo, hi], axis=1).astype(bf16)
        g = jnp.dot(xb, wg_bf[...], preferred_element_type=f32)
        u = jnp.dot(xb, wu_bf[...], preferred_element_type=f32)
        hm = (g * jax.nn.sigmoid(g) * u).astype(bf16)
        ybuf[slot] = _pack_rows(jnp.dot(hm, wd_bf[...], preferred_element_type=f32))
        start_scatters(j, slot)

        @pl.when(j == n_used - 1)
        def _():
            wait_scatters(slot)

            @pl.when(j >= 1)
            def _():
                wait_scatters(1 - slot)


def _experts(block_expert, next_expert, row_pack, n_used, xn_packed, w_gate, w_up, w_down):
    n_rows = row_pack.shape[0]
    n_tok = xn_packed.shape[0]
    tile = xn_packed.shape[1:]
    _, d, f = w_gate.shape
    assert d % CAST_ROWS == 0 and f % CAST_ROWS == 0
    any_space = pl.BlockSpec(memory_space=pl.ANY)
    grid_spec = pltpu.PrefetchScalarGridSpec(
        num_scalar_prefetch=4,
        grid=(n_rows // ROW_BLOCK,),
        in_specs=[any_space, any_space, any_space, any_space],
        out_specs=any_space,
        scratch_shapes=[pltpu.VMEM((2, ROW_BLOCK) + tile, jnp.uint32),
                        pltpu.VMEM((2, ROW_BLOCK) + tile, jnp.uint32),
                        pltpu.VMEM((d, f), f32), pltpu.VMEM((d, f), f32), pltpu.VMEM((f, d), f32),
                        pltpu.VMEM((d, f), bf16), pltpu.VMEM((d, f), bf16), pltpu.VMEM((f, d), bf16),
                        pltpu.SemaphoreType.DMA((2,)), pltpu.SemaphoreType.DMA((2,)),
                        pltpu.SemaphoreType.DMA((3,))],
    )
    return pl.pallas_call(
        functools.partial(_experts_kernel, n_tok=n_tok),
        grid_spec=grid_spec,
        out_shape=jax.ShapeDtypeStruct((2 * n_tok + 2 * ROW_BLOCK,) + tile, jnp.uint32),
        compiler_params=_cparams(("arbitrary",)),
        name="experts",
    )(block_expert, next_expert, row_pack, n_used, xn_packed, w_gate, w_up, w_down)


def _combine_kernel(h_ref, y0_ref, y1_ref, gate_ref, g_ref, o_ref):
    gate = gate_ref[...]
    y0 = jnp.concatenate(_unpack_rows(y0_ref[...]), axis=1)
    y1 = jnp.concatenate(_unpack_rows(y1_ref[...]), axis=1)
    hh = h_ref[...] + gate[:, 0:1] * y0 + gate[:, 1:2] * y1
    ms = jnp.mean(hh * hh, axis=-1, keepdims=True)
    o_ref[...] = hh * lax.rsqrt(ms + EPS) * g_ref[...]


def _combine(h, yt, gate, g_final, tm):
    t, d = h.shape
    nt = t // tm
    return pl.pallas_call(
        _combine_kernel,
        grid=(nt,),
        in_specs=[
            pl.BlockSpec((tm, d), lambda i: (i, 0)),
            pl.BlockSpec((tm,) + yt.shape[1:], lambda i: (i, 0, 0)),
            pl.BlockSpec((tm,) + yt.shape[1:], lambda i: (nt + i, 0, 0)),
            pl.BlockSpec((tm, LANES), lambda i: (i, 0)),
            pl.BlockSpec((1, d), lambda i: (0, 0)),
        ],
        out_specs=pl.BlockSpec((tm, d), lambda i: (i, 0)),
        out_shape=jax.ShapeDtypeStruct((t, d), f32),
        compiler_params=_cparams(("parallel",)),
        name="combine",
    )(h, yt, yt, gate, g_final)


def _route_kernel(eid_ref, cnt_ref, rpack_ref, bexp_ref, nexp_ref, nused_ref, *cur_refs, n_tok, n_blocks):
    tok_bits = (n_tok - 1).bit_length()
    chunk = 2 * n_tok // ROUTE_CHAINS

    def no_next(k, carry):
        nexp_ref[k] = -1
        return carry

    lax.fori_loop(0, n_blocks, no_next, 0)

    def per_expert(e, carry):
        blk, prev_blk, prev_nb = carry
        start = blk * ROW_BLOCK
        run = start
        for c in range(ROUTE_CHAINS):
            cur_refs[c][e] = run
            run = run + cnt_ref[c * N_EXPERTS + e]
        nb = (run - start + ROW_BLOCK - 1) // ROW_BLOCK

        def set_block(k, c):
            bexp_ref[blk + k] = e
            return c

        lax.fori_loop(0, nb, set_block, 0)

        def set_next(k, c):
            nexp_ref[prev_blk + k] = e
            return c

        lax.fori_loop(0, jnp.where(nb > 0, prev_nb, 0), set_next, 0)

        def set_pad(r, c):
            rpack_ref[r] = (2 * n_tok + (r & (2 * ROW_BLOCK - 1))) << tok_bits
            return c

        lax.fori_loop(run, start + nb * ROW_BLOCK, set_pad, 0)
        return blk + nb, jnp.where(nb > 0, blk, prev_blk), jnp.where(nb > 0, nb, prev_nb)

    n_used, _, _ = lax.fori_loop(0, N_EXPERTS, per_expert, (0, 0, 0))
    nused_ref[0] = n_used

    def tail_block(k, carry):
        bexp_ref[k] = N_EXPERTS - 1
        return carry

    lax.fori_loop(n_used, n_blocks, tail_block, 0)

    def tail_row(r, carry):
        rpack_ref[r] = (2 * n_tok + (r & (2 * ROW_BLOCK - 1))) << tok_bits
        return carry

    lax.fori_loop(n_used * ROW_BLOCK, n_blocks * ROW_BLOCK, tail_row, 0)

    def place(i, carry):
        for c in range(ROUTE_CHAINS):
            a = c * chunk + i
            e = eid_ref[a]
            p = cur_refs[c][e]
            cur_refs[c][e] = p + 1
            rpack_ref[p] = (a << tok_bits) | (a - (c * chunk // n_tok) * n_tok)
        return carry

    lax.fori_loop(0, chunk, place, 0)


def _route(eid_flat, counts, n_tok):
    n_asg = eid_flat.shape[0]
    n_rows = -(-(n_asg + N_EXPERTS * (ROW_BLOCK - 1)) // ROW_BLOCK) * ROW_BLOCK
    n_blocks = n_rows // ROW_BLOCK
    smem = pl.BlockSpec(memory_space=pltpu.SMEM)
    return pl.pallas_call(
        functools.partial(_route_kernel, n_tok=n_tok, n_blocks=n_blocks),
        in_specs=[smem, smem],
        out_specs=[smem, smem, smem, smem],
        out_shape=[jax.ShapeDtypeStruct((n_rows,), jnp.int32),
                   jax.ShapeDtypeStruct((n_blocks,), jnp.int32),
                   jax.ShapeDtypeStruct((n_blocks,), jnp.int32),
                   jax.ShapeDtypeStruct((1,), jnp.int32)],
        scratch_shapes=[pltpu.SMEM((N_EXPERTS,), jnp.int32)] * ROUTE_CHAINS,
        name="route",
    )(eid_flat, counts)


def _pick(n, prefs):
    for p in prefs:
        if n % p == 0:
            return p
    return n


def kernel(x, g_mix_norm, w_in, g_kv, w_uv, g_ret, w_branch, w_out, g_ffn_norm, w_router_group,
           b_router_group, w_router_expert, b_router_expert, w_expert_gate, w_expert_up,
           w_expert_down, g_final):
    b, s, d = x.shape
    t = b * s
    depth = w_in.shape[0]
    n_sel = min(TOPK_MAX, s // 4)
    assert s % RET_CHUNK == 0 and s % Q_TILE == 0

    cos_t, sin_t, d_in, d_q, d_k, d_c = _retention_tables(s)
    h2 = x.reshape(t, d)
    for l in range(depth):
        wl = w_in[l]
        sp = np.cumsum([0, H_A * D_LATENT, D_LATENT, H_IDX * D_IDX, D_IDX, H_IDX,
                        H_R * DK_R, H_R * DK_R, H_R * DV_R, H_R * DV_R, N_BRANCH * d])
        seg = [wl[:, sp[k]:sp[k + 1]].astype(bf16) for k in range(10)]
        kw_pad = jnp.zeros((d, LANES - D_IDX - H_IDX), bf16)
        w_p = jnp.concatenate([seg[0], seg[9], seg[5], seg[6], seg[7], seg[8], seg[2], seg[1],
                               seg[3], seg[4], kw_pad], axis=1)
        assert w_p.shape[1] == D_IN_P

        proj = _proj(h2, g_mix_norm[l].reshape(1, d), w_p, _pick(t, (1024, 512, 256)), 1664)
        proj3 = proj.reshape(b, s, D_IN_P)

        o_a = _attn(proj3, g_kv[l].reshape(1, D_LATENT), w_uv[l].astype(bf16), n_sel)
        o_b = _ret(proj3, cos_t, sin_t, d_in, d_q, d_k, d_c, g_ret[l].reshape(1, H_R * DV_R))

        mixed = _mix(o_a.reshape(t, D_BRANCH), o_b.reshape(t, D_BRANCH), w_branch[l].astype(bf16),
                     proj, _pick(t, (1024, 512, 256)), 512)

        w_r = jnp.concatenate([w_router_group[l], w_router_expert[l],
                               jnp.zeros((d, LANES - N_GROUPS - N_EXPERTS), f32)], axis=1)
        b_r = jnp.concatenate([b_router_group[l], b_router_expert[l],
                               jnp.zeros((LANES - N_GROUPS - N_EXPERTS,), f32)]).reshape(1, LANES)
        wr_hi = w_r.astype(bf16)
        wr_lo = (w_r - wr_hi.astype(f32)).astype(bf16)
        h2, xn, eid_t, gate, cnt = _outproj(mixed, h2, w_out[l].astype(bf16), g_ffn_norm[l].reshape(1, d),
                                            jnp.concatenate([wr_hi, wr_lo], axis=1), b_r, _pick(t, (256,)))

        row_pack, block_expert, next_expert, n_used = _route(
            eid_t[:2].reshape(-1), cnt[:ROUTE_CHAINS, N_GROUPS:N_GROUPS + N_EXPERTS].reshape(-1), t)
        yt = _experts(block_expert, next_expert, row_pack, n_used, xn,
                      w_expert_gate[l], w_expert_up[l], w_expert_down[l])
        assert depth == 1
        h2 = _combine(h2, yt, gate, g_final.reshape(1, d), _pick(t, (256,)))
    return h2.reshape(b, s, d)
```

```python
import functools

import jax
import jax.numpy as jnp
import numpy as np
from jax import lax
from jax.experimental import pallas as pl
from jax.experimental.pallas import tpu as pltpu

EPS = 1e-6
CHUNK = 64
H_A = 8
D_LATENT = 128
DH_A = 128
H_IDX = 8
D_IDX = 64
TOPK_MAX = 256
H_R = 8
DK_R = 128
DV_R = 128
ROPE_BASE = 10000.0
D_BRANCH = 1024
N_BRANCH = 2
N_GROUPS = 4
EXP_PER_GROUP = 8
N_EXPERTS = N_GROUPS * EXP_PER_GROUP
D_EXPERT = 1024

LANES = 128
KEY_TILE = 256
Q_TILE = 512
RET_CHUNK = 256
ROW_BLOCK = 256
ROUTE_HALVES = 4
ROUTE_CHAINS = 2 * ROUTE_HALVES
VMEM_LIMIT = 56 * 1024 * 1024

C_QLAT = 0
C_GBR = 1024
C_QR = 5120
C_KR = 6144
C_VR = 7168
C_GR = 8192
C_QIDX = 9216
C_CKV = 9728
C_KW = 9856
D_IN_P = 9984

INT_MIN = np.int32(-2 ** 31)
NEG_BIG = -1e30

bf16 = jnp.bfloat16
f32 = jnp.float32


def _cparams(sem):
    return pltpu.CompilerParams(dimension_semantics=sem, vmem_limit_bytes=VMEM_LIMIT)


def _proj_kernel(x_ref, g_ref, w_ref, o_ref, xn_ref):
    @pl.when(pl.program_id(1) == 0)
    def _():
        x = x_ref[...]
        ms = jnp.mean(x * x, axis=-1, keepdims=True)
        xn_ref[...] = (x * lax.rsqrt(ms + EPS) * g_ref[...]).astype(bf16)

    o_ref[...] = jnp.dot(xn_ref[...], w_ref[...], preferred_element_type=f32).astype(o_ref.dtype)


def _proj(x2, g, w_p, tm, tn):
    t, d = x2.shape
    n = w_p.shape[1]
    return pl.pallas_call(
        _proj_kernel,
        grid=(t // tm, n // tn),
        in_specs=[
            pl.BlockSpec((tm, d), lambda i, j: (i, 0)),
            pl.BlockSpec((1, d), lambda i, j: (0, 0)),
            pl.BlockSpec((d, tn), lambda i, j: (0, j)),
        ],
        out_specs=pl.BlockSpec((tm, tn), lambda i, j: (i, j)),
        out_shape=jax.ShapeDtypeStruct((t, n), bf16),
        scratch_shapes=[pltpu.VMEM((tm, d), bf16)],
        compiler_params=_cparams(("parallel", "arbitrary")),
        name="proj",
    )(x2, g, w_p)


def _float_key(s):
    bits = pltpu.bitcast(s, jnp.int32)
    key = bits ^ ((bits >> 31) & jnp.int32(0x7FFFFFFF))
    return jnp.where(s == 0.0, jnp.int32(0), key)


def _attn_kernel(qlat_ref, qidx_ref, kwq_ref, ckv_ref, kwk_ref, gkv_ref, wuv_ref, o_ref,
                 kv_s, kvT_s, kidx_s, key_s, bias_s, qT_s, qiT_s, acc_s, *, n_sel, n_kt):
    i = pl.program_id(1)
    idx_scale = (H_IDX ** -0.5) * (D_IDX ** -0.5)
    attn_scale = D_LATENT ** -0.5
    hq = H_A * Q_TILE

    @pl.when(i == 0)
    def _():
        g = gkv_ref[...]
        for t in range(n_kt):
            c = ckv_ref[t * KEY_TILE:(t + 1) * KEY_TILE, :].astype(f32)
            ms = jnp.mean(c * c, axis=-1, keepdims=True)
            kv = c * lax.rsqrt(ms + EPS) * g
            kv_s[t] = kv.astype(bf16)
            kvT_s[t] = kv.T.astype(bf16)
            kidx_s[t] = kwk_ref[t * KEY_TILE:(t + 1) * KEY_TILE, :D_IDX]

    nk = ((i + 1) * Q_TILE + KEY_TILE - 1) // KEY_TILE
    lane = lax.broadcasted_iota(jnp.int32, (1, Q_TILE), 1)
    sub = lax.broadcasted_iota(jnp.int32, (KEY_TILE, 1), 0)
    q_chunk = (i * Q_TILE + lane) // CHUNK

    wT = kwq_ref[...].astype(f32).T
    for h in range(H_A):
        qT_s[:, h * Q_TILE:(h + 1) * Q_TILE] = qlat_ref[:, h * D_LATENT:(h + 1) * D_LATENT].astype(f32).T.astype(bf16)
    for h in range(H_IDX):
        qiT_s[:, h * Q_TILE:(h + 1) * Q_TILE] = qidx_ref[:, h * D_IDX:(h + 1) * D_IDX].astype(f32).T.astype(bf16)

    def score_tile(t, carry):
        d_all = jnp.dot(kidx_s[t], qiT_s[...], preferred_element_type=f32)
        acc = jnp.zeros((KEY_TILE, Q_TILE), f32)
        for h in range(H_IDX):
            d = d_all[:, h * Q_TILE:(h + 1) * Q_TILE]
            acc = acc + wT[D_IDX + h:D_IDX + h + 1, :] * jnp.maximum(d, 0.0)
        score = acc * idx_scale
        k_chunk = (t * KEY_TILE + sub) // CHUNK
        key_s[t] = jnp.where(k_chunk <= q_chunk, _float_key(score), INT_MIN)
        return carry

    lax.fori_loop(0, nk, score_tile, 0)

    @pl.when(nk % 2 == 1)
    def _():
        key_s[nk] = jnp.full((KEY_TILE, Q_TILE), INT_MIN, jnp.int32)

    n_pairs = (nk + 1) // 2

    def count(pred):
        def body(p, c):
            for t in (2 * p, 2 * p + 1):
                m = pred(key_s[t], t).astype(jnp.int32)
                c = c + jnp.sum(m.reshape(KEY_TILE // 8, 8, Q_TILE), axis=0)
            return c
        c8 = lax.fori_loop(0, n_pairs, body, jnp.zeros((8, Q_TILE), jnp.int32))
        return jnp.sum(c8, axis=0, keepdims=True)

    thr0 = jnp.where(count(lambda k, t: k >= 0) >= n_sel, jnp.int32(0), INT_MIN)
    thr0 = jnp.broadcast_to(thr0, (1, Q_TILE)).astype(jnp.int32)

    def bit_step(j, thr):
        cand = thr | (jnp.int32(1) << (jnp.int32(30) - j))
        return jnp.where(count(lambda k, t: k >= cand) >= n_sel, cand, thr)

    thr = lax.fori_loop(0, 31, bit_step, thr0)

    c_gt = count(lambda k, t: k > thr)
    c_ge = count(lambda k, t: k >= thr)
    need = n_sel - c_gt
    has_tie = jnp.max(jnp.where((c_ge > n_sel) & (thr > INT_MIN), 1, 0)) > 0

    def tie_limit():
        def step(j, m):
            cand = m | (jnp.int32(1) << (jnp.int32(14) - j))
            c = count(lambda k, t: (k == thr) & ((t * KEY_TILE + sub) < cand))
            return jnp.where(c < need, cand, m)
        return lax.fori_loop(0, 15, step, jnp.zeros((1, Q_TILE), jnp.int32))

    m_lim = lax.cond(has_tie, tie_limit, lambda: jnp.full((1, Q_TILE), 2 ** 30, jnp.int32))

    def bias_pair(p, carry):
        for t in (2 * p, 2 * p + 1):
            k = key_s[t]
            sel = (k > thr) | ((k == thr) & ((t * KEY_TILE + sub) <= m_lim))
            sel = sel & (k > INT_MIN)
            bias_s[t] = jnp.where(sel, 0.0, NEG_BIG).astype(f32)
        return carry

    lax.fori_loop(0, n_pairs, bias_pair, 0)

    acc_s[...] = jnp.zeros_like(acc_s)

    def att_pair(p, carry):
        m_run, l_run = carry
        kv2 = jnp.concatenate([kv_s[2 * p], kv_s[2 * p + 1]], axis=0)
        kvT2 = jnp.concatenate([kvT_s[2 * p], kvT_s[2 * p + 1]], axis=1)
        bias2 = jnp.concatenate([bias_s[2 * p], bias_s[2 * p + 1]], axis=0)
        logit = jnp.dot(kv2, qT_s[...], preferred_element_type=f32) * attn_scale
        logit = logit + jnp.concatenate([bias2] * H_A, axis=1)
        m_new = jnp.maximum(m_run, jnp.max(logit, axis=0, keepdims=True))
        alpha = jnp.exp(m_run - m_new)
        pr = jnp.exp(logit - m_new)
        l_new = alpha * l_run + jnp.sum(pr, axis=0, keepdims=True)
        acc_s[...] = alpha * acc_s[...] + jnp.dot(kvT2, pr.astype(bf16), preferred_element_type=f32)
        return m_new, l_new

    init = (jnp.full((1, hq), NEG_BIG, f32), jnp.zeros((1, hq), f32))
    _, l_fin = lax.fori_loop(0, n_pairs, att_pair, init)
    inv_l = 1.0 / l_fin
    for h in range(H_A):
        sl = slice(h * Q_TILE, (h + 1) * Q_TILE)
        o_lat = (acc_s[:, sl] * inv_l[:, sl]).T
        o_ref[:, h * DH_A:(h + 1) * DH_A] = jnp.dot(
            o_lat.astype(bf16), wuv_ref[h], preferred_element_type=f32).astype(o_ref.dtype)


def _attn(proj3, g_kv, w_uv_bf, n_sel):
    b, s, _ = proj3.shape
    n_kt = s // KEY_TILE
    kern = functools.partial(_attn_kernel, n_sel=n_sel, n_kt=n_kt)
    return pl.pallas_call(
        kern,
        grid=(b, s // Q_TILE),
        in_specs=[
            pl.BlockSpec((None, Q_TILE, H_A * D_LATENT), lambda bi, i: (bi, i, C_QLAT // 1024)),
            pl.BlockSpec((None, Q_TILE, H_IDX * D_IDX), lambda bi, i: (bi, i, C_QIDX // 512)),
            pl.BlockSpec((None, Q_TILE, LANES), lambda bi, i: (bi, i, C_KW // LANES)),
            pl.BlockSpec((None, s, LANES), lambda bi, i: (bi, 0, C_CKV // LANES)),
            pl.BlockSpec((None, s, LANES), lambda bi, i: (bi, 0, C_KW // LANES)),
            pl.BlockSpec((1, D_LATENT), lambda bi, i: (0, 0)),
            pl.BlockSpec((H_A, D_LATENT, DH_A), lambda bi, i: (0, 0, 0)),
        ],
        out_specs=pl.BlockSpec((None, Q_TILE, D_BRANCH), lambda bi, i: (bi, i, 0)),
        out_shape=jax.ShapeDtypeStruct((b, s, D_BRANCH), bf16),
        scratch_shapes=[
            pltpu.VMEM((n_kt, KEY_TILE, D_LATENT), bf16),
            pltpu.VMEM((n_kt, D_LATENT, KEY_TILE), bf16),
            pltpu.VMEM((n_kt, KEY_TILE, D_IDX), bf16),
            pltpu.VMEM((n_kt, KEY_TILE, Q_TILE), jnp.int32),
            pltpu.VMEM((n_kt, KEY_TILE, Q_TILE), f32),
            pltpu.VMEM((D_LATENT, H_A * Q_TILE), bf16),
            pltpu.VMEM((D_IDX, H_IDX * Q_TILE), bf16),
            pltpu.VMEM((D_LATENT, H_A * Q_TILE), f32),
        ],
        compiler_params=_cparams(("parallel", "arbitrary")),
        name="attn",
    )(proj3, proj3, proj3, proj3, proj3, g_kv, w_uv_bf)


def _ret_kernel(q_ref, k_ref, v_ref, gr_ref, cos_ref, sin_ref, din_ref, dq_ref, dk_ref, dc_ref,
                gret_ref, o_ref, state_s):
    @pl.when(pl.program_id(1) == 0)
    def _():
        state_s[...] = jnp.zeros_like(state_s)

    cos = cos_ref[...]
    sin = sin_ref[...]

    def rot(x):
        return x * cos + pltpu.roll(x, DK_R // 2, axis=1) * sin

    for h in range(H_R):
        sl = slice(h * DK_R, (h + 1) * DK_R)
        q = rot(q_ref[:, sl].astype(f32)).astype(bf16)
        kf = rot(k_ref[:, sl].astype(f32)) * (DK_R ** -0.5)
        k = kf.astype(bf16)
        v = v_ref[:, sl]
        inner = lax.dot_general(q, k, (((1,), (1,)), ((), ())), preferred_element_type=f32) * din_ref[h]
        o = jnp.dot(inner.astype(bf16), v, preferred_element_type=f32)
        st = state_s[h]
        o = o + jnp.dot(q, st.astype(bf16), preferred_element_type=f32) * dq_ref[h]
        kd = (kf * dk_ref[h]).astype(bf16)
        state_s[h] = st * dc_ref[h] + jnp.dot(kd.T, v, preferred_element_type=f32)
        mu = jnp.mean(o, axis=-1, keepdims=True)
        var = jnp.mean(jnp.square(o - mu), axis=-1, keepdims=True)
        y = (o - mu) * lax.rsqrt(var + EPS) * gret_ref[:, sl]
        gate = gr_ref[:, sl].astype(f32)
        o_ref[:, sl] = (gate * jax.nn.sigmoid(gate) * y).astype(o_ref.dtype)


def _ret(proj3, cos_t, sin_t, d_in, d_q, d_k, d_c, g_ret):
    b, s, _ = proj3.shape
    c = RET_CHUNK
    w = H_R * DK_R

    def col(off):
        return pl.BlockSpec((None, c, w), lambda bi, ci: (bi, ci, off // w))

    return pl.pallas_call(
        _ret_kernel,
        grid=(b, s // c),
        in_specs=[
            col(C_QR), col(C_KR), col(C_VR), col(C_GR),
            pl.BlockSpec((c, DK_R), lambda bi, ci: (ci, 0)),
            pl.BlockSpec((c, DK_R), lambda bi, ci: (ci, 0)),
            pl.BlockSpec((H_R, c, c), lambda bi, ci: (0, 0, 0)),
            pl.BlockSpec((H_R, c, DK_R), lambda bi, ci: (0, 0, 0)),
            pl.BlockSpec((H_R, c, DK_R), lambda bi, ci: (0, 0, 0)),
            pl.BlockSpec((H_R, 1, DK_R), lambda bi, ci: (0, 0, 0)),
            pl.BlockSpec((1, w), lambda bi, ci: (0, 0)),
        ],
        out_specs=pl.BlockSpec((None, c, w), lambda bi, ci: (bi, ci, 0)),
        out_shape=jax.ShapeDtypeStruct((b, s, w), bf16),
        scratch_shapes=[pltpu.VMEM((H_R, DK_R, DV_R), f32)],
        compiler_params=_cparams(("parallel", "arbitrary")),
        name="ret",
    )(proj3, proj3, proj3, proj3, cos_t, sin_t, d_in, d_q, d_k, d_c, g_ret)


def _retention_tables(s):
    c = RET_CHUNK
    half = DK_R // 2
    freq = ROPE_BASE ** (-jnp.arange(half, dtype=f32) / half)
    ang = jnp.arange(s, dtype=f32)[:, None] * freq[None, :]
    cos = jnp.cos(ang)
    sin = jnp.sin(ang)
    cos_t = jnp.concatenate([cos, cos], axis=-1)
    sin_t = jnp.concatenate([-sin, sin], axis=-1)
    log_gamma = jnp.log1p(-jnp.exp2(-5.0 - jnp.arange(H_R, dtype=f32)))
    n = jnp.arange(c, dtype=f32)
    diff = n[:, None] - n[None, :]
    d_in = jnp.where(diff >= 0, jnp.exp(log_gamma[:, None, None] * jnp.maximum(diff, 0.0)), 0.0)
    d_q = jnp.broadcast_to(jnp.exp(log_gamma[:, None] * (n + 1.0))[:, :, None], (H_R, c, DK_R))
    d_k = jnp.broadcast_to(jnp.exp(log_gamma[:, None] * (c - 1.0 - n))[:, :, None], (H_R, c, DK_R))
    d_c = jnp.broadcast_to(jnp.exp(log_gamma * c)[:, None, None], (H_R, 1, DK_R))
    return cos_t, sin_t, d_in, d_q, d_k, d_c


def _mix_kernel(oa_ref, ob_ref, wb_ref, ga_ref, gb_ref, o_ref):
    a = jnp.dot(oa_ref[...], wb_ref[0], preferred_element_type=f32)
    b = jnp.dot(ob_ref[...], wb_ref[1], preferred_element_type=f32)
    ga = jax.nn.sigmoid(ga_ref[...].astype(f32))
    gb = jax.nn.sigmoid(gb_ref[...].astype(f32))
    o_ref[...] = (ga * a + gb * b).astype(o_ref.dtype)


def _mix(o_a, o_b, w_branch_bf, proj, tm, tn):
    t = o_a.shape[0]
    d = w_branch_bf.shape[2]
    return pl.pallas_call(
        _mix_kernel,
        grid=(t // tm, d // tn),
        in_specs=[
            pl.BlockSpec((tm, D_BRANCH), lambda i, j: (i, 0)),
            pl.BlockSpec((tm, D_BRANCH), lambda i, j: (i, 0)),
            pl.BlockSpec((N_BRANCH, D_BRANCH, tn), lambda i, j: (0, 0, j)),
            pl.BlockSpec((tm, tn), lambda i, j: (i, C_GBR // tn + j)),
            pl.BlockSpec((tm, tn), lambda i, j: (i, (C_GBR + d) // tn + j)),
        ],
        out_specs=pl.BlockSpec((tm, tn), lambda i, j: (i, j)),
        out_shape=jax.ShapeDtypeStruct((t, d), bf16),
        compiler_params=_cparams(("parallel", "parallel")),
        name="mix",
    )(o_a, o_b, w_branch_bf, proj, proj)


def _pack_rows(v):
    n = v.shape[1] // 2
    r = pltpu.bitcast(v.astype(bf16).astype(f32), jnp.uint32)
    w = (r[:, :n] >> 16) | (r[:, n:] & jnp.uint32(0xFFFF0000))
    return pltpu.einshape("r(ab)->rab", w, b=LANES)


def _unpack_rows(p):
    w = pltpu.einshape("rab->r(ab)", p)
    lo = pltpu.bitcast(w << 16, f32)
    hi = pltpu.bitcast(w & jnp.uint32(0xFFFF0000), f32)
    return lo, hi


def _split_bf16(a):
    hi = a.astype(bf16)
    lo = (a - hi.astype(f32)).astype(bf16)
    return hi, lo


def _outproj_kernel(mixed_ref, x_ref, wo_ref, g_ref, wr_ref, br_ref,
                    h_ref, xn_ref, eid_ref, gate_ref, cnt_ref):
    h = x_ref[...] + jnp.dot(mixed_ref[...], wo_ref[...], preferred_element_type=f32)
    h_ref[...] = h
    ms = jnp.mean(h * h, axis=-1, keepdims=True)
    xn = h * lax.rsqrt(ms + EPS) * g_ref[...]
    xn_ref[...] = _pack_rows(xn)

    x_hi, x_lo = _split_bf16(xn)
    hh_hl = jnp.dot(x_hi, wr_ref[...], preferred_element_type=f32)
    logit = (hh_hl[:, :LANES] + hh_hl[:, LANES:]
             + jnp.dot(x_lo, wr_ref[:, :LANES], preferred_element_type=f32)) + br_ref[...]

    lane = lax.broadcasted_iota(jnp.int32, logit.shape, 1)
    lanef = lane.astype(f32)
    neg = -jnp.inf

    def first_argmax(v, m):
        return jnp.min(jnp.where(v == m, lanef, float(LANES)), axis=-1, keepdims=True)

    lg = jnp.where(lane < N_GROUPS, logit, neg)
    mg = jnp.max(lg, axis=-1, keepdims=True)
    p_grp = 1.0 / jnp.sum(jnp.exp(lg - mg), axis=-1, keepdims=True)
    grp = first_argmax(lg, mg).astype(jnp.int32)

    e_lane = lane - N_GROUPS
    in_grp = (e_lane >= 0) & (e_lane < N_EXPERTS) & ((e_lane // EXP_PER_GROUP) == grp)
    le = jnp.where(in_grp, logit, neg)
    m1 = jnp.max(le, axis=-1, keepdims=True)
    i1 = first_argmax(le, m1)
    le2 = jnp.where(lanef == i1, neg, le)
    m2 = jnp.max(le2, axis=-1, keepdims=True)
    i2 = first_argmax(le2, m2)
    e2 = jnp.exp(m2 - m1)
    g1 = p_grp / (1.0 + e2)
    g2 = p_grp * e2 / (1.0 + e2)

    eid = jnp.where(lane == 0, i1, jnp.where(lane == 1, i2, float(N_GROUPS))) - float(N_GROUPS)
    eid_ref[...] = eid.astype(jnp.int32).T[:8, :]
    gate_ref[...] = jnp.where(lane == 0, g1, jnp.where(lane == 1, g2, 0.0))

    @pl.when(pl.program_id(0) == 0)
    def _():
        cnt_ref[...] = jnp.zeros_like(cnt_ref)

    half = pl.program_id(0) // (pl.num_programs(0) // ROUTE_HALVES)
    sub8 = lax.broadcasted_iota(jnp.int32, (8, LANES), 0)
    for s, idx in enumerate((i1, i2)):
        c = jnp.sum((lanef == idx).astype(jnp.int32), axis=0, keepdims=True)
        cnt_ref[...] += jnp.where(sub8 == s * ROUTE_HALVES + half, c, 0)


def _outproj(mixed, x2, w_out_bf, g_ffn, wr_hi_lo, b_r, tm):
    t, d = x2.shape
    row = lambda i: (i, 0)
    fixed = lambda i: (0, 0)
    return pl.pallas_call(
        _outproj_kernel,
        grid=(t // tm,),
        in_specs=[
            pl.BlockSpec((tm, d), row),
            pl.BlockSpec((tm, d), row),
            pl.BlockSpec((d, d), fixed),
            pl.BlockSpec((1, d), fixed),
            pl.BlockSpec((d, 2 * LANES), fixed),
            pl.BlockSpec((1, LANES), fixed),
        ],
        out_specs=[
            pl.BlockSpec((tm, d), row),
            pl.BlockSpec((tm, d // (2 * LANES), LANES), lambda i: (i, 0, 0)),
            pl.BlockSpec((8, tm), lambda i: (0, i)),
            pl.BlockSpec((tm, LANES), row),
            pl.BlockSpec((8, LANES), fixed),
        ],
        out_shape=[
            jax.ShapeDtypeStruct((t, d), f32),
            jax.ShapeDtypeStruct((t, d // (2 * LANES), LANES), jnp.uint32),
            jax.ShapeDtypeStruct((8, t), jnp.int32),
            jax.ShapeDtypeStruct((t, LANES), f32),
            jax.ShapeDtypeStruct((8, LANES), jnp.int32),
        ],
        compiler_params=_cparams(("arbitrary",)),
        name="outproj",
    )(mixed, x2, w_out_bf, g_ffn, wr_hi_lo, b_r)


ISSUE_UNROLL = 8
CAST_ROWS = 256


def _experts_kernel(bexp_ref, nexp_ref, rpack_ref, nused_ref, xn_hbm, wg_hbm, wu_hbm, wd_hbm, yt_hbm,
                    xbuf, ybuf, wg_st, wu_st, wd_st, wg_bf, wu_bf, wd_bf, gsem, ssem, wsem, *, n_tok):
    j = pl.program_id(0)
    n_used = nused_ref[0]
    slot = j % 2
    tok_bits = (n_tok - 1).bit_length()

    def rows_of(blk, fn):
        base = blk * ROW_BLOCK

        def body(k, c):
            r0 = pl.multiple_of(k * ISSUE_UNROLL, ISSUE_UNROLL)
            for u in range(ISSUE_UNROLL):
                fn(r0 + u, rpack_ref[base + r0 + u], 1)
            return c

        lax.fori_loop(0, ROW_BLOCK // ISSUE_UNROLL, body, 0)

    def start_gathers(blk, sl):
        def one(r, packed, queue):
            tok = packed & ((1 << tok_bits) - 1)
            pltpu.make_async_copy(xn_hbm.at[tok], xbuf.at[sl, r], gsem.at[sl]).start(priority=queue)
        rows_of(blk, one)

    def start_scatters(blk, sl):
        def one(r, packed, queue):
            row = lax.shift_right_logical(packed, tok_bits)
            pltpu.make_async_copy(ybuf.at[sl, r], yt_hbm.at[row], ssem.at[sl]).start(priority=queue)
        rows_of(blk, one)

    def wait_gathers(sl):
        pltpu.make_async_copy(xn_hbm.at[pl.ds(0, ROW_BLOCK)], xbuf.at[sl], gsem.at[sl]).wait()

    def wait_scatters(sl):
        pltpu.make_async_copy(ybuf.at[sl], yt_hbm.at[pl.ds(0, ROW_BLOCK)], ssem.at[sl]).wait()

    staged = ((wg_hbm, wg_st, wg_bf), (wu_hbm, wu_st, wu_bf), (wd_hbm, wd_st, wd_bf))

    def start_weights(e):
        for q, (src, st, _) in enumerate(staged):
            pltpu.make_async_copy(src.at[e], st, wsem.at[q]).start()

    def wait_and_cast_weights():
        for q, (src, st, dst) in enumerate(staged):
            pltpu.make_async_copy(src.at[0], st, wsem.at[q]).wait()

            def cast(c, carry, st=st, dst=dst):
                r = pl.multiple_of(c * CAST_ROWS, CAST_ROWS)
                dst[pl.ds(r, CAST_ROWS), :] = st[pl.ds(r, CAST_ROWS), :].astype(bf16)
                return carry

            lax.fori_loop(0, st.shape[0] // CAST_ROWS, cast, 0)

    @pl.when(j == 0)
    def _():
        start_weights(bexp_ref[0])
        start_gathers(0, 0)
        ybuf[...] = jnp.zeros_like(ybuf)
        for sl in range(2):
            spare = yt_hbm.at[pl.ds(2 * n_tok + sl * ROW_BLOCK, ROW_BLOCK)]
            pltpu.make_async_copy(ybuf.at[sl], spare, ssem.at[sl]).start()
        for sl in range(2):
            wait_scatters(sl)

    @pl.when(j < n_used)
    def _():
        e = bexp_ref[j]

        @pl.when((j == 0) | (bexp_ref[jnp.maximum(j - 1, 0)] != e))
        def _():
            wait_and_cast_weights()

            @pl.when(nexp_ref[j] >= 0)
            def _():
                start_weights(nexp_ref[j])

        wait_gathers(slot)

        @pl.when(j + 1 < n_used)
        def _():
            start_gathers(j + 1, 1 - slot)

        @pl.when(j >= 2)
        def _():
            wait_scatters(slot)

        lo, hi = _unpack_rows(xbuf[slot])
        xb = jnp.concatenate([lo, hi], axis=1).astype(bf16)
        g = jnp.dot(xb, wg_bf[...], preferred_element_type=f32)
        u = jnp.dot(xb, wu_bf[...], preferred_element_type=f32)
        hm = (g * jax.nn.sigmoid(g) * u).astype(bf16)
        ybuf[slot] = _pack_rows(jnp.dot(hm, wd_bf[...], preferred_element_type=f32))
        start_scatters(j, slot)

        @pl.when(j == n_used - 1)
        def _():
            wait_scatters(slot)

            @pl.when(j >= 1)
            def _():
                wait_scatters(1 - slot)


def _experts(block_expert, next_expert, row_pack, n_used, xn_packed, w_gate, w_up, w_down):
    n_rows = row_pack.shape[0]
    n_tok = xn_packed.shape[0]
    tile = xn_packed.shape[1:]
    _, d, f = w_gate.shape
    assert d % CAST_ROWS == 0 and f % CAST_ROWS == 0
    any_space = pl.BlockSpec(memory_space=pl.ANY)
    grid_spec = pltpu.PrefetchScalarGridSpec(
        num_scalar_prefetch=4,
        grid=(n_rows // ROW_BLOCK,),
        in_specs=[any_space, any_space, any_space, any_space],
        out_specs=any_space,
        scratch_shapes=[pltpu.VMEM((2, ROW_BLOCK) + tile, jnp.uint32),
                        pltpu.VMEM((2, ROW_BLOCK) + tile, jnp.uint32),
                        pltpu.VMEM((d, f), f32), pltpu.VMEM((d, f), f32), pltpu.VMEM((f, d), f32),
                        pltpu.VMEM((d, f), bf16), pltpu.VMEM((d, f), bf16), pltpu.VMEM((f, d), bf16),
                        pltpu.SemaphoreType.DMA((2,)), pltpu.SemaphoreType.DMA((2,)),
                        pltpu.SemaphoreType.DMA((3,))],
    )
    return pl.pallas_call(
        functools.partial(_experts_kernel, n_tok=n_tok),
        grid_spec=grid_spec,
        out_shape=jax.ShapeDtypeStruct((2 * n_tok + 2 * ROW_BLOCK,) + tile, jnp.uint32),
        compiler_params=_cparams(("arbitrary",)),
        name="experts",
    )(block_expert, next_expert, row_pack, n_used, xn_packed, w_gate, w_up, w_down)


def _combine_kernel(h_ref, y0_ref, y1_ref, gate_ref, g_ref, o_ref):
    gate = gate_ref[...]
    y0 = jnp.concatenate(_unpack_rows(y0_ref[...]), axis=1)
    y1 = jnp.concatenate(_unpack_rows(y1_ref[...]), axis=1)
    hh = h_ref[...] + gate[:, 0:1] * y0 + gate[:, 1:2] * y1
    ms = jnp.mean(hh * hh, axis=-1, keepdims=True)
    o_ref[...] = hh * lax.rsqrt(ms + EPS) * g_ref[...]


def _combine(h, yt, gate, g_final, tm):
    t, d = h.shape
    nt = t // tm
    return pl.pallas_call(
        _combine_kernel,
        grid=(nt,),
        in_specs=[
            pl.BlockSpec((tm, d), lambda i: (i, 0)),
            pl.BlockSpec((tm,) + yt.shape[1:], lambda i: (i, 0, 0)),
            pl.BlockSpec((tm,) + yt.shape[1:], lambda i: (nt + i, 0, 0)),
            pl.BlockSpec((tm, LANES), lambda i: (i, 0)),
            pl.BlockSpec((1, d), lambda i: (0, 0)),
        ],
        out_specs=pl.BlockSpec((tm, d), lambda i: (i, 0)),
        out_shape=jax.ShapeDtypeStruct((t, d), f32),
        compiler_params=_cparams(("parallel",)),
        name="combine",
    )(h, yt, yt, gate, g_final)


def _route_kernel(eid_ref, cnt_ref, rpack_ref, bexp_ref, nexp_ref, nused_ref, *cur_refs, n_tok, n_blocks):
    tok_bits = (n_tok - 1).bit_length()
    chunk = 2 * n_tok // ROUTE_CHAINS

    def no_next(k, carry):
        nexp_ref[k] = -1
        return carry

    lax.fori_loop(0, n_blocks, no_next, 0)

    def per_expert(e, carry):
        blk, prev_blk, prev_nb = carry
        start = blk * ROW_BLOCK
        run = start
        for c in range(ROUTE_CHAINS):
            cur_refs[c][e] = run
            run = run + cnt_ref[c * N_EXPERTS + e]
        nb = (run - start + ROW_BLOCK - 1) // ROW_BLOCK

        def set_block(k, c):
            bexp_ref[blk + k] = e
            return c

        lax.fori_loop(0, nb, set_block, 0)

        def set_next(k, c):
            nexp_ref[prev_blk + k] = e
            return c

        lax.fori_loop(0, jnp.where(nb > 0, prev_nb, 0), set_next, 0)

        def set_pad(r, c):
            rpack_ref[r] = (2 * n_tok + (r & (2 * ROW_BLOCK - 1))) << tok_bits
            return c

        lax.fori_loop(run, start + nb * ROW_BLOCK, set_pad, 0)
        return blk + nb, jnp.where(nb > 0, blk, prev_blk), jnp.where(nb > 0, nb, prev_nb)

    n_used, _, _ = lax.fori_loop(0, N_EXPERTS, per_expert, (0, 0, 0))
    nused_ref[0] = n_used

    def tail_block(k, carry):
        bexp_ref[k] = N_EXPERTS - 1
        return carry

    lax.fori_loop(n_used, n_blocks, tail_block, 0)

    def tail_row(r, carry):
        rpack_ref[r] = (2 * n_tok + (r & (2 * ROW_BLOCK - 1))) << tok_bits
        return carry

    lax.fori_loop(n_used * ROW_BLOCK, n_blocks * ROW_BLOCK, tail_row, 0)

    def place(i, carry):
        for c in range(ROUTE_CHAINS):
            a = c * chunk + i
            e = eid_ref[a]
            p = cur_refs[c][e]
            cur_refs[c][e] = p + 1
            rpack_ref[p] = (a << tok_bits) | (a - (c * chunk // n_tok) * n_tok)
        return carry

    lax.fori_loop(0, chunk, place, 0)


def _route(eid_flat, counts, n_tok):
    n_asg = eid_flat.shape[0]
    n_rows = -(-(n_asg + N_EXPERTS * (ROW_BLOCK - 1)) // ROW_BLOCK) * ROW_BLOCK
    n_blocks = n_rows // ROW_BLOCK
    smem = pl.BlockSpec(memory_space=pltpu.SMEM)
    return pl.pallas_call(
        functools.partial(_route_kernel, n_tok=n_tok, n_blocks=n_blocks),
        in_specs=[smem, smem],
        out_specs=[smem, smem, smem, smem],
        out_shape=[jax.ShapeDtypeStruct((n_rows,), jnp.int32),
                   jax.ShapeDtypeStruct((n_blocks,), jnp.int32),
                   jax.ShapeDtypeStruct((n_blocks,), jnp.int32),
                   jax.ShapeDtypeStruct((1,), jnp.int32)],
        scratch_shapes=[pltpu.SMEM((N_EXPERTS,), jnp.int32)] * ROUTE_CHAINS,
        name="route",
    )(eid_flat, counts)


def _pick(n, prefs):
    for p in prefs:
        if n % p == 0:
            return p
    return n


def kernel(x, g_mix_norm, w_in, g_kv, w_uv, g_ret, w_branch, w_out, g_ffn_norm, w_router_group,
           b_router_group, w_router_expert, b_router_expert, w_expert_gate, w_expert_up,
           w_expert_down, g_final):
    b, s, d = x.shape
    t = b * s
    depth = w_in.shape[0]
    n_sel = min(TOPK_MAX, s // 4)
    assert s % RET_CHUNK == 0 and s % Q_TILE == 0

    cos_t, sin_t, d_in, d_q, d_k, d_c = _retention_tables(s)
    h2 = x.reshape(t, d)
    for l in range(depth):
        wl = w_in[l]
        sp = np.cumsum([0, H_A * D_LATENT, D_LATENT, H_IDX * D_IDX, D_IDX, H_IDX,
                        H_R * DK_R, H_R * DK_R, H_R * DV_R, H_R * DV_R, N_BRANCH * d])
        seg = [wl[:, sp[k]:sp[k + 1]].astype(bf16) for k in range(10)]
        kw_pad = jnp.zeros((d, LANES - D_IDX - H_IDX), bf16)
        w_p = jnp.concatenate([seg[0], seg[9], seg[5], seg[6], seg[7], seg[8], seg[2], seg[1],
                               seg[3], seg[4], kw_pad], axis=1)
        assert w_p.shape[1] == D_IN_P

        proj = _proj(h2, g_mix_norm[l].reshape(1, d), w_p, _pick(t, (1024, 512, 256)), 1664)
        proj3 = proj.reshape(b, s, D_IN_P)

        o_a = _attn(proj3, g_kv[l].reshape(1, D_LATENT), w_uv[l].astype(bf16), n_sel)
        o_b = _ret(proj3, cos_t, sin_t, d_in, d_q, d_k, d_c, g_ret[l].reshape(1, H_R * DV_R))

        mixed = _mix(o_a.reshape(t, D_BRANCH), o_b.reshape(t, D_BRANCH), w_branch[l].astype(bf16),
                     proj, _pick(t, (1024, 512, 256)), 512)

        w_r = jnp.concatenate([w_router_group[l], w_router_expert[l],
                               jnp.zeros((d, LANES - N_GROUPS - N_EXPERTS), f32)], axis=1)
        b_r = jnp.concatenate([b_router_group[l], b_router_expert[l],
                               jnp.zeros((LANES - N_GROUPS - N_EXPERTS,), f32)]).reshape(1, LANES)
        wr_hi = w_r.astype(bf16)
        wr_lo = (w_r - wr_hi.astype(f32)).astype(bf16)
        h2, xn, eid_t, gate, cnt = _outproj(mixed, h2, w_out[l].astype(bf16), g_ffn_norm[l].reshape(1, d),
                                            jnp.concatenate([wr_hi, wr_lo], axis=1), b_r, _pick(t, (256,)))

        row_pack, block_expert, next_expert, n_used = _route(
            eid_t[:2].reshape(-1), cnt[:ROUTE_CHAINS, N_GROUPS:N_GROUPS + N_EXPERTS].reshape(-1), t)
        yt = _experts(block_expert, next_expert, row_pack, n_used, xn,
                      w_expert_gate[l], w_expert_up[l], w_expert_down[l])
        assert depth == 1
        h2 = _combine(h2, yt, gate, g_final.reshape(1, d), _pick(t, (256,)))
    return h2.reshape(b, s, d)
```

```python
import functools

import jax
import jax.numpy as jnp
import numpy as np
from jax import lax
from jax.experimental import pallas as pl
from jax.experimental.pallas import tpu as pltpu

EPS = 1e-6
CHUNK = 64
H_A = 8
D_LATENT = 128
DH_A = 128
H_IDX = 8
D_IDX = 64
TOPK_MAX = 256
H_R = 8
DK_R = 128
DV_R = 128
ROPE_BASE = 10000.0
D_BRANCH = 1024
N_BRANCH = 2
N_GROUPS = 4
EXP_PER_GROUP = 8
N_EXPERTS = N_GROUPS * EXP_PER_GROUP
D_EXPERT = 1024

LANES = 128
KEY_TILE = 256
Q_TILE = 512
RET_CHUNK = 256
ROW_BLOCK = 256
ROUTE_HALVES = 4
ROUTE_CHAINS = 2 * ROUTE_HALVES
VMEM_LIMIT = 56 * 1024 * 1024

C_QLAT = 0
C_GBR = 1024
C_QR = 5120
C_KR = 6144
C_VR = 7168
C_GR = 8192
C_QIDX = 9216
C_CKV = 9728
C_KW = 9856
D_IN_P = 9984

INT_MIN = np.int32(-2 ** 31)
NEG_BIG = -1e30

bf16 = jnp.bfloat16
f32 = jnp.float32


def _cparams(sem):
    return pltpu.CompilerParams(dimension_semantics=sem, vmem_limit_bytes=VMEM_LIMIT)


def _proj_kernel(x_ref, g_ref, w_ref, o_ref, xn_ref):
    @pl.when(pl.program_id(1) == 0)
    def _():
        x = x_ref[...]
        ms = jnp.mean(x * x, axis=-1, keepdims=True)
        xn_ref[...] = (x * lax.rsqrt(ms + EPS) * g_ref[...]).astype(bf16)

    o_ref[...] = jnp.dot(xn_ref[...], w_ref[...], preferred_element_type=f32).astype(o_ref.dtype)


def _proj(x2, g, w_p, tm, tn):
    t, d = x2.shape
    n = w_p.shape[1]
    return pl.pallas_call(
        _proj_kernel,
        grid=(t // tm, n // tn),
        in_specs=[
            pl.BlockSpec((tm, d), lambda i, j: (i, 0)),
            pl.BlockSpec((1, d), lambda i, j: (0, 0)),
            pl.BlockSpec((d, tn), lambda i, j: (0, j)),
        ],
        out_specs=pl.BlockSpec((tm, tn), lambda i, j: (i, j)),
        out_shape=jax.ShapeDtypeStruct((t, n), bf16),
        scratch_shapes=[pltpu.VMEM((tm, d), bf16)],
        compiler_params=_cparams(("parallel", "arbitrary")),
        name="proj",
    )(x2, g, w_p)


def _float_key(s):
    bits = pltpu.bitcast(s, jnp.int32)
    key = bits ^ ((bits >> 31) & jnp.int32(0x7FFFFFFF))
    return jnp.where(s == 0.0, jnp.int32(0), key)


def _attn_kernel(qlat_ref, qidx_ref, kwq_ref, ckv_ref, kwk_ref, gkv_ref, wuv_ref, o_ref,
                 kv_s, kvT_s, kidx_s, key_s, bias_s, qT_s, qiT_s, acc_s, *, n_sel, n_kt):
    i = pl.program_id(1)
    idx_scale = (H_IDX ** -0.5) * (D_IDX ** -0.5)
    attn_scale = D_LATENT ** -0.5
    hq = H_A * Q_TILE

    @pl.when(i == 0)
    def _():
        g = gkv_ref[...]
        for t in range(n_kt):
            c = ckv_ref[t * KEY_TILE:(t + 1) * KEY_TILE, :].astype(f32)
            ms = jnp.mean(c * c, axis=-1, keepdims=True)
            kv = c * lax.rsqrt(ms + EPS) * g
            kv_s[t] = kv.astype(bf16)
            kvT_s[t] = kv.T.astype(bf16)
            kidx_s[t] = kwk_ref[t * KEY_TILE:(t + 1) * KEY_TILE, :D_IDX]

    nk = ((i + 1) * Q_TILE + KEY_TILE - 1) // KEY_TILE
    lane = lax.broadcasted_iota(jnp.int32, (1, Q_TILE), 1)
    sub = lax.broadcasted_iota(jnp.int32, (KEY_TILE, 1), 0)
    q_chunk = (i * Q_TILE + lane) // CHUNK

    wT = kwq_ref[...].astype(f32).T
    for h in range(H_A):
        qT_s[:, h * Q_TILE:(h + 1) * Q_TILE] = qlat_ref[:, h * D_LATENT:(h + 1) * D_LATENT].astype(f32).T.astype(bf16)
    for h in range(H_IDX):
        qiT_s[:, h * Q_TILE:(h + 1) * Q_TILE] = qidx_ref[:, h * D_IDX:(h + 1) * D_IDX].astype(f32).T.astype(bf16)

    def score_tile(t, carry):
        d_all = jnp.dot(kidx_s[t], qiT_s[...], preferred_element_type=f32)
        acc = jnp.zeros((KEY_TILE, Q_TILE), f32)
        for h in range(H_IDX):
            d = d_all[:, h * Q_TILE:(h + 1) * Q_TILE]
            acc = acc + wT[D_IDX + h:D_IDX + h + 1, :] * jnp.maximum(d, 0.0)
        score = acc * idx_scale
        k_chunk = (t * KEY_TILE + sub) // CHUNK
        key_s[t] = jnp.where(k_chunk <= q_chunk, _float_key(score), INT_MIN)
        return carry

    lax.fori_loop(0, nk, score_tile, 0)

    @pl.when(nk % 2 == 1)
    def _():
        key_s[nk] = jnp.full((KEY_TILE, Q_TILE), INT_MIN, jnp.int32)

    n_pairs = (nk + 1) // 2

    def count(pred):
        def body(p, c):
            for t in (2 * p, 2 * p + 1):
                m = pred(key_s[t], t).astype(jnp.int32)
                c = c + jnp.sum(m.reshape(KEY_TILE // 8, 8, Q_TILE), axis=0)
            return c
        c8 = lax.fori_loop(0, n_pairs, body, jnp.zeros((8, Q_TILE), jnp.int32))
        return jnp.sum(c8, axis=0, keepdims=True)

    thr0 = jnp.where(count(lambda k, t: k >= 0) >= n_sel, jnp.int32(0), INT_MIN)
    thr0 = jnp.broadcast_to(thr0, (1, Q_TILE)).astype(jnp.int32)

    def bit_step(j, thr):
        cand = thr | (jnp.int32(1) << (jnp.int32(30) - j))
        return jnp.where(count(lambda k, t: k >= cand) >= n_sel, cand, thr)

    thr = lax.fori_loop(0, 31, bit_step, thr0)

    c_gt = count(lambda k, t: k > thr)
    c_ge = count(lambda k, t: k >= thr)
    need = n_sel - c_gt
    has_tie = jnp.max(jnp.where((c_ge > n_sel) & (thr > INT_MIN), 1, 0)) > 0

    def tie_limit():
        def step(j, m):
            cand = m | (jnp.int32(1) << (jnp.int32(14) - j))
            c = count(lambda k, t: (k == thr) & ((t * KEY_TILE + sub) < cand))
            return jnp.where(c < need, cand, m)
        return lax.fori_loop(0, 15, step, jnp.zeros((1, Q_TILE), jnp.int32))

    m_lim = lax.cond(has_tie, tie_limit, lambda: jnp.full((1, Q_TILE), 2 ** 30, jnp.int32))

    def bias_pair(p, carry):
        for t in (2 * p, 2 * p + 1):
            k = key_s[t]
            sel = (k > thr) | ((k == thr) & ((t * KEY_TILE + sub) <= m_lim))
            sel = sel & (k > INT_MIN)
            bias_s[t] = jnp.where(sel, 0.0, NEG_BIG).astype(f32)
        return carry

    lax.fori_loop(0, n_pairs, bias_pair, 0)

    acc_s[...] = jnp.zeros_like(acc_s)

    def att_pair(p, carry):
        m_run, l_run = carry
        kv2 = jnp.concatenate([kv_s[2 * p], kv_s[2 * p + 1]], axis=0)
        kvT2 = jnp.concatenate([kvT_s[2 * p], kvT_s[2 * p + 1]], axis=1)
        bias2 = jnp.concatenate([bias_s[2 * p], bias_s[2 * p + 1]], axis=0)
        logit = jnp.dot(kv2, qT_s[...], preferred_element_type=f32) * attn_scale
        logit = logit + jnp.concatenate([bias2] * H_A, axis=1)
        m_new = jnp.maximum(m_run, jnp.max(logit, axis=0, keepdims=True))
        alpha = jnp.exp(m_run - m_new)
        pr = jnp.exp(logit - m_new)
        l_new = alpha * l_run + jnp.sum(pr, axis=0, keepdims=True)
        acc_s[...] = alpha * acc_s[...] + jnp.dot(kvT2, pr.astype(bf16), preferred_element_type=f32)
        return m_new, l_new

    init = (jnp.full((1, hq), NEG_BIG, f32), jnp.zeros((1, hq), f32))
    _, l_fin = lax.fori_loop(0, n_pairs, att_pair, init)
    inv_l = 1.0 / l_fin
    for h in range(H_A):
        sl = slice(h * Q_TILE, (h + 1) * Q_TILE)
        o_lat = (acc_s[:, sl] * inv_l[:, sl]).T
        o_ref[:, h * DH_A:(h + 1) * DH_A] = jnp.dot(
            o_lat.astype(bf16), wuv_ref[h], preferred_element_type=f32).astype(o_ref.dtype)


def _attn(proj3, g_kv, w_uv_bf, n_sel):
    b, s, _ = proj3.shape
    n_kt = s // KEY_TILE
    kern = functools.partial(_attn_kernel, n_sel=n_sel, n_kt=n_kt)
    return pl.pallas_call(
        kern,
        grid=(b, s // Q_TILE),
        in_specs=[
            pl.BlockSpec((None, Q_TILE, H_A * D_LATENT), lambda bi, i: (bi, i, C_QLAT // 1024)),
            pl.BlockSpec((None, Q_TILE, H_IDX * D_IDX), lambda bi, i: (bi, i, C_QIDX // 512)),
            pl.BlockSpec((None, Q_TILE, LANES), lambda bi, i: (bi, i, C_KW // LANES)),
            pl.BlockSpec((None, s, LANES), lambda bi, i: (bi, 0, C_CKV // LANES)),
            pl.BlockSpec((None, s, LANES), lambda bi, i: (bi, 0, C_KW // LANES)),
            pl.BlockSpec((1, D_LATENT), lambda bi, i: (0, 0)),
            pl.BlockSpec((H_A, D_LATENT, DH_A), lambda bi, i: (0, 0, 0)),
        ],
        out_specs=pl.BlockSpec((None, Q_TILE, D_BRANCH), lambda bi, i: (bi, i, 0)),
        out_shape=jax.ShapeDtypeStruct((b, s, D_BRANCH), bf16),
        scratch_shapes=[
            pltpu.VMEM((n_kt, KEY_TILE, D_LATENT), bf16),
            pltpu.VMEM((n_kt, D_LATENT, KEY_TILE), bf16),
            pltpu.VMEM((n_kt, KEY_TILE, D_IDX), bf16),
            pltpu.VMEM((n_kt, KEY_TILE, Q_TILE), jnp.int32),
            pltpu.VMEM((n_kt, KEY_TILE, Q_TILE), f32),
            pltpu.VMEM((D_LATENT, H_A * Q_TILE), bf16),
            pltpu.VMEM((D_IDX, H_IDX * Q_TILE), bf16),
            pltpu.VMEM((D_LATENT, H_A * Q_TILE), f32),
        ],
        compiler_params=_cparams(("parallel", "arbitrary")),
        name="attn",
    )(proj3, proj3, proj3, proj3, proj3, g_kv, w_uv_bf)


def _ret_kernel(q_ref, k_ref, v_ref, gr_ref, cos_ref, sin_ref, din_ref, dq_ref, dk_ref, dc_ref,
                gret_ref, o_ref, state_s):
    @pl.when(pl.program_id(1) == 0)
    def _():
        state_s[...] = jnp.zeros_like(state_s)

    cos = cos_ref[...]
    sin = sin_ref[...]

    def rot(x):
        return x * cos + pltpu.roll(x, DK_R // 2, axis=1) * sin

    for h in range(H_R):
        sl = slice(h * DK_R, (h + 1) * DK_R)
        q = rot(q_ref[:, sl].astype(f32)).astype(bf16)
        kf = rot(k_ref[:, sl].astype(f32)) * (DK_R ** -0.5)
        k = kf.astype(bf16)
        v = v_ref[:, sl]
        inner = lax.dot_general(q, k, (((1,), (1,)), ((), ())), preferred_element_type=f32) * din_ref[h]
        o = jnp.dot(inner.astype(bf16), v, preferred_element_type=f32)
        st = state_s[h]
        o = o + jnp.dot(q, st.astype(bf16), preferred_element_type=f32) * dq_ref[h]
        kd = (kf * dk_ref[h]).astype(bf16)
        state_s[h] = st * dc_ref[h] + jnp.dot(kd.T, v, preferred_element_type=f32)
        mu = jnp.mean(o, axis=-1, keepdims=True)
        var = jnp.mean(jnp.square(o - mu), axis=-1, keepdims=True)
        y = (o - mu) * lax.rsqrt(var + EPS) * gret_ref[:, sl]
        gate = gr_ref[:, sl].astype(f32)
        o_ref[:, sl] = (gate * jax.nn.sigmoid(gate) * y).astype(o_ref.dtype)


def _ret(proj3, cos_t, sin_t, d_in, d_q, d_k, d_c, g_ret):
    b, s, _ = proj3.shape
    c = RET_CHUNK
    w = H_R * DK_R

    def col(off):
        return pl.BlockSpec((None, c, w), lambda bi, ci: (bi, ci, off // w))

    return pl.pallas_call(
        _ret_kernel,
        grid=(b, s // c),
        in_specs=[
            col(C_QR), col(C_KR), col(C_VR), col(C_GR),
            pl.BlockSpec((c, DK_R), lambda bi, ci: (ci, 0)),
            pl.BlockSpec((c, DK_R), lambda bi, ci: (ci, 0)),
            pl.BlockSpec((H_R, c, c), lambda bi, ci: (0, 0, 0)),
            pl.BlockSpec((H_R, c, DK_R), lambda bi, ci: (0, 0, 0)),
            pl.BlockSpec((H_R, c, DK_R), lambda bi, ci: (0, 0, 0)),
            pl.BlockSpec((H_R, 1, DK_R), lambda bi, ci: (0, 0, 0)),
            pl.BlockSpec((1, w), lambda bi, ci: (0, 0)),
        ],
        out_specs=pl.BlockSpec((None, c, w), lambda bi, ci: (bi, ci, 0)),
        out_shape=jax.ShapeDtypeStruct((b, s, w), bf16),
        scratch_shapes=[pltpu.VMEM((H_R, DK_R, DV_R), f32)],
        compiler_params=_cparams(("parallel", "arbitrary")),
        name="ret",
    )(proj3, proj3, proj3, proj3, cos_t, sin_t, d_in, d_q, d_k, d_c, g_ret)


def _retention_tables(s):
    c = RET_CHUNK
    half = DK_R // 2
    freq = ROPE_BASE ** (-jnp.arange(half, dtype=f32) / half)
    ang = jnp.arange(s, dtype=f32)[:, None] * freq[None, :]
    cos = jnp.cos(ang)
    sin = jnp.sin(ang)
    cos_t = jnp.concatenate([cos, cos], axis=-1)
    sin_t = jnp.concatenate([-sin, sin], axis=-1)
    log_gamma = jnp.log1p(-jnp.exp2(-5.0 - jnp.arange(H_R, dtype=f32)))
    n = jnp.arange(c, dtype=f32)
    diff = n[:, None] - n[None, :]
    d_in = jnp.where(diff >= 0, jnp.exp(log_gamma[:, None, None] * jnp.maximum(diff, 0.0)), 0.0)
    d_q = jnp.broadcast_to(jnp.exp(log_gamma[:, None] * (n + 1.0))[:, :, None], (H_R, c, DK_R))
    d_k = jnp.broadcast_to(jnp.exp(log_gamma[:, None] * (c - 1.0 - n))[:, :, None], (H_R, c, DK_R))
    d_c = jnp.broadcast_to(jnp.exp(log_gamma * c)[:, None, None], (H_R, 1, DK_R))
    return cos_t, sin_t, d_in, d_q, d_k, d_c


MIX_CHUNK = 512


def _mix_kernel(oa_ref, ob_ref, wb_ref, ga0_ref, ga1_ref, gb0_ref, gb1_ref, o_ref):
    oa = oa_ref[...]
    ob = ob_ref[...]
    half = ga0_ref.shape[1]
    for c in range(0, o_ref.shape[1], MIX_CHUNK):
        ga_ref, gb_ref, off = (ga0_ref, gb0_ref, c) if c < half else (ga1_ref, gb1_ref, c - half)
        a = jnp.dot(oa, wb_ref[0, :, c:c + MIX_CHUNK], preferred_element_type=f32)
        b = jnp.dot(ob, wb_ref[1, :, c:c + MIX_CHUNK], preferred_element_type=f32)
        ga = jax.nn.sigmoid(ga_ref[:, off:off + MIX_CHUNK].astype(f32))
        gb = jax.nn.sigmoid(gb_ref[:, off:off + MIX_CHUNK].astype(f32))
        o_ref[:, c:c + MIX_CHUNK] = (ga * a + gb * b).astype(o_ref.dtype)


def _mix(o_a, o_b, w_branch_bf, proj, tm):
    t = o_a.shape[0]
    d = w_branch_bf.shape[2]
    half = d // 2
    assert C_GBR % half == 0 and half % MIX_CHUNK == 0

    def gate(k):
        return pl.BlockSpec((tm, half), lambda i: (i, C_GBR // half + k))

    return pl.pallas_call(
        _mix_kernel,
        grid=(t // tm,),
        in_specs=[
            pl.BlockSpec((tm, D_BRANCH), lambda i: (i, 0)),
            pl.BlockSpec((tm, D_BRANCH), lambda i: (i, 0)),
            pl.BlockSpec((N_BRANCH, D_BRANCH, d), lambda i: (0, 0, 0)),
            gate(0), gate(1), gate(2), gate(3),
        ],
        out_specs=pl.BlockSpec((tm, d), lambda i: (i, 0)),
        out_shape=jax.ShapeDtypeStruct((t, d), bf16),
        compiler_params=_cparams(("parallel",)),
        name="mix",
    )(o_a, o_b, w_branch_bf, proj, proj, proj, proj)


def _pack_rows(v):
    n = v.shape[1] // 2
    r = pltpu.bitcast(v.astype(bf16).astype(f32), jnp.uint32)
    w = (r[:, :n] >> 16) | (r[:, n:] & jnp.uint32(0xFFFF0000))
    return pltpu.einshape("r(ab)->rab", w, b=LANES)


def _unpack_rows(p):
    w = pltpu.einshape("rab->r(ab)", p)
    lo = pltpu.bitcast(w << 16, f32)
    hi = pltpu.bitcast(w & jnp.uint32(0xFFFF0000), f32)
    return lo, hi


def _split_bf16(a):
    hi = a.astype(bf16)
    lo = (a - hi.astype(f32)).astype(bf16)
    return hi, lo


def _outproj_kernel(mixed_ref, x_ref, wo_ref, g_ref, wr_ref, br_ref,
                    h_ref, xn_ref, eid_ref, gate_ref, cnt_ref):
    h = x_ref[...] + jnp.dot(mixed_ref[...], wo_ref[...], preferred_element_type=f32)
    h_ref[...] = h
    ms = jnp.mean(h * h, axis=-1, keepdims=True)
    xn = h * lax.rsqrt(ms + EPS) * g_ref[...]
    xn_ref[...] = _pack_rows(xn)

    x_hi, x_lo = _split_bf16(xn)
    hh_hl = jnp.dot(x_hi, wr_ref[...], preferred_element_type=f32)
    logit = (hh_hl[:, :LANES] + hh_hl[:, LANES:]
             + jnp.dot(x_lo, wr_ref[:, :LANES], preferred_element_type=f32)) + br_ref[...]

    lane = lax.broadcasted_iota(jnp.int32, logit.shape, 1)
    lanef = lane.astype(f32)
    neg = -jnp.inf

    def first_argmax(v, m):
        return jnp.min(jnp.where(v == m, lanef, float(LANES)), axis=-1, keepdims=True)

    lg = jnp.where(lane < N_GROUPS, logit, neg)
    mg = jnp.max(lg, axis=-1, keepdims=True)
    p_grp = 1.0 / jnp.sum(jnp.exp(lg - mg), axis=-1, keepdims=True)
    grp = first_argmax(lg, mg).astype(jnp.int32)

    e_lane = lane - N_GROUPS
    in_grp = (e_lane >= 0) & (e_lane < N_EXPERTS) & ((e_lane // EXP_PER_GROUP) == grp)
    le = jnp.where(in_grp, logit, neg)
    m1 = jnp.max(le, axis=-1, keepdims=True)
    i1 = first_argmax(le, m1)
    le2 = jnp.where(lanef == i1, neg, le)
    m2 = jnp.max(le2, axis=-1, keepdims=True)
    i2 = first_argmax(le2, m2)
    e2 = jnp.exp(m2 - m1)
    g1 = p_grp / (1.0 + e2)
    g2 = p_grp * e2 / (1.0 + e2)

    eid = jnp.where(lane == 0, i1, jnp.where(lane == 1, i2, float(N_GROUPS))) - float(N_GROUPS)
    eid_ref[...] = eid.astype(jnp.int32).T[:8, :]
    gate_ref[...] = jnp.where(lane == 0, g1, jnp.where(lane == 1, g2, 0.0))

    @pl.when(pl.program_id(0) == 0)
    def _():
        cnt_ref[...] = jnp.zeros_like(cnt_ref)

    half = pl.program_id(0) // (pl.num_programs(0) // ROUTE_HALVES)
    sub8 = lax.broadcasted_iota(jnp.int32, (8, LANES), 0)
    for s, idx in enumerate((i1, i2)):
        c = jnp.sum((lanef == idx).astype(jnp.int32), axis=0, keepdims=True)
        cnt_ref[...] += jnp.where(sub8 == s * ROUTE_HALVES + half, c, 0)


def _outproj(mixed, x2, w_out_bf, g_ffn, wr_hi_lo, b_r, tm):
    t, d = x2.shape
    row = lambda i: (i, 0)
    fixed = lambda i: (0, 0)
    return pl.pallas_call(
        _outproj_kernel,
        grid=(t // tm,),
        in_specs=[
            pl.BlockSpec((tm, d), row),
            pl.BlockSpec((tm, d), row),
            pl.BlockSpec((d, d), fixed),
            pl.BlockSpec((1, d), fixed),
            pl.BlockSpec((d, 2 * LANES), fixed),
            pl.BlockSpec((1, LANES), fixed),
        ],
        out_specs=[
            pl.BlockSpec((tm, d), row),
            pl.BlockSpec((tm, d // (2 * LANES), LANES), lambda i: (i, 0, 0)),
            pl.BlockSpec((8, tm), lambda i: (0, i)),
            pl.BlockSpec((tm, LANES), row),
            pl.BlockSpec((8, LANES), fixed),
        ],
        out_shape=[
            jax.ShapeDtypeStruct((t, d), f32),
            jax.ShapeDtypeStruct((t, d // (2 * LANES), LANES), jnp.uint32),
            jax.ShapeDtypeStruct((8, t), jnp.int32),
            jax.ShapeDtypeStruct((t, LANES), f32),
            jax.ShapeDtypeStruct((8, LANES), jnp.int32),
        ],
        compiler_params=_cparams(("arbitrary",)),
        name="outproj",
    )(mixed, x2, w_out_bf, g_ffn, wr_hi_lo, b_r)


ISSUE_UNROLL = 8
CAST_ROWS = 256


def _experts_kernel(bexp_ref, nexp_ref, rpack_ref, nused_ref, xn_hbm, wg_hbm, wu_hbm, wd_hbm, yt_hbm,
                    xbuf, ybuf, wg_st, wu_st, wd_st, wg_bf, wu_bf, wd_bf, gsem, ssem, wsem, *, n_tok):
    j = pl.program_id(0)
    n_used = nused_ref[0]
    slot = j % 2
    tok_bits = (n_tok - 1).bit_length()

    def rows_of(blk, fn):
        base = blk * ROW_BLOCK

        def body(k, c):
            r0 = pl.multiple_of(k * ISSUE_UNROLL, ISSUE_UNROLL)
            for u in range(ISSUE_UNROLL):
                fn(r0 + u, rpack_ref[base + r0 + u], 1)
            return c

        lax.fori_loop(0, ROW_BLOCK // ISSUE_UNROLL, body, 0)

    def start_gathers(blk, sl):
        def one(r, packed, queue):
            tok = packed & ((1 << tok_bits) - 1)
            pltpu.make_async_copy(xn_hbm.at[tok], xbuf.at[sl, r], gsem.at[sl]).start(priority=queue)
        rows_of(blk, one)

    def start_scatters(blk, sl):
        def one(r, packed, queue):
            row = lax.shift_right_logical(packed, tok_bits)
            pltpu.make_async_copy(ybuf.at[sl, r], yt_hbm.at[row], ssem.at[sl]).start(priority=queue)
        rows_of(blk, one)

    def wait_gathers(sl):
        pltpu.make_async_copy(xn_hbm.at[pl.ds(0, ROW_BLOCK)], xbuf.at[sl], gsem.at[sl]).wait()

    def wait_scatters(sl):
        pltpu.make_async_copy(ybuf.at[sl], yt_hbm.at[pl.ds(0, ROW_BLOCK)], ssem.at[sl]).wait()

    staged = ((wg_hbm, wg_st, wg_bf), (wu_hbm, wu_st, wu_bf), (wd_hbm, wd_st, wd_bf))

    def start_weights(e):
        for q, (src, st, _) in enumerate(staged):
            pltpu.make_async_copy(src.at[e], st, wsem.at[q]).start()

    def wait_and_cast_weights():
        for q, (src, st, dst) in enumerate(staged):
            pltpu.make_async_copy(src.at[0], st, wsem.at[q]).wait()

            def cast(c, carry, st=st, dst=dst):
                r = pl.multiple_of(c * CAST_ROWS, CAST_ROWS)
                dst[pl.ds(r, CAST_ROWS), :] = st[pl.ds(r, CAST_ROWS), :].astype(bf16)
                return carry

            lax.fori_loop(0, st.shape[0] // CAST_ROWS, cast, 0)

    @pl.when(j == 0)
    def _():
        start_weights(bexp_ref[0])
        start_gathers(0, 0)
        ybuf[...] = jnp.zeros_like(ybuf)
        for sl in range(2):
            spare = yt_hbm.at[pl.ds(2 * n_tok + sl * ROW_BLOCK, ROW_BLOCK)]
            pltpu.make_async_copy(ybuf.at[sl], spare, ssem.at[sl]).start()
        for sl in range(2):
            wait_scatters(sl)

    @pl.when(j < n_used)
    def _():
        e = bexp_ref[j]

        @pl.when((j == 0) | (bexp_ref[jnp.maximum(j - 1, 0)] != e))
        def _():
            wait_and_cast_weights()

            @pl.when(nexp_ref[j] >= 0)
            def _():
                start_weights(nexp_ref[j])

        wait_gathers(slot)

        @pl.when(j + 1 < n_used)
        def _():
            start_gathers(j + 1, 1 - slot)

        @pl.when(j >= 2)
        def _():
            wait_scatters(slot)

        lo, hi = _unpack_rows(xbuf[slot])
        xb = jnp.concatenate([lo, hi], axis=1).astype(bf16)
        g = jnp.dot(xb, wg_bf[...], preferred_element_type=f32)
        u = jnp.dot(xb, wu_bf[...], preferred_element_type=f32)
        hm = (g * jax.nn.sigmoid(g) * u).astype(bf16)
        ybuf[slot] = _pack_rows(jnp.dot(hm, wd_bf[...], preferred_element_type=f32))
        start_scatters(j, slot)

        @pl.when(j == n_used - 1)
        def _():
            wait_scatters(slot)

            @pl.when(j >= 1)
            def _():
                wait_scatters(1 - slot)


def _experts(block_expert, next_expert, row_pack, n_used, xn_packed, w_gate, w_up, w_down):
    n_rows = row_pack.shape[0]
    n_tok = xn_packed.shape[0]
    tile = xn_packed.shape[1:]
    _, d, f = w_gate.shape
    assert d % CAST_ROWS == 0 and f % CAST_ROWS == 0
    any_space = pl.BlockSpec(memory_space=pl.ANY)
    grid_spec = pltpu.PrefetchScalarGridSpec(
        num_scalar_prefetch=4,
        grid=(n_rows // ROW_BLOCK,),
        in_specs=[any_space, any_space, any_space, any_space],
        out_specs=any_space,
        scratch_shapes=[pltpu.VMEM((2, ROW_BLOCK) + tile, jnp.uint32),
                        pltpu.VMEM((2, ROW_BLOCK) + tile, jnp.uint32),
                        pltpu.VMEM((d, f), f32), pltpu.VMEM((d, f), f32), pltpu.VMEM((f, d), f32),
                        pltpu.VMEM((d, f), bf16), pltpu.VMEM((d, f), bf16), pltpu.VMEM((f, d), bf16),
                        pltpu.SemaphoreType.DMA((2,)), pltpu.SemaphoreType.DMA((2,)),
                        pltpu.SemaphoreType.DMA((3,))],
    )
    return pl.pallas_call(
        functools.partial(_experts_kernel, n_tok=n_tok),
        grid_spec=grid_spec,
        out_shape=jax.ShapeDtypeStruct((2 * n_tok + 2 * ROW_BLOCK,) + tile, jnp.uint32),
        compiler_params=_cparams(("arbitrary",)),
        name="experts",
    )(block_expert, next_expert, row_pack, n_used, xn_packed, w_gate, w_up, w_down)


def _combine_kernel(h_ref, y0_ref, y1_ref, gate_ref, g_ref, o_ref):
    gate = gate_ref[...]
    y0 = jnp.concatenate(_unpack_rows(y0_ref[...]), axis=1)
    y1 = jnp.concatenate(_unpack_rows(y1_ref[...]), axis=1)
    hh = h_ref[...] + gate[:, 0:1] * y0 + gate[:, 1:2] * y1
    ms = jnp.mean(hh * hh, axis=-1, keepdims=True)
    o_ref[...] = hh * lax.rsqrt(ms + EPS) * g_ref[...]


def _combine(h, yt, gate, g_final, tm):
    t, d = h.shape
    nt = t // tm
    return pl.pallas_call(
        _combine_kernel,
        grid=(nt,),
        in_specs=[
            pl.BlockSpec((tm, d), lambda i: (i, 0)),
            pl.BlockSpec((tm,) + yt.shape[1:], lambda i: (i, 0, 0)),
            pl.BlockSpec((tm,) + yt.shape[1:], lambda i: (nt + i, 0, 0)),
            pl.BlockSpec((tm, LANES), lambda i: (i, 0)),
            pl.BlockSpec((1, d), lambda i: (0, 0)),
        ],
        out_specs=pl.BlockSpec((tm, d), lambda i: (i, 0)),
        out_shape=jax.ShapeDtypeStruct((t, d), f32),
        compiler_params=_cparams(("parallel",)),
        name="combine",
    )(h, yt, yt, gate, g_final)


def _route_kernel(eid_ref, cnt_ref, rpack_ref, bexp_ref, nexp_ref, nused_ref, *cur_refs, n_tok, n_blocks):
    tok_bits = (n_tok - 1).bit_length()
    chunk = 2 * n_tok // ROUTE_CHAINS

    def no_next(k, carry):
        nexp_ref[k] = -1
        return carry

    lax.fori_loop(0, n_blocks, no_next, 0)

    def per_expert(e, carry):
        blk, prev_blk, prev_nb = carry
        start = blk * ROW_BLOCK
        run = start
        for c in range(ROUTE_CHAINS):
            cur_refs[c][e] = run
            run = run + cnt_ref[c * N_EXPERTS + e]
        nb = (run - start + ROW_BLOCK - 1) // ROW_BLOCK

        def set_block(k, c):
            bexp_ref[blk + k] = e
            return c

        lax.fori_loop(0, nb, set_block, 0)

        def set_next(k, c):
            nexp_ref[prev_blk + k] = e
            return c

        lax.fori_loop(0, jnp.where(nb > 0, prev_nb, 0), set_next, 0)

        def set_pad(r, c):
            rpack_ref[r] = (2 * n_tok + (r & (2 * ROW_BLOCK - 1))) << tok_bits
            return c

        lax.fori_loop(run, start + nb * ROW_BLOCK, set_pad, 0)
        return blk + nb, jnp.where(nb > 0, blk, prev_blk), jnp.where(nb > 0, nb, prev_nb)

    n_used, _, _ = lax.fori_loop(0, N_EXPERTS, per_expert, (0, 0, 0))
    nused_ref[0] = n_used

    def tail_block(k, carry):
        bexp_ref[k] = N_EXPERTS - 1
        return carry

    lax.fori_loop(n_used, n_blocks, tail_block, 0)

    def tail_row(r, carry):
        rpack_ref[r] = (2 * n_tok + (r & (2 * ROW_BLOCK - 1))) << tok_bits
        return carry

    lax.fori_loop(n_used * ROW_BLOCK, n_blocks * ROW_BLOCK, tail_row, 0)

    def place(i, carry):
        for c in range(ROUTE_CHAINS):
            a = c * chunk + i
            e = eid_ref[a]
            p = cur_refs[c][e]
            cur_refs[c][e] = p + 1
            rpack_ref[p] = (a << tok_bits) | (a - (c * chunk // n_tok) * n_tok)
        return carry

    lax.fori_loop(0, chunk, place, 0)


def _route(eid_flat, counts, n_tok):
    n_asg = eid_flat.shape[0]
    n_rows = -(-(n_asg + N_EXPERTS * (ROW_BLOCK - 1)) // ROW_BLOCK) * ROW_BLOCK
    n_blocks = n_rows // ROW_BLOCK
    smem = pl.BlockSpec(memory_space=pltpu.SMEM)
    return pl.pallas_call(
        functools.partial(_route_kernel, n_tok=n_tok, n_blocks=n_blocks),
        in_specs=[smem, smem],
        out_specs=[smem, smem, smem, smem],
        out_shape=[jax.ShapeDtypeStruct((n_rows,), jnp.int32),
                   jax.ShapeDtypeStruct((n_blocks,), jnp.int32),
                   jax.ShapeDtypeStruct((n_blocks,), jnp.int32),
                   jax.ShapeDtypeStruct((1,), jnp.int32)],
        scratch_shapes=[pltpu.SMEM((N_EXPERTS,), jnp.int32)] * ROUTE_CHAINS,
        name="route",
    )(eid_flat, counts)


def _pick(n, prefs):
    for p in prefs:
        if n % p == 0:
            return p
    return n


def kernel(x, g_mix_norm, w_in, g_kv, w_uv, g_ret, w_branch, w_out, g_ffn_norm, w_router_group,
           b_router_group, w_router_expert, b_router_expert, w_expert_gate, w_expert_up,
           w_expert_down, g_final):
    b, s, d = x.shape
    t = b * s
    depth = w_in.shape[0]
    n_sel = min(TOPK_MAX, s // 4)
    assert s % RET_CHUNK == 0 and s % Q_TILE == 0

    cos_t, sin_t, d_in, d_q, d_k, d_c = _retention_tables(s)
    h2 = x.reshape(t, d)
    for l in range(depth):
        wl = w_in[l]
        sp = np.cumsum([0, H_A * D_LATENT, D_LATENT, H_IDX * D_IDX, D_IDX, H_IDX,
                        H_R * DK_R, H_R * DK_R, H_R * DV_R, H_R * DV_R, N_BRANCH * d])
        seg = [wl[:, sp[k]:sp[k + 1]].astype(bf16) for k in range(10)]
        kw_pad = jnp.zeros((d, LANES - D_IDX - H_IDX), bf16)
        w_p = jnp.concatenate([seg[0], seg[9], seg[5], seg[6], seg[7], seg[8], seg[2], seg[1],
                               seg[3], seg[4], kw_pad], axis=1)
        assert w_p.shape[1] == D_IN_P

        proj = _proj(h2, g_mix_norm[l].reshape(1, d), w_p, _pick(t, (512, 256)), 3328)
        proj3 = proj.reshape(b, s, D_IN_P)

        o_a = _attn(proj3, g_kv[l].reshape(1, D_LATENT), w_uv[l].astype(bf16), n_sel)
        o_b = _ret(proj3, cos_t, sin_t, d_in, d_q, d_k, d_c, g_ret[l].reshape(1, H_R * DV_R))

        mixed = _mix(o_a.reshape(t, D_BRANCH), o_b.reshape(t, D_BRANCH), w_branch[l].astype(bf16),
                     proj, _pick(t, (512, 256)))

        w_r = jnp.concatenate([w_router_group[l], w_router_expert[l],
                               jnp.zeros((d, LANES - N_GROUPS - N_EXPERTS), f32)], axis=1)
        b_r = jnp.concatenate([b_router_group[l], b_router_expert[l],
                               jnp.zeros((LANES - N_GROUPS - N_EXPERTS,), f32)]).reshape(1, LANES)
        wr_hi = w_r.astype(bf16)
        wr_lo = (w_r - wr_hi.astype(f32)).astype(bf16)
        h2, xn, eid_t, gate, cnt = _outproj(mixed, h2, w_out[l].astype(bf16), g_ffn_norm[l].reshape(1, d),
                                            jnp.concatenate([wr_hi, wr_lo], axis=1), b_r, _pick(t, (256,)))

        row_pack, block_expert, next_expert, n_used = _route(
            eid_t[:2].reshape(-1), cnt[:ROUTE_CHAINS, N_GROUPS:N_GROUPS + N_EXPERTS].reshape(-1), t)
        yt = _experts(block_expert, next_expert, row_pack, n_used, xn,
                      w_expert_gate[l], w_expert_up[l], w_expert_down[l])
        assert depth == 1
        h2 = _combine(h2, yt, gate, g_final.reshape(1, d), _pick(t, (256,)))
    return h2.reshape(b, s, d)
```

```python
import functools

import jax
import jax.numpy as jnp
import numpy as np
from jax import lax
from jax.experimental import pallas as pl
from jax.experimental.pallas import tpu as pltpu

EPS = 1e-6
CHUNK = 64
H_A = 8
D_LATENT = 128
DH_A = 128
H_IDX = 8
D_IDX = 64
TOPK_MAX = 256
H_R = 8
DK_R = 128
DV_R = 128
ROPE_BASE = 10000.0
D_BRANCH = 1024
N_BRANCH = 2
N_GROUPS = 4
EXP_PER_GROUP = 8
N_EXPERTS = N_GROUPS * EXP_PER_GROUP
D_EXPERT = 1024

LANES = 128
KEY_TILE = 256
Q_TILE = 512
RET_CHUNK = 256
ROW_BLOCK = 256
ROUTE_HALVES = 4
ROUTE_CHAINS = 2 * ROUTE_HALVES
VMEM_LIMIT = 56 * 1024 * 1024

C_QLAT = 0
C_GBR = 1024
C_QR = 5120
C_KR = 6144
C_VR = 7168
C_GR = 8192
C_QIDX = 9216
C_CKV = 9728
C_KW = 9856
D_IN_P = 9984

INT_MIN = np.int32(-2 ** 31)
NEG_BIG = -1e30

bf16 = jnp.bfloat16
f32 = jnp.float32


def _cparams(sem):
    return pltpu.CompilerParams(dimension_semantics=sem, vmem_limit_bytes=VMEM_LIMIT)


def _proj_kernel(x_ref, g_ref, w_ref, o_ref, xn_ref):
    @pl.when(pl.program_id(1) == 0)
    def _():
        x = x_ref[...]
        ms = jnp.mean(x * x, axis=-1, keepdims=True)
        xn_ref[...] = (x * lax.rsqrt(ms + EPS) * g_ref[...]).astype(bf16)

    o_ref[...] = jnp.dot(xn_ref[...], w_ref[...], preferred_element_type=f32).astype(o_ref.dtype)


def _proj(x2, g, w_p, tm, tn):
    t, d = x2.shape
    n = w_p.shape[1]
    return pl.pallas_call(
        _proj_kernel,
        grid=(t // tm, n // tn),
        in_specs=[
            pl.BlockSpec((tm, d), lambda i, j: (i, 0)),
            pl.BlockSpec((1, d), lambda i, j: (0, 0)),
            pl.BlockSpec((d, tn), lambda i, j: (0, j)),
        ],
        out_specs=pl.BlockSpec((tm, tn), lambda i, j: (i, j)),
        out_shape=jax.ShapeDtypeStruct((t, n), bf16),
        scratch_shapes=[pltpu.VMEM((tm, d), bf16)],
        compiler_params=_cparams(("parallel", "arbitrary")),
        name="proj",
    )(x2, g, w_p)


def _float_key(s):
    bits = pltpu.bitcast(s, jnp.int32)
    key = bits ^ ((bits >> 31) & jnp.int32(0x7FFFFFFF))
    return jnp.where(s == 0.0, jnp.int32(0), key)


def _attn_kernel(qlat_ref, qidx_ref, kwq_ref, ckv_ref, kwk_ref, gkv_ref, wuv_ref, o_ref,
                 kv_s, kvT_s, kidx_s, key_s, bias_s, qT_s, qiT_s, acc_s, *, n_sel, n_kt):
    i = pl.program_id(1)
    idx_scale = (H_IDX ** -0.5) * (D_IDX ** -0.5)
    attn_scale = D_LATENT ** -0.5
    hq = H_A * Q_TILE

    @pl.when(i == 0)
    def _():
        g = gkv_ref[...]
        for t in range(n_kt):
            c = ckv_ref[t * KEY_TILE:(t + 1) * KEY_TILE, :].astype(f32)
            ms = jnp.mean(c * c, axis=-1, keepdims=True)
            kv = c * lax.rsqrt(ms + EPS) * g
            kv_s[t] = kv.astype(bf16)
            kvT_s[t] = kv.T.astype(bf16)
            kidx_s[t] = kwk_ref[t * KEY_TILE:(t + 1) * KEY_TILE, :D_IDX]

    nk = ((i + 1) * Q_TILE + KEY_TILE - 1) // KEY_TILE
    lane = lax.broadcasted_iota(jnp.int32, (1, Q_TILE), 1)
    sub = lax.broadcasted_iota(jnp.int32, (KEY_TILE, 1), 0)
    q_chunk = (i * Q_TILE + lane) // CHUNK

    wT = kwq_ref[...].astype(f32).T
    for h in range(H_A):
        qT_s[:, h * Q_TILE:(h + 1) * Q_TILE] = qlat_ref[:, h * D_LATENT:(h + 1) * D_LATENT].astype(f32).T.astype(bf16)
    for h in range(H_IDX):
        qiT_s[:, h * Q_TILE:(h + 1) * Q_TILE] = qidx_ref[:, h * D_IDX:(h + 1) * D_IDX].astype(f32).T.astype(bf16)

    def score_tile(t, carry):
        d_all = jnp.dot(kidx_s[t], qiT_s[...], preferred_element_type=f32)
        acc = jnp.zeros((KEY_TILE, Q_TILE), f32)
        for h in range(H_IDX):
            d = d_all[:, h * Q_TILE:(h + 1) * Q_TILE]
            acc = acc + wT[D_IDX + h:D_IDX + h + 1, :] * jnp.maximum(d, 0.0)
        score = acc * idx_scale
        k_chunk = (t * KEY_TILE + sub) // CHUNK
        key_s[t] = jnp.where(k_chunk <= q_chunk, _float_key(score), INT_MIN)
        return carry

    lax.fori_loop(0, nk, score_tile, 0)

    @pl.when(nk % 2 == 1)
    def _():
        key_s[nk] = jnp.full((KEY_TILE, Q_TILE), INT_MIN, jnp.int32)

    n_pairs = (nk + 1) // 2

    def count(pred):
        def body(p, c):
            for t in (2 * p, 2 * p + 1):
                m = pred(key_s[t], t).astype(jnp.int32)
                c = c + jnp.sum(m.reshape(KEY_TILE // 8, 8, Q_TILE), axis=0)
            return c
        c8 = lax.fori_loop(0, n_pairs, body, jnp.zeros((8, Q_TILE), jnp.int32))
        return jnp.sum(c8, axis=0, keepdims=True)

    thr0 = jnp.where(count(lambda k, t: k >= 0) >= n_sel, jnp.int32(0), INT_MIN)
    thr0 = jnp.broadcast_to(thr0, (1, Q_TILE)).astype(jnp.int32)

    def bit_step(j, thr):
        cand = thr | (jnp.int32(1) << (jnp.int32(30) - j))
        return jnp.where(count(lambda k, t: k >= cand) >= n_sel, cand, thr)

    thr = lax.fori_loop(0, 31, bit_step, thr0)

    c_gt = count(lambda k, t: k > thr)
    c_ge = count(lambda k, t: k >= thr)
    need = n_sel - c_gt
    has_tie = jnp.max(jnp.where((c_ge > n_sel) & (thr > INT_MIN), 1, 0)) > 0

    def tie_limit():
        def step(j, m):
            cand = m | (jnp.int32(1) << (jnp.int32(14) - j))
            c = count(lambda k, t: (k == thr) & ((t * KEY_TILE + sub) < cand))
            return jnp.where(c < need, cand, m)
        return lax.fori_loop(0, 15, step, jnp.zeros((1, Q_TILE), jnp.int32))

    m_lim = lax.cond(has_tie, tie_limit, lambda: jnp.full((1, Q_TILE), 2 ** 30, jnp.int32))

    def bias_pair(p, carry):
        for t in (2 * p, 2 * p + 1):
            k = key_s[t]
            sel = (k > thr) | ((k == thr) & ((t * KEY_TILE + sub) <= m_lim))
            sel = sel & (k > INT_MIN)
            bias_s[t] = jnp.where(sel, 0.0, NEG_BIG).astype(f32)
        return carry

    lax.fori_loop(0, n_pairs, bias_pair, 0)

    acc_s[...] = jnp.zeros_like(acc_s)

    def att_pair(p, carry):
        m_run, l_run = carry
        kv2 = jnp.concatenate([kv_s[2 * p], kv_s[2 * p + 1]], axis=0)
        kvT2 = jnp.concatenate([kvT_s[2 * p], kvT_s[2 * p + 1]], axis=1)
        bias2 = jnp.concatenate([bias_s[2 * p], bias_s[2 * p + 1]], axis=0)
        logit = jnp.dot(kv2, qT_s[...], preferred_element_type=f32) * attn_scale
        logit = logit + jnp.concatenate([bias2] * H_A, axis=1)
        m_new = jnp.maximum(m_run, jnp.max(logit, axis=0, keepdims=True))
        alpha = jnp.exp(m_run - m_new)
        pr = jnp.exp(logit - m_new)
        l_new = alpha * l_run + jnp.sum(pr, axis=0, keepdims=True)
        acc_s[...] = alpha * acc_s[...] + jnp.dot(kvT2, pr.astype(bf16), preferred_element_type=f32)
        return m_new, l_new

    init = (jnp.full((1, hq), NEG_BIG, f32), jnp.zeros((1, hq), f32))
    _, l_fin = lax.fori_loop(0, n_pairs, att_pair, init)
    inv_l = 1.0 / l_fin
    for h in range(H_A):
        sl = slice(h * Q_TILE, (h + 1) * Q_TILE)
        o_lat = (acc_s[:, sl] * inv_l[:, sl]).T
        o_ref[:, h * DH_A:(h + 1) * DH_A] = jnp.dot(
            o_lat.astype(bf16), wuv_ref[h], preferred_element_type=f32).astype(o_ref.dtype)


def _attn(proj3, g_kv, w_uv_bf, n_sel):
    b, s, _ = proj3.shape
    n_kt = s // KEY_TILE
    kern = functools.partial(_attn_kernel, n_sel=n_sel, n_kt=n_kt)
    return pl.pallas_call(
        kern,
        grid=(b, s // Q_TILE),
        in_specs=[
            pl.BlockSpec((None, Q_TILE, H_A * D_LATENT), lambda bi, i: (bi, i, C_QLAT // 1024)),
            pl.BlockSpec((None, Q_TILE, H_IDX * D_IDX), lambda bi, i: (bi, i, C_QIDX // 512)),
            pl.BlockSpec((None, Q_TILE, LANES), lambda bi, i: (bi, i, C_KW // LANES)),
            pl.BlockSpec((None, s, LANES), lambda bi, i: (bi, 0, C_CKV // LANES)),
            pl.BlockSpec((None, s, LANES), lambda bi, i: (bi, 0, C_KW // LANES)),
            pl.BlockSpec((1, D_LATENT), lambda bi, i: (0, 0)),
            pl.BlockSpec((H_A, D_LATENT, DH_A), lambda bi, i: (0, 0, 0)),
        ],
        out_specs=pl.BlockSpec((None, Q_TILE, D_BRANCH), lambda bi, i: (bi, i, 0)),
        out_shape=jax.ShapeDtypeStruct((b, s, D_BRANCH), bf16),
        scratch_shapes=[
            pltpu.VMEM((n_kt, KEY_TILE, D_LATENT), bf16),
            pltpu.VMEM((n_kt, D_LATENT, KEY_TILE), bf16),
            pltpu.VMEM((n_kt, KEY_TILE, D_IDX), bf16),
            pltpu.VMEM((n_kt, KEY_TILE, Q_TILE), jnp.int32),
            pltpu.VMEM((n_kt, KEY_TILE, Q_TILE), f32),
            pltpu.VMEM((D_LATENT, H_A * Q_TILE), bf16),
            pltpu.VMEM((D_IDX, H_IDX * Q_TILE), bf16),
            pltpu.VMEM((D_LATENT, H_A * Q_TILE), f32),
        ],
        compiler_params=_cparams(("parallel", "arbitrary")),
        name="attn",
    )(proj3, proj3, proj3, proj3, proj3, g_kv, w_uv_bf)


def _ret_kernel(q_ref, k_ref, v_ref, gr_ref, cos_ref, sin_ref, din_ref, dq_ref, dk_ref, dc_ref,
                gret_ref, o_ref, state_s):
    @pl.when(pl.program_id(1) == 0)
    def _():
        state_s[...] = jnp.zeros_like(state_s)

    cos = cos_ref[...]
    sin = sin_ref[...]

    def rot(x):
        return x * cos + pltpu.roll(x, DK_R // 2, axis=1) * sin

    for h in range(H_R):
        sl = slice(h * DK_R, (h + 1) * DK_R)
        q = rot(q_ref[:, sl].astype(f32)).astype(bf16)
        kf = rot(k_ref[:, sl].astype(f32)) * (DK_R ** -0.5)
        k = kf.astype(bf16)
        v = v_ref[:, sl]
        inner = lax.dot_general(q, k, (((1,), (1,)), ((), ())), preferred_element_type=f32) * din_ref[h]
        o = jnp.dot(inner.astype(bf16), v, preferred_element_type=f32)
        st = state_s[h]
        o = o + jnp.dot(q, st.astype(bf16), preferred_element_type=f32) * dq_ref[h]
        kd = (kf * dk_ref[h]).astype(bf16)
        state_s[h] = st * dc_ref[h] + jnp.dot(kd.T, v, preferred_element_type=f32)
        mu = jnp.mean(o, axis=-1, keepdims=True)
        var = jnp.mean(jnp.square(o - mu), axis=-1, keepdims=True)
        y = (o - mu) * lax.rsqrt(var + EPS) * gret_ref[:, sl]
        gate = gr_ref[:, sl].astype(f32)
        o_ref[:, sl] = (gate * jax.nn.sigmoid(gate) * y).astype(o_ref.dtype)


def _ret(proj3, cos_t, sin_t, d_in, d_q, d_k, d_c, g_ret):
    b, s, _ = proj3.shape
    c = RET_CHUNK
    w = H_R * DK_R

    def col(off):
        return pl.BlockSpec((None, c, w), lambda bi, ci: (bi, ci, off // w))

    return pl.pallas_call(
        _ret_kernel,
        grid=(b, s // c),
        in_specs=[
            col(C_QR), col(C_KR), col(C_VR), col(C_GR),
            pl.BlockSpec((c, DK_R), lambda bi, ci: (ci, 0)),
            pl.BlockSpec((c, DK_R), lambda bi, ci: (ci, 0)),
            pl.BlockSpec((H_R, c, c), lambda bi, ci: (0, 0, 0)),
            pl.BlockSpec((H_R, c, DK_R), lambda bi, ci: (0, 0, 0)),
            pl.BlockSpec((H_R, c, DK_R), lambda bi, ci: (0, 0, 0)),
            pl.BlockSpec((H_R, 1, DK_R), lambda bi, ci: (0, 0, 0)),
            pl.BlockSpec((1, w), lambda bi, ci: (0, 0)),
        ],
        out_specs=pl.BlockSpec((None, c, w), lambda bi, ci: (bi, ci, 0)),
        out_shape=jax.ShapeDtypeStruct((b, s, w), bf16),
        scratch_shapes=[pltpu.VMEM((H_R, DK_R, DV_R), f32)],
        compiler_params=_cparams(("parallel", "arbitrary")),
        name="ret",
    )(proj3, proj3, proj3, proj3, cos_t, sin_t, d_in, d_q, d_k, d_c, g_ret)


def _retention_tables(s):
    c = RET_CHUNK
    half = DK_R // 2
    freq = ROPE_BASE ** (-jnp.arange(half, dtype=f32) / half)
    ang = jnp.arange(s, dtype=f32)[:, None] * freq[None, :]
    cos = jnp.cos(ang)
    sin = jnp.sin(ang)
    cos_t = jnp.concatenate([cos, cos], axis=-1)
    sin_t = jnp.concatenate([-sin, sin], axis=-1)
    log_gamma = jnp.log1p(-jnp.exp2(-5.0 - jnp.arange(H_R, dtype=f32)))
    n = jnp.arange(c, dtype=f32)
    diff = n[:, None] - n[None, :]
    d_in = jnp.where(diff >= 0, jnp.exp(log_gamma[:, None, None] * jnp.maximum(diff, 0.0)), 0.0)
    d_q = jnp.broadcast_to(jnp.exp(log_gamma[:, None] * (n + 1.0))[:, :, None], (H_R, c, DK_R))
    d_k = jnp.broadcast_to(jnp.exp(log_gamma[:, None] * (c - 1.0 - n))[:, :, None], (H_R, c, DK_R))
    d_c = jnp.broadcast_to(jnp.exp(log_gamma * c)[:, None, None], (H_R, 1, DK_R))
    return cos_t, sin_t, d_in, d_q, d_k, d_c


MIX_CHUNK = 512


def _mix_kernel(oa_ref, ob_ref, wb_ref, ga0_ref, ga1_ref, gb0_ref, gb1_ref, o_ref):
    oa = oa_ref[...]
    ob = ob_ref[...]
    half = ga0_ref.shape[1]
    for c in range(0, o_ref.shape[1], MIX_CHUNK):
        ga_ref, gb_ref, off = (ga0_ref, gb0_ref, c) if c < half else (ga1_ref, gb1_ref, c - half)
        a = jnp.dot(oa, wb_ref[0, :, c:c + MIX_CHUNK], preferred_element_type=f32)
        b = jnp.dot(ob, wb_ref[1, :, c:c + MIX_CHUNK], preferred_element_type=f32)
        ga = jax.nn.sigmoid(ga_ref[:, off:off + MIX_CHUNK].astype(f32))
        gb = jax.nn.sigmoid(gb_ref[:, off:off + MIX_CHUNK].astype(f32))
        o_ref[:, c:c + MIX_CHUNK] = (ga * a + gb * b).astype(o_ref.dtype)


def _mix(o_a, o_b, w_branch_bf, proj, tm):
    t = o_a.shape[0]
    d = w_branch_bf.shape[2]
    half = d // 2
    assert C_GBR % half == 0 and half % MIX_CHUNK == 0

    def gate(k):
        return pl.BlockSpec((tm, half), lambda i: (i, C_GBR // half + k))

    return pl.pallas_call(
        _mix_kernel,
        grid=(t // tm,),
        in_specs=[
            pl.BlockSpec((tm, D_BRANCH), lambda i: (i, 0)),
            pl.BlockSpec((tm, D_BRANCH), lambda i: (i, 0)),
            pl.BlockSpec((N_BRANCH, D_BRANCH, d), lambda i: (0, 0, 0)),
            gate(0), gate(1), gate(2), gate(3),
        ],
        out_specs=pl.BlockSpec((tm, d), lambda i: (i, 0)),
        out_shape=jax.ShapeDtypeStruct((t, d), bf16),
        compiler_params=_cparams(("parallel",)),
        name="mix",
    )(o_a, o_b, w_branch_bf, proj, proj, proj, proj)


def _pack_rows(v):
    n = v.shape[1] // 2
    r = pltpu.bitcast(v.astype(bf16).astype(f32), jnp.uint32)
    w = (r[:, :n] >> 16) | (r[:, n:] & jnp.uint32(0xFFFF0000))
    return pltpu.einshape("r(ab)->rab", w, b=LANES)


def _unpack_rows(p):
    w = pltpu.einshape("rab->r(ab)", p)
    lo = pltpu.bitcast(w << 16, f32)
    hi = pltpu.bitcast(w & jnp.uint32(0xFFFF0000), f32)
    return lo, hi


def _split_bf16(a):
    hi = a.astype(bf16)
    lo = (a - hi.astype(f32)).astype(bf16)
    return hi, lo


def _outproj_kernel(mixed_ref, x_ref, wo_ref, g_ref, wr_ref, br_ref,
                    h_ref, xn_ref, eid_ref, gate_ref, cnt_ref):
    h = x_ref[...] + jnp.dot(mixed_ref[...], wo_ref[...], preferred_element_type=f32)
    h_ref[...] = h
    ms = jnp.mean(h * h, axis=-1, keepdims=True)
    xn = h * lax.rsqrt(ms + EPS) * g_ref[...]
    xn_ref[...] = _pack_rows(xn)

    x_hi, x_lo = _split_bf16(xn)
    hh_hl = jnp.dot(x_hi, wr_ref[...], preferred_element_type=f32)
    logit = (hh_hl[:, :LANES] + hh_hl[:, LANES:]
             + jnp.dot(x_lo, wr_ref[:, :LANES], preferred_element_type=f32)) + br_ref[...]

    lane = lax.broadcasted_iota(jnp.int32, logit.shape, 1)
    lanef = lane.astype(f32)
    neg = -jnp.inf

    def first_argmax(v, m):
        return jnp.min(jnp.where(v == m, lanef, float(LANES)), axis=-1, keepdims=True)

    lg = jnp.where(lane < N_GROUPS, logit, neg)
    mg = jnp.max(lg, axis=-1, keepdims=True)
    p_grp = 1.0 / jnp.sum(jnp.exp(lg - mg), axis=-1, keepdims=True)
    grp = first_argmax(lg, mg).astype(jnp.int32)

    e_lane = lane - N_GROUPS
    in_grp = (e_lane >= 0) & (e_lane < N_EXPERTS) & ((e_lane // EXP_PER_GROUP) == grp)
    le = jnp.where(in_grp, logit, neg)
    m1 = jnp.max(le, axis=-1, keepdims=True)
    i1 = first_argmax(le, m1)
    le2 = jnp.where(lanef == i1, neg, le)
    m2 = jnp.max(le2, axis=-1, keepdims=True)
    i2 = first_argmax(le2, m2)
    e2 = jnp.exp(m2 - m1)
    g1 = p_grp / (1.0 + e2)
    g2 = p_grp * e2 / (1.0 + e2)

    eid = jnp.where(lane == 0, i1, jnp.where(lane == 1, i2, float(N_GROUPS))) - float(N_GROUPS)
    eid_ref[...] = eid.astype(jnp.int32).T[:8, :]
    gate_ref[...] = jnp.where(lane == 0, g1, jnp.where(lane == 1, g2, 0.0))

    @pl.when(pl.program_id(0) == 0)
    def _():
        cnt_ref[...] = jnp.zeros_like(cnt_ref)

    half = pl.program_id(0) // (pl.num_programs(0) // ROUTE_HALVES)
    sub8 = lax.broadcasted_iota(jnp.int32, (8, LANES), 0)
    for s, idx in enumerate((i1, i2)):
        c = jnp.sum((lanef == idx).astype(jnp.int32), axis=0, keepdims=True)
        cnt_ref[...] += jnp.where(sub8 == s * ROUTE_HALVES + half, c, 0)


def _outproj(mixed, x2, w_out_bf, g_ffn, wr_hi_lo, b_r, tm):
    t, d = x2.shape
    row = lambda i: (i, 0)
    fixed = lambda i: (0, 0)
    return pl.pallas_call(
        _outproj_kernel,
        grid=(t // tm,),
        in_specs=[
            pl.BlockSpec((tm, d), row),
            pl.BlockSpec((tm, d), row),
            pl.BlockSpec((d, d), fixed),
            pl.BlockSpec((1, d), fixed),
            pl.BlockSpec((d, 2 * LANES), fixed),
            pl.BlockSpec((1, LANES), fixed),
        ],
        out_specs=[
            pl.BlockSpec((tm, d), row),
            pl.BlockSpec((tm, d // (2 * LANES), LANES), lambda i: (i, 0, 0)),
            pl.BlockSpec((8, tm), lambda i: (0, i)),
            pl.BlockSpec((tm, LANES), row),
            pl.BlockSpec((8, LANES), fixed),
        ],
        out_shape=[
            jax.ShapeDtypeStruct((t, d), f32),
            jax.ShapeDtypeStruct((t, d // (2 * LANES), LANES), jnp.uint32),
            jax.ShapeDtypeStruct((8, t), jnp.int32),
            jax.ShapeDtypeStruct((t, LANES), f32),
            jax.ShapeDtypeStruct((8, LANES), jnp.int32),
        ],
        compiler_params=_cparams(("arbitrary",)),
        name="outproj",
    )(mixed, x2, w_out_bf, g_ffn, wr_hi_lo, b_r)


ISSUE_UNROLL = 8
CAST_ROWS = 256


def _experts_kernel(bexp_ref, nexp_ref, rpack_ref, nused_ref, xn_hbm, wg_hbm, wu_hbm, wd_hbm, yt_hbm,
                    xbuf, ybuf, hm_s, wg_st, wu_st, wd_st, wg_bf, wu_bf, wd_bf, gsem, ssem, wsem, *, n_tok):
    j = pl.program_id(0)
    n_used = nused_ref[0]
    slot = j % 2
    tok_bits = (n_tok - 1).bit_length()

    def rows_of(blk, fn):
        base = blk * ROW_BLOCK

        def body(k, c):
            r0 = pl.multiple_of(k * ISSUE_UNROLL, ISSUE_UNROLL)
            for u in range(ISSUE_UNROLL):
                fn(r0 + u, rpack_ref[base + r0 + u], 1)
            return c

        lax.fori_loop(0, ROW_BLOCK // ISSUE_UNROLL, body, 0)

    def start_gathers(blk, sl):
        def one(r, packed, queue):
            tok = packed & ((1 << tok_bits) - 1)
            pltpu.make_async_copy(xn_hbm.at[tok], xbuf.at[sl, r], gsem.at[sl]).start(priority=queue)
        rows_of(blk, one)

    def start_scatters(blk, sl):
        def one(r, packed, queue):
            row = lax.shift_right_logical(packed, tok_bits)
            pltpu.make_async_copy(ybuf.at[sl, r], yt_hbm.at[row], ssem.at[sl]).start(priority=queue)
        rows_of(blk, one)

    def wait_gathers(sl):
        pltpu.make_async_copy(xn_hbm.at[pl.ds(0, ROW_BLOCK)], xbuf.at[sl], gsem.at[sl]).wait()

    def wait_scatters(sl):
        pltpu.make_async_copy(ybuf.at[sl], yt_hbm.at[pl.ds(0, ROW_BLOCK)], ssem.at[sl]).wait()

    staged = ((wg_hbm, wg_st, wg_bf), (wu_hbm, wu_st, wu_bf), (wd_hbm, wd_st, wd_bf))

    def start_weights(e):
        for q, (src, st, _) in enumerate(staged):
            pltpu.make_async_copy(src.at[e], st, wsem.at[q]).start()

    def wait_and_cast_weights():
        for q, (src, st, dst) in enumerate(staged):
            pltpu.make_async_copy(src.at[0], st, wsem.at[q]).wait()

            def cast(c, carry, st=st, dst=dst):
                r = pl.multiple_of(c * CAST_ROWS, CAST_ROWS)
                dst[pl.ds(r, CAST_ROWS), :] = st[pl.ds(r, CAST_ROWS), :].astype(bf16)
                return carry

            lax.fori_loop(0, st.shape[0] // CAST_ROWS, cast, 0)

    @pl.when(j == 0)
    def _():
        start_weights(bexp_ref[0])
        start_gathers(0, 0)
        ybuf[...] = jnp.zeros_like(ybuf)
        for sl in range(2):
            spare = yt_hbm.at[pl.ds(2 * n_tok + sl * ROW_BLOCK, ROW_BLOCK)]
            pltpu.make_async_copy(ybuf.at[sl], spare, ssem.at[sl]).start()
        for sl in range(2):
            wait_scatters(sl)

    @pl.when(j < n_used)
    def _():
        e = bexp_ref[j]

        @pl.when((j == 0) | (bexp_ref[jnp.maximum(j - 1, 0)] != e))
        def _():
            wait_and_cast_weights()

            @pl.when(nexp_ref[j] >= 0)
            def _():
                start_weights(nexp_ref[j])

        wait_gathers(slot)

        @pl.when(j >= 2)
        def _():
            wait_scatters(slot)

        lo, hi = _unpack_rows(xbuf[slot])
        xb = jnp.concatenate([lo, hi], axis=1).astype(bf16)
        g = jnp.dot(xb, wg_bf[...], preferred_element_type=f32)
        u = jnp.dot(xb, wu_bf[...], preferred_element_type=f32)
        hm_s[...] = (g * jax.nn.sigmoid(g) * u).astype(bf16)

        @pl.when(j + 1 < n_used)
        def _():
            start_gathers(j + 1, 1 - slot)

        ybuf[slot] = _pack_rows(jnp.dot(hm_s[...], wd_bf[...], preferred_element_type=f32))
        start_scatters(j, slot)

        @pl.when(j == n_used - 1)
        def _():
            wait_scatters(slot)

            @pl.when(j >= 1)
            def _():
                wait_scatters(1 - slot)


def _experts(block_expert, next_expert, row_pack, n_used, xn_packed, w_gate, w_up, w_down):
    n_rows = row_pack.shape[0]
    n_tok = xn_packed.shape[0]
    tile = xn_packed.shape[1:]
    _, d, f = w_gate.shape
    assert d % CAST_ROWS == 0 and f % CAST_ROWS == 0
    any_space = pl.BlockSpec(memory_space=pl.ANY)
    grid_spec = pltpu.PrefetchScalarGridSpec(
        num_scalar_prefetch=4,
        grid=(n_rows // ROW_BLOCK,),
        in_specs=[any_space, any_space, any_space, any_space],
        out_specs=any_space,
        scratch_shapes=[pltpu.VMEM((2, ROW_BLOCK) + tile, jnp.uint32),
                        pltpu.VMEM((2, ROW_BLOCK) + tile, jnp.uint32),
                        pltpu.VMEM((ROW_BLOCK, f), bf16),
                        pltpu.VMEM((d, f), f32), pltpu.VMEM((d, f), f32), pltpu.VMEM((f, d), f32),
                        pltpu.VMEM((d, f), bf16), pltpu.VMEM((d, f), bf16), pltpu.VMEM((f, d), bf16),
                        pltpu.SemaphoreType.DMA((2,)), pltpu.SemaphoreType.DMA((2,)),
                        pltpu.SemaphoreType.DMA((3,))],
    )
    return pl.pallas_call(
        functools.partial(_experts_kernel, n_tok=n_tok),
        grid_spec=grid_spec,
        out_shape=jax.ShapeDtypeStruct((2 * n_tok + 2 * ROW_BLOCK,) + tile, jnp.uint32),
        compiler_params=_cparams(("arbitrary",)),
        name="experts",
    )(block_expert, next_expert, row_pack, n_used, xn_packed, w_gate, w_up, w_down)


def _combine_kernel(h_ref, y0_ref, y1_ref, gate_ref, g_ref, o_ref):
    gate = gate_ref[...]
    y0 = jnp.concatenate(_unpack_rows(y0_ref[...]), axis=1)
    y1 = jnp.concatenate(_unpack_rows(y1_ref[...]), axis=1)
    hh = h_ref[...] + gate[:, 0:1] * y0 + gate[:, 1:2] * y1
    ms = jnp.mean(hh * hh, axis=-1, keepdims=True)
    o_ref[...] = hh * lax.rsqrt(ms + EPS) * g_ref[...]


def _combine(h, yt, gate, g_final, tm):
    t, d = h.shape
    nt = t // tm
    return pl.pallas_call(
        _combine_kernel,
        grid=(nt,),
        in_specs=[
            pl.BlockSpec((tm, d), lambda i: (i, 0)),
            pl.BlockSpec((tm,) + yt.shape[1:], lambda i: (i, 0, 0)),
            pl.BlockSpec((tm,) + yt.shape[1:], lambda i: (nt + i, 0, 0)),
            pl.BlockSpec((tm, LANES), lambda i: (i, 0)),
            pl.BlockSpec((1, d), lambda i: (0, 0)),
        ],
        out_specs=pl.BlockSpec((tm, d), lambda i: (i, 0)),
        out_shape=jax.ShapeDtypeStruct((t, d), f32),
        compiler_params=_cparams(("parallel",)),
        name="combine",
    )(h, yt, yt, gate, g_final)


def _route_kernel(eid_ref, cnt_ref, rpack_ref, bexp_ref, nexp_ref, nused_ref, *cur_refs, n_tok, n_blocks):
    tok_bits = (n_tok - 1).bit_length()
    chunk = 2 * n_tok // ROUTE_CHAINS

    def no_next(k, carry):
        nexp_ref[k] = -1
        return carry

    lax.fori_loop(0, n_blocks, no_next, 0)

    def per_expert(e, carry):
        blk, prev_blk, prev_nb = carry
        start = blk * ROW_BLOCK
        run = start
        for c in range(ROUTE_CHAINS):
            cur_refs[c][e] = run
            run = run + cnt_ref[c * N_EXPERTS + e]
        nb = (run - start + ROW_BLOCK - 1) // ROW_BLOCK

        def set_block(k, c):
            bexp_ref[blk + k] = e
            return c

        lax.fori_loop(0, nb, set_block, 0)

        def set_next(k, c):
            nexp_ref[prev_blk + k] = e
            return c

        lax.fori_loop(0, jnp.where(nb > 0, prev_nb, 0), set_next, 0)

        def set_pad(r, c):
            rpack_ref[r] = (2 * n_tok + (r & (2 * ROW_BLOCK - 1))) << tok_bits
            return c

        lax.fori_loop(run, start + nb * ROW_BLOCK, set_pad, 0)
        return blk + nb, jnp.where(nb > 0, blk, prev_blk), jnp.where(nb > 0, nb, prev_nb)

    n_used, _, _ = lax.fori_loop(0, N_EXPERTS, per_expert, (0, 0, 0))
    nused_ref[0] = n_used

    def tail_block(k, carry):
        bexp_ref[k] = N_EXPERTS - 1
        return carry

    lax.fori_loop(n_used, n_blocks, tail_block, 0)

    def tail_row(r, carry):
        rpack_ref[r] = (2 * n_tok + (r & (2 * ROW_BLOCK - 1))) << tok_bits
        return carry

    lax.fori_loop(n_used * ROW_BLOCK, n_blocks * ROW_BLOCK, tail_row, 0)

    def place(i, carry):
        for c in range(ROUTE_CHAINS):
            a = c * chunk + i
            e = eid_ref[a]
            p = cur_refs[c][e]
            cur_refs[c][e] = p + 1
            rpack_ref[p] = (a << tok_bits) | (a - (c * chunk // n_tok) * n_tok)
        return carry

    lax.fori_loop(0, chunk, place, 0)


def _route(eid_flat, counts, n_tok):
    n_asg = eid_flat.shape[0]
    n_rows = -(-(n_asg + N_EXPERTS * (ROW_BLOCK - 1)) // ROW_BLOCK) * ROW_BLOCK
    n_blocks = n_rows // ROW_BLOCK
    smem = pl.BlockSpec(memory_space=pltpu.SMEM)
    return pl.pallas_call(
        functools.partial(_route_kernel, n_tok=n_tok, n_blocks=n_blocks),
        in_specs=[smem, smem],
        out_specs=[smem, smem, smem, smem],
        out_shape=[jax.ShapeDtypeStruct((n_rows,), jnp.int32),
                   jax.ShapeDtypeStruct((n_blocks,), jnp.int32),
                   jax.ShapeDtypeStruct((n_blocks,), jnp.int32),
                   jax.ShapeDtypeStruct((1,), jnp.int32)],
        scratch_shapes=[pltpu.SMEM((N_EXPERTS,), jnp.int32)] * ROUTE_CHAINS,
        name="route",
    )(eid_flat, counts)


def _pick(n, prefs):
    for p in prefs:
        if n % p == 0:
            return p
    return n


def kernel(x, g_mix_norm, w_in, g_kv, w_uv, g_ret, w_branch, w_out, g_ffn_norm, w_router_group,
           b_router_group, w_router_expert, b_router_expert, w_expert_gate, w_expert_up,
           w_expert_down, g_final):
    b, s, d = x.shape
    t = b * s
    depth = w_in.shape[0]
    n_sel = min(TOPK_MAX, s // 4)
    assert s % RET_CHUNK == 0 and s % Q_TILE == 0

    cos_t, sin_t, d_in, d_q, d_k, d_c = _retention_tables(s)
    h2 = x.reshape(t, d)
    for l in range(depth):
        wl = w_in[l]
        sp = np.cumsum([0, H_A * D_LATENT, D_LATENT, H_IDX * D_IDX, D_IDX, H_IDX,
                        H_R * DK_R, H_R * DK_R, H_R * DV_R, H_R * DV_R, N_BRANCH * d])
        seg = [wl[:, sp[k]:sp[k + 1]].astype(bf16) for k in range(10)]
        kw_pad = jnp.zeros((d, LANES - D_IDX - H_IDX), bf16)
        w_p = jnp.concatenate([seg[0], seg[9], seg[5], seg[6], seg[7], seg[8], seg[2], seg[1],
                               seg[3], seg[4], kw_pad], axis=1)
        assert w_p.shape[1] == D_IN_P

        proj = _proj(h2, g_mix_norm[l].reshape(1, d), w_p, _pick(t, (512, 256)), 3328)
        proj3 = proj.reshape(b, s, D_IN_P)

        o_a = _attn(proj3, g_kv[l].reshape(1, D_LATENT), w_uv[l].astype(bf16), n_sel)
        o_b = _ret(proj3, cos_t, sin_t, d_in, d_q, d_k, d_c, g_ret[l].reshape(1, H_R * DV_R))

        mixed = _mix(o_a.reshape(t, D_BRANCH), o_b.reshape(t, D_BRANCH), w_branch[l].astype(bf16),
                     proj, _pick(t, (512, 256)))

        w_r = jnp.concatenate([w_router_group[l], w_router_expert[l],
                               jnp.zeros((d, LANES - N_GROUPS - N_EXPERTS), f32)], axis=1)
        b_r = jnp.concatenate([b_router_group[l], b_router_expert[l],
                               jnp.zeros((LANES - N_GROUPS - N_EXPERTS,), f32)]).reshape(1, LANES)
        wr_hi = w_r.astype(bf16)
        wr_lo = (w_r - wr_hi.astype(f32)).astype(bf16)
        h2, xn, eid_t, gate, cnt = _outproj(mixed, h2, w_out[l].astype(bf16), g_ffn_norm[l].reshape(1, d),
                                            jnp.concatenate([wr_hi, wr_lo], axis=1), b_r, _pick(t, (512, 256)))

        row_pack, block_expert, next_expert, n_used = _route(
            eid_t[:2].reshape(-1), cnt[:ROUTE_CHAINS, N_GROUPS:N_GROUPS + N_EXPERTS].reshape(-1), t)
        yt = _experts(block_expert, next_expert, row_pack, n_used, xn,
                      w_expert_gate[l], w_expert_up[l], w_expert_down[l])
        assert depth == 1
        h2 = _combine(h2, yt, gate, g_final.reshape(1, d), _pick(t, (256,)))
    return h2.reshape(b, s, d)
```

```python
import functools

import jax
import jax.numpy as jnp
import numpy as np
from jax import lax
from jax.experimental import pallas as pl
from jax.experimental.pallas import tpu as pltpu

EPS = 1e-6
CHUNK = 64
H_A = 8
D_LATENT = 128
DH_A = 128
H_IDX = 8
D_IDX = 64
TOPK_MAX = 256
H_R = 8
DK_R = 128
DV_R = 128
ROPE_BASE = 10000.0
D_BRANCH = 1024
N_BRANCH = 2
N_GROUPS = 4
EXP_PER_GROUP = 8
N_EXPERTS = N_GROUPS * EXP_PER_GROUP
D_EXPERT = 1024

LANES = 128
KEY_TILE = 256
Q_TILE = 512
RET_CHUNK = 256
ROW_BLOCK = 256
ROUTE_HALVES = 4
ROUTE_CHAINS = 2 * ROUTE_HALVES
VMEM_LIMIT = 56 * 1024 * 1024

C_QLAT = 0
C_GBR = 1024
C_QR = 5120
C_KR = 6144
C_VR = 7168
C_GR = 8192
C_QIDX = 9216
C_CKV = 9728
C_KW = 9856
D_IN_P = 9984

INT_MIN = np.int32(-2 ** 31)
NEG_BIG = -1e30

bf16 = jnp.bfloat16
f32 = jnp.float32


def _cparams(sem):
    return pltpu.CompilerParams(dimension_semantics=sem, vmem_limit_bytes=VMEM_LIMIT)


def _proj_kernel(x_ref, g_ref, w_ref, o_ref, xn_ref):
    @pl.when(pl.program_id(1) == 0)
    def _():
        x = x_ref[...]
        ms = jnp.mean(x * x, axis=-1, keepdims=True)
        xn_ref[...] = (x * lax.rsqrt(ms + EPS) * g_ref[...]).astype(bf16)

    o_ref[...] = jnp.dot(xn_ref[...], w_ref[...], preferred_element_type=f32).astype(o_ref.dtype)


def _proj(x2, g, w_p, tm, tn):
    t, d = x2.shape
    n = w_p.shape[1]
    return pl.pallas_call(
        _proj_kernel,
        grid=(t // tm, n // tn),
        in_specs=[
            pl.BlockSpec((tm, d), lambda i, j: (i, 0)),
            pl.BlockSpec((1, d), lambda i, j: (0, 0)),
            pl.BlockSpec((d, tn), lambda i, j: (0, j)),
        ],
        out_specs=pl.BlockSpec((tm, tn), lambda i, j: (i, j)),
        out_shape=jax.ShapeDtypeStruct((t, n), bf16),
        scratch_shapes=[pltpu.VMEM((tm, d), bf16)],
        compiler_params=_cparams(("parallel", "arbitrary")),
        name="proj",
    )(x2, g, w_p)


def _float_key(s):
    bits = pltpu.bitcast(s, jnp.int32)
    key = bits ^ ((bits >> 31) & jnp.int32(0x7FFFFFFF))
    return jnp.where(s == 0.0, jnp.int32(0), key)


def _attn_kernel(qlat_ref, qidx_ref, kwq_ref, ckv_ref, kwk_ref, gkv_ref, wuv_ref, o_ref,
                 kv_s, kvT_s, kidx_s, key_s, bias_s, qT_s, qiT_s, acc_s, *, n_sel, n_kt):
    i = pl.program_id(1)
    idx_scale = (H_IDX ** -0.5) * (D_IDX ** -0.5)
    attn_scale = D_LATENT ** -0.5
    hq = H_A * Q_TILE

    @pl.when(i == 0)
    def _():
        g = gkv_ref[...]
        for t in range(n_kt):
            c = ckv_ref[t * KEY_TILE:(t + 1) * KEY_TILE, :].astype(f32)
            ms = jnp.mean(c * c, axis=-1, keepdims=True)
            kv = c * lax.rsqrt(ms + EPS) * g
            kv_s[t] = kv.astype(bf16)
            kvT_s[t] = kv.T.astype(bf16)
            kidx_s[t] = kwk_ref[t * KEY_TILE:(t + 1) * KEY_TILE, :D_IDX]

    nk = ((i + 1) * Q_TILE + KEY_TILE - 1) // KEY_TILE
    lane = lax.broadcasted_iota(jnp.int32, (1, Q_TILE), 1)
    sub = lax.broadcasted_iota(jnp.int32, (KEY_TILE, 1), 0)
    q_chunk = (i * Q_TILE + lane) // CHUNK

    wT = kwq_ref[...].astype(f32).T
    for h in range(H_A):
        qT_s[:, h * Q_TILE:(h + 1) * Q_TILE] = qlat_ref[:, h * D_LATENT:(h + 1) * D_LATENT].astype(f32).T.astype(bf16)
    for h in range(H_IDX):
        qiT_s[:, h * Q_TILE:(h + 1) * Q_TILE] = qidx_ref[:, h * D_IDX:(h + 1) * D_IDX].astype(f32).T.astype(bf16)

    def score_tile(t, carry):
        d_all = jnp.dot(kidx_s[t], qiT_s[...], preferred_element_type=f32)
        acc = jnp.zeros((KEY_TILE, Q_TILE), f32)
        for h in range(H_IDX):
            d = d_all[:, h * Q_TILE:(h + 1) * Q_TILE]
            acc = acc + wT[D_IDX + h:D_IDX + h + 1, :] * jnp.maximum(d, 0.0)
        score = acc * idx_scale
        k_chunk = (t * KEY_TILE + sub) // CHUNK
        key_s[t] = jnp.where(k_chunk <= q_chunk, _float_key(score), INT_MIN)
        return carry

    lax.fori_loop(0, nk, score_tile, 0)

    @pl.when(nk % 2 == 1)
    def _():
        key_s[nk] = jnp.full((KEY_TILE, Q_TILE), INT_MIN, jnp.int32)

    n_pairs = (nk + 1) // 2

    def count(pred):
        def body(p, c):
            for t in (2 * p, 2 * p + 1):
                m = pred(key_s[t], t).astype(jnp.int32)
                c = c + jnp.sum(m.reshape(KEY_TILE // 8, 8, Q_TILE), axis=0)
            return c
        c8 = lax.fori_loop(0, n_pairs, body, jnp.zeros((8, Q_TILE), jnp.int32))
        return jnp.sum(c8, axis=0, keepdims=True)

    thr0 = jnp.where(count(lambda k, t: k >= 0) >= n_sel, jnp.int32(0), INT_MIN)
    thr0 = jnp.broadcast_to(thr0, (1, Q_TILE)).astype(jnp.int32)

    def bit_step(j, thr):
        cand = thr | (jnp.int32(1) << (jnp.int32(30) - j))
        return jnp.where(count(lambda k, t: k >= cand) >= n_sel, cand, thr)

    thr = lax.fori_loop(0, 31, bit_step, thr0)

    c_gt = count(lambda k, t: k > thr)
    c_ge = count(lambda k, t: k >= thr)
    need = n_sel - c_gt
    has_tie = jnp.max(jnp.where((c_ge > n_sel) & (thr > INT_MIN), 1, 0)) > 0

    def tie_limit():
        def step(j, m):
            cand = m | (jnp.int32(1) << (jnp.int32(14) - j))
            c = count(lambda k, t: (k == thr) & ((t * KEY_TILE + sub) < cand))
            return jnp.where(c < need, cand, m)
        return lax.fori_loop(0, 15, step, jnp.zeros((1, Q_TILE), jnp.int32))

    m_lim = lax.cond(has_tie, tie_limit, lambda: jnp.full((1, Q_TILE), 2 ** 30, jnp.int32))

    def bias_pair(p, carry):
        for t in (2 * p, 2 * p + 1):
            k = key_s[t]
            sel = (k > thr) | ((k == thr) & ((t * KEY_TILE + sub) <= m_lim))
            sel = sel & (k > INT_MIN)
            bias_s[t] = jnp.where(sel, 0.0, NEG_BIG).astype(f32)
        return carry

    lax.fori_loop(0, n_pairs, bias_pair, 0)

    acc_s[...] = jnp.zeros_like(acc_s)

    def att_pair(p, carry):
        m_run, l_run = carry
        kv2 = jnp.concatenate([kv_s[2 * p], kv_s[2 * p + 1]], axis=0)
        kvT2 = jnp.concatenate([kvT_s[2 * p], kvT_s[2 * p + 1]], axis=1)
        bias2 = jnp.concatenate([bias_s[2 * p], bias_s[2 * p + 1]], axis=0)
        logit = jnp.dot(kv2, qT_s[...], preferred_element_type=f32) * attn_scale
        logit = logit + jnp.concatenate([bias2] * H_A, axis=1)
        m_new = jnp.maximum(m_run, jnp.max(logit, axis=0, keepdims=True))
        alpha = jnp.exp(m_run - m_new)
        pr = jnp.exp(logit - m_new)
        l_new = alpha * l_run + jnp.sum(pr, axis=0, keepdims=True)
        acc_s[...] = alpha * acc_s[...] + jnp.dot(kvT2, pr.astype(bf16), preferred_element_type=f32)
        return m_new, l_new

    init = (jnp.full((1, hq), NEG_BIG, f32), jnp.zeros((1, hq), f32))
    _, l_fin = lax.fori_loop(0, n_pairs, att_pair, init)
    inv_l = 1.0 / l_fin
    for h in range(H_A):
        sl = slice(h * Q_TILE, (h + 1) * Q_TILE)
        o_lat = (acc_s[:, sl] * inv_l[:, sl]).T
        o_ref[:, h * DH_A:(h + 1) * DH_A] = jnp.dot(
            o_lat.astype(bf16), wuv_ref[h], preferred_element_type=f32).astype(o_ref.dtype)


def _attn(proj3, g_kv, w_uv_bf, n_sel):
    b, s, _ = proj3.shape
    n_kt = s // KEY_TILE
    kern = functools.partial(_attn_kernel, n_sel=n_sel, n_kt=n_kt)
    return pl.pallas_call(
        kern,
        grid=(b, s // Q_TILE),
        in_specs=[
            pl.BlockSpec((None, Q_TILE, H_A * D_LATENT), lambda bi, i: (bi, i, C_QLAT // 1024)),
            pl.BlockSpec((None, Q_TILE, H_IDX * D_IDX), lambda bi, i: (bi, i, C_QIDX // 512)),
            pl.BlockSpec((None, Q_TILE, LANES), lambda bi, i: (bi, i, C_KW // LANES)),
            pl.BlockSpec((None, s, LANES), lambda bi, i: (bi, 0, C_CKV // LANES)),
            pl.BlockSpec((None, s, LANES), lambda bi, i: (bi, 0, C_KW // LANES)),
            pl.BlockSpec((1, D_LATENT), lambda bi, i: (0, 0)),
            pl.BlockSpec((H_A, D_LATENT, DH_A), lambda bi, i: (0, 0, 0)),
        ],
        out_specs=pl.BlockSpec((None, Q_TILE, D_BRANCH), lambda bi, i: (bi, i, 0)),
        out_shape=jax.ShapeDtypeStruct((b, s, D_BRANCH), bf16),
        scratch_shapes=[
            pltpu.VMEM((n_kt, KEY_TILE, D_LATENT), bf16),
            pltpu.VMEM((n_kt, D_LATENT, KEY_TILE), bf16),
            pltpu.VMEM((n_kt, KEY_TILE, D_IDX), bf16),
            pltpu.VMEM((n_kt, KEY_TILE, Q_TILE), jnp.int32),
            pltpu.VMEM((n_kt, KEY_TILE, Q_TILE), f32),
            pltpu.VMEM((D_LATENT, H_A * Q_TILE), bf16),
            pltpu.VMEM((D_IDX, H_IDX * Q_TILE), bf16),
            pltpu.VMEM((D_LATENT, H_A * Q_TILE), f32),
        ],
        compiler_params=_cparams(("parallel", "arbitrary")),
        name="attn",
    )(proj3, proj3, proj3, proj3, proj3, g_kv, w_uv_bf)


def _ret_kernel(q_ref, k_ref, v_ref, gr_ref, cos_ref, sin_ref, din_ref, dq_ref, dk_ref, dc_ref,
                gret_ref, o_ref, state_s):
    @pl.when(pl.program_id(1) == 0)
    def _():
        state_s[...] = jnp.zeros_like(state_s)

    cos = cos_ref[...]
    sin = sin_ref[...]

    def rot(x):
        return x * cos + pltpu.roll(x, DK_R // 2, axis=1) * sin

    for h in range(H_R):
        sl = slice(h * DK_R, (h + 1) * DK_R)
        q = rot(q_ref[:, sl].astype(f32)).astype(bf16)
        kf = rot(k_ref[:, sl].astype(f32)) * (DK_R ** -0.5)
        k = kf.astype(bf16)
        v = v_ref[:, sl]
        inner = lax.dot_general(q, k, (((1,), (1,)), ((), ())), preferred_element_type=f32) * din_ref[h]
        o = jnp.dot(inner.astype(bf16), v, preferred_element_type=f32)
        st = state_s[h]
        o = o + jnp.dot(q, st.astype(bf16), preferred_element_type=f32) * dq_ref[h]
        kd = (kf * dk_ref[h]).astype(bf16)
        state_s[h] = st * dc_ref[h] + jnp.dot(kd.T, v, preferred_element_type=f32)
        mu = jnp.mean(o, axis=-1, keepdims=True)
        var = jnp.mean(jnp.square(o - mu), axis=-1, keepdims=True)
        y = (o - mu) * lax.rsqrt(var + EPS) * gret_ref[:, sl]
        gate = gr_ref[:, sl].astype(f32)
        o_ref[:, sl] = (gate * jax.nn.sigmoid(gate) * y).astype(o_ref.dtype)


def _ret(proj3, cos_t, sin_t, d_in, d_q, d_k, d_c, g_ret):
    b, s, _ = proj3.shape
    c = RET_CHUNK
    w = H_R * DK_R

    def col(off):
        return pl.BlockSpec((None, c, w), lambda bi, ci: (bi, ci, off // w))

    return pl.pallas_call(
        _ret_kernel,
        grid=(b, s // c),
        in_specs=[
            col(C_QR), col(C_KR), col(C_VR), col(C_GR),
            pl.BlockSpec((c, DK_R), lambda bi, ci: (ci, 0)),
            pl.BlockSpec((c, DK_R), lambda bi, ci: (ci, 0)),
            pl.BlockSpec((H_R, c, c), lambda bi, ci: (0, 0, 0)),
            pl.BlockSpec((H_R, c, DK_R), lambda bi, ci: (0, 0, 0)),
            pl.BlockSpec((H_R, c, DK_R), lambda bi, ci: (0, 0, 0)),
            pl.BlockSpec((H_R, 1, DK_R), lambda bi, ci: (0, 0, 0)),
            pl.BlockSpec((1, w), lambda bi, ci: (0, 0)),
        ],
        out_specs=pl.BlockSpec((None, c, w), lambda bi, ci: (bi, ci, 0)),
        out_shape=jax.ShapeDtypeStruct((b, s, w), bf16),
        scratch_shapes=[pltpu.VMEM((H_R, DK_R, DV_R), f32)],
        compiler_params=_cparams(("parallel", "arbitrary")),
        name="ret",
    )(proj3, proj3, proj3, proj3, cos_t, sin_t, d_in, d_q, d_k, d_c, g_ret)


def _retention_tables(s):
    c = RET_CHUNK
    half = DK_R // 2
    freq = ROPE_BASE ** (-jnp.arange(half, dtype=f32) / half)
    ang = jnp.arange(s, dtype=f32)[:, None] * freq[None, :]
    cos = jnp.cos(ang)
    sin = jnp.sin(ang)
    cos_t = jnp.concatenate([cos, cos], axis=-1)
    sin_t = jnp.concatenate([-sin, sin], axis=-1)
    log_gamma = jnp.log1p(-jnp.exp2(-5.0 - jnp.arange(H_R, dtype=f32)))
    n = jnp.arange(c, dtype=f32)
    diff = n[:, None] - n[None, :]
    d_in = jnp.where(diff >= 0, jnp.exp(log_gamma[:, None, None] * jnp.maximum(diff, 0.0)), 0.0)
    d_q = jnp.broadcast_to(jnp.exp(log_gamma[:, None] * (n + 1.0))[:, :, None], (H_R, c, DK_R))
    d_k = jnp.broadcast_to(jnp.exp(log_gamma[:, None] * (c - 1.0 - n))[:, :, None], (H_R, c, DK_R))
    d_c = jnp.broadcast_to(jnp.exp(log_gamma * c)[:, None, None], (H_R, 1, DK_R))
    return cos_t, sin_t, d_in, d_q, d_k, d_c


MIX_CHUNK = 512


def _mix_kernel(oa_ref, ob_ref, wb_ref, ga0_ref, ga1_ref, gb0_ref, gb1_ref, o_ref):
    oa = oa_ref[...]
    ob = ob_ref[...]
    half = ga0_ref.shape[1]
    for c in range(0, o_ref.shape[1], MIX_CHUNK):
        ga_ref, gb_ref, off = (ga0_ref, gb0_ref, c) if c < half else (ga1_ref, gb1_ref, c - half)
        a = jnp.dot(oa, wb_ref[0, :, c:c + MIX_CHUNK], preferred_element_type=f32)
        b = jnp.dot(ob, wb_ref[1, :, c:c + MIX_CHUNK], preferred_element_type=f32)
        ga = jax.nn.sigmoid(ga_ref[:, off:off + MIX_CHUNK].astype(f32))
        gb = jax.nn.sigmoid(gb_ref[:, off:off + MIX_CHUNK].astype(f32))
        o_ref[:, c:c + MIX_CHUNK] = (ga * a + gb * b).astype(o_ref.dtype)


def _mix(o_a, o_b, w_branch_bf, proj, tm):
    t = o_a.shape[0]
    d = w_branch_bf.shape[2]
    half = d // 2
    assert C_GBR % half == 0 and half % MIX_CHUNK == 0

    def gate(k):
        return pl.BlockSpec((tm, half), lambda i: (i, C_GBR // half + k))

    return pl.pallas_call(
        _mix_kernel,
        grid=(t // tm,),
        in_specs=[
            pl.BlockSpec((tm, D_BRANCH), lambda i: (i, 0)),
            pl.BlockSpec((tm, D_BRANCH), lambda i: (i, 0)),
            pl.BlockSpec((N_BRANCH, D_BRANCH, d), lambda i: (0, 0, 0)),
            gate(0), gate(1), gate(2), gate(3),
        ],
        out_specs=pl.BlockSpec((tm, d), lambda i: (i, 0)),
        out_shape=jax.ShapeDtypeStruct((t, d), bf16),
        compiler_params=_cparams(("parallel",)),
        name="mix",
    )(o_a, o_b, w_branch_bf, proj, proj, proj, proj)


def _pack_rows(v):
    n = v.shape[1] // 2
    r = pltpu.bitcast(v.astype(bf16).astype(f32), jnp.uint32)
    w = (r[:, :n] >> 16) | (r[:, n:] & jnp.uint32(0xFFFF0000))
    return pltpu.einshape("r(ab)->rab", w, b=LANES)


def _unpack_rows(p):
    w = pltpu.einshape("rab->r(ab)", p)
    lo = pltpu.bitcast(w << 16, f32)
    hi = pltpu.bitcast(w & jnp.uint32(0xFFFF0000), f32)
    return lo, hi


def _split_bf16(a):
    hi = a.astype(bf16)
    lo = (a - hi.astype(f32)).astype(bf16)
    return hi, lo


def _outproj_kernel(mixed_ref, x_ref, wo_ref, g_ref, wr_ref, br_ref,
                    h_ref, xn_ref, eid_ref, gate_ref, cnt_ref):
    h = x_ref[...] + jnp.dot(mixed_ref[...], wo_ref[...], preferred_element_type=f32)
    h_ref[...] = h
    ms = jnp.mean(h * h, axis=-1, keepdims=True)
    xn = h * lax.rsqrt(ms + EPS) * g_ref[...]
    xn_ref[...] = _pack_rows(xn)

    x_hi, x_lo = _split_bf16(xn)
    hh_hl = jnp.dot(x_hi, wr_ref[...], preferred_element_type=f32)
    logit = (hh_hl[:, :LANES] + hh_hl[:, LANES:]
             + jnp.dot(x_lo, wr_ref[:, :LANES], preferred_element_type=f32)) + br_ref[...]

    lane = lax.broadcasted_iota(jnp.int32, logit.shape, 1)
    lanef = lane.astype(f32)
    neg = -jnp.inf

    def first_argmax(v, m):
        return jnp.min(jnp.where(v == m, lanef, float(LANES)), axis=-1, keepdims=True)

    lg = jnp.where(lane < N_GROUPS, logit, neg)
    mg = jnp.max(lg, axis=-1, keepdims=True)
    p_grp = 1.0 / jnp.sum(jnp.exp(lg - mg), axis=-1, keepdims=True)
    grp = first_argmax(lg, mg).astype(jnp.int32)

    e_lane = lane - N_GROUPS
    in_grp = (e_lane >= 0) & (e_lane < N_EXPERTS) & ((e_lane // EXP_PER_GROUP) == grp)
    le = jnp.where(in_grp, logit, neg)
    m1 = jnp.max(le, axis=-1, keepdims=True)
    i1 = first_argmax(le, m1)
    le2 = jnp.where(lanef == i1, neg, le)
    m2 = jnp.max(le2, axis=-1, keepdims=True)
    i2 = first_argmax(le2, m2)
    e2 = jnp.exp(m2 - m1)
    g1 = p_grp / (1.0 + e2)
    g2 = p_grp * e2 / (1.0 + e2)

    eid = jnp.where(lane == 0, i1, jnp.where(lane == 1, i2, float(N_GROUPS))) - float(N_GROUPS)
    eid_ref[...] = eid.astype(jnp.int32).T[:8, :]
    gate_ref[...] = jnp.where(lane == 0, g1, jnp.where(lane == 1, g2, 0.0))

    @pl.when(pl.program_id(0) == 0)
    def _():
        cnt_ref[...] = jnp.zeros_like(cnt_ref)

    half = pl.program_id(0) // (pl.num_programs(0) // ROUTE_HALVES)
    sub8 = lax.broadcasted_iota(jnp.int32, (8, LANES), 0)
    for s, idx in enumerate((i1, i2)):
        c = jnp.sum((lanef == idx).astype(jnp.int32), axis=0, keepdims=True)
        cnt_ref[...] += jnp.where(sub8 == s * ROUTE_HALVES + half, c, 0)


def _outproj(mixed, x2, w_out_bf, g_ffn, wr_hi_lo, b_r, tm):
    t, d = x2.shape
    row = lambda i: (i, 0)
    fixed = lambda i: (0, 0)
    return pl.pallas_call(
        _outproj_kernel,
        grid=(t // tm,),
        in_specs=[
            pl.BlockSpec((tm, d), row),
            pl.BlockSpec((tm, d), row),
            pl.BlockSpec((d, d), fixed),
            pl.BlockSpec((1, d), fixed),
            pl.BlockSpec((d, 2 * LANES), fixed),
            pl.BlockSpec((1, LANES), fixed),
        ],
        out_specs=[
            pl.BlockSpec((tm, d), row),
            pl.BlockSpec((tm, d // (2 * LANES), LANES), lambda i: (i, 0, 0)),
            pl.BlockSpec((8, tm), lambda i: (0, i)),
            pl.BlockSpec((tm, LANES), row),
            pl.BlockSpec((8, LANES), fixed),
        ],
        out_shape=[
            jax.ShapeDtypeStruct((t, d), f32),
            jax.ShapeDtypeStruct((t, d // (2 * LANES), LANES), jnp.uint32),
            jax.ShapeDtypeStruct((8, t), jnp.int32),
            jax.ShapeDtypeStruct((t, LANES), f32),
            jax.ShapeDtypeStruct((8, LANES), jnp.int32),
        ],
        compiler_params=_cparams(("arbitrary",)),
        name="outproj",
    )(mixed, x2, w_out_bf, g_ffn, wr_hi_lo, b_r)


ISSUE_UNROLL = 8
CAST_ROWS = 256
GATHER_BUFS = 3


def _experts_kernel(bexp_ref, nexp_ref, rpack_ref, nused_ref, xn_hbm, wg_hbm, wu_hbm, wd_hbm, yt_hbm,
                    xbuf, ybuf, wg_st, wu_st, wd_st, wg_bf, wu_bf, wd_bf, gsem, ssem, wsem, *, n_tok):
    j = pl.program_id(0)
    n_used = nused_ref[0]
    slot = j % 2
    tok_bits = (n_tok - 1).bit_length()

    def rows_of(blk, fn):
        base = blk * ROW_BLOCK

        def body(k, c):
            r0 = pl.multiple_of(k * ISSUE_UNROLL, ISSUE_UNROLL)
            for u in range(ISSUE_UNROLL):
                fn(r0 + u, rpack_ref[base + r0 + u], 1)
            return c

        lax.fori_loop(0, ROW_BLOCK // ISSUE_UNROLL, body, 0)

    def start_gathers(blk, sl):
        def one(r, packed, queue):
            tok = packed & ((1 << tok_bits) - 1)
            pltpu.make_async_copy(xn_hbm.at[tok], xbuf.at[sl, r], gsem.at[sl]).start(priority=queue)
        rows_of(blk, one)

    def start_scatters(blk, sl):
        def one(r, packed, queue):
            row = lax.shift_right_logical(packed, tok_bits)
            pltpu.make_async_copy(ybuf.at[sl, r], yt_hbm.at[row], ssem.at[sl]).start(priority=queue)
        rows_of(blk, one)

    def wait_gathers(sl):
        pltpu.make_async_copy(xn_hbm.at[pl.ds(0, ROW_BLOCK)], xbuf.at[sl], gsem.at[sl]).wait()

    def wait_scatters(sl):
        pltpu.make_async_copy(ybuf.at[sl], yt_hbm.at[pl.ds(0, ROW_BLOCK)], ssem.at[sl]).wait()

    staged = ((wg_hbm, wg_st, wg_bf), (wu_hbm, wu_st, wu_bf), (wd_hbm, wd_st, wd_bf))

    def start_weights(e):
        for q, (src, st, _) in enumerate(staged):
            pltpu.make_async_copy(src.at[e], st, wsem.at[q]).start()

    def wait_and_cast_weights():
        for q, (src, st, dst) in enumerate(staged):
            pltpu.make_async_copy(src.at[0], st, wsem.at[q]).wait()

            def cast(c, carry, st=st, dst=dst):
                r = pl.multiple_of(c * CAST_ROWS, CAST_ROWS)
                dst[pl.ds(r, CAST_ROWS), :] = st[pl.ds(r, CAST_ROWS), :].astype(bf16)
                return carry

            lax.fori_loop(0, st.shape[0] // CAST_ROWS, cast, 0)

    @pl.when(j == 0)
    def _():
        start_weights(bexp_ref[0])
        for b in range(GATHER_BUFS - 1):
            @pl.when(b < n_used)
            def _(b=b):
                start_gathers(b, b)
        ybuf[...] = jnp.zeros_like(ybuf)
        for sl in range(2):
            spare = yt_hbm.at[pl.ds(2 * n_tok + sl * ROW_BLOCK, ROW_BLOCK)]
            pltpu.make_async_copy(ybuf.at[sl], spare, ssem.at[sl]).start()
        for sl in range(2):
            wait_scatters(sl)

    @pl.when(j < n_used)
    def _():
        e = bexp_ref[j]

        @pl.when((j == 0) | (bexp_ref[jnp.maximum(j - 1, 0)] != e))
        def _():
            wait_and_cast_weights()

            @pl.when(nexp_ref[j] >= 0)
            def _():
                start_weights(nexp_ref[j])

        gslot = j % GATHER_BUFS
        wait_gathers(gslot)

        @pl.when(j + GATHER_BUFS - 1 < n_used)
        def _():
            start_gathers(j + GATHER_BUFS - 1, (j + GATHER_BUFS - 1) % GATHER_BUFS)

        @pl.when(j >= 2)
        def _():
            wait_scatters(slot)

        lo, hi = _unpack_rows(xbuf[gslot])
        xb = jnp.concatenate([lo, hi], axis=1).astype(bf16)
        g = jnp.dot(xb, wg_bf[...], preferred_element_type=f32)
        u = jnp.dot(xb, wu_bf[...], preferred_element_type=f32)
        hm = (g * jax.nn.sigmoid(g) * u).astype(bf16)
        ybuf[slot] = _pack_rows(jnp.dot(hm, wd_bf[...], preferred_element_type=f32))
        start_scatters(j, slot)

        @pl.when(j == n_used - 1)
        def _():
            wait_scatters(slot)

            @pl.when(j >= 1)
            def _():
                wait_scatters(1 - slot)


def _experts(block_expert, next_expert, row_pack, n_used, xn_packed, w_gate, w_up, w_down):
    n_rows = row_pack.shape[0]
    n_tok = xn_packed.shape[0]
    tile = xn_packed.shape[1:]
    _, d, f = w_gate.shape
    assert d % CAST_ROWS == 0 and f % CAST_ROWS == 0
    any_space = pl.BlockSpec(memory_space=pl.ANY)
    grid_spec = pltpu.PrefetchScalarGridSpec(
        num_scalar_prefetch=4,
        grid=(n_rows // ROW_BLOCK,),
        in_specs=[any_space, any_space, any_space, any_space],
        out_specs=any_space,
        scratch_shapes=[pltpu.VMEM((GATHER_BUFS, ROW_BLOCK) + tile, jnp.uint32),
                        pltpu.VMEM((2, ROW_BLOCK) + tile, jnp.uint32),
                        pltpu.VMEM((d, f), f32), pltpu.VMEM((d, f), f32), pltpu.VMEM((f, d), f32),
                        pltpu.VMEM((d, f), bf16), pltpu.VMEM((d, f), bf16), pltpu.VMEM((f, d), bf16),
                        pltpu.SemaphoreType.DMA((GATHER_BUFS,)), pltpu.SemaphoreType.DMA((2,)),
                        pltpu.SemaphoreType.DMA((3,))],
    )
    return pl.pallas_call(
        functools.partial(_experts_kernel, n_tok=n_tok),
        grid_spec=grid_spec,
        out_shape=jax.ShapeDtypeStruct((2 * n_tok + 2 * ROW_BLOCK,) + tile, jnp.uint32),
        compiler_params=_cparams(("arbitrary",)),
        name="experts",
    )(block_expert, next_expert, row_pack, n_used, xn_packed, w_gate, w_up, w_down)


def _combine_kernel(h_ref, y0_ref, y1_ref, gate_ref, g_ref, o_ref):
    gate = gate_ref[...]
    y0 = jnp.concatenate(_unpack_rows(y0_ref[...]), axis=1)
    y1 = jnp.concatenate(_unpack_rows(y1_ref[...]), axis=1)
    hh = h_ref[...] + gate[:, 0:1] * y0 + gate[:, 1:2] * y1
    ms = jnp.mean(hh * hh, axis=-1, keepdims=True)
    o_ref[...] = hh * lax.rsqrt(ms + EPS) * g_ref[...]


def _combine(h, yt, gate, g_final, tm):
    t, d = h.shape
    nt = t // tm
    return pl.pallas_call(
        _combine_kernel,
        grid=(nt,),
        in_specs=[
            pl.BlockSpec((tm, d), lambda i: (i, 0)),
            pl.BlockSpec((tm,) + yt.shape[1:], lambda i: (i, 0, 0)),
            pl.BlockSpec((tm,) + yt.shape[1:], lambda i: (nt + i, 0, 0)),
            pl.BlockSpec((tm, LANES), lambda i: (i, 0)),
            pl.BlockSpec((1, d), lambda i: (0, 0)),
        ],
        out_specs=pl.BlockSpec((tm, d), lambda i: (i, 0)),
        out_shape=jax.ShapeDtypeStruct((t, d), f32),
        compiler_params=_cparams(("parallel",)),
        name="combine",
    )(h, yt, yt, gate, g_final)


def _route_kernel(eid_ref, cnt_ref, rpack_ref, bexp_ref, nexp_ref, nused_ref, *cur_refs, n_tok, n_blocks):
    tok_bits = (n_tok - 1).bit_length()
    chunk = 2 * n_tok // ROUTE_CHAINS

    def no_next(k, carry):
        nexp_ref[k] = -1
        return carry

    lax.fori_loop(0, n_blocks, no_next, 0)

    def per_expert(e, carry):
        blk, prev_blk, prev_nb = carry
        start = blk * ROW_BLOCK
        run = start
        for c in range(ROUTE_CHAINS):
            cur_refs[c][e] = run
            run = run + cnt_ref[c * N_EXPERTS + e]
        nb = (run - start + ROW_BLOCK - 1) // ROW_BLOCK

        def set_block(k, c):
            bexp_ref[blk + k] = e
            return c

        lax.fori_loop(0, nb, set_block, 0)

        def set_next(k, c):
            nexp_ref[prev_blk + k] = e
            return c

        lax.fori_loop(0, jnp.where(nb > 0, prev_nb, 0), set_next, 0)

        def set_pad(r, c):
            rpack_ref[r] = (2 * n_tok + (r & (2 * ROW_BLOCK - 1))) << tok_bits
            return c

        lax.fori_loop(run, start + nb * ROW_BLOCK, set_pad, 0)
        return blk + nb, jnp.where(nb > 0, blk, prev_blk), jnp.where(nb > 0, nb, prev_nb)

    n_used, _, _ = lax.fori_loop(0, N_EXPERTS, per_expert, (0, 0, 0))
    nused_ref[0] = n_used

    def tail_block(k, carry):
        bexp_ref[k] = N_EXPERTS - 1
        return carry

    lax.fori_loop(n_used, n_blocks, tail_block, 0)

    def tail_row(r, carry):
        rpack_ref[r] = (2 * n_tok + (r & (2 * ROW_BLOCK - 1))) << tok_bits
        return carry

    lax.fori_loop(n_used * ROW_BLOCK, n_blocks * ROW_BLOCK, tail_row, 0)

    def place(i, carry):
        for c in range(ROUTE_CHAINS):
            a = c * chunk + i
            e = eid_ref[a]
            p = cur_refs[c][e]
            cur_refs[c][e] = p + 1
            rpack_ref[p] = (a << tok_bits) | (a - (c * chunk // n_tok) * n_tok)
        return carry

    lax.fori_loop(0, chunk, place, 0)


def _route(eid_flat, counts, n_tok):
    n_asg = eid_flat.shape[0]
    n_rows = -(-(n_asg + N_EXPERTS * (ROW_BLOCK - 1)) // ROW_BLOCK) * ROW_BLOCK
    n_blocks = n_rows // ROW_BLOCK
    smem = pl.BlockSpec(memory_space=pltpu.SMEM)
    return pl.pallas_call(
        functools.partial(_route_kernel, n_tok=n_tok, n_blocks=n_blocks),
        in_specs=[smem, smem],
        out_specs=[smem, smem, smem, smem],
        out_shape=[jax.ShapeDtypeStruct((n_rows,), jnp.int32),
                   jax.ShapeDtypeStruct((n_blocks,), jnp.int32),
                   jax.ShapeDtypeStruct((n_blocks,), jnp.int32),
                   jax.ShapeDtypeStruct((1,), jnp.int32)],
        scratch_shapes=[pltpu.SMEM((N_EXPERTS,), jnp.int32)] * ROUTE_CHAINS,
        name="route",
    )(eid_flat, counts)


def _pick(n, prefs):
    for p in prefs:
        if n % p == 0:
            return p
    return n


def kernel(x, g_mix_norm, w_in, g_kv, w_uv, g_ret, w_branch, w_out, g_ffn_norm, w_router_group,
           b_router_group, w_router_expert, b_router_expert, w_expert_gate, w_expert_up,
           w_expert_down, g_final):
    b, s, d = x.shape
    t = b * s
    depth = w_in.shape[0]
    n_sel = min(TOPK_MAX, s // 4)
    assert s % RET_CHUNK == 0 and s % Q_TILE == 0

    cos_t, sin_t, d_in, d_q, d_k, d_c = _retention_tables(s)
    h2 = x.reshape(t, d)
    for l in range(depth):
        wl = w_in[l]
        sp = np.cumsum([0, H_A * D_LATENT, D_LATENT, H_IDX * D_IDX, D_IDX, H_IDX,
                        H_R * DK_R, H_R * DK_R, H_R * DV_R, H_R * DV_R, N_BRANCH * d])
        seg = [wl[:, sp[k]:sp[k + 1]].astype(bf16) for k in range(10)]
        kw_pad = jnp.zeros((d, LANES - D_IDX - H_IDX), bf16)
        w_p = jnp.concatenate([seg[0], seg[9], seg[5], seg[6], seg[7], seg[8], seg[2], seg[1],
                               seg[3], seg[4], kw_pad], axis=1)
        assert w_p.shape[1] == D_IN_P

        proj = _proj(h2, g_mix_norm[l].reshape(1, d), w_p, _pick(t, (512, 256)), 3328)
        proj3 = proj.reshape(b, s, D_IN_P)

        o_a = _attn(proj3, g_kv[l].reshape(1, D_LATENT), w_uv[l].astype(bf16), n_sel)
        o_b = _ret(proj3, cos_t, sin_t, d_in, d_q, d_k, d_c, g_ret[l].reshape(1, H_R * DV_R))

        mixed = _mix(o_a.reshape(t, D_BRANCH), o_b.reshape(t, D_BRANCH), w_branch[l].astype(bf16),
                     proj, _pick(t, (512, 256)))

        w_r = jnp.concatenate([w_router_group[l], w_router_expert[l],
                               jnp.zeros((d, LANES - N_GROUPS - N_EXPERTS), f32)], axis=1)
        b_r = jnp.concatenate([b_router_group[l], b_router_expert[l],
                               jnp.zeros((LANES - N_GROUPS - N_EXPERTS,), f32)]).reshape(1, LANES)
        wr_hi = w_r.astype(bf16)
        wr_lo = (w_r - wr_hi.astype(f32)).astype(bf16)
        h2, xn, eid_t, gate, cnt = _outproj(mixed, h2, w_out[l].astype(bf16), g_ffn_norm[l].reshape(1, d),
                                            jnp.concatenate([wr_hi, wr_lo], axis=1), b_r, _pick(t, (512, 256)))

        row_pack, block_expert, next_expert, n_used = _route(
            eid_t[:2].reshape(-1), cnt[:ROUTE_CHAINS, N_GROUPS:N_GROUPS + N_EXPERTS].reshape(-1), t)
        yt = _experts(block_expert, next_expert, row_pack, n_used, xn,
                      w_expert_gate[l], w_expert_up[l], w_expert_down[l])
        assert depth == 1
        h2 = _combine(h2, yt, gate, g_final.reshape(1, d), _pick(t, (256,)))
    return h2.reshape(b, s, d)
```

```python
import functools

import jax
import jax.numpy as jnp
import numpy as np
from jax import lax
from jax.experimental import pallas as pl
from jax.experimental.pallas import tpu as pltpu

EPS = 1e-6
CHUNK = 64
H_A = 8
D_LATENT = 128
DH_A = 128
H_IDX = 8
D_IDX = 64
TOPK_MAX = 256
H_R = 8
DK_R = 128
DV_R = 128
ROPE_BASE = 10000.0
D_BRANCH = 1024
N_BRANCH = 2
N_GROUPS = 4
EXP_PER_GROUP = 8
N_EXPERTS = N_GROUPS * EXP_PER_GROUP
D_EXPERT = 1024

LANES = 128
KEY_TILE = 256
Q_TILE = 512
RET_CHUNK = 256
ROW_BLOCK = 256
ROUTE_HALVES = 4
ROUTE_CHAINS = 2 * ROUTE_HALVES
VMEM_LIMIT = 56 * 1024 * 1024

C_QLAT = 0
C_GBR = 1024
C_QR = 5120
C_KR = 6144
C_VR = 7168
C_GR = 8192
C_QIDX = 9216
C_CKV = 9728
C_KW = 9856
D_IN_P = 9984

INT_MIN = np.int32(-2 ** 31)
NEG_BIG = -1e30

bf16 = jnp.bfloat16
f32 = jnp.float32


def _cparams(sem):
    return pltpu.CompilerParams(dimension_semantics=sem, vmem_limit_bytes=VMEM_LIMIT)


def _proj_kernel(x_ref, g_ref, w_ref, o_ref, xn_ref):
    @pl.when(pl.program_id(1) == 0)
    def _():
        x = x_ref[...]
        ms = jnp.mean(x * x, axis=-1, keepdims=True)
        xn_ref[...] = (x * lax.rsqrt(ms + EPS) * g_ref[...]).astype(bf16)

    o_ref[...] = jnp.dot(xn_ref[...], w_ref[...], preferred_element_type=f32).astype(o_ref.dtype)


def _proj(x2, g, w_p, tm, tn):
    t, d = x2.shape
    n = w_p.shape[1]
    return pl.pallas_call(
        _proj_kernel,
        grid=(t // tm, n // tn),
        in_specs=[
            pl.BlockSpec((tm, d), lambda i, j: (i, 0)),
            pl.BlockSpec((1, d), lambda i, j: (0, 0)),
            pl.BlockSpec((d, tn), lambda i, j: (0, j)),
        ],
        out_specs=pl.BlockSpec((tm, tn), lambda i, j: (i, j)),
        out_shape=jax.ShapeDtypeStruct((t, n), bf16),
        scratch_shapes=[pltpu.VMEM((tm, d), bf16)],
        compiler_params=_cparams(("parallel", "arbitrary")),
        name="proj",
    )(x2, g, w_p)


def _float_key(s):
    bits = pltpu.bitcast(s, jnp.int32)
    key = bits ^ ((bits >> 31) & jnp.int32(0x7FFFFFFF))
    return jnp.where(s == 0.0, jnp.int32(0), key)


def _attn_kernel(qlat_ref, qidx_ref, kwq_ref, ckv_ref, kwk_ref, gkv_ref, wuv_ref, o_ref,
                 kv_s, kvT_s, kidx_s, key_s, bias_s, qT_s, qiT_s, acc_s, *, n_sel, n_kt):
    i = pl.program_id(1)
    idx_scale = (H_IDX ** -0.5) * (D_IDX ** -0.5)
    attn_scale = D_LATENT ** -0.5
    hq = H_A * Q_TILE

    @pl.when(i == 0)
    def _():
        g = gkv_ref[...]
        for t in range(n_kt):
            c = ckv_ref[t * KEY_TILE:(t + 1) * KEY_TILE, :].astype(f32)
            ms = jnp.mean(c * c, axis=-1, keepdims=True)
            kv = c * lax.rsqrt(ms + EPS) * g
            kv_s[t] = kv.astype(bf16)
            kvT_s[t] = kv.T.astype(bf16)
            kidx_s[t] = kwk_ref[t * KEY_TILE:(t + 1) * KEY_TILE, :D_IDX]

    nk = ((i + 1) * Q_TILE + KEY_TILE - 1) // KEY_TILE
    lane = lax.broadcasted_iota(jnp.int32, (1, Q_TILE), 1)
    sub = lax.broadcasted_iota(jnp.int32, (KEY_TILE, 1), 0)
    q_chunk = (i * Q_TILE + lane) // CHUNK

    wT = kwq_ref[...].astype(f32).T
    for h in range(H_A):
        qT_s[:, h * Q_TILE:(h + 1) * Q_TILE] = qlat_ref[:, h * D_LATENT:(h + 1) * D_LATENT].astype(f32).T.astype(bf16)
    for h in range(H_IDX):
        qiT_s[:, h * Q_TILE:(h + 1) * Q_TILE] = qidx_ref[:, h * D_IDX:(h + 1) * D_IDX].astype(f32).T.astype(bf16)

    def score_tile(t, carry):
        d_all = jnp.dot(kidx_s[t], qiT_s[...], preferred_element_type=f32)
        acc = jnp.zeros((KEY_TILE, Q_TILE), f32)
        for h in range(H_IDX):
            d = d_all[:, h * Q_TILE:(h + 1) * Q_TILE]
            acc = acc + wT[D_IDX + h:D_IDX + h + 1, :] * jnp.maximum(d, 0.0)
        score = acc * idx_scale
        k_chunk = (t * KEY_TILE + sub) // CHUNK
        key_s[t] = jnp.where(k_chunk <= q_chunk, _float_key(score), INT_MIN)
        return carry

    lax.fori_loop(0, nk, score_tile, 0)

    @pl.when(nk % 2 == 1)
    def _():
        key_s[nk] = jnp.full((KEY_TILE, Q_TILE), INT_MIN, jnp.int32)

    n_pairs = (nk + 1) // 2

    def count(pred):
        def body(p, c):
            for t in (2 * p, 2 * p + 1):
                m = pred(key_s[t], t).astype(jnp.int32)
                c = c + jnp.sum(m.reshape(KEY_TILE // 8, 8, Q_TILE), axis=0)
            return c
        c8 = lax.fori_loop(0, n_pairs, body, jnp.zeros((8, Q_TILE), jnp.int32))
        return jnp.sum(c8, axis=0, keepdims=True)

    thr0 = jnp.where(count(lambda k, t: k >= 0) >= n_sel, jnp.int32(0), INT_MIN)
    thr0 = jnp.broadcast_to(thr0, (1, Q_TILE)).astype(jnp.int32)

    def bit_step(j, thr):
        cand = thr | (jnp.int32(1) << (jnp.int32(30) - j))
        return jnp.where(count(lambda k, t: k >= cand) >= n_sel, cand, thr)

    thr = lax.fori_loop(0, 31, bit_step, thr0)

    c_gt = count(lambda k, t: k > thr)
    c_ge = count(lambda k, t: k >= thr)
    need = n_sel - c_gt
    has_tie = jnp.max(jnp.where((c_ge > n_sel) & (thr > INT_MIN), 1, 0)) > 0

    def tie_limit():
        def step(j, m):
            cand = m | (jnp.int32(1) << (jnp.int32(14) - j))
            c = count(lambda k, t: (k == thr) & ((t * KEY_TILE + sub) < cand))
            return jnp.where(c < need, cand, m)
        return lax.fori_loop(0, 15, step, jnp.zeros((1, Q_TILE), jnp.int32))

    m_lim = lax.cond(has_tie, tie_limit, lambda: jnp.full((1, Q_TILE), 2 ** 30, jnp.int32))

    def bias_pair(p, carry):
        for t in (2 * p, 2 * p + 1):
            k = key_s[t]
            sel = (k > thr) | ((k == thr) & ((t * KEY_TILE + sub) <= m_lim))
            sel = sel & (k > INT_MIN)
            bias_s[t] = jnp.where(sel, 0.0, NEG_BIG).astype(f32)
        return carry

    lax.fori_loop(0, n_pairs, bias_pair, 0)

    acc_s[...] = jnp.zeros_like(acc_s)

    def att_pair(p, carry):
        m_run, l_run = carry
        kv2 = jnp.concatenate([kv_s[2 * p], kv_s[2 * p + 1]], axis=0)
        kvT2 = jnp.concatenate([kvT_s[2 * p], kvT_s[2 * p + 1]], axis=1)
        bias2 = jnp.concatenate([bias_s[2 * p], bias_s[2 * p + 1]], axis=0)
        logit = jnp.dot(kv2, qT_s[...], preferred_element_type=f32) * attn_scale
        logit = logit + jnp.concatenate([bias2] * H_A, axis=1)
        m_new = jnp.maximum(m_run, jnp.max(logit, axis=0, keepdims=True))
        alpha = jnp.exp(m_run - m_new)
        pr = jnp.exp(logit - m_new)
        l_new = alpha * l_run + jnp.sum(pr, axis=0, keepdims=True)
        acc_s[...] = alpha * acc_s[...] + jnp.dot(kvT2, pr.astype(bf16), preferred_element_type=f32)
        return m_new, l_new

    init = (jnp.full((1, hq), NEG_BIG, f32), jnp.zeros((1, hq), f32))
    _, l_fin = lax.fori_loop(0, n_pairs, att_pair, init)
    inv_l = 1.0 / l_fin
    for h in range(H_A):
        sl = slice(h * Q_TILE, (h + 1) * Q_TILE)
        o_lat = (acc_s[:, sl] * inv_l[:, sl]).T
        o_ref[:, h * DH_A:(h + 1) * DH_A] = jnp.dot(
            o_lat.astype(bf16), wuv_ref[h], preferred_element_type=f32).astype(o_ref.dtype)


def _attn(proj3, g_kv, w_uv_bf, n_sel):
    b, s, _ = proj3.shape
    n_kt = s // KEY_TILE
    kern = functools.partial(_attn_kernel, n_sel=n_sel, n_kt=n_kt)
    return pl.pallas_call(
        kern,
        grid=(b, s // Q_TILE),
        in_specs=[
            pl.BlockSpec((None, Q_TILE, H_A * D_LATENT), lambda bi, i: (bi, i, C_QLAT // 1024)),
            pl.BlockSpec((None, Q_TILE, H_IDX * D_IDX), lambda bi, i: (bi, i, C_QIDX // 512)),
            pl.BlockSpec((None, Q_TILE, LANES), lambda bi, i: (bi, i, C_KW // LANES)),
            pl.BlockSpec((None, s, LANES), lambda bi, i: (bi, 0, C_CKV // LANES)),
            pl.BlockSpec((None, s, LANES), lambda bi, i: (bi, 0, C_KW // LANES)),
            pl.BlockSpec((1, D_LATENT), lambda bi, i: (0, 0)),
            pl.BlockSpec((H_A, D_LATENT, DH_A), lambda bi, i: (0, 0, 0)),
        ],
        out_specs=pl.BlockSpec((None, Q_TILE, D_BRANCH), lambda bi, i: (bi, i, 0)),
        out_shape=jax.ShapeDtypeStruct((b, s, D_BRANCH), bf16),
        scratch_shapes=[
            pltpu.VMEM((n_kt, KEY_TILE, D_LATENT), bf16),
            pltpu.VMEM((n_kt, D_LATENT, KEY_TILE), bf16),
            pltpu.VMEM((n_kt, KEY_TILE, D_IDX), bf16),
            pltpu.VMEM((n_kt, KEY_TILE, Q_TILE), jnp.int32),
            pltpu.VMEM((n_kt, KEY_TILE, Q_TILE), f32),
            pltpu.VMEM((D_LATENT, H_A * Q_TILE), bf16),
            pltpu.VMEM((D_IDX, H_IDX * Q_TILE), bf16),
            pltpu.VMEM((D_LATENT, H_A * Q_TILE), f32),
        ],
        compiler_params=_cparams(("parallel", "arbitrary")),
        name="attn",
    )(proj3, proj3, proj3, proj3, proj3, g_kv, w_uv_bf)


def _ret_kernel(q_ref, k_ref, v_ref, gr_ref, cos_ref, sin_ref, din_ref, dq_ref, dk_ref, dc_ref,
                gret_ref, o_ref, state_s):
    @pl.when(pl.program_id(1) == 0)
    def _():
        state_s[...] = jnp.zeros_like(state_s)

    cos = cos_ref[...]
    sin = sin_ref[...]

    def rot(x):
        return x * cos + pltpu.roll(x, DK_R // 2, axis=1) * sin

    for h in range(H_R):
        sl = slice(h * DK_R, (h + 1) * DK_R)
        q = rot(q_ref[:, sl].astype(f32)).astype(bf16)
        kf = rot(k_ref[:, sl].astype(f32)) * (DK_R ** -0.5)
        k = kf.astype(bf16)
        v = v_ref[:, sl]
        inner = lax.dot_general(q, k, (((1,), (1,)), ((), ())), preferred_element_type=f32) * din_ref[h]
        o = jnp.dot(inner.astype(bf16), v, preferred_element_type=f32)
        st = state_s[h]
        o = o + jnp.dot(q, st.astype(bf16), preferred_element_type=f32) * dq_ref[h]
        kd = (kf * dk_ref[h]).astype(bf16)
        state_s[h] = st * dc_ref[h] + jnp.dot(kd.T, v, preferred_element_type=f32)
        mu = jnp.mean(o, axis=-1, keepdims=True)
        var = jnp.mean(jnp.square(o - mu), axis=-1, keepdims=True)
        y = (o - mu) * lax.rsqrt(var + EPS) * gret_ref[:, sl]
        gate = gr_ref[:, sl].astype(f32)
        o_ref[:, sl] = (gate * jax.nn.sigmoid(gate) * y).astype(o_ref.dtype)


def _ret(proj3, cos_t, sin_t, d_in, d_q, d_k, d_c, g_ret):
    b, s, _ = proj3.shape
    c = RET_CHUNK
    w = H_R * DK_R

    def col(off):
        return pl.BlockSpec((None, c, w), lambda bi, ci: (bi, ci, off // w))

    return pl.pallas_call(
        _ret_kernel,
        grid=(b, s // c),
        in_specs=[
            col(C_QR), col(C_KR), col(C_VR), col(C_GR),
            pl.BlockSpec((c, DK_R), lambda bi, ci: (ci, 0)),
            pl.BlockSpec((c, DK_R), lambda bi, ci: (ci, 0)),
            pl.BlockSpec((H_R, c, c), lambda bi, ci: (0, 0, 0)),
            pl.BlockSpec((H_R, c, DK_R), lambda bi, ci: (0, 0, 0)),
            pl.BlockSpec((H_R, c, DK_R), lambda bi, ci: (0, 0, 0)),
            pl.BlockSpec((H_R, 1, DK_R), lambda bi, ci: (0, 0, 0)),
            pl.BlockSpec((1, w), lambda bi, ci: (0, 0)),
        ],
        out_specs=pl.BlockSpec((None, c, w), lambda bi, ci: (bi, ci, 0)),
        out_shape=jax.ShapeDtypeStruct((b, s, w), bf16),
        scratch_shapes=[pltpu.VMEM((H_R, DK_R, DV_R), f32)],
        compiler_params=_cparams(("parallel", "arbitrary")),
        name="ret",
    )(proj3, proj3, proj3, proj3, cos_t, sin_t, d_in, d_q, d_k, d_c, g_ret)


def _retention_tables(s):
    c = RET_CHUNK
    half = DK_R // 2
    freq = ROPE_BASE ** (-jnp.arange(half, dtype=f32) / half)
    ang = jnp.arange(s, dtype=f32)[:, None] * freq[None, :]
    cos = jnp.cos(ang)
    sin = jnp.sin(ang)
    cos_t = jnp.concatenate([cos, cos], axis=-1)
    sin_t = jnp.concatenate([-sin, sin], axis=-1)
    log_gamma = jnp.log1p(-jnp.exp2(-5.0 - jnp.arange(H_R, dtype=f32)))
    n = jnp.arange(c, dtype=f32)
    diff = n[:, None] - n[None, :]
    d_in = jnp.where(diff >= 0, jnp.exp(log_gamma[:, None, None] * jnp.maximum(diff, 0.0)), 0.0)
    d_q = jnp.broadcast_to(jnp.exp(log_gamma[:, None] * (n + 1.0))[:, :, None], (H_R, c, DK_R))
    d_k = jnp.broadcast_to(jnp.exp(log_gamma[:, None] * (c - 1.0 - n))[:, :, None], (H_R, c, DK_R))
    d_c = jnp.broadcast_to(jnp.exp(log_gamma * c)[:, None, None], (H_R, 1, DK_R))
    return cos_t, sin_t, d_in, d_q, d_k, d_c


MIX_CHUNK = 512


def _mix_kernel(oa_ref, ob_ref, wb_ref, ga0_ref, ga1_ref, gb0_ref, gb1_ref, o_ref):
    oa = oa_ref[...]
    ob = ob_ref[...]
    half = ga0_ref.shape[1]
    for c in range(0, o_ref.shape[1], MIX_CHUNK):
        ga_ref, gb_ref, off = (ga0_ref, gb0_ref, c) if c < half else (ga1_ref, gb1_ref, c - half)
        a = jnp.dot(oa, wb_ref[0, :, c:c + MIX_CHUNK], preferred_element_type=f32)
        b = jnp.dot(ob, wb_ref[1, :, c:c + MIX_CHUNK], preferred_element_type=f32)
        ga = jax.nn.sigmoid(ga_ref[:, off:off + MIX_CHUNK].astype(f32))
        gb = jax.nn.sigmoid(gb_ref[:, off:off + MIX_CHUNK].astype(f32))
        o_ref[:, c:c + MIX_CHUNK] = (ga * a + gb * b).astype(o_ref.dtype)


def _mix(o_a, o_b, w_branch_bf, proj, tm):
    t = o_a.shape[0]
    d = w_branch_bf.shape[2]
    half = d // 2
    assert C_GBR % half == 0 and half % MIX_CHUNK == 0

    def gate(k):
        return pl.BlockSpec((tm, half), lambda i: (i, C_GBR // half + k))

    return pl.pallas_call(
        _mix_kernel,
        grid=(t // tm,),
        in_specs=[
            pl.BlockSpec((tm, D_BRANCH), lambda i: (i, 0)),
            pl.BlockSpec((tm, D_BRANCH), lambda i: (i, 0)),
            pl.BlockSpec((N_BRANCH, D_BRANCH, d), lambda i: (0, 0, 0)),
            gate(0), gate(1), gate(2), gate(3),
        ],
        out_specs=pl.BlockSpec((tm, d), lambda i: (i, 0)),
        out_shape=jax.ShapeDtypeStruct((t, d), bf16),
        compiler_params=_cparams(("parallel",)),
        name="mix",
    )(o_a, o_b, w_branch_bf, proj, proj, proj, proj)


def _pack_rows(v):
    n = v.shape[1] // 2
    r = pltpu.bitcast(v.astype(bf16).astype(f32), jnp.uint32)
    w = (r[:, :n] >> 16) | (r[:, n:] & jnp.uint32(0xFFFF0000))
    return pltpu.einshape("r(ab)->rab", w, b=LANES)


def _unpack_rows(p):
    w = pltpu.einshape("rab->r(ab)", p)
    lo = pltpu.bitcast(w << 16, f32)
    hi = pltpu.bitcast(w & jnp.uint32(0xFFFF0000), f32)
    return lo, hi


def _split_bf16(a):
    hi = a.astype(bf16)
    lo = (a - hi.astype(f32)).astype(bf16)
    return hi, lo


def _outproj_kernel(mixed_ref, x_ref, wo_ref, g_ref, wr_ref, br_ref,
                    h_ref, xn_ref, eid_ref, gate_ref, cnt_ref):
    h = x_ref[...] + jnp.dot(mixed_ref[...], wo_ref[...], preferred_element_type=f32)
    h_ref[...] = h
    ms = jnp.mean(h * h, axis=-1, keepdims=True)
    xn = h * lax.rsqrt(ms + EPS) * g_ref[...]
    xn_ref[...] = _pack_rows(xn)

    x_hi, x_lo = _split_bf16(xn)
    hh_hl = jnp.dot(x_hi, wr_ref[...], preferred_element_type=f32)
    logit = (hh_hl[:, :LANES] + hh_hl[:, LANES:]
             + jnp.dot(x_lo, wr_ref[:, :LANES], preferred_element_type=f32)) + br_ref[...]

    lane = lax.broadcasted_iota(jnp.int32, logit.shape, 1)
    lanef = lane.astype(f32)
    neg = -jnp.inf

    def first_argmax(v, m):
        return jnp.min(jnp.where(v == m, lanef, float(LANES)), axis=-1, keepdims=True)

    lg = jnp.where(lane < N_GROUPS, logit, neg)
    mg = jnp.max(lg, axis=-1, keepdims=True)
    p_grp = 1.0 / jnp.sum(jnp.exp(lg - mg), axis=-1, keepdims=True)
    grp = first_argmax(lg, mg).astype(jnp.int32)

    e_lane = lane - N_GROUPS
    in_grp = (e_lane >= 0) & (e_lane < N_EXPERTS) & ((e_lane // EXP_PER_GROUP) == grp)
    le = jnp.where(in_grp, logit, neg)
    m1 = jnp.max(le, axis=-1, keepdims=True)
    i1 = first_argmax(le, m1)
    le2 = jnp.where(lanef == i1, neg, le)
    m2 = jnp.max(le2, axis=-1, keepdims=True)
    i2 = first_argmax(le2, m2)
    e2 = jnp.exp(m2 - m1)
    g1 = p_grp / (1.0 + e2)
    g2 = p_grp * e2 / (1.0 + e2)

    eid = jnp.where(lane == 0, i1, jnp.where(lane == 1, i2, float(N_GROUPS))) - float(N_GROUPS)
    eid_ref[...] = eid.astype(jnp.int32).T[:8, :]
    gate_ref[...] = jnp.where(lane == 0, g1, jnp.where(lane == 1, g2, 0.0))

    @pl.when(pl.program_id(0) == 0)
    def _():
        cnt_ref[...] = jnp.zeros_like(cnt_ref)

    half = pl.program_id(0) // (pl.num_programs(0) // ROUTE_HALVES)
    sub8 = lax.broadcasted_iota(jnp.int32, (8, LANES), 0)
    for s, idx in enumerate((i1, i2)):
        c = jnp.sum((lanef == idx).astype(jnp.int32), axis=0, keepdims=True)
        cnt_ref[...] += jnp.where(sub8 == s * ROUTE_HALVES + half, c, 0)


def _outproj(mixed, x2, w_out_bf, g_ffn, wr_hi_lo, b_r, tm):
    t, d = x2.shape
    row = lambda i: (i, 0)
    fixed = lambda i: (0, 0)
    return pl.pallas_call(
        _outproj_kernel,
        grid=(t // tm,),
        in_specs=[
            pl.BlockSpec((tm, d), row),
            pl.BlockSpec((tm, d), row),
            pl.BlockSpec((d, d), fixed),
            pl.BlockSpec((1, d), fixed),
            pl.BlockSpec((d, 2 * LANES), fixed),
            pl.BlockSpec((1, LANES), fixed),
        ],
        out_specs=[
            pl.BlockSpec((tm, d), row),
            pl.BlockSpec((tm, d // (2 * LANES), LANES), lambda i: (i, 0, 0)),
            pl.BlockSpec((8, tm), lambda i: (0, i)),
            pl.BlockSpec((tm, LANES), row),
            pl.BlockSpec((8, LANES), fixed),
        ],
        out_shape=[
            jax.ShapeDtypeStruct((t, d), f32),
            jax.ShapeDtypeStruct((t, d // (2 * LANES), LANES), jnp.uint32),
            jax.ShapeDtypeStruct((8, t), jnp.int32),
            jax.ShapeDtypeStruct((t, LANES), f32),
            jax.ShapeDtypeStruct((8, LANES), jnp.int32),
        ],
        compiler_params=_cparams(("arbitrary",)),
        name="outproj",
    )(mixed, x2, w_out_bf, g_ffn, wr_hi_lo, b_r)


ISSUE_UNROLL = 8
CAST_ROWS = 256
GATHER_BUFS = 4


def _experts_kernel(bexp_ref, nexp_ref, rpack_ref, nused_ref, xn_hbm, wg_hbm, wu_hbm, wd_hbm, yt_hbm,
                    xbuf, ybuf, wg_st, wu_st, wd_st, wg_bf, wu_bf, wd_bf, gsem, ssem, wsem, *, n_tok):
    j = pl.program_id(0)
    n_used = nused_ref[0]
    slot = j % 2
    tok_bits = (n_tok - 1).bit_length()

    def rows_of(blk, fn):
        base = blk * ROW_BLOCK

        def body(k, c):
            r0 = pl.multiple_of(k * ISSUE_UNROLL, ISSUE_UNROLL)
            for u in range(ISSUE_UNROLL):
                fn(r0 + u, rpack_ref[base + r0 + u], 1)
            return c

        lax.fori_loop(0, ROW_BLOCK // ISSUE_UNROLL, body, 0)

    def start_gathers(blk, sl):
        def one(r, packed, queue):
            tok = packed & ((1 << tok_bits) - 1)
            pltpu.make_async_copy(xn_hbm.at[tok], xbuf.at[sl, r], gsem.at[sl]).start(priority=queue)
        rows_of(blk, one)

    def start_scatters(blk, sl):
        def one(r, packed, queue):
            row = lax.shift_right_logical(packed, tok_bits)
            pltpu.make_async_copy(ybuf.at[sl, r], yt_hbm.at[row], ssem.at[sl]).start(priority=queue)
        rows_of(blk, one)

    def wait_gathers(sl):
        pltpu.make_async_copy(xn_hbm.at[pl.ds(0, ROW_BLOCK)], xbuf.at[sl], gsem.at[sl]).wait()

    def wait_scatters(sl):
        pltpu.make_async_copy(ybuf.at[sl], yt_hbm.at[pl.ds(0, ROW_BLOCK)], ssem.at[sl]).wait()

    staged = ((wg_hbm, wg_st, wg_bf), (wu_hbm, wu_st, wu_bf), (wd_hbm, wd_st, wd_bf))

    def start_weights(e):
        for q, (src, st, _) in enumerate(staged):
            pltpu.make_async_copy(src.at[e], st, wsem.at[q]).start()

    def wait_and_cast_weights():
        for q, (src, st, dst) in enumerate(staged):
            pltpu.make_async_copy(src.at[0], st, wsem.at[q]).wait()

            def cast(c, carry, st=st, dst=dst):
                r = pl.multiple_of(c * CAST_ROWS, CAST_ROWS)
                dst[pl.ds(r, CAST_ROWS), :] = st[pl.ds(r, CAST_ROWS), :].astype(bf16)
                return carry

            lax.fori_loop(0, st.shape[0] // CAST_ROWS, cast, 0)

    @pl.when(j == 0)
    def _():
        start_weights(bexp_ref[0])
        for b in range(GATHER_BUFS - 1):
            @pl.when(b < n_used)
            def _(b=b):
                start_gathers(b, b)
        ybuf[...] = jnp.zeros_like(ybuf)
        for sl in range(2):
            spare = yt_hbm.at[pl.ds(2 * n_tok + sl * ROW_BLOCK, ROW_BLOCK)]
            pltpu.make_async_copy(ybuf.at[sl], spare, ssem.at[sl]).start()
        for sl in range(2):
            wait_scatters(sl)

    @pl.when(j < n_used)
    def _():
        e = bexp_ref[j]

        @pl.when((j == 0) | (bexp_ref[jnp.maximum(j - 1, 0)] != e))
        def _():
            wait_and_cast_weights()

            @pl.when(nexp_ref[j] >= 0)
            def _():
                start_weights(nexp_ref[j])

        gslot = j % GATHER_BUFS
        wait_gathers(gslot)

        @pl.when(j + GATHER_BUFS - 1 < n_used)
        def _():
            start_gathers(j + GATHER_BUFS - 1, (j + GATHER_BUFS - 1) % GATHER_BUFS)

        @pl.when(j >= 2)
        def _():
            wait_scatters(slot)

        lo, hi = _unpack_rows(xbuf[gslot])
        xb = jnp.concatenate([lo, hi], axis=1).astype(bf16)
        g = jnp.dot(xb, wg_bf[...], preferred_element_type=f32)
        u = jnp.dot(xb, wu_bf[...], preferred_element_type=f32)
        hm = (g * jax.nn.sigmoid(g) * u).astype(bf16)
        ybuf[slot] = _pack_rows(jnp.dot(hm, wd_bf[...], preferred_element_type=f32))
        start_scatters(j, slot)

        @pl.when(j == n_used - 1)
        def _():
            wait_scatters(slot)

            @pl.when(j >= 1)
            def _():
                wait_scatters(1 - slot)


def _experts(block_expert, next_expert, row_pack, n_used, xn_packed, w_gate, w_up, w_down):
    n_rows = row_pack.shape[0]
    n_tok = xn_packed.shape[0]
    tile = xn_packed.shape[1:]
    _, d, f = w_gate.shape
    assert d % CAST_ROWS == 0 and f % CAST_ROWS == 0
    any_space = pl.BlockSpec(memory_space=pl.ANY)
    grid_spec = pltpu.PrefetchScalarGridSpec(
        num_scalar_prefetch=4,
        grid=(n_rows // ROW_BLOCK,),
        in_specs=[any_space, any_space, any_space, any_space],
        out_specs=any_space,
        scratch_shapes=[pltpu.VMEM((GATHER_BUFS, ROW_BLOCK) + tile, jnp.uint32),
                        pltpu.VMEM((2, ROW_BLOCK) + tile, jnp.uint32),
                        pltpu.VMEM((d, f), f32), pltpu.VMEM((d, f), f32), pltpu.VMEM((f, d), f32),
                        pltpu.VMEM((d, f), bf16), pltpu.VMEM((d, f), bf16), pltpu.VMEM((f, d), bf16),
                        pltpu.SemaphoreType.DMA((GATHER_BUFS,)), pltpu.SemaphoreType.DMA((2,)),
                        pltpu.SemaphoreType.DMA((3,))],
    )
    return pl.pallas_call(
        functools.partial(_experts_kernel, n_tok=n_tok),
        grid_spec=grid_spec,
        out_shape=jax.ShapeDtypeStruct((2 * n_tok + 2 * ROW_BLOCK,) + tile, jnp.uint32),
        compiler_params=_cparams(("arbitrary",)),
        name="experts",
    )(block_expert, next_expert, row_pack, n_used, xn_packed, w_gate, w_up, w_down)


def _combine_kernel(h_ref, y0_ref, y1_ref, gate_ref, g_ref, o_ref):
    gate = gate_ref[...]
    y0 = jnp.concatenate(_unpack_rows(y0_ref[...]), axis=1)
    y1 = jnp.concatenate(_unpack_rows(y1_ref[...]), axis=1)
    hh = h_ref[...] + gate[:, 0:1] * y0 + gate[:, 1:2] * y1
    ms = jnp.mean(hh * hh, axis=-1, keepdims=True)
    o_ref[...] = hh * lax.rsqrt(ms + EPS) * g_ref[...]


def _combine(h, yt, gate, g_final, tm):
    t, d = h.shape
    nt = t // tm
    return pl.pallas_call(
        _combine_kernel,
        grid=(nt,),
        in_specs=[
            pl.BlockSpec((tm, d), lambda i: (i, 0)),
            pl.BlockSpec((tm,) + yt.shape[1:], lambda i: (i, 0, 0)),
            pl.BlockSpec((tm,) + yt.shape[1:], lambda i: (nt + i, 0, 0)),
            pl.BlockSpec((tm, LANES), lambda i: (i, 0)),
            pl.BlockSpec((1, d), lambda i: (0, 0)),
        ],
        out_specs=pl.BlockSpec((tm, d), lambda i: (i, 0)),
        out_shape=jax.ShapeDtypeStruct((t, d), f32),
        compiler_params=_cparams(("parallel",)),
        name="combine",
    )(h, yt, yt, gate, g_final)


def _route_kernel(eid_ref, cnt_ref, rpack_ref, bexp_ref, nexp_ref, nused_ref, *cur_refs, n_tok, n_blocks):
    tok_bits = (n_tok - 1).bit_length()
    chunk = 2 * n_tok // ROUTE_CHAINS

    def no_next(k, carry):
        nexp_ref[k] = -1
        return carry

    lax.fori_loop(0, n_blocks, no_next, 0)

    def per_expert(e, carry):
        blk, prev_blk, prev_nb = carry
        start = blk * ROW_BLOCK
        run = start
        for c in range(ROUTE_CHAINS):
            cur_refs[c][e] = run
            run = run + cnt_ref[c * N_EXPERTS + e]
        nb = (run - start + ROW_BLOCK - 1) // ROW_BLOCK

        def set_block(k, c):
            bexp_ref[blk + k] = e
            return c

        lax.fori_loop(0, nb, set_block, 0)

        def set_next(k, c):
            nexp_ref[prev_blk + k] = e
            return c

        lax.fori_loop(0, jnp.where(nb > 0, prev_nb, 0), set_next, 0)

        def set_pad(r, c):
            rpack_ref[r] = (2 * n_tok + (r & (2 * ROW_BLOCK - 1))) << tok_bits
            return c

        lax.fori_loop(run, start + nb * ROW_BLOCK, set_pad, 0)
        return blk + nb, jnp.where(nb > 0, blk, prev_blk), jnp.where(nb > 0, nb, prev_nb)

    n_used, _, _ = lax.fori_loop(0, N_EXPERTS, per_expert, (0, 0, 0))
    nused_ref[0] = n_used

    def tail_block(k, carry):
        bexp_ref[k] = N_EXPERTS - 1
        return carry

    lax.fori_loop(n_used, n_blocks, tail_block, 0)

    def tail_row(r, carry):
        rpack_ref[r] = (2 * n_tok + (r & (2 * ROW_BLOCK - 1))) << tok_bits
        return carry

    lax.fori_loop(n_used * ROW_BLOCK, n_blocks * ROW_BLOCK, tail_row, 0)

    def place(i, carry):
        for c in range(ROUTE_CHAINS):
            a = c * chunk + i
            e = eid_ref[a]
            p = cur_refs[c][e]
            cur_refs[c][e] = p + 1
            rpack_ref[p] = (a << tok_bits) | (a - (c * chunk // n_tok) * n_tok)
        return carry

    lax.fori_loop(0, chunk, place, 0)


def _route(eid_flat, counts, n_tok):
    n_asg = eid_flat.shape[0]
    n_rows = -(-(n_asg + N_EXPERTS * (ROW_BLOCK - 1)) // ROW_BLOCK) * ROW_BLOCK
    n_blocks = n_rows // ROW_BLOCK
    smem = pl.BlockSpec(memory_space=pltpu.SMEM)
    return pl.pallas_call(
        functools.partial(_route_kernel, n_tok=n_tok, n_blocks=n_blocks),
        in_specs=[smem, smem],
        out_specs=[smem, smem, smem, smem],
        out_shape=[jax.ShapeDtypeStruct((n_rows,), jnp.int32),
                   jax.ShapeDtypeStruct((n_blocks,), jnp.int32),
                   jax.ShapeDtypeStruct((n_blocks,), jnp.int32),
                   jax.ShapeDtypeStruct((1,), jnp.int32)],
        scratch_shapes=[pltpu.SMEM((N_EXPERTS,), jnp.int32)] * ROUTE_CHAINS,
        name="route",
    )(eid_flat, counts)


def _pick(n, prefs):
    for p in prefs:
        if n % p == 0:
            return p
    return n


def kernel(x, g_mix_norm, w_in, g_kv, w_uv, g_ret, w_branch, w_out, g_ffn_norm, w_router_group,
           b_router_group, w_router_expert, b_router_expert, w_expert_gate, w_expert_up,
           w_expert_down, g_final):
    b, s, d = x.shape
    t = b * s
    depth = w_in.shape[0]
    n_sel = min(TOPK_MAX, s // 4)
    assert s % RET_CHUNK == 0 and s % Q_TILE == 0

    cos_t, sin_t, d_in, d_q, d_k, d_c = _retention_tables(s)
    h2 = x.reshape(t, d)
    for l in range(depth):
        wl = w_in[l]
        sp = np.cumsum([0, H_A * D_LATENT, D_LATENT, H_IDX * D_IDX, D_IDX, H_IDX,
                        H_R * DK_R, H_R * DK_R, H_R * DV_R, H_R * DV_R, N_BRANCH * d])
        seg = [wl[:, sp[k]:sp[k + 1]].astype(bf16) for k in range(10)]
        kw_pad = jnp.zeros((d, LANES - D_IDX - H_IDX), bf16)
        w_p = jnp.concatenate([seg[0], seg[9], seg[5], seg[6], seg[7], seg[8], seg[2], seg[1],
                               seg[3], seg[4], kw_pad], axis=1)
        assert w_p.shape[1] == D_IN_P

        proj = _proj(h2, g_mix_norm[l].reshape(1, d), w_p, _pick(t, (512, 256)), 3328)
        proj3 = proj.reshape(b, s, D_IN_P)

        o_a = _attn(proj3, g_kv[l].reshape(1, D_LATENT), w_uv[l].astype(bf16), n_sel)
        o_b = _ret(proj3, cos_t, sin_t, d_in, d_q, d_k, d_c, g_ret[l].reshape(1, H_R * DV_R))

        mixed = _mix(o_a.reshape(t, D_BRANCH), o_b.reshape(t, D_BRANCH), w_branch[l].astype(bf16),
                     proj, _pick(t, (512, 256)))

        w_r = jnp.concatenate([w_router_group[l], w_router_expert[l],
                               jnp.zeros((d, LANES - N_GROUPS - N_EXPERTS), f32)], axis=1)
        b_r = jnp.concatenate([b_router_group[l], b_router_expert[l],
                               jnp.zeros((LANES - N_GROUPS - N_EXPERTS,), f32)]).reshape(1, LANES)
        wr_hi = w_r.astype(bf16)
        wr_lo = (w_r - wr_hi.astype(f32)).astype(bf16)
        h2, xn, eid_t, gate, cnt = _outproj(mixed, h2, w_out[l].astype(bf16), g_ffn_norm[l].reshape(1, d),
                                            jnp.concatenate([wr_hi, wr_lo], axis=1), b_r, _pick(t, (512, 256)))

        row_pack, block_expert, next_expert, n_used = _route(
            eid_t[:2].reshape(-1), cnt[:ROUTE_CHAINS, N_GROUPS:N_GROUPS + N_EXPERTS].reshape(-1), t)
        yt = _experts(block_expert, next_expert, row_pack, n_used, xn,
                      w_expert_gate[l], w_expert_up[l], w_expert_down[l])
        assert depth == 1
        h2 = _combine(h2, yt, gate, g_final.reshape(1, d), _pick(t, (256,)))
    return h2.reshape(b, s, d)
```

```python
import functools

import jax
import jax.numpy as jnp
import numpy as np
from jax import lax
from jax.experimental import pallas as pl
from jax.experimental.pallas import tpu as pltpu

EPS = 1e-6
CHUNK = 64
H_A = 8
D_LATENT = 128
DH_A = 128
H_IDX = 8
D_IDX = 64
TOPK_MAX = 256
H_R = 8
DK_R = 128
DV_R = 128
ROPE_BASE = 10000.0
D_BRANCH = 1024
N_BRANCH = 2
N_GROUPS = 4
EXP_PER_GROUP = 8
N_EXPERTS = N_GROUPS * EXP_PER_GROUP
D_EXPERT = 1024

LANES = 128
KEY_TILE = 256
Q_TILE = 512
RET_CHUNK = 256
ROW_BLOCK = 256
ROUTE_HALVES = 4
ROUTE_CHAINS = 2 * ROUTE_HALVES
VMEM_LIMIT = 56 * 1024 * 1024

C_QLAT = 0
C_GBR = 1024
C_QR = 5120
C_KR = 6144
C_VR = 7168
C_GR = 8192
C_QIDX = 9216
C_CKV = 9728
C_KW = 9856
D_IN_P = 9984

INT_MIN = np.int32(-2 ** 31)
NEG_BIG = -1e30

bf16 = jnp.bfloat16
f32 = jnp.float32


def _cparams(sem):
    return pltpu.CompilerParams(dimension_semantics=sem, vmem_limit_bytes=VMEM_LIMIT)


def _relayout_table(d_model):
    sp = np.cumsum([0, H_A * D_LATENT, D_LATENT, H_IDX * D_IDX, D_IDX, H_IDX,
                    H_R * DK_R, H_R * DK_R, H_R * DV_R, H_R * DV_R, N_BRANCH * d_model])
    order = [(sp[0], C_GBR - C_QLAT), (sp[9], C_QR - C_GBR), (sp[5], C_KR - C_QR), (sp[6], C_VR - C_KR),
             (sp[7], C_GR - C_VR), (sp[8], C_QIDX - C_GR), (sp[2], C_CKV - C_QIDX), (sp[1], C_KW - C_CKV),
             (sp[3], D_IN_P - C_KW)]
    blocks, shifts = [], []
    for off, width in order:
        for k in range(width // LANES):
            blocks.append(int(off + k * LANES) // LANES)
            shifts.append(int(off + k * LANES) % LANES)
    shift = max(shifts)
    assert set(shifts) <= {0, shift} and len(blocks) == D_IN_P // LANES
    return np.asarray(blocks, np.int32), np.asarray(shifts, np.int32), shift


def _relayout_kernel(blk_ref, shf_ref, a_ref, b_ref, o_ref, *, shift):
    j = pl.program_id(0)

    @pl.when(shf_ref[j] == 0)
    def _():
        o_ref[...] = a_ref[...].astype(o_ref.dtype)

    @pl.when(shf_ref[j] != 0)
    def _():
        o_ref[...] = jnp.concatenate([a_ref[:, shift:], b_ref[:, :shift]], axis=1).astype(o_ref.dtype)


def _relayout(w, d_model):
    blocks, shifts, shift = _relayout_table(d_model)
    last = (w.shape[1] - 1) // LANES
    grid_spec = pltpu.PrefetchScalarGridSpec(
        num_scalar_prefetch=2,
        grid=(D_IN_P // LANES,),
        in_specs=[
            pl.BlockSpec((w.shape[0], LANES), lambda j, blk, shf: (0, blk[j])),
            pl.BlockSpec((w.shape[0], LANES), lambda j, blk, shf: (0, jnp.minimum(blk[j] + 1, last))),
        ],
        out_specs=pl.BlockSpec((w.shape[0], LANES), lambda j, blk, shf: (0, j)),
    )
    return pl.pallas_call(
        functools.partial(_relayout_kernel, shift=shift),
        grid_spec=grid_spec,
        out_shape=jax.ShapeDtypeStruct((w.shape[0], D_IN_P), bf16),
        compiler_params=_cparams(("parallel",)),
        name="relayout",
    )(jnp.asarray(blocks), jnp.asarray(shifts), w, w)


def _proj_kernel(x_ref, g_ref, w_ref, o_ref, xn_ref):
    @pl.when(pl.program_id(1) == 0)
    def _():
        x = x_ref[...]
        ms = jnp.mean(x * x, axis=-1, keepdims=True)
        xn_ref[...] = (x * lax.rsqrt(ms + EPS) * g_ref[...]).astype(bf16)

    o_ref[...] = jnp.dot(xn_ref[...], w_ref[...], preferred_element_type=f32).astype(o_ref.dtype)


def _proj(x2, g, w_p, tm, tn):
    t, d = x2.shape
    n = w_p.shape[1]
    return pl.pallas_call(
        _proj_kernel,
        grid=(t // tm, n // tn),
        in_specs=[
            pl.BlockSpec((tm, d), lambda i, j: (i, 0)),
            pl.BlockSpec((1, d), lambda i, j: (0, 0)),
            pl.BlockSpec((d, tn), lambda i, j: (0, j)),
        ],
        out_specs=pl.BlockSpec((tm, tn), lambda i, j: (i, j)),
        out_shape=jax.ShapeDtypeStruct((t, n), bf16),
        scratch_shapes=[pltpu.VMEM((tm, d), bf16)],
        compiler_params=_cparams(("parallel", "arbitrary")),
        name="proj",
    )(x2, g, w_p)


def _float_key(s):
    bits = pltpu.bitcast(s, jnp.int32)
    key = bits ^ ((bits >> 31) & jnp.int32(0x7FFFFFFF))
    return jnp.where(s == 0.0, jnp.int32(0), key)


def _attn_kernel(qlat_ref, qidx_ref, kwq_ref, ckv_ref, kwk_ref, gkv_ref, wuv_ref, o_ref,
                 kv_s, kvT_s, kidx_s, key_s, bias_s, qT_s, qiT_s, acc_s, *, n_sel, n_kt):
    i = pl.program_id(1)
    idx_scale = (H_IDX ** -0.5) * (D_IDX ** -0.5)
    attn_scale = D_LATENT ** -0.5
    hq = H_A * Q_TILE

    @pl.when(i == 0)
    def _():
        g = gkv_ref[...]
        for t in range(n_kt):
            c = ckv_ref[t * KEY_TILE:(t + 1) * KEY_TILE, :].astype(f32)
            ms = jnp.mean(c * c, axis=-1, keepdims=True)
            kv = c * lax.rsqrt(ms + EPS) * g
            kv_s[t] = kv.astype(bf16)
            kvT_s[t] = kv.T.astype(bf16)
            kidx_s[t] = kwk_ref[t * KEY_TILE:(t + 1) * KEY_TILE, :D_IDX]

    nk = ((i + 1) * Q_TILE + KEY_TILE - 1) // KEY_TILE
    lane = lax.broadcasted_iota(jnp.int32, (1, Q_TILE), 1)
    sub = lax.broadcasted_iota(jnp.int32, (KEY_TILE, 1), 0)
    q_chunk = (i * Q_TILE + lane) // CHUNK

    wT = kwq_ref[...].astype(f32).T
    for h in range(H_A):
        qT_s[:, h * Q_TILE:(h + 1) * Q_TILE] = qlat_ref[:, h * D_LATENT:(h + 1) * D_LATENT].astype(f32).T.astype(bf16)
    for h in range(H_IDX):
        qiT_s[:, h * Q_TILE:(h + 1) * Q_TILE] = qidx_ref[:, h * D_IDX:(h + 1) * D_IDX].astype(f32).T.astype(bf16)

    def score_tile(t, carry):
        d_all = jnp.dot(kidx_s[t], qiT_s[...], preferred_element_type=f32)
        acc = jnp.zeros((KEY_TILE, Q_TILE), f32)
        for h in range(H_IDX):
            d = d_all[:, h * Q_TILE:(h + 1) * Q_TILE]
            acc = acc + wT[D_IDX + h:D_IDX + h + 1, :] * jnp.maximum(d, 0.0)
        score = acc * idx_scale
        k_chunk = (t * KEY_TILE + sub) // CHUNK
        key_s[t] = jnp.where(k_chunk <= q_chunk, _float_key(score), INT_MIN)
        return carry

    lax.fori_loop(0, nk, score_tile, 0)

    @pl.when(nk % 2 == 1)
    def _():
        key_s[nk] = jnp.full((KEY_TILE, Q_TILE), INT_MIN, jnp.int32)

    n_pairs = (nk + 1) // 2

    def count(pred):
        def body(p, c):
            for t in (2 * p, 2 * p + 1):
                m = pred(key_s[t], t).astype(jnp.int32)
                c = c + jnp.sum(m.reshape(KEY_TILE // 8, 8, Q_TILE), axis=0)
            return c
        c8 = lax.fori_loop(0, n_pairs, body, jnp.zeros((8, Q_TILE), jnp.int32))
        return jnp.sum(c8, axis=0, keepdims=True)

    thr0 = jnp.where(count(lambda k, t: k >= 0) >= n_sel, jnp.int32(0), INT_MIN)
    thr0 = jnp.broadcast_to(thr0, (1, Q_TILE)).astype(jnp.int32)

    def bit_step(j, thr):
        cand = thr | (jnp.int32(1) << (jnp.int32(30) - j))
        return jnp.where(count(lambda k, t: k >= cand) >= n_sel, cand, thr)

    thr = lax.fori_loop(0, 31, bit_step, thr0)

    c_gt = count(lambda k, t: k > thr)
    c_ge = count(lambda k, t: k >= thr)
    need = n_sel - c_gt
    has_tie = jnp.max(jnp.where((c_ge > n_sel) & (thr > INT_MIN), 1, 0)) > 0

    def tie_limit():
        def step(j, m):
            cand = m | (jnp.int32(1) << (jnp.int32(14) - j))
            c = count(lambda k, t: (k == thr) & ((t * KEY_TILE + sub) < cand))
            return jnp.where(c < need, cand, m)
        return lax.fori_loop(0, 15, step, jnp.zeros((1, Q_TILE), jnp.int32))

    m_lim = lax.cond(has_tie, tie_limit, lambda: jnp.full((1, Q_TILE), 2 ** 30, jnp.int32))

    def bias_pair(p, carry):
        for t in (2 * p, 2 * p + 1):
            k = key_s[t]
            sel = (k > thr) | ((k == thr) & ((t * KEY_TILE + sub) <= m_lim))
            sel = sel & (k > INT_MIN)
            bias_s[t] = jnp.where(sel, 0.0, NEG_BIG).astype(f32)
        return carry

    lax.fori_loop(0, n_pairs, bias_pair, 0)

    acc_s[...] = jnp.zeros_like(acc_s)

    def att_pair(p, carry):
        m_run, l_run = carry
        kv2 = jnp.concatenate([kv_s[2 * p], kv_s[2 * p + 1]], axis=0)
        kvT2 = jnp.concatenate([kvT_s[2 * p], kvT_s[2 * p + 1]], axis=1)
        bias2 = jnp.concatenate([bias_s[2 * p], bias_s[2 * p + 1]], axis=0)
        logit = jnp.dot(kv2, qT_s[...], preferred_element_type=f32) * attn_scale
        logit = logit + jnp.concatenate([bias2] * H_A, axis=1)
        m_new = jnp.maximum(m_run, jnp.max(logit, axis=0, keepdims=True))
        alpha = jnp.exp(m_run - m_new)
        pr = jnp.exp(logit - m_new)
        l_new = alpha * l_run + jnp.sum(pr, axis=0, keepdims=True)
        acc_s[...] = alpha * acc_s[...] + jnp.dot(kvT2, pr.astype(bf16), preferred_element_type=f32)
        return m_new, l_new

    init = (jnp.full((1, hq), NEG_BIG, f32), jnp.zeros((1, hq), f32))
    _, l_fin = lax.fori_loop(0, n_pairs, att_pair, init)
    inv_l = 1.0 / l_fin
    for h in range(H_A):
        sl = slice(h * Q_TILE, (h + 1) * Q_TILE)
        o_lat = (acc_s[:, sl] * inv_l[:, sl]).T
        o_ref[:, h * DH_A:(h + 1) * DH_A] = jnp.dot(
            o_lat.astype(bf16), wuv_ref[h], preferred_element_type=f32).astype(o_ref.dtype)


def _attn(proj3, g_kv, w_uv_bf, n_sel):
    b, s, _ = proj3.shape
    n_kt = s // KEY_TILE
    kern = functools.partial(_attn_kernel, n_sel=n_sel, n_kt=n_kt)
    return pl.pallas_call(
        kern,
        grid=(b, s // Q_TILE),
        in_specs=[
            pl.BlockSpec((None, Q_TILE, H_A * D_LATENT), lambda bi, i: (bi, i, C_QLAT // 1024)),
            pl.BlockSpec((None, Q_TILE, H_IDX * D_IDX), lambda bi, i: (bi, i, C_QIDX // 512)),
            pl.BlockSpec((None, Q_TILE, LANES), lambda bi, i: (bi, i, C_KW // LANES)),
            pl.BlockSpec((None, s, LANES), lambda bi, i: (bi, 0, C_CKV // LANES)),
            pl.BlockSpec((None, s, LANES), lambda bi, i: (bi, 0, C_KW // LANES)),
            pl.BlockSpec((1, D_LATENT), lambda bi, i: (0, 0)),
            pl.BlockSpec((H_A, D_LATENT, DH_A), lambda bi, i: (0, 0, 0)),
        ],
        out_specs=pl.BlockSpec((None, Q_TILE, D_BRANCH), lambda bi, i: (bi, i, 0)),
        out_shape=jax.ShapeDtypeStruct((b, s, D_BRANCH), bf16),
        scratch_shapes=[
            pltpu.VMEM((n_kt, KEY_TILE, D_LATENT), bf16),
            pltpu.VMEM((n_kt, D_LATENT, KEY_TILE), bf16),
            pltpu.VMEM((n_kt, KEY_TILE, D_IDX), bf16),
            pltpu.VMEM((n_kt, KEY_TILE, Q_TILE), jnp.int32),
            pltpu.VMEM((n_kt, KEY_TILE, Q_TILE), f32),
            pltpu.VMEM((D_LATENT, H_A * Q_TILE), bf16),
            pltpu.VMEM((D_IDX, H_IDX * Q_TILE), bf16),
            pltpu.VMEM((D_LATENT, H_A * Q_TILE), f32),
        ],
        compiler_params=_cparams(("parallel", "arbitrary")),
        name="attn",
    )(proj3, proj3, proj3, proj3, proj3, g_kv, w_uv_bf)


def _ret_kernel(q_ref, k_ref, v_ref, gr_ref, cos_ref, sin_ref, din_ref, dq_ref, dk_ref, dc_ref,
                gret_ref, o_ref, state_s):
    @pl.when(pl.program_id(1) == 0)
    def _():
        state_s[...] = jnp.zeros_like(state_s)

    cos = cos_ref[...]
    sin = sin_ref[...]

    def rot(x):
        return x * cos + pltpu.roll(x, DK_R // 2, axis=1) * sin

    for h in range(H_R):
        sl = slice(h * DK_R, (h + 1) * DK_R)
        q = rot(q_ref[:, sl].astype(f32)).astype(bf16)
        kf = rot(k_ref[:, sl].astype(f32)) * (DK_R ** -0.5)
        k = kf.astype(bf16)
        v = v_ref[:, sl]
        inner = lax.dot_general(q, k, (((1,), (1,)), ((), ())), preferred_element_type=f32) * din_ref[h]
        o = jnp.dot(inner.astype(bf16), v, preferred_element_type=f32)
        st = state_s[h]
        o = o + jnp.dot(q, st.astype(bf16), preferred_element_type=f32) * dq_ref[h]
        kd = (kf * dk_ref[h]).astype(bf16)
        state_s[h] = st * dc_ref[h] + jnp.dot(kd.T, v, preferred_element_type=f32)
        mu = jnp.mean(o, axis=-1, keepdims=True)
        var = jnp.mean(jnp.square(o - mu), axis=-1, keepdims=True)
        y = (o - mu) * lax.rsqrt(var + EPS) * gret_ref[:, sl]
        gate = gr_ref[:, sl].astype(f32)
        o_ref[:, sl] = (gate * jax.nn.sigmoid(gate) * y).astype(o_ref.dtype)


def _ret(proj3, cos_t, sin_t, d_in, d_q, d_k, d_c, g_ret):
    b, s, _ = proj3.shape
    c = RET_CHUNK
    w = H_R * DK_R

    def col(off):
        return pl.BlockSpec((None, c, w), lambda bi, ci: (bi, ci, off // w))

    return pl.pallas_call(
        _ret_kernel,
        grid=(b, s // c),
        in_specs=[
            col(C_QR), col(C_KR), col(C_VR), col(C_GR),
            pl.BlockSpec((c, DK_R), lambda bi, ci: (ci, 0)),
            pl.BlockSpec((c, DK_R), lambda bi, ci: (ci, 0)),
            pl.BlockSpec((H_R, c, c), lambda bi, ci: (0, 0, 0)),
            pl.BlockSpec((H_R, c, DK_R), lambda bi, ci: (0, 0, 0)),
            pl.BlockSpec((H_R, c, DK_R), lambda bi, ci: (0, 0, 0)),
            pl.BlockSpec((H_R, 1, DK_R), lambda bi, ci: (0, 0, 0)),
            pl.BlockSpec((1, w), lambda bi, ci: (0, 0)),
        ],
        out_specs=pl.BlockSpec((None, c, w), lambda bi, ci: (bi, ci, 0)),
        out_shape=jax.ShapeDtypeStruct((b, s, w), bf16),
        scratch_shapes=[pltpu.VMEM((H_R, DK_R, DV_R), f32)],
        compiler_params=_cparams(("parallel", "arbitrary")),
        name="ret",
    )(proj3, proj3, proj3, proj3, cos_t, sin_t, d_in, d_q, d_k, d_c, g_ret)


def _retention_tables(s):
    c = RET_CHUNK
    half = DK_R // 2
    freq = ROPE_BASE ** (-jnp.arange(half, dtype=f32) / half)
    ang = jnp.arange(s, dtype=f32)[:, None] * freq[None, :]
    cos = jnp.cos(ang)
    sin = jnp.sin(ang)
    cos_t = jnp.concatenate([cos, cos], axis=-1)
    sin_t = jnp.concatenate([-sin, sin], axis=-1)
    log_gamma = jnp.log1p(-jnp.exp2(-5.0 - jnp.arange(H_R, dtype=f32)))
    n = jnp.arange(c, dtype=f32)
    diff = n[:, None] - n[None, :]
    d_in = jnp.where(diff >= 0, jnp.exp(log_gamma[:, None, None] * jnp.maximum(diff, 0.0)), 0.0)
    d_q = jnp.broadcast_to(jnp.exp(log_gamma[:, None] * (n + 1.0))[:, :, None], (H_R, c, DK_R))
    d_k = jnp.broadcast_to(jnp.exp(log_gamma[:, None] * (c - 1.0 - n))[:, :, None], (H_R, c, DK_R))
    d_c = jnp.broadcast_to(jnp.exp(log_gamma * c)[:, None, None], (H_R, 1, DK_R))
    return cos_t, sin_t, d_in, d_q, d_k, d_c


MIX_CHUNK = 512


def _mix_kernel(oa_ref, ob_ref, wb_ref, ga0_ref, ga1_ref, gb0_ref, gb1_ref, o_ref):
    oa = oa_ref[...]
    ob = ob_ref[...]
    half = ga0_ref.shape[1]
    for c in range(0, o_ref.shape[1], MIX_CHUNK):
        ga_ref, gb_ref, off = (ga0_ref, gb0_ref, c) if c < half else (ga1_ref, gb1_ref, c - half)
        a = jnp.dot(oa, wb_ref[0, :, c:c + MIX_CHUNK], preferred_element_type=f32)
        b = jnp.dot(ob, wb_ref[1, :, c:c + MIX_CHUNK], preferred_element_type=f32)
        ga = jax.nn.sigmoid(ga_ref[:, off:off + MIX_CHUNK].astype(f32))
        gb = jax.nn.sigmoid(gb_ref[:, off:off + MIX_CHUNK].astype(f32))
        o_ref[:, c:c + MIX_CHUNK] = (ga * a + gb * b).astype(o_ref.dtype)


def _mix(o_a, o_b, w_branch_bf, proj, tm):
    t = o_a.shape[0]
    d = w_branch_bf.shape[2]
    half = d // 2
    assert C_GBR % half == 0 and half % MIX_CHUNK == 0

    def gate(k):
        return pl.BlockSpec((tm, half), lambda i: (i, C_GBR // half + k))

    return pl.pallas_call(
        _mix_kernel,
        grid=(t // tm,),
        in_specs=[
            pl.BlockSpec((tm, D_BRANCH), lambda i: (i, 0)),
            pl.BlockSpec((tm, D_BRANCH), lambda i: (i, 0)),
            pl.BlockSpec((N_BRANCH, D_BRANCH, d), lambda i: (0, 0, 0)),
            gate(0), gate(1), gate(2), gate(3),
        ],
        out_specs=pl.BlockSpec((tm, d), lambda i: (i, 0)),
        out_shape=jax.ShapeDtypeStruct((t, d), bf16),
        compiler_params=_cparams(("parallel",)),
        name="mix",
    )(o_a, o_b, w_branch_bf, proj, proj, proj, proj)


def _pack_rows(v):
    n = v.shape[1] // 2
    r = pltpu.bitcast(v.astype(bf16).astype(f32), jnp.uint32)
    w = (r[:, :n] >> 16) | (r[:, n:] & jnp.uint32(0xFFFF0000))
    return pltpu.einshape("r(ab)->rab", w, b=LANES)


def _unpack_rows(p):
    w = pltpu.einshape("rab->r(ab)", p)
    lo = pltpu.bitcast(w << 16, f32)
    hi = pltpu.bitcast(w & jnp.uint32(0xFFFF0000), f32)
    return lo, hi


def _split_bf16(a):
    hi = a.astype(bf16)
    lo = (a - hi.astype(f32)).astype(bf16)
    return hi, lo


def _outproj_kernel(mixed_ref, x_ref, wo_ref, g_ref, wr_ref, br_ref,
                    h_ref, xn_ref, eid_ref, gate_ref, cnt_ref):
    h = x_ref[...] + jnp.dot(mixed_ref[...], wo_ref[...], preferred_element_type=f32)
    h_ref[...] = h
    ms = jnp.mean(h * h, axis=-1, keepdims=True)
    xn = h * lax.rsqrt(ms + EPS) * g_ref[...]
    xn_ref[...] = _pack_rows(xn)

    x_hi, x_lo = _split_bf16(xn)
    hh_hl = jnp.dot(x_hi, wr_ref[...], preferred_element_type=f32)
    logit = (hh_hl[:, :LANES] + hh_hl[:, LANES:]
             + jnp.dot(x_lo, wr_ref[:, :LANES], preferred_element_type=f32)) + br_ref[...]

    lane = lax.broadcasted_iota(jnp.int32, logit.shape, 1)
    lanef = lane.astype(f32)
    neg = -jnp.inf

    def first_argmax(v, m):
        return jnp.min(jnp.where(v == m, lanef, float(LANES)), axis=-1, keepdims=True)

    lg = jnp.where(lane < N_GROUPS, logit, neg)
    mg = jnp.max(lg, axis=-1, keepdims=True)
    p_grp = 1.0 / jnp.sum(jnp.exp(lg - mg), axis=-1, keepdims=True)
    grp = first_argmax(lg, mg).astype(jnp.int32)

    e_lane = lane - N_GROUPS
    in_grp = (e_lane >= 0) & (e_lane < N_EXPERTS) & ((e_lane // EXP_PER_GROUP) == grp)
    le = jnp.where(in_grp, logit, neg)
    m1 = jnp.max(le, axis=-1, keepdims=True)
    i1 = first_argmax(le, m1)
    le2 = jnp.where(lanef == i1, neg, le)
    m2 = jnp.max(le2, axis=-1, keepdims=True)
    i2 = first_argmax(le2, m2)
    e2 = jnp.exp(m2 - m1)
    g1 = p_grp / (1.0 + e2)
    g2 = p_grp * e2 / (1.0 + e2)

    eid = jnp.where(lane == 0, i1, jnp.where(lane == 1, i2, float(N_GROUPS))) - float(N_GROUPS)
    eid_ref[...] = eid.astype(jnp.int32).T[:8, :]
    gate_ref[...] = jnp.where(lane == 0, g1, jnp.where(lane == 1, g2, 0.0))

    @pl.when(pl.program_id(0) == 0)
    def _():
        cnt_ref[...] = jnp.zeros_like(cnt_ref)

    half = pl.program_id(0) // (pl.num_programs(0) // ROUTE_HALVES)
    sub8 = lax.broadcasted_iota(jnp.int32, (8, LANES), 0)
    for s, idx in enumerate((i1, i2)):
        c = jnp.sum((lanef == idx).astype(jnp.int32), axis=0, keepdims=True)
        cnt_ref[...] += jnp.where(sub8 == s * ROUTE_HALVES + half, c, 0)


def _outproj(mixed, x2, w_out_bf, g_ffn, wr_hi_lo, b_r, tm):
    t, d = x2.shape
    row = lambda i: (i, 0)
    fixed = lambda i: (0, 0)
    return pl.pallas_call(
        _outproj_kernel,
        grid=(t // tm,),
        in_specs=[
            pl.BlockSpec((tm, d), row),
            pl.BlockSpec((tm, d), row),
            pl.BlockSpec((d, d), fixed),
            pl.BlockSpec((1, d), fixed),
            pl.BlockSpec((d, 2 * LANES), fixed),
            pl.BlockSpec((1, LANES), fixed),
        ],
        out_specs=[
            pl.BlockSpec((tm, d), row),
            pl.BlockSpec((tm, d // (2 * LANES), LANES), lambda i: (i, 0, 0)),
            pl.BlockSpec((8, tm), lambda i: (0, i)),
            pl.BlockSpec((tm, LANES), row),
            pl.BlockSpec((8, LANES), fixed),
        ],
        out_shape=[
            jax.ShapeDtypeStruct((t, d), f32),
            jax.ShapeDtypeStruct((t, d // (2 * LANES), LANES), jnp.uint32),
            jax.ShapeDtypeStruct((8, t), jnp.int32),
            jax.ShapeDtypeStruct((t, LANES), f32),
            jax.ShapeDtypeStruct((8, LANES), jnp.int32),
        ],
        compiler_params=_cparams(("arbitrary",)),
        name="outproj",
    )(mixed, x2, w_out_bf, g_ffn, wr_hi_lo, b_r)


ISSUE_UNROLL = 8
CAST_ROWS = 256
GATHER_BUFS = 4


def _experts_kernel(bexp_ref, nexp_ref, rpack_ref, nused_ref, xn_hbm, wg_hbm, wu_hbm, wd_hbm, yt_hbm,
                    xbuf, ybuf, wg_st, wu_st, wd_st, wg_bf, wu_bf, wd_bf, gsem, ssem, wsem, *, n_tok):
    j = pl.program_id(0)
    n_used = nused_ref[0]
    slot = j % 2
    tok_bits = (n_tok - 1).bit_length()

    def rows_of(blk, fn):
        base = blk * ROW_BLOCK

        def body(k, c):
            r0 = pl.multiple_of(k * ISSUE_UNROLL, ISSUE_UNROLL)
            for u in range(ISSUE_UNROLL):
                fn(r0 + u, rpack_ref[base + r0 + u], 1)
            return c

        lax.fori_loop(0, ROW_BLOCK // ISSUE_UNROLL, body, 0)

    def start_gathers(blk, sl):
        def one(r, packed, queue):
            tok = packed & ((1 << tok_bits) - 1)
            pltpu.make_async_copy(xn_hbm.at[tok], xbuf.at[sl, r], gsem.at[sl]).start(priority=queue)
        rows_of(blk, one)

    def start_scatters(blk, sl):
        def one(r, packed, queue):
            row = lax.shift_right_logical(packed, tok_bits)
            pltpu.make_async_copy(ybuf.at[sl, r], yt_hbm.at[row], ssem.at[sl]).start(priority=queue)
        rows_of(blk, one)

    def wait_gathers(sl):
        pltpu.make_async_copy(xn_hbm.at[pl.ds(0, ROW_BLOCK)], xbuf.at[sl], gsem.at[sl]).wait()

    def wait_scatters(sl):
        pltpu.make_async_copy(ybuf.at[sl], yt_hbm.at[pl.ds(0, ROW_BLOCK)], ssem.at[sl]).wait()

    staged = ((wg_hbm, wg_st, wg_bf), (wu_hbm, wu_st, wu_bf), (wd_hbm, wd_st, wd_bf))

    def start_weights(e):
        for q, (src, st, _) in enumerate(staged):
            pltpu.make_async_copy(src.at[e], st, wsem.at[q]).start()

    def wait_and_cast_weights():
        for q, (src, st, dst) in enumerate(staged):
            pltpu.make_async_copy(src.at[0], st, wsem.at[q]).wait()

            def cast(c, carry, st=st, dst=dst):
                r = pl.multiple_of(c * CAST_ROWS, CAST_ROWS)
                dst[pl.ds(r, CAST_ROWS), :] = st[pl.ds(r, CAST_ROWS), :].astype(bf16)
                return carry

            lax.fori_loop(0, st.shape[0] // CAST_ROWS, cast, 0)

    @pl.when(j == 0)
    def _():
        start_weights(bexp_ref[0])
        for b in range(GATHER_BUFS - 1):
            @pl.when(b < n_used)
            def _(b=b):
                start_gathers(b, b)
        ybuf[...] = jnp.zeros_like(ybuf)
        for sl in range(2):
            spare = yt_hbm.at[pl.ds(2 * n_tok + sl * ROW_BLOCK, ROW_BLOCK)]
            pltpu.make_async_copy(ybuf.at[sl], spare, ssem.at[sl]).start()
        for sl in range(2):
            wait_scatters(sl)

    @pl.when(j < n_used)
    def _():
        e = bexp_ref[j]

        @pl.when((j == 0) | (bexp_ref[jnp.maximum(j - 1, 0)] != e))
        def _():
            wait_and_cast_weights()

            @pl.when(nexp_ref[j] >= 0)
            def _():
                start_weights(nexp_ref[j])

        gslot = j % GATHER_BUFS
        wait_gathers(gslot)

        @pl.when(j + GATHER_BUFS - 1 < n_used)
        def _():
            start_gathers(j + GATHER_BUFS - 1, (j + GATHER_BUFS - 1) % GATHER_BUFS)

        @pl.when(j >= 2)
        def _():
            wait_scatters(slot)

        lo, hi = _unpack_rows(xbuf[gslot])
        xb = jnp.concatenate([lo, hi], axis=1).astype(bf16)
        g = jnp.dot(xb, wg_bf[...], preferred_element_type=f32)
        u = jnp.dot(xb, wu_bf[...], preferred_element_type=f32)
        hm = (g * jax.nn.sigmoid(g) * u).astype(bf16)
        ybuf[slot] = _pack_rows(jnp.dot(hm, wd_bf[...], preferred_element_type=f32))
        start_scatters(j, slot)

        @pl.when(j == n_used - 1)
        def _():
            wait_scatters(slot)

            @pl.when(j >= 1)
            def _():
                wait_scatters(1 - slot)


def _experts(block_expert, next_expert, row_pack, n_used, xn_packed, w_gate, w_up, w_down):
    n_rows = row_pack.shape[0]
    n_tok = xn_packed.shape[0]
    tile = xn_packed.shape[1:]
    _, d, f = w_gate.shape
    assert d % CAST_ROWS == 0 and f % CAST_ROWS == 0
    any_space = pl.BlockSpec(memory_space=pl.ANY)
    grid_spec = pltpu.PrefetchScalarGridSpec(
        num_scalar_prefetch=4,
        grid=(n_rows // ROW_BLOCK,),
        in_specs=[any_space, any_space, any_space, any_space],
        out_specs=any_space,
        scratch_shapes=[pltpu.VMEM((GATHER_BUFS, ROW_BLOCK) + tile, jnp.uint32),
                        pltpu.VMEM((2, ROW_BLOCK) + tile, jnp.uint32),
                        pltpu.VMEM((d, f), f32), pltpu.VMEM((d, f), f32), pltpu.VMEM((f, d), f32),
                        pltpu.VMEM((d, f), bf16), pltpu.VMEM((d, f), bf16), pltpu.VMEM((f, d), bf16),
                        pltpu.SemaphoreType.DMA((GATHER_BUFS,)), pltpu.SemaphoreType.DMA((2,)),
                        pltpu.SemaphoreType.DMA((3,))],
    )
    return pl.pallas_call(
        functools.partial(_experts_kernel, n_tok=n_tok),
        grid_spec=grid_spec,
        out_shape=jax.ShapeDtypeStruct((2 * n_tok + 2 * ROW_BLOCK,) + tile, jnp.uint32),
        compiler_params=_cparams(("arbitrary",)),
        name="experts",
    )(block_expert, next_expert, row_pack, n_used, xn_packed, w_gate, w_up, w_down)


def _combine_kernel(h_ref, y0_ref, y1_ref, gate_ref, g_ref, o_ref):
    gate = gate_ref[...]
    y0 = jnp.concatenate(_unpack_rows(y0_ref[...]), axis=1)
    y1 = jnp.concatenate(_unpack_rows(y1_ref[...]), axis=1)
    hh = h_ref[...] + gate[:, 0:1] * y0 + gate[:, 1:2] * y1
    ms = jnp.mean(hh * hh, axis=-1, keepdims=True)
    o_ref[...] = hh * lax.rsqrt(ms + EPS) * g_ref[...]


def _combine(h, yt, gate, g_final, tm):
    t, d = h.shape
    nt = t // tm
    return pl.pallas_call(
        _combine_kernel,
        grid=(nt,),
        in_specs=[
            pl.BlockSpec((tm, d), lambda i: (i, 0)),
            pl.BlockSpec((tm,) + yt.shape[1:], lambda i: (i, 0, 0)),
            pl.BlockSpec((tm,) + yt.shape[1:], lambda i: (nt + i, 0, 0)),
            pl.BlockSpec((tm, LANES), lambda i: (i, 0)),
            pl.BlockSpec((1, d), lambda i: (0, 0)),
        ],
        out_specs=pl.BlockSpec((tm, d), lambda i: (i, 0)),
        out_shape=jax.ShapeDtypeStruct((t, d), f32),
        compiler_params=_cparams(("parallel",)),
        name="combine",
    )(h, yt, yt, gate, g_final)


def _route_kernel(eid_ref, cnt_ref, rpack_ref, bexp_ref, nexp_ref, nused_ref, *cur_refs, n_tok, n_blocks):
    tok_bits = (n_tok - 1).bit_length()
    chunk = 2 * n_tok // ROUTE_CHAINS

    def no_next(k, carry):
        nexp_ref[k] = -1
        return carry

    lax.fori_loop(0, n_blocks, no_next, 0)

    def per_expert(e, carry):
        blk, prev_blk, prev_nb = carry
        start = blk * ROW_BLOCK
        run = start
        for c in range(ROUTE_CHAINS):
            cur_refs[c][e] = run
            run = run + cnt_ref[c * N_EXPERTS + e]
        nb = (run - start + ROW_BLOCK - 1) // ROW_BLOCK

        def set_block(k, c):
            bexp_ref[blk + k] = e
            return c

        lax.fori_loop(0, nb, set_block, 0)

        def set_next(k, c):
            nexp_ref[prev_blk + k] = e
            return c

        lax.fori_loop(0, jnp.where(nb > 0, prev_nb, 0), set_next, 0)

        def set_pad(r, c):
            rpack_ref[r] = (2 * n_tok + (r & (2 * ROW_BLOCK - 1))) << tok_bits
            return c

        lax.fori_loop(run, start + nb * ROW_BLOCK, set_pad, 0)
        return blk + nb, jnp.where(nb > 0, blk, prev_blk), jnp.where(nb > 0, nb, prev_nb)

    n_used, _, _ = lax.fori_loop(0, N_EXPERTS, per_expert, (0, 0, 0))
    nused_ref[0] = n_used

    def tail_block(k, carry):
        bexp_ref[k] = N_EXPERTS - 1
        return carry

    lax.fori_loop(n_used, n_blocks, tail_block, 0)

    def tail_row(r, carry):
        rpack_ref[r] = (2 * n_tok + (r & (2 * ROW_BLOCK - 1))) << tok_bits
        return carry

    lax.fori_loop(n_used * ROW_BLOCK, n_blocks * ROW_BLOCK, tail_row, 0)

    def place(i, carry):
        for c in range(ROUTE_CHAINS):
            a = c * chunk + i
            e = eid_ref[a]
            p = cur_refs[c][e]
            cur_refs[c][e] = p + 1
            rpack_ref[p] = (a << tok_bits) | (a - (c * chunk // n_tok) * n_tok)
        return carry

    lax.fori_loop(0, chunk, place, 0)


def _route(eid_flat, counts, n_tok):
    n_asg = eid_flat.shape[0]
    n_rows = -(-(n_asg + N_EXPERTS * (ROW_BLOCK - 1)) // ROW_BLOCK) * ROW_BLOCK
    n_blocks = n_rows // ROW_BLOCK
    smem = pl.BlockSpec(memory_space=pltpu.SMEM)
    return pl.pallas_call(
        functools.partial(_route_kernel, n_tok=n_tok, n_blocks=n_blocks),
        in_specs=[smem, smem],
        out_specs=[smem, smem, smem, smem],
        out_shape=[jax.ShapeDtypeStruct((n_rows,), jnp.int32),
                   jax.ShapeDtypeStruct((n_blocks,), jnp.int32),
                   jax.ShapeDtypeStruct((n_blocks,), jnp.int32),
                   jax.ShapeDtypeStruct((1,), jnp.int32)],
        scratch_shapes=[pltpu.SMEM((N_EXPERTS,), jnp.int32)] * ROUTE_CHAINS,
        name="route",
    )(eid_flat, counts)


def _pick(n, prefs):
    for p in prefs:
        if n % p == 0:
            return p
    return n


def kernel(x, g_mix_norm, w_in, g_kv, w_uv, g_ret, w_branch, w_out, g_ffn_norm, w_router_group,
           b_router_group, w_router_expert, b_router_expert, w_expert_gate, w_expert_up,
           w_expert_down, g_final):
    b, s, d = x.shape
    t = b * s
    depth = w_in.shape[0]
    n_sel = min(TOPK_MAX, s // 4)
    assert s % RET_CHUNK == 0 and s % Q_TILE == 0

    cos_t, sin_t, d_in, d_q, d_k, d_c = _retention_tables(s)
    h2 = x.reshape(t, d)
    for l in range(depth):
        w_p = _relayout(w_in[l], d)

        proj = _proj(h2, g_mix_norm[l].reshape(1, d), w_p, _pick(t, (512, 256)), 3328)
        proj3 = proj.reshape(b, s, D_IN_P)

        o_a = _attn(proj3, g_kv[l].reshape(1, D_LATENT), w_uv[l].astype(bf16), n_sel)
        o_b = _ret(proj3, cos_t, sin_t, d_in, d_q, d_k, d_c, g_ret[l].reshape(1, H_R * DV_R))

        mixed = _mix(o_a.reshape(t, D_BRANCH), o_b.reshape(t, D_BRANCH), w_branch[l].astype(bf16),
                     proj, _pick(t, (512, 256)))

        w_r = jnp.concatenate([w_router_group[l], w_router_expert[l],
                               jnp.zeros((d, LANES - N_GROUPS - N_EXPERTS), f32)], axis=1)
        b_r = jnp.concatenate([b_router_group[l], b_router_expert[l],
                               jnp.zeros((LANES - N_GROUPS - N_EXPERTS,), f32)]).reshape(1, LANES)
        wr_hi = w_r.astype(bf16)
        wr_lo = (w_r - wr_hi.astype(f32)).astype(bf16)
        h2, xn, eid_t, gate, cnt = _outproj(mixed, h2, w_out[l].astype(bf16), g_ffn_norm[l].reshape(1, d),
                                            jnp.concatenate([wr_hi, wr_lo], axis=1), b_r, _pick(t, (512, 256)))

        row_pack, block_expert, next_expert, n_used = _route(
            eid_t[:2].reshape(-1), cnt[:ROUTE_CHAINS, N_GROUPS:N_GROUPS + N_EXPERTS].reshape(-1), t)
        yt = _experts(block_expert, next_expert, row_pack, n_used, xn,
                      w_expert_gate[l], w_expert_up[l], w_expert_down[l])
        assert depth == 1
        h2 = _combine(h2, yt, gate, g_final.reshape(1, d), _pick(t, (256,)))
    return h2.reshape(b, s, d)
```

```python
import functools

import jax
import jax.numpy as jnp
import numpy as np
from jax import lax
from jax.experimental import pallas as pl
from jax.experimental.pallas import tpu as pltpu

EPS = 1e-6
CHUNK = 64
H_A = 8
D_LATENT = 128
DH_A = 128
H_IDX = 8
D_IDX = 64
TOPK_MAX = 256
H_R = 8
DK_R = 128
DV_R = 128
ROPE_BASE = 10000.0
D_BRANCH = 1024
N_BRANCH = 2
N_GROUPS = 4
EXP_PER_GROUP = 8
N_EXPERTS = N_GROUPS * EXP_PER_GROUP
D_EXPERT = 1024

LANES = 128
KEY_TILE = 256
Q_TILE = 512
RET_CHUNK = 256
ROW_BLOCK = 256
ROUTE_HALVES = 4
ROUTE_CHAINS = 2 * ROUTE_HALVES
VMEM_LIMIT = 56 * 1024 * 1024

C_QLAT = 0
C_GBR = 1024
C_QR = 5120
C_KR = 6144
C_VR = 7168
C_GR = 8192
C_QIDX = 9216
C_CKV = 9728
C_KW = 9856
D_IN_P = 9984

INT_MIN = np.int32(-2 ** 31)
NEG_BIG = -1e30

bf16 = jnp.bfloat16
f32 = jnp.float32


def _cparams(sem):
    return pltpu.CompilerParams(dimension_semantics=sem, vmem_limit_bytes=VMEM_LIMIT)


RELAYOUT_ROWS = 128


def _relayout_table(d_model):
    sp = np.cumsum([0, H_A * D_LATENT, D_LATENT, H_IDX * D_IDX, D_IDX, H_IDX,
                    H_R * DK_R, H_R * DK_R, H_R * DV_R, H_R * DV_R, N_BRANCH * d_model])
    order = [(sp[0], C_GBR - C_QLAT), (sp[9], C_QR - C_GBR), (sp[5], C_KR - C_QR), (sp[6], C_VR - C_KR),
             (sp[7], C_GR - C_VR), (sp[8], C_QIDX - C_GR), (sp[2], C_CKV - C_QIDX), (sp[1], C_KW - C_CKV),
             (sp[3], D_IN_P - C_KW)]
    rows = [int(off) + k * RELAYOUT_ROWS for off, height in order for k in range(height // RELAYOUT_ROWS)]
    assert len(rows) == D_IN_P // RELAYOUT_ROWS and all(r % 8 == 0 for r in rows)
    return np.asarray(rows, np.int32)


def _relayout_kernel(row_ref, wt_hbm, o_ref, buf, sem):
    j = pl.program_id(0)
    slot = j % 2

    def fetch(step, sl):
        src = wt_hbm.at[pl.ds(pl.multiple_of(row_ref[step], 8), RELAYOUT_ROWS)]
        return pltpu.make_async_copy(src, buf.at[sl], sem.at[sl])

    @pl.when(j == 0)
    def _():
        fetch(0, 0).start()

    @pl.when(j + 1 < pl.num_programs(0))
    def _():
        fetch(j + 1, 1 - slot).start()

    fetch(j, slot).wait()
    o_ref[...] = buf[slot].astype(o_ref.dtype)


def _relayout(wt):
    d_in, d_model = wt.shape
    rows = _relayout_table(d_model)
    assert int(rows.max()) + RELAYOUT_ROWS <= d_in
    grid_spec = pltpu.PrefetchScalarGridSpec(
        num_scalar_prefetch=1,
        grid=(D_IN_P // RELAYOUT_ROWS,),
        in_specs=[pl.BlockSpec(memory_space=pl.ANY)],
        out_specs=pl.BlockSpec((RELAYOUT_ROWS, d_model), lambda j, row: (j, 0)),
        scratch_shapes=[pltpu.VMEM((2, RELAYOUT_ROWS, d_model), f32), pltpu.SemaphoreType.DMA((2,))],
    )
    return pl.pallas_call(
        _relayout_kernel,
        grid_spec=grid_spec,
        out_shape=jax.ShapeDtypeStruct((D_IN_P, d_model), bf16),
        compiler_params=_cparams(("arbitrary",)),
        name="relayout",
    )(jnp.asarray(rows), wt)


def _proj_kernel(x_ref, g_ref, w_ref, o_ref, xn_ref):
    @pl.when(pl.program_id(1) == 0)
    def _():
        x = x_ref[...]
        ms = jnp.mean(x * x, axis=-1, keepdims=True)
        xn_ref[...] = (x * lax.rsqrt(ms + EPS) * g_ref[...]).astype(bf16)

    o_ref[...] = lax.dot_general(xn_ref[...], w_ref[...], (((1,), (1,)), ((), ())),
                                 preferred_element_type=f32).astype(o_ref.dtype)


def _proj(x2, g, w_pt, tm, tn):
    t, d = x2.shape
    n = w_pt.shape[0]
    return pl.pallas_call(
        _proj_kernel,
        grid=(t // tm, n // tn),
        in_specs=[
            pl.BlockSpec((tm, d), lambda i, j: (i, 0)),
            pl.BlockSpec((1, d), lambda i, j: (0, 0)),
            pl.BlockSpec((tn, d), lambda i, j: (j, 0)),
        ],
        out_specs=pl.BlockSpec((tm, tn), lambda i, j: (i, j)),
        out_shape=jax.ShapeDtypeStruct((t, n), bf16),
        scratch_shapes=[pltpu.VMEM((tm, d), bf16)],
        compiler_params=_cparams(("parallel", "arbitrary")),
        name="proj",
    )(x2, g, w_pt)


def _float_key(s):
    bits = pltpu.bitcast(s, jnp.int32)
    key = bits ^ ((bits >> 31) & jnp.int32(0x7FFFFFFF))
    return jnp.where(s == 0.0, jnp.int32(0), key)


def _attn_kernel(qlat_ref, qidx_ref, kwq_ref, ckv_ref, kwk_ref, gkv_ref, wuv_ref, o_ref,
                 kv_s, kvT_s, kidx_s, key_s, bias_s, qT_s, qiT_s, acc_s, *, n_sel, n_kt):
    i = pl.program_id(1)
    idx_scale = (H_IDX ** -0.5) * (D_IDX ** -0.5)
    attn_scale = D_LATENT ** -0.5
    hq = H_A * Q_TILE

    @pl.when(i == 0)
    def _():
        g = gkv_ref[...]
        for t in range(n_kt):
            c = ckv_ref[t * KEY_TILE:(t + 1) * KEY_TILE, :].astype(f32)
            ms = jnp.mean(c * c, axis=-1, keepdims=True)
            kv = c * lax.rsqrt(ms + EPS) * g
            kv_s[t] = kv.astype(bf16)
            kvT_s[t] = kv.T.astype(bf16)
            kidx_s[t] = kwk_ref[t * KEY_TILE:(t + 1) * KEY_TILE, :D_IDX]

    nk = ((i + 1) * Q_TILE + KEY_TILE - 1) // KEY_TILE
    lane = lax.broadcasted_iota(jnp.int32, (1, Q_TILE), 1)
    sub = lax.broadcasted_iota(jnp.int32, (KEY_TILE, 1), 0)
    q_chunk = (i * Q_TILE + lane) // CHUNK

    wT = kwq_ref[...].astype(f32).T
    for h in range(H_A):
        qT_s[:, h * Q_TILE:(h + 1) * Q_TILE] = qlat_ref[:, h * D_LATENT:(h + 1) * D_LATENT].astype(f32).T.astype(bf16)
    for h in range(H_IDX):
        qiT_s[:, h * Q_TILE:(h + 1) * Q_TILE] = qidx_ref[:, h * D_IDX:(h + 1) * D_IDX].astype(f32).T.astype(bf16)

    def score_tile(t, carry):
        d_all = jnp.dot(kidx_s[t], qiT_s[...], preferred_element_type=f32)
        acc = jnp.zeros((KEY_TILE, Q_TILE), f32)
        for h in range(H_IDX):
            d = d_all[:, h * Q_TILE:(h + 1) * Q_TILE]
            acc = acc + wT[D_IDX + h:D_IDX + h + 1, :] * jnp.maximum(d, 0.0)
        score = acc * idx_scale
        k_chunk = (t * KEY_TILE + sub) // CHUNK
        key_s[t] = jnp.where(k_chunk <= q_chunk, _float_key(score), INT_MIN)
        return carry

    lax.fori_loop(0, nk, score_tile, 0)

    @pl.when(nk % 2 == 1)
    def _():
        key_s[nk] = jnp.full((KEY_TILE, Q_TILE), INT_MIN, jnp.int32)

    n_pairs = (nk + 1) // 2

    def count(pred):
        def body(p, c):
            for t in (2 * p, 2 * p + 1):
                m = pred(key_s[t], t).astype(jnp.int32)
                c = c + jnp.sum(m.reshape(KEY_TILE // 8, 8, Q_TILE), axis=0)
            return c
        c8 = lax.fori_loop(0, n_pairs, body, jnp.zeros((8, Q_TILE), jnp.int32))
        return jnp.sum(c8, axis=0, keepdims=True)

    thr0 = jnp.where(count(lambda k, t: k >= 0) >= n_sel, jnp.int32(0), INT_MIN)
    thr0 = jnp.broadcast_to(thr0, (1, Q_TILE)).astype(jnp.int32)

    def bit_step(j, thr):
        cand = thr | (jnp.int32(1) << (jnp.int32(30) - j))
        return jnp.where(count(lambda k, t: k >= cand) >= n_sel, cand, thr)

    thr = lax.fori_loop(0, 31, bit_step, thr0)

    c_gt = count(lambda k, t: k > thr)
    c_ge = count(lambda k, t: k >= thr)
    need = n_sel - c_gt
    has_tie = jnp.max(jnp.where((c_ge > n_sel) & (thr > INT_MIN), 1, 0)) > 0

    def tie_limit():
        def step(j, m):
            cand = m | (jnp.int32(1) << (jnp.int32(14) - j))
            c = count(lambda k, t: (k == thr) & ((t * KEY_TILE + sub) < cand))
            return jnp.where(c < need, cand, m)
        return lax.fori_loop(0, 15, step, jnp.zeros((1, Q_TILE), jnp.int32))

    m_lim = lax.cond(has_tie, tie_limit, lambda: jnp.full((1, Q_TILE), 2 ** 30, jnp.int32))

    def bias_pair(p, carry):
        for t in (2 * p, 2 * p + 1):
            k = key_s[t]
            sel = (k > thr) | ((k == thr) & ((t * KEY_TILE + sub) <= m_lim))
            sel = sel & (k > INT_MIN)
            bias_s[t] = jnp.where(sel, 0.0, NEG_BIG).astype(f32)
        return carry

    lax.fori_loop(0, n_pairs, bias_pair, 0)

    acc_s[...] = jnp.zeros_like(acc_s)

    def att_pair(p, carry):
        m_run, l_run = carry
        kv2 = jnp.concatenate([kv_s[2 * p], kv_s[2 * p + 1]], axis=0)
        kvT2 = jnp.concatenate([kvT_s[2 * p], kvT_s[2 * p + 1]], axis=1)
        bias2 = jnp.concatenate([bias_s[2 * p], bias_s[2 * p + 1]], axis=0)
        logit = jnp.dot(kv2, qT_s[...], preferred_element_type=f32) * attn_scale
        logit = logit + jnp.concatenate([bias2] * H_A, axis=1)
        m_new = jnp.maximum(m_run, jnp.max(logit, axis=0, keepdims=True))
        alpha = jnp.exp(m_run - m_new)
        pr = jnp.exp(logit - m_new)
        l_new = alpha * l_run + jnp.sum(pr, axis=0, keepdims=True)
        acc_s[...] = alpha * acc_s[...] + jnp.dot(kvT2, pr.astype(bf16), preferred_element_type=f32)
        return m_new, l_new

    init = (jnp.full((1, hq), NEG_BIG, f32), jnp.zeros((1, hq), f32))
    _, l_fin = lax.fori_loop(0, n_pairs, att_pair, init)
    inv_l = 1.0 / l_fin
    for h in range(H_A):
        sl = slice(h * Q_TILE, (h + 1) * Q_TILE)
        o_lat = (acc_s[:, sl] * inv_l[:, sl]).T
        o_ref[:, h * DH_A:(h + 1) * DH_A] = jnp.dot(
            o_lat.astype(bf16), wuv_ref[h], preferred_element_type=f32).astype(o_ref.dtype)


def _attn(proj3, g_kv, w_uv_bf, n_sel):
    b, s, _ = proj3.shape
    n_kt = s // KEY_TILE
    kern = functools.partial(_attn_kernel, n_sel=n_sel, n_kt=n_kt)
    return pl.pallas_call(
        kern,
        grid=(b, s // Q_TILE),
        in_specs=[
            pl.BlockSpec((None, Q_TILE, H_A * D_LATENT), lambda bi, i: (bi, i, C_QLAT // 1024)),
            pl.BlockSpec((None, Q_TILE, H_IDX * D_IDX), lambda bi, i: (bi, i, C_QIDX // 512)),
            pl.BlockSpec((None, Q_TILE, LANES), lambda bi, i: (bi, i, C_KW // LANES)),
            pl.BlockSpec((None, s, LANES), lambda bi, i: (bi, 0, C_CKV // LANES)),
            pl.BlockSpec((None, s, LANES), lambda bi, i: (bi, 0, C_KW // LANES)),
            pl.BlockSpec((1, D_LATENT), lambda bi, i: (0, 0)),
            pl.BlockSpec((H_A, D_LATENT, DH_A), lambda bi, i: (0, 0, 0)),
        ],
        out_specs=pl.BlockSpec((None, Q_TILE, D_BRANCH), lambda bi, i: (bi, i, 0)),
        out_shape=jax.ShapeDtypeStruct((b, s, D_BRANCH), bf16),
        scratch_shapes=[
            pltpu.VMEM((n_kt, KEY_TILE, D_LATENT), bf16),
            pltpu.VMEM((n_kt, D_LATENT, KEY_TILE), bf16),
            pltpu.VMEM((n_kt, KEY_TILE, D_IDX), bf16),
            pltpu.VMEM((n_kt, KEY_TILE, Q_TILE), jnp.int32),
            pltpu.VMEM((n_kt, KEY_TILE, Q_TILE), f32),
            pltpu.VMEM((D_LATENT, H_A * Q_TILE), bf16),
            pltpu.VMEM((D_IDX, H_IDX * Q_TILE), bf16),
            pltpu.VMEM((D_LATENT, H_A * Q_TILE), f32),
        ],
        compiler_params=_cparams(("parallel", "arbitrary")),
        name="attn",
    )(proj3, proj3, proj3, proj3, proj3, g_kv, w_uv_bf)


def _ret_kernel(q_ref, k_ref, v_ref, gr_ref, cos_ref, sin_ref, din_ref, dq_ref, dk_ref, dc_ref,
                gret_ref, o_ref, state_s):
    @pl.when(pl.program_id(1) == 0)
    def _():
        state_s[...] = jnp.zeros_like(state_s)

    cos = cos_ref[...]
    sin = sin_ref[...]

    def rot(x):
        return x * cos + pltpu.roll(x, DK_R // 2, axis=1) * sin

    for h in range(H_R):
        sl = slice(h * DK_R, (h + 1) * DK_R)
        q = rot(q_ref[:, sl].astype(f32)).astype(bf16)
        kf = rot(k_ref[:, sl].astype(f32)) * (DK_R ** -0.5)
        k = kf.astype(bf16)
        v = v_ref[:, sl]
        inner = lax.dot_general(q, k, (((1,), (1,)), ((), ())), preferred_element_type=f32) * din_ref[h]
        o = jnp.dot(inner.astype(bf16), v, preferred_element_type=f32)
        st = state_s[h]
        o = o + jnp.dot(q, st.astype(bf16), preferred_element_type=f32) * dq_ref[h]
        kd = (kf * dk_ref[h]).astype(bf16)
        state_s[h] = st * dc_ref[h] + jnp.dot(kd.T, v, preferred_element_type=f32)
        mu = jnp.mean(o, axis=-1, keepdims=True)
        var = jnp.mean(jnp.square(o - mu), axis=-1, keepdims=True)
        y = (o - mu) * lax.rsqrt(var + EPS) * gret_ref[:, sl]
        gate = gr_ref[:, sl].astype(f32)
        o_ref[:, sl] = (gate * jax.nn.sigmoid(gate) * y).astype(o_ref.dtype)


def _ret(proj3, cos_t, sin_t, d_in, d_q, d_k, d_c, g_ret):
    b, s, _ = proj3.shape
    c = RET_CHUNK
    w = H_R * DK_R

    def col(off):
        return pl.BlockSpec((None, c, w), lambda bi, ci: (bi, ci, off // w))

    return pl.pallas_call(
        _ret_kernel,
        grid=(b, s // c),
        in_specs=[
            col(C_QR), col(C_KR), col(C_VR), col(C_GR),
            pl.BlockSpec((c, DK_R), lambda bi, ci: (ci, 0)),
            pl.BlockSpec((c, DK_R), lambda bi, ci: (ci, 0)),
            pl.BlockSpec((H_R, c, c), lambda bi, ci: (0, 0, 0)),
            pl.BlockSpec((H_R, c, DK_R), lambda bi, ci: (0, 0, 0)),
            pl.BlockSpec((H_R, c, DK_R), lambda bi, ci: (0, 0, 0)),
            pl.BlockSpec((H_R, 1, DK_R), lambda bi, ci: (0, 0, 0)),
            pl.BlockSpec((1, w), lambda bi, ci: (0, 0)),
        ],
        out_specs=pl.BlockSpec((None, c, w), lambda bi, ci: (bi, ci, 0)),
        out_shape=jax.ShapeDtypeStruct((b, s, w), bf16),
        scratch_shapes=[pltpu.VMEM((H_R, DK_R, DV_R), f32)],
        compiler_params=_cparams(("parallel", "arbitrary")),
        name="ret",
    )(proj3, proj3, proj3, proj3, cos_t, sin_t, d_in, d_q, d_k, d_c, g_ret)


def _retention_tables(s):
    c = RET_CHUNK
    half = DK_R // 2
    freq = ROPE_BASE ** (-jnp.arange(half, dtype=f32) / half)
    ang = jnp.arange(s, dtype=f32)[:, None] * freq[None, :]
    cos = jnp.cos(ang)
    sin = jnp.sin(ang)
    cos_t = jnp.concatenate([cos, cos], axis=-1)
    sin_t = jnp.concatenate([-sin, sin], axis=-1)
    log_gamma = jnp.log1p(-jnp.exp2(-5.0 - jnp.arange(H_R, dtype=f32)))
    n = jnp.arange(c, dtype=f32)
    diff = n[:, None] - n[None, :]
    d_in = jnp.where(diff >= 0, jnp.exp(log_gamma[:, None, None] * jnp.maximum(diff, 0.0)), 0.0)
    d_q = jnp.broadcast_to(jnp.exp(log_gamma[:, None] * (n + 1.0))[:, :, None], (H_R, c, DK_R))
    d_k = jnp.broadcast_to(jnp.exp(log_gamma[:, None] * (c - 1.0 - n))[:, :, None], (H_R, c, DK_R))
    d_c = jnp.broadcast_to(jnp.exp(log_gamma * c)[:, None, None], (H_R, 1, DK_R))
    return cos_t, sin_t, d_in, d_q, d_k, d_c


MIX_CHUNK = 512


def _mix_kernel(oa_ref, ob_ref, wb_ref, ga0_ref, ga1_ref, gb0_ref, gb1_ref, o_ref):
    oa = oa_ref[...]
    ob = ob_ref[...]
    half = ga0_ref.shape[1]
    for c in range(0, o_ref.shape[1], MIX_CHUNK):
        ga_ref, gb_ref, off = (ga0_ref, gb0_ref, c) if c < half else (ga1_ref, gb1_ref, c - half)
        a = jnp.dot(oa, wb_ref[0, :, c:c + MIX_CHUNK], preferred_element_type=f32)
        b = jnp.dot(ob, wb_ref[1, :, c:c + MIX_CHUNK], preferred_element_type=f32)
        ga = jax.nn.sigmoid(ga_ref[:, off:off + MIX_CHUNK].astype(f32))
        gb = jax.nn.sigmoid(gb_ref[:, off:off + MIX_CHUNK].astype(f32))
        o_ref[:, c:c + MIX_CHUNK] = (ga * a + gb * b).astype(o_ref.dtype)


def _mix(o_a, o_b, w_branch_bf, proj, tm):
    t = o_a.shape[0]
    d = w_branch_bf.shape[2]
    half = d // 2
    assert C_GBR % half == 0 and half % MIX_CHUNK == 0

    def gate(k):
        return pl.BlockSpec((tm, half), lambda i: (i, C_GBR // half + k))

    return pl.pallas_call(
        _mix_kernel,
        grid=(t // tm,),
        in_specs=[
            pl.BlockSpec((tm, D_BRANCH), lambda i: (i, 0)),
            pl.BlockSpec((tm, D_BRANCH), lambda i: (i, 0)),
            pl.BlockSpec((N_BRANCH, D_BRANCH, d), lambda i: (0, 0, 0)),
            gate(0), gate(1), gate(2), gate(3),
        ],
        out_specs=pl.BlockSpec((tm, d), lambda i: (i, 0)),
        out_shape=jax.ShapeDtypeStruct((t, d), bf16),
        compiler_params=_cparams(("parallel",)),
        name="mix",
    )(o_a, o_b, w_branch_bf, proj, proj, proj, proj)


def _pack_rows(v):
    n = v.shape[1] // 2
    r = pltpu.bitcast(v.astype(bf16).astype(f32), jnp.uint32)
    w = (r[:, :n] >> 16) | (r[:, n:] & jnp.uint32(0xFFFF0000))
    return pltpu.einshape("r(ab)->rab", w, b=LANES)


def _unpack_rows(p):
    w = pltpu.einshape("rab->r(ab)", p)
    lo = pltpu.bitcast(w << 16, f32)
    hi = pltpu.bitcast(w & jnp.uint32(0xFFFF0000), f32)
    return lo, hi


def _split_bf16(a):
    hi = a.astype(bf16)
    lo = (a - hi.astype(f32)).astype(bf16)
    return hi, lo


def _outproj_kernel(mixed_ref, x_ref, wo_ref, g_ref, wr_ref, br_ref,
                    h_ref, xn_ref, eid_ref, gate_ref, cnt_ref):
    h = x_ref[...] + jnp.dot(mixed_ref[...], wo_ref[...], preferred_element_type=f32)
    h_ref[...] = h
    ms = jnp.mean(h * h, axis=-1, keepdims=True)
    xn = h * lax.rsqrt(ms + EPS) * g_ref[...]
    xn_ref[...] = _pack_rows(xn)

    x_hi, x_lo = _split_bf16(xn)
    hh_hl = jnp.dot(x_hi, wr_ref[...], preferred_element_type=f32)
    logit = (hh_hl[:, :LANES] + hh_hl[:, LANES:]
             + jnp.dot(x_lo, wr_ref[:, :LANES], preferred_element_type=f32)) + br_ref[...]

    lane = lax.broadcasted_iota(jnp.int32, logit.shape, 1)
    lanef = lane.astype(f32)
    neg = -jnp.inf

    def first_argmax(v, m):
        return jnp.min(jnp.where(v == m, lanef, float(LANES)), axis=-1, keepdims=True)

    lg = jnp.where(lane < N_GROUPS, logit, neg)
    mg = jnp.max(lg, axis=-1, keepdims=True)
    p_grp = 1.0 / jnp.sum(jnp.exp(lg - mg), axis=-1, keepdims=True)
    grp = first_argmax(lg, mg).astype(jnp.int32)

    e_lane = lane - N_GROUPS
    in_grp = (e_lane >= 0) & (e_lane < N_EXPERTS) & ((e_lane // EXP_PER_GROUP) == grp)
    le = jnp.where(in_grp, logit, neg)
    m1 = jnp.max(le, axis=-1, keepdims=True)
    i1 = first_argmax(le, m1)
    le2 = jnp.where(lanef == i1, neg, le)
    m2 = jnp.max(le2, axis=-1, keepdims=True)
    i2 = first_argmax(le2, m2)
    e2 = jnp.exp(m2 - m1)
    g1 = p_grp / (1.0 + e2)
    g2 = p_grp * e2 / (1.0 + e2)

    eid = jnp.where(lane == 0, i1, jnp.where(lane == 1, i2, float(N_GROUPS))) - float(N_GROUPS)
    eid_ref[...] = eid.astype(jnp.int32).T[:8, :]
    gate_ref[...] = jnp.where(lane == 0, g1, jnp.where(lane == 1, g2, 0.0))

    @pl.when(pl.program_id(0) == 0)
    def _():
        cnt_ref[...] = jnp.zeros_like(cnt_ref)

    half = pl.program_id(0) // (pl.num_programs(0) // ROUTE_HALVES)
    sub8 = lax.broadcasted_iota(jnp.int32, (8, LANES), 0)
    for s, idx in enumerate((i1, i2)):
        c = jnp.sum((lanef == idx).astype(jnp.int32), axis=0, keepdims=True)
        cnt_ref[...] += jnp.where(sub8 == s * ROUTE_HALVES + half, c, 0)


def _outproj(mixed, x2, w_out_bf, g_ffn, wr_hi_lo, b_r, tm):
    t, d = x2.shape
    row = lambda i: (i, 0)
    fixed = lambda i: (0, 0)
    return pl.pallas_call(
        _outproj_kernel,
        grid=(t // tm,),
        in_specs=[
            pl.BlockSpec((tm, d), row),
            pl.BlockSpec((tm, d), row),
            pl.BlockSpec((d, d), fixed),
            pl.BlockSpec((1, d), fixed),
            pl.BlockSpec((d, 2 * LANES), fixed),
            pl.BlockSpec((1, LANES), fixed),
        ],
        out_specs=[
            pl.BlockSpec((tm, d), row),
            pl.BlockSpec((tm, d // (2 * LANES), LANES), lambda i: (i, 0, 0)),
            pl.BlockSpec((8, tm), lambda i: (0, i)),
            pl.BlockSpec((tm, LANES), row),
            pl.BlockSpec((8, LANES), fixed),
        ],
        out_shape=[
            jax.ShapeDtypeStruct((t, d), f32),
            jax.ShapeDtypeStruct((t, d // (2 * LANES), LANES), jnp.uint32),
            jax.ShapeDtypeStruct((8, t), jnp.int32),
            jax.ShapeDtypeStruct((t, LANES), f32),
            jax.ShapeDtypeStruct((8, LANES), jnp.int32),
        ],
        compiler_params=_cparams(("arbitrary",)),
        name="outproj",
    )(mixed, x2, w_out_bf, g_ffn, wr_hi_lo, b_r)


ISSUE_UNROLL = 8
CAST_ROWS = 256
GATHER_BUFS = 4


def _experts_kernel(bexp_ref, nexp_ref, rpack_ref, nused_ref, xn_hbm, wg_hbm, wu_hbm, wd_hbm, yt_hbm,
                    xbuf, ybuf, wg_st, wu_st, wd_st, wg_bf, wu_bf, wd_bf, gsem, ssem, wsem, *, n_tok):
    j = pl.program_id(0)
    n_used = nused_ref[0]
    slot = j % 2
    tok_bits = (n_tok - 1).bit_length()

    def rows_of(blk, fn):
        base = blk * ROW_BLOCK

        def body(k, c):
            r0 = pl.multiple_of(k * ISSUE_UNROLL, ISSUE_UNROLL)
            for u in range(ISSUE_UNROLL):
                fn(r0 + u, rpack_ref[base + r0 + u], 1)
            return c

        lax.fori_loop(0, ROW_BLOCK // ISSUE_UNROLL, body, 0)

    def start_gathers(blk, sl):
        def one(r, packed, queue):
            tok = packed & ((1 << tok_bits) - 1)
            pltpu.make_async_copy(xn_hbm.at[tok], xbuf.at[sl, r], gsem.at[sl]).start(priority=queue)
        rows_of(blk, one)

    def start_scatters(blk, sl):
        def one(r, packed, queue):
            row = lax.shift_right_logical(packed, tok_bits)
            pltpu.make_async_copy(ybuf.at[sl, r], yt_hbm.at[row], ssem.at[sl]).start(priority=queue)
        rows_of(blk, one)

    def wait_gathers(sl):
        pltpu.make_async_copy(xn_hbm.at[pl.ds(0, ROW_BLOCK)], xbuf.at[sl], gsem.at[sl]).wait()

    def wait_scatters(sl):
        pltpu.make_async_copy(ybuf.at[sl], yt_hbm.at[pl.ds(0, ROW_BLOCK)], ssem.at[sl]).wait()

    staged = ((wg_hbm, wg_st, wg_bf), (wu_hbm, wu_st, wu_bf), (wd_hbm, wd_st, wd_bf))

    def start_weights(e):
        for q, (src, st, _) in enumerate(staged):
            pltpu.make_async_copy(src.at[e], st, wsem.at[q]).start()

    def wait_and_cast_weights():
        for q, (src, st, dst) in enumerate(staged):
            pltpu.make_async_copy(src.at[0], st, wsem.at[q]).wait()

            def cast(c, carry, st=st, dst=dst):
                r = pl.multiple_of(c * CAST_ROWS, CAST_ROWS)
                dst[pl.ds(r, CAST_ROWS), :] = st[pl.ds(r, CAST_ROWS), :].astype(bf16)
                return carry

            lax.fori_loop(0, st.shape[0] // CAST_ROWS, cast, 0)

    @pl.when(j == 0)
    def _():
        start_weights(bexp_ref[0])
        for b in range(GATHER_BUFS - 1):
            @pl.when(b < n_used)
            def _(b=b):
                start_gathers(b, b)
        ybuf[...] = jnp.zeros_like(ybuf)
        for sl in range(2):
            spare = yt_hbm.at[pl.ds(2 * n_tok + sl * ROW_BLOCK, ROW_BLOCK)]
            pltpu.make_async_copy(ybuf.at[sl], spare, ssem.at[sl]).start()
        for sl in range(2):
            wait_scatters(sl)

    @pl.when(j < n_used)
    def _():
        e = bexp_ref[j]

        @pl.when((j == 0) | (bexp_ref[jnp.maximum(j - 1, 0)] != e))
        def _():
            wait_and_cast_weights()

            @pl.when(nexp_ref[j] >= 0)
            def _():
                start_weights(nexp_ref[j])

        gslot = j % GATHER_BUFS
        wait_gathers(gslot)

        @pl.when(j + GATHER_BUFS - 1 < n_used)
        def _():
            start_gathers(j + GATHER_BUFS - 1, (j + GATHER_BUFS - 1) % GATHER_BUFS)

        @pl.when(j >= 2)
        def _():
            wait_scatters(slot)

        lo, hi = _unpack_rows(xbuf[gslot])
        xb = jnp.concatenate([lo, hi], axis=1).astype(bf16)
        g = jnp.dot(xb, wg_bf[...], preferred_element_type=f32)
        u = jnp.dot(xb, wu_bf[...], preferred_element_type=f32)
        hm = (g * jax.nn.sigmoid(g) * u).astype(bf16)
        ybuf[slot] = _pack_rows(jnp.dot(hm, wd_bf[...], preferred_element_type=f32))
        start_scatters(j, slot)

        @pl.when(j == n_used - 1)
        def _():
            wait_scatters(slot)

            @pl.when(j >= 1)
            def _():
                wait_scatters(1 - slot)


def _experts(block_expert, next_expert, row_pack, n_used, xn_packed, w_gate, w_up, w_down):
    n_rows = row_pack.shape[0]
    n_tok = xn_packed.shape[0]
    tile = xn_packed.shape[1:]
    _, d, f = w_gate.shape
    assert d % CAST_ROWS == 0 and f % CAST_ROWS == 0
    any_space = pl.BlockSpec(memory_space=pl.ANY)
    grid_spec = pltpu.PrefetchScalarGridSpec(
        num_scalar_prefetch=4,
        grid=(n_rows // ROW_BLOCK,),
        in_specs=[any_space, any_space, any_space, any_space],
        out_specs=any_space,
        scratch_shapes=[pltpu.VMEM((GATHER_BUFS, ROW_BLOCK) + tile, jnp.uint32),
                        pltpu.VMEM((2, ROW_BLOCK) + tile, jnp.uint32),
                        pltpu.VMEM((d, f), f32), pltpu.VMEM((d, f), f32), pltpu.VMEM((f, d), f32),
                        pltpu.VMEM((d, f), bf16), pltpu.VMEM((d, f), bf16), pltpu.VMEM((f, d), bf16),
                        pltpu.SemaphoreType.DMA((GATHER_BUFS,)), pltpu.SemaphoreType.DMA((2,)),
                        pltpu.SemaphoreType.DMA((3,))],
    )
    return pl.pallas_call(
        functools.partial(_experts_kernel, n_tok=n_tok),
        grid_spec=grid_spec,
        out_shape=jax.ShapeDtypeStruct((2 * n_tok + 2 * ROW_BLOCK,) + tile, jnp.uint32),
        compiler_params=_cparams(("arbitrary",)),
        name="experts",
    )(block_expert, next_expert, row_pack, n_used, xn_packed, w_gate, w_up, w_down)


def _combine_kernel(h_ref, y0_ref, y1_ref, gate_ref, g_ref, o_ref):
    gate = gate_ref[...]
    y0 = jnp.concatenate(_unpack_rows(y0_ref[...]), axis=1)
    y1 = jnp.concatenate(_unpack_rows(y1_ref[...]), axis=1)
    hh = h_ref[...] + gate[:, 0:1] * y0 + gate[:, 1:2] * y1
    ms = jnp.mean(hh * hh, axis=-1, keepdims=True)
    o_ref[...] = hh * lax.rsqrt(ms + EPS) * g_ref[...]


def _combine(h, yt, gate, g_final, tm):
    t, d = h.shape
    nt = t // tm
    return pl.pallas_call(
        _combine_kernel,
        grid=(nt,),
        in_specs=[
            pl.BlockSpec((tm, d), lambda i: (i, 0)),
            pl.BlockSpec((tm,) + yt.shape[1:], lambda i: (i, 0, 0)),
            pl.BlockSpec((tm,) + yt.shape[1:], lambda i: (nt + i, 0, 0)),
            pl.BlockSpec((tm, LANES), lambda i: (i, 0)),
            pl.BlockSpec((1, d), lambda i: (0, 0)),
        ],
        out_specs=pl.BlockSpec((tm, d), lambda i: (i, 0)),
        out_shape=jax.ShapeDtypeStruct((t, d), f32),
        compiler_params=_cparams(("parallel",)),
        name="combine",
    )(h, yt, yt, gate, g_final)


def _route_kernel(eid_ref, cnt_ref, rpack_ref, bexp_ref, nexp_ref, nused_ref, *cur_refs, n_tok, n_blocks):
    tok_bits = (n_tok - 1).bit_length()
    chunk = 2 * n_tok // ROUTE_CHAINS

    def no_next(k, carry):
        nexp_ref[k] = -1
        return carry

    lax.fori_loop(0, n_blocks, no_next, 0)

    def per_expert(e, carry):
        blk, prev_blk, prev_nb = carry
        start = blk * ROW_BLOCK
        run = start
        for c in range(ROUTE_CHAINS):
            cur_refs[c][e] = run
            run = run + cnt_ref[c * N_EXPERTS + e]
        nb = (run - start + ROW_BLOCK - 1) // ROW_BLOCK

        def set_block(k, c):
            bexp_ref[blk + k] = e
            return c

        lax.fori_loop(0, nb, set_block, 0)

        def set_next(k, c):
            nexp_ref[prev_blk + k] = e
            return c

        lax.fori_loop(0, jnp.where(nb > 0, prev_nb, 0), set_next, 0)

        def set_pad(r, c):
            rpack_ref[r] = (2 * n_tok + (r & (2 * ROW_BLOCK - 1))) << tok_bits
            return c

        lax.fori_loop(run, start + nb * ROW_BLOCK, set_pad, 0)
        return blk + nb, jnp.where(nb > 0, blk, prev_blk), jnp.where(nb > 0, nb, prev_nb)

    n_used, _, _ = lax.fori_loop(0, N_EXPERTS, per_expert, (0, 0, 0))
    nused_ref[0] = n_used

    def tail_block(k, carry):
        bexp_ref[k] = N_EXPERTS - 1
        return carry

    lax.fori_loop(n_used, n_blocks, tail_block, 0)

    def tail_row(r, carry):
        rpack_ref[r] = (2 * n_tok + (r & (2 * ROW_BLOCK - 1))) << tok_bits
        return carry

    lax.fori_loop(n_used * ROW_BLOCK, n_blocks * ROW_BLOCK, tail_row, 0)

    def place(i, carry):
        for c in range(ROUTE_CHAINS):
            a = c * chunk + i
            e = eid_ref[a]
            p = cur_refs[c][e]
            cur_refs[c][e] = p + 1
            rpack_ref[p] = (a << tok_bits) | (a - (c * chunk // n_tok) * n_tok)
        return carry

    lax.fori_loop(0, chunk, place, 0)


def _route(eid_flat, counts, n_tok):
    n_asg = eid_flat.shape[0]
    n_rows = -(-(n_asg + N_EXPERTS * (ROW_BLOCK - 1)) // ROW_BLOCK) * ROW_BLOCK
    n_blocks = n_rows // ROW_BLOCK
    smem = pl.BlockSpec(memory_space=pltpu.SMEM)
    return pl.pallas_call(
        functools.partial(_route_kernel, n_tok=n_tok, n_blocks=n_blocks),
        in_specs=[smem, smem],
        out_specs=[smem, smem, smem, smem],
        out_shape=[jax.ShapeDtypeStruct((n_rows,), jnp.int32),
                   jax.ShapeDtypeStruct((n_blocks,), jnp.int32),
                   jax.ShapeDtypeStruct((n_blocks,), jnp.int32),
                   jax.ShapeDtypeStruct((1,), jnp.int32)],
        scratch_shapes=[pltpu.SMEM((N_EXPERTS,), jnp.int32)] * ROUTE_CHAINS,
        name="route",
    )(eid_flat, counts)


def _pick(n, prefs):
    for p in prefs:
        if n % p == 0:
            return p
    return n


def kernel(x, g_mix_norm, w_in, g_kv, w_uv, g_ret, w_branch, w_out, g_ffn_norm, w_router_group,
           b_router_group, w_router_expert, b_router_expert, w_expert_gate, w_expert_up,
           w_expert_down, g_final):
    b, s, d = x.shape
    t = b * s
    depth = w_in.shape[0]
    n_sel = min(TOPK_MAX, s // 4)
    assert s % RET_CHUNK == 0 and s % Q_TILE == 0

    cos_t, sin_t, d_in, d_q, d_k, d_c = _retention_tables(s)
    h2 = x.reshape(t, d)
    for l in range(depth):
        w_pt = _relayout(jnp.swapaxes(w_in, 1, 2)[l])

        proj = _proj(h2, g_mix_norm[l].reshape(1, d), w_pt,_pick(t, (512, 256)), 3328)
        proj3 = proj.reshape(b, s, D_IN_P)

        o_a = _attn(proj3, g_kv[l].reshape(1, D_LATENT), w_uv[l].astype(bf16), n_sel)
        o_b = _ret(proj3, cos_t, sin_t, d_in, d_q, d_k, d_c, g_ret[l].reshape(1, H_R * DV_R))

        mixed = _mix(o_a.reshape(t, D_BRANCH), o_b.reshape(t, D_BRANCH), w_branch[l].astype(bf16),
                     proj, _pick(t, (512, 256)))

        w_r = jnp.concatenate([w_router_group[l], w_router_expert[l],
                               jnp.zeros((d, LANES - N_GROUPS - N_EXPERTS), f32)], axis=1)
        b_r = jnp.concatenate([b_router_group[l], b_router_expert[l],
                               jnp.zeros((LANES - N_GROUPS - N_EXPERTS,), f32)]).reshape(1, LANES)
        wr_hi = w_r.astype(bf16)
        wr_lo = (w_r - wr_hi.astype(f32)).astype(bf16)
        h2, xn, eid_t, gate, cnt = _outproj(mixed, h2, w_out[l].astype(bf16), g_ffn_norm[l].reshape(1, d),
                                            jnp.concatenate([wr_hi, wr_lo], axis=1), b_r, _pick(t, (512, 256)))

        row_pack, block_expert, next_expert, n_used = _route(
            eid_t[:2].reshape(-1), cnt[:ROUTE_CHAINS, N_GROUPS:N_GROUPS + N_EXPERTS].reshape(-1), t)
        yt = _experts(block_expert, next_expert, row_pack, n_used, xn,
                      w_expert_gate[l], w_expert_up[l], w_expert_down[l])
        assert depth == 1
        h2 = _combine(h2, yt, gate, g_final.reshape(1, d), _pick(t, (256,)))
    return h2.reshape(b, s, d)
```

```python
import functools

import jax
import jax.numpy as jnp
import numpy as np
from jax import lax
from jax.experimental import pallas as pl
from jax.experimental.pallas import tpu as pltpu

EPS = 1e-6
CHUNK = 64
H_A = 8
D_LATENT = 128
DH_A = 128
H_IDX = 8
D_IDX = 64
TOPK_MAX = 256
H_R = 8
DK_R = 128
DV_R = 128
ROPE_BASE = 10000.0
D_BRANCH = 1024
N_BRANCH = 2
N_GROUPS = 4
EXP_PER_GROUP = 8
N_EXPERTS = N_GROUPS * EXP_PER_GROUP
D_EXPERT = 1024

LANES = 128
KEY_TILE = 256
Q_TILE = 512
RET_CHUNK = 256
ROW_BLOCK = 256
ROUTE_HALVES = 4
ROUTE_CHAINS = 2 * ROUTE_HALVES
VMEM_LIMIT = 56 * 1024 * 1024

C_QLAT = 0
C_GBR = 1024
C_QR = 5120
C_KR = 6144
C_VR = 7168
C_GR = 8192
C_QIDX = 9216
C_CKV = 9728
C_KW = 9856
D_IN_P = 9984

INT_MIN = np.int32(-2 ** 31)
NEG_BIG = -1e30

bf16 = jnp.bfloat16
f32 = jnp.float32


def _cparams(sem):
    return pltpu.CompilerParams(dimension_semantics=sem, vmem_limit_bytes=VMEM_LIMIT)


RELAYOUT_ROWS = 128
RELAYOUT_GROUP = 6


def _relayout_table(d_model):
    sp = np.cumsum([0, H_A * D_LATENT, D_LATENT, H_IDX * D_IDX, D_IDX, H_IDX,
                    H_R * DK_R, H_R * DK_R, H_R * DV_R, H_R * DV_R, N_BRANCH * d_model])
    order = [(sp[0], C_GBR - C_QLAT), (sp[9], C_QR - C_GBR), (sp[5], C_KR - C_QR), (sp[6], C_VR - C_KR),
             (sp[7], C_GR - C_VR), (sp[8], C_QIDX - C_GR), (sp[2], C_CKV - C_QIDX), (sp[1], C_KW - C_CKV),
             (sp[3], D_IN_P - C_KW)]
    rows = [int(off) + k * RELAYOUT_ROWS for off, height in order for k in range(height // RELAYOUT_ROWS)]
    assert len(rows) == D_IN_P // RELAYOUT_ROWS and all(r % 8 == 0 for r in rows)
    return np.asarray(rows, np.int32)


def _relayout_kernel(row_ref, wt_hbm, o_ref, buf, sem):
    j = pl.program_id(0)
    slot = j % 2

    def fetch(step, sl, k):
        src = wt_hbm.at[pl.ds(pl.multiple_of(row_ref[step * RELAYOUT_GROUP + k], 8), RELAYOUT_ROWS)]
        return pltpu.make_async_copy(src, buf.at[sl, pl.ds(k * RELAYOUT_ROWS, RELAYOUT_ROWS)], sem.at[sl])

    @pl.when(j == 0)
    def _():
        for k in range(RELAYOUT_GROUP):
            fetch(0, 0, k).start()

    @pl.when(j + 1 < pl.num_programs(0))
    def _():
        for k in range(RELAYOUT_GROUP):
            fetch(j + 1, 1 - slot, k).start()

    for k in range(RELAYOUT_GROUP):
        fetch(j, slot, k).wait()
    o_ref[...] = buf[slot].astype(o_ref.dtype)


def _relayout(wt):
    d_in, d_model = wt.shape
    rows = _relayout_table(d_model)
    step_rows = RELAYOUT_GROUP * RELAYOUT_ROWS
    assert int(rows.max()) + RELAYOUT_ROWS <= d_in and D_IN_P % step_rows == 0
    grid_spec = pltpu.PrefetchScalarGridSpec(
        num_scalar_prefetch=1,
        grid=(D_IN_P // step_rows,),
        in_specs=[pl.BlockSpec(memory_space=pl.ANY)],
        out_specs=pl.BlockSpec((step_rows, d_model), lambda j, row: (j, 0)),
        scratch_shapes=[pltpu.VMEM((2, step_rows, d_model), f32), pltpu.SemaphoreType.DMA((2,))],
    )
    return pl.pallas_call(
        _relayout_kernel,
        grid_spec=grid_spec,
        out_shape=jax.ShapeDtypeStruct((D_IN_P, d_model), bf16),
        compiler_params=_cparams(("arbitrary",)),
        name="relayout",
    )(jnp.asarray(rows), wt)


def _proj_kernel(x_ref, g_ref, w_ref, o_ref, xn_ref):
    @pl.when(pl.program_id(1) == 0)
    def _():
        x = x_ref[...]
        ms = jnp.mean(x * x, axis=-1, keepdims=True)
        xn_ref[...] = (x * lax.rsqrt(ms + EPS) * g_ref[...]).astype(bf16)

    o_ref[...] = lax.dot_general(xn_ref[...], w_ref[...], (((1,), (1,)), ((), ())),
                                 preferred_element_type=f32).astype(o_ref.dtype)


def _proj(x2, g, w_pt, tm, tn):
    t, d = x2.shape
    n = w_pt.shape[0]
    return pl.pallas_call(
        _proj_kernel,
        grid=(t // tm, n // tn),
        in_specs=[
            pl.BlockSpec((tm, d), lambda i, j: (i, 0)),
            pl.BlockSpec((1, d), lambda i, j: (0, 0)),
            pl.BlockSpec((tn, d), lambda i, j: (j, 0)),
        ],
        out_specs=pl.BlockSpec((tm, tn), lambda i, j: (i, j)),
        out_shape=jax.ShapeDtypeStruct((t, n), bf16),
        scratch_shapes=[pltpu.VMEM((tm, d), bf16)],
        compiler_params=_cparams(("parallel", "arbitrary")),
        name="proj",
    )(x2, g, w_pt)


def _float_key(s):
    bits = pltpu.bitcast(s, jnp.int32)
    key = bits ^ ((bits >> 31) & jnp.int32(0x7FFFFFFF))
    return jnp.where(s == 0.0, jnp.int32(0), key)


def _attn_kernel(qlat_ref, qidx_ref, kwq_ref, ckv_ref, kwk_ref, gkv_ref, wuv_ref, o_ref,
                 kv_s, kvT_s, kidx_s, key_s, bias_s, qT_s, qiT_s, acc_s, *, n_sel, n_kt):
    i = pl.program_id(1)
    idx_scale = (H_IDX ** -0.5) * (D_IDX ** -0.5)
    attn_scale = D_LATENT ** -0.5
    hq = H_A * Q_TILE

    @pl.when(i == 0)
    def _():
        g = gkv_ref[...]
        for t in range(n_kt):
            c = ckv_ref[t * KEY_TILE:(t + 1) * KEY_TILE, :].astype(f32)
            ms = jnp.mean(c * c, axis=-1, keepdims=True)
            kv = c * lax.rsqrt(ms + EPS) * g
            kv_s[t] = kv.astype(bf16)
            kvT_s[t] = kv.T.astype(bf16)
            kidx_s[t] = kwk_ref[t * KEY_TILE:(t + 1) * KEY_TILE, :D_IDX]

    nk = ((i + 1) * Q_TILE + KEY_TILE - 1) // KEY_TILE
    lane = lax.broadcasted_iota(jnp.int32, (1, Q_TILE), 1)
    sub = lax.broadcasted_iota(jnp.int32, (KEY_TILE, 1), 0)
    q_chunk = (i * Q_TILE + lane) // CHUNK

    wT = kwq_ref[...].astype(f32).T
    for h in range(H_A):
        qT_s[:, h * Q_TILE:(h + 1) * Q_TILE] = qlat_ref[:, h * D_LATENT:(h + 1) * D_LATENT].astype(f32).T.astype(bf16)
    for h in range(H_IDX):
        qiT_s[:, h * Q_TILE:(h + 1) * Q_TILE] = qidx_ref[:, h * D_IDX:(h + 1) * D_IDX].astype(f32).T.astype(bf16)

    n_pairs = (nk + 1) // 2

    def score_pair(p, carry):
        for t in (2 * p, 2 * p + 1):
            d_all = jnp.dot(kidx_s[t], qiT_s[...], preferred_element_type=f32)
            acc = jnp.zeros((KEY_TILE, Q_TILE), f32)
            for h in range(H_IDX):
                d = d_all[:, h * Q_TILE:(h + 1) * Q_TILE]
                acc = acc + wT[D_IDX + h:D_IDX + h + 1, :] * jnp.maximum(d, 0.0)
            score = acc * idx_scale
            k_chunk = (t * KEY_TILE + sub) // CHUNK
            key_s[t] = jnp.where(k_chunk <= q_chunk, _float_key(score), INT_MIN)
        return carry

    lax.fori_loop(0, n_pairs, score_pair, 0)


    def count(pred):
        def body(p, c):
            for t in (2 * p, 2 * p + 1):
                m = pred(key_s[t], t).astype(jnp.int32)
                c = c + jnp.sum(m.reshape(KEY_TILE // 8, 8, Q_TILE), axis=0)
            return c
        c8 = lax.fori_loop(0, n_pairs, body, jnp.zeros((8, Q_TILE), jnp.int32))
        return jnp.sum(c8, axis=0, keepdims=True)

    thr0 = jnp.where(count(lambda k, t: k >= 0) >= n_sel, jnp.int32(0), INT_MIN)
    thr0 = jnp.broadcast_to(thr0, (1, Q_TILE)).astype(jnp.int32)

    def bit_step(j, thr):
        cand = thr | (jnp.int32(1) << (jnp.int32(30) - j))
        return jnp.where(count(lambda k, t: k >= cand) >= n_sel, cand, thr)

    thr = lax.fori_loop(0, 31, bit_step, thr0)

    c_gt = count(lambda k, t: k > thr)
    c_ge = count(lambda k, t: k >= thr)
    need = n_sel - c_gt
    has_tie = jnp.max(jnp.where((c_ge > n_sel) & (thr > INT_MIN), 1, 0)) > 0

    def tie_limit():
        def step(j, m):
            cand = m | (jnp.int32(1) << (jnp.int32(14) - j))
            c = count(lambda k, t: (k == thr) & ((t * KEY_TILE + sub) < cand))
            return jnp.where(c < need, cand, m)
        return lax.fori_loop(0, 15, step, jnp.zeros((1, Q_TILE), jnp.int32))

    m_lim = lax.cond(has_tie, tie_limit, lambda: jnp.full((1, Q_TILE), 2 ** 30, jnp.int32))

    def bias_pair(p, carry):
        for t in (2 * p, 2 * p + 1):
            k = key_s[t]
            sel = (k > thr) | ((k == thr) & ((t * KEY_TILE + sub) <= m_lim))
            sel = sel & (k > INT_MIN)
            bias_s[t] = jnp.where(sel, 0.0, NEG_BIG).astype(f32)
        return carry

    lax.fori_loop(0, n_pairs, bias_pair, 0)

    acc_s[...] = jnp.zeros_like(acc_s)

    def att_pair(p, carry):
        m_run, l_run = carry
        kv2 = jnp.concatenate([kv_s[2 * p], kv_s[2 * p + 1]], axis=0)
        kvT2 = jnp.concatenate([kvT_s[2 * p], kvT_s[2 * p + 1]], axis=1)
        bias2 = jnp.concatenate([bias_s[2 * p], bias_s[2 * p + 1]], axis=0)
        logit = jnp.dot(kv2, qT_s[...], preferred_element_type=f32) * attn_scale
        logit = logit + jnp.concatenate([bias2] * H_A, axis=1)
        m_new = jnp.maximum(m_run, jnp.max(logit, axis=0, keepdims=True))
        alpha = jnp.exp(m_run - m_new)
        pr = jnp.exp(logit - m_new)
        l_new = alpha * l_run + jnp.sum(pr, axis=0, keepdims=True)
        acc_s[...] = alpha * acc_s[...] + jnp.dot(kvT2, pr.astype(bf16), preferred_element_type=f32)
        return m_new, l_new

    init = (jnp.full((1, hq), NEG_BIG, f32), jnp.zeros((1, hq), f32))
    _, l_fin = lax.fori_loop(0, n_pairs, att_pair, init)
    inv_l = 1.0 / l_fin
    for h in range(H_A):
        sl = slice(h * Q_TILE, (h + 1) * Q_TILE)
        o_lat = (acc_s[:, sl] * inv_l[:, sl]).T
        o_ref[:, h * DH_A:(h + 1) * DH_A] = jnp.dot(
            o_lat.astype(bf16), wuv_ref[h], preferred_element_type=f32).astype(o_ref.dtype)


def _attn(proj3, g_kv, w_uv_bf, n_sel):
    b, s, _ = proj3.shape
    n_kt = s // KEY_TILE
    kern = functools.partial(_attn_kernel, n_sel=n_sel, n_kt=n_kt)
    return pl.pallas_call(
        kern,
        grid=(b, s // Q_TILE),
        in_specs=[
            pl.BlockSpec((None, Q_TILE, H_A * D_LATENT), lambda bi, i: (bi, i, C_QLAT // 1024)),
            pl.BlockSpec((None, Q_TILE, H_IDX * D_IDX), lambda bi, i: (bi, i, C_QIDX // 512)),
            pl.BlockSpec((None, Q_TILE, LANES), lambda bi, i: (bi, i, C_KW // LANES)),
            pl.BlockSpec((None, s, LANES), lambda bi, i: (bi, 0, C_CKV // LANES)),
            pl.BlockSpec((None, s, LANES), lambda bi, i: (bi, 0, C_KW // LANES)),
            pl.BlockSpec((1, D_LATENT), lambda bi, i: (0, 0)),
            pl.BlockSpec((H_A, D_LATENT, DH_A), lambda bi, i: (0, 0, 0)),
        ],
        out_specs=pl.BlockSpec((None, Q_TILE, D_BRANCH), lambda bi, i: (bi, i, 0)),
        out_shape=jax.ShapeDtypeStruct((b, s, D_BRANCH), bf16),
        scratch_shapes=[
            pltpu.VMEM((n_kt, KEY_TILE, D_LATENT), bf16),
            pltpu.VMEM((n_kt, D_LATENT, KEY_TILE), bf16),
            pltpu.VMEM((n_kt, KEY_TILE, D_IDX), bf16),
            pltpu.VMEM((n_kt, KEY_TILE, Q_TILE), jnp.int32),
            pltpu.VMEM((n_kt, KEY_TILE, Q_TILE), f32),
            pltpu.VMEM((D_LATENT, H_A * Q_TILE), bf16),
            pltpu.VMEM((D_IDX, H_IDX * Q_TILE), bf16),
            pltpu.VMEM((D_LATENT, H_A * Q_TILE), f32),
        ],
        compiler_params=_cparams(("parallel", "arbitrary")),
        name="attn",
    )(proj3, proj3, proj3, proj3, proj3, g_kv, w_uv_bf)


def _ret_kernel(q_ref, k_ref, v_ref, gr_ref, cos_ref, sin_ref, din_ref, dq_ref, dk_ref, dc_ref,
                gret_ref, o_ref, state_s):
    @pl.when(pl.program_id(1) == 0)
    def _():
        state_s[...] = jnp.zeros_like(state_s)

    cos = cos_ref[...]
    sin = sin_ref[...]

    def rot(x):
        return x * cos + pltpu.roll(x, DK_R // 2, axis=1) * sin

    for h in range(H_R):
        sl = slice(h * DK_R, (h + 1) * DK_R)
        q = rot(q_ref[:, sl].astype(f32)).astype(bf16)
        kf = rot(k_ref[:, sl].astype(f32)) * (DK_R ** -0.5)
        k = kf.astype(bf16)
        v = v_ref[:, sl]
        inner = lax.dot_general(q, k, (((1,), (1,)), ((), ())), preferred_element_type=f32) * din_ref[h]
        o = jnp.dot(inner.astype(bf16), v, preferred_element_type=f32)
        st = state_s[h]
        o = o + jnp.dot(q, st.astype(bf16), preferred_element_type=f32) * dq_ref[h]
        kd = (kf * dk_ref[h]).astype(bf16)
        state_s[h] = st * dc_ref[h] + jnp.dot(kd.T, v, preferred_element_type=f32)
        mu = jnp.mean(o, axis=-1, keepdims=True)
        var = jnp.mean(jnp.square(o - mu), axis=-1, keepdims=True)
        y = (o - mu) * lax.rsqrt(var + EPS) * gret_ref[:, sl]
        gate = gr_ref[:, sl].astype(f32)
        o_ref[:, sl] = (gate * jax.nn.sigmoid(gate) * y).astype(o_ref.dtype)


def _ret(proj3, cos_t, sin_t, d_in, d_q, d_k, d_c, g_ret):
    b, s, _ = proj3.shape
    c = RET_CHUNK
    w = H_R * DK_R

    def col(off):
        return pl.BlockSpec((None, c, w), lambda bi, ci: (bi, ci, off // w))

    return pl.pallas_call(
        _ret_kernel,
        grid=(b, s // c),
        in_specs=[
            col(C_QR), col(C_KR), col(C_VR), col(C_GR),
            pl.BlockSpec((c, DK_R), lambda bi, ci: (ci, 0)),
            pl.BlockSpec((c, DK_R), lambda bi, ci: (ci, 0)),
            pl.BlockSpec((H_R, c, c), lambda bi, ci: (0, 0, 0)),
            pl.BlockSpec((H_R, c, DK_R), lambda bi, ci: (0, 0, 0)),
            pl.BlockSpec((H_R, c, DK_R), lambda bi, ci: (0, 0, 0)),
            pl.BlockSpec((H_R, 1, DK_R), lambda bi, ci: (0, 0, 0)),
            pl.BlockSpec((1, w), lambda bi, ci: (0, 0)),
        ],
        out_specs=pl.BlockSpec((None, c, w), lambda bi, ci: (bi, ci, 0)),
        out_shape=jax.ShapeDtypeStruct((b, s, w), bf16),
        scratch_shapes=[pltpu.VMEM((H_R, DK_R, DV_R), f32)],
        compiler_params=_cparams(("parallel", "arbitrary")),
        name="ret",
    )(proj3, proj3, proj3, proj3, cos_t, sin_t, d_in, d_q, d_k, d_c, g_ret)


def _retention_tables(s):
    c = RET_CHUNK
    half = DK_R // 2
    freq = ROPE_BASE ** (-jnp.arange(half, dtype=f32) / half)
    ang = jnp.arange(s, dtype=f32)[:, None] * freq[None, :]
    cos = jnp.cos(ang)
    sin = jnp.sin(ang)
    cos_t = jnp.concatenate([cos, cos], axis=-1)
    sin_t = jnp.concatenate([-sin, sin], axis=-1)
    log_gamma = jnp.log1p(-jnp.exp2(-5.0 - jnp.arange(H_R, dtype=f32)))
    n = jnp.arange(c, dtype=f32)
    diff = n[:, None] - n[None, :]
    d_in = jnp.where(diff >= 0, jnp.exp(log_gamma[:, None, None] * jnp.maximum(diff, 0.0)), 0.0)
    d_q = jnp.broadcast_to(jnp.exp(log_gamma[:, None] * (n + 1.0))[:, :, None], (H_R, c, DK_R))
    d_k = jnp.broadcast_to(jnp.exp(log_gamma[:, None] * (c - 1.0 - n))[:, :, None], (H_R, c, DK_R))
    d_c = jnp.broadcast_to(jnp.exp(log_gamma * c)[:, None, None], (H_R, 1, DK_R))
    return cos_t, sin_t, d_in, d_q, d_k, d_c


MIX_CHUNK = 512


def _mix_kernel(oa_ref, ob_ref, wb_ref, ga0_ref, ga1_ref, gb0_ref, gb1_ref, o_ref):
    oa = oa_ref[...]
    ob = ob_ref[...]
    half = ga0_ref.shape[1]
    for c in range(0, o_ref.shape[1], MIX_CHUNK):
        ga_ref, gb_ref, off = (ga0_ref, gb0_ref, c) if c < half else (ga1_ref, gb1_ref, c - half)
        a = jnp.dot(oa, wb_ref[0, :, c:c + MIX_CHUNK], preferred_element_type=f32)
        b = jnp.dot(ob, wb_ref[1, :, c:c + MIX_CHUNK], preferred_element_type=f32)
        ga = jax.nn.sigmoid(ga_ref[:, off:off + MIX_CHUNK].astype(f32))
        gb = jax.nn.sigmoid(gb_ref[:, off:off + MIX_CHUNK].astype(f32))
        o_ref[:, c:c + MIX_CHUNK] = (ga * a + gb * b).astype(o_ref.dtype)


def _mix(o_a, o_b, w_branch_bf, proj, tm):
    t = o_a.shape[0]
    d = w_branch_bf.shape[2]
    half = d // 2
    assert C_GBR % half == 0 and half % MIX_CHUNK == 0

    def gate(k):
        return pl.BlockSpec((tm, half), lambda i: (i, C_GBR // half + k))

    return pl.pallas_call(
        _mix_kernel,
        grid=(t // tm,),
        in_specs=[
            pl.BlockSpec((tm, D_BRANCH), lambda i: (i, 0)),
            pl.BlockSpec((tm, D_BRANCH), lambda i: (i, 0)),
            pl.BlockSpec((N_BRANCH, D_BRANCH, d), lambda i: (0, 0, 0)),
            gate(0), gate(1), gate(2), gate(3),
        ],
        out_specs=pl.BlockSpec((tm, d), lambda i: (i, 0)),
        out_shape=jax.ShapeDtypeStruct((t, d), bf16),
        compiler_params=_cparams(("parallel",)),
        name="mix",
    )(o_a, o_b, w_branch_bf, proj, proj, proj, proj)


def _pack_rows(v):
    n = v.shape[1] // 2
    r = pltpu.bitcast(v.astype(bf16).astype(f32), jnp.uint32)
    w = (r[:, :n] >> 16) | (r[:, n:] & jnp.uint32(0xFFFF0000))
    return pltpu.einshape("r(ab)->rab", w, b=LANES)


def _unpack_rows(p):
    w = pltpu.einshape("rab->r(ab)", p)
    lo = pltpu.bitcast(w << 16, f32)
    hi = pltpu.bitcast(w & jnp.uint32(0xFFFF0000), f32)
    return lo, hi


def _split_bf16(a):
    hi = a.astype(bf16)
    lo = (a - hi.astype(f32)).astype(bf16)
    return hi, lo


def _outproj_kernel(mixed_ref, x_ref, wo_ref, g_ref, wr_ref, br_ref,
                    h_ref, xn_ref, eid_ref, gate_ref, cnt_ref):
    h = x_ref[...] + jnp.dot(mixed_ref[...], wo_ref[...], preferred_element_type=f32)
    h_ref[...] = h
    ms = jnp.mean(h * h, axis=-1, keepdims=True)
    xn = h * lax.rsqrt(ms + EPS) * g_ref[...]
    xn_ref[...] = _pack_rows(xn)

    x_hi, x_lo = _split_bf16(xn)
    hh_hl = jnp.dot(x_hi, wr_ref[...], preferred_element_type=f32)
    logit = (hh_hl[:, :LANES] + hh_hl[:, LANES:]
             + jnp.dot(x_lo, wr_ref[:, :LANES], preferred_element_type=f32)) + br_ref[...]

    lane = lax.broadcasted_iota(jnp.int32, logit.shape, 1)
    lanef = lane.astype(f32)
    neg = -jnp.inf

    def first_argmax(v, m):
        return jnp.min(jnp.where(v == m, lanef, float(LANES)), axis=-1, keepdims=True)

    lg = jnp.where(lane < N_GROUPS, logit, neg)
    mg = jnp.max(lg, axis=-1, keepdims=True)
    p_grp = 1.0 / jnp.sum(jnp.exp(lg - mg), axis=-1, keepdims=True)
    grp = first_argmax(lg, mg).astype(jnp.int32)

    e_lane = lane - N_GROUPS
    in_grp = (e_lane >= 0) & (e_lane < N_EXPERTS) & ((e_lane // EXP_PER_GROUP) == grp)
    le = jnp.where(in_grp, logit, neg)
    m1 = jnp.max(le, axis=-1, keepdims=True)
    i1 = first_argmax(le, m1)
    le2 = jnp.where(lanef == i1, neg, le)
    m2 = jnp.max(le2, axis=-1, keepdims=True)
    i2 = first_argmax(le2, m2)
    e2 = jnp.exp(m2 - m1)
    g1 = p_grp / (1.0 + e2)
    g2 = p_grp * e2 / (1.0 + e2)

    eid = jnp.where(lane == 0, i1, jnp.where(lane == 1, i2, float(N_GROUPS))) - float(N_GROUPS)
    eid_ref[...] = eid.astype(jnp.int32).T[:8, :]
    gate_ref[...] = jnp.where(lane == 0, g1, jnp.where(lane == 1, g2, 0.0))

    @pl.when(pl.program_id(0) == 0)
    def _():
        cnt_ref[...] = jnp.zeros_like(cnt_ref)

    half = pl.program_id(0) // (pl.num_programs(0) // ROUTE_HALVES)
    sub8 = lax.broadcasted_iota(jnp.int32, (8, LANES), 0)
    for s, idx in enumerate((i1, i2)):
        c = jnp.sum((lanef == idx).astype(jnp.int32), axis=0, keepdims=True)
        cnt_ref[...] += jnp.where(sub8 == s * ROUTE_HALVES + half, c, 0)


def _outproj(mixed, x2, w_out_bf, g_ffn, wr_hi_lo, b_r, tm):
    t, d = x2.shape
    row = lambda i: (i, 0)
    fixed = lambda i: (0, 0)
    return pl.pallas_call(
        _outproj_kernel,
        grid=(t // tm,),
        in_specs=[
            pl.BlockSpec((tm, d), row),
            pl.BlockSpec((tm, d), row),
            pl.BlockSpec((d, d), fixed),
            pl.BlockSpec((1, d), fixed),
            pl.BlockSpec((d, 2 * LANES), fixed),
            pl.BlockSpec((1, LANES), fixed),
        ],
        out_specs=[
            pl.BlockSpec((tm, d), row),
            pl.BlockSpec((tm, d // (2 * LANES), LANES), lambda i: (i, 0, 0)),
            pl.BlockSpec((8, tm), lambda i: (0, i)),
            pl.BlockSpec((tm, LANES), row),
            pl.BlockSpec((8, LANES), fixed),
        ],
        out_shape=[
            jax.ShapeDtypeStruct((t, d), f32),
            jax.ShapeDtypeStruct((t, d // (2 * LANES), LANES), jnp.uint32),
            jax.ShapeDtypeStruct((8, t), jnp.int32),
            jax.ShapeDtypeStruct((t, LANES), f32),
            jax.ShapeDtypeStruct((8, LANES), jnp.int32),
        ],
        compiler_params=_cparams(("arbitrary",)),
        name="outproj",
    )(mixed, x2, w_out_bf, g_ffn, wr_hi_lo, b_r)


ISSUE_UNROLL = 8
CAST_ROWS = 256
GATHER_BUFS = 4


def _experts_kernel(bexp_ref, nexp_ref, rpack_ref, nused_ref, xn_hbm, wg_hbm, wu_hbm, wd_hbm, yt_hbm,
                    xbuf, ybuf, wg_st, wu_st, wd_st, wg_bf, wu_bf, wd_bf, gsem, ssem, wsem, *, n_tok):
    j = pl.program_id(0)
    n_used = nused_ref[0]
    slot = j % 2
    tok_bits = (n_tok - 1).bit_length()

    def rows_of(blk, fn):
        base = blk * ROW_BLOCK

        def body(k, c):
            r0 = pl.multiple_of(k * ISSUE_UNROLL, ISSUE_UNROLL)
            for u in range(ISSUE_UNROLL):
                fn(r0 + u, rpack_ref[base + r0 + u], 1)
            return c

        lax.fori_loop(0, ROW_BLOCK // ISSUE_UNROLL, body, 0)

    def start_gathers(blk, sl):
        def one(r, packed, queue):
            tok = packed & ((1 << tok_bits) - 1)
            pltpu.make_async_copy(xn_hbm.at[tok], xbuf.at[sl, r], gsem.at[sl]).start(priority=queue)
        rows_of(blk, one)

    def start_scatters(blk, sl):
        def one(r, packed, queue):
            row = lax.shift_right_logical(packed, tok_bits)
            pltpu.make_async_copy(ybuf.at[sl, r], yt_hbm.at[row], ssem.at[sl]).start(priority=queue)
        rows_of(blk, one)

    def wait_gathers(sl):
        pltpu.make_async_copy(xn_hbm.at[pl.ds(0, ROW_BLOCK)], xbuf.at[sl], gsem.at[sl]).wait()

    def wait_scatters(sl):
        pltpu.make_async_copy(ybuf.at[sl], yt_hbm.at[pl.ds(0, ROW_BLOCK)], ssem.at[sl]).wait()

    staged = ((wg_hbm, wg_st, wg_bf), (wu_hbm, wu_st, wu_bf), (wd_hbm, wd_st, wd_bf))

    def start_weights(e):
        for q, (src, st, _) in enumerate(staged):
            pltpu.make_async_copy(src.at[e], st, wsem.at[q]).start()

    def wait_and_cast_weights():
        for q, (src, st, dst) in enumerate(staged):
            pltpu.make_async_copy(src.at[0], st, wsem.at[q]).wait()

            def cast(c, carry, st=st, dst=dst):
                r = pl.multiple_of(c * CAST_ROWS, CAST_ROWS)
                dst[pl.ds(r, CAST_ROWS), :] = st[pl.ds(r, CAST_ROWS), :].astype(bf16)
                return carry

            lax.fori_loop(0, st.shape[0] // CAST_ROWS, cast, 0)

    @pl.when(j == 0)
    def _():
        start_weights(bexp_ref[0])
        for b in range(GATHER_BUFS - 1):
            @pl.when(b < n_used)
            def _(b=b):
                start_gathers(b, b)
        ybuf[...] = jnp.zeros_like(ybuf)
        for sl in range(2):
            spare = yt_hbm.at[pl.ds(2 * n_tok + sl * ROW_BLOCK, ROW_BLOCK)]
            pltpu.make_async_copy(ybuf.at[sl], spare, ssem.at[sl]).start()
        for sl in range(2):
            wait_scatters(sl)

    @pl.when(j < n_used)
    def _():
        e = bexp_ref[j]

        @pl.when((j == 0) | (bexp_ref[jnp.maximum(j - 1, 0)] != e))
        def _():
            wait_and_cast_weights()

            @pl.when(nexp_ref[j] >= 0)
            def _():
                start_weights(nexp_ref[j])

        gslot = j % GATHER_BUFS
        wait_gathers(gslot)

        @pl.when(j + GATHER_BUFS - 1 < n_used)
        def _():
            start_gathers(j + GATHER_BUFS - 1, (j + GATHER_BUFS - 1) % GATHER_BUFS)

        @pl.when(j >= 2)
        def _():
            wait_scatters(slot)

        lo, hi = _unpack_rows(xbuf[gslot])
        xb = jnp.concatenate([lo, hi], axis=1).astype(bf16)
        g = jnp.dot(xb, wg_bf[...], preferred_element_type=f32)
        u = jnp.dot(xb, wu_bf[...], preferred_element_type=f32)
        hm = (g * jax.nn.sigmoid(g) * u).astype(bf16)
        ybuf[slot] = _pack_rows(jnp.dot(hm, wd_bf[...], preferred_element_type=f32))
        start_scatters(j, slot)

        @pl.when(j == n_used - 1)
        def _():
            wait_scatters(slot)

            @pl.when(j >= 1)
            def _():
                wait_scatters(1 - slot)


def _experts(block_expert, next_expert, row_pack, n_used, xn_packed, w_gate, w_up, w_down):
    n_rows = row_pack.shape[0]
    n_tok = xn_packed.shape[0]
    tile = xn_packed.shape[1:]
    _, d, f = w_gate.shape
    assert d % CAST_ROWS == 0 and f % CAST_ROWS == 0
    any_space = pl.BlockSpec(memory_space=pl.ANY)
    grid_spec = pltpu.PrefetchScalarGridSpec(
        num_scalar_prefetch=4,
        grid=(n_rows // ROW_BLOCK,),
        in_specs=[any_space, any_space, any_space, any_space],
        out_specs=any_space,
        scratch_shapes=[pltpu.VMEM((GATHER_BUFS, ROW_BLOCK) + tile, jnp.uint32),
                        pltpu.VMEM((2, ROW_BLOCK) + tile, jnp.uint32),
                        pltpu.VMEM((d, f), f32), pltpu.VMEM((d, f), f32), pltpu.VMEM((f, d), f32),
                        pltpu.VMEM((d, f), bf16), pltpu.VMEM((d, f), bf16), pltpu.VMEM((f, d), bf16),
                        pltpu.SemaphoreType.DMA((GATHER_BUFS,)), pltpu.SemaphoreType.DMA((2,)),
                        pltpu.SemaphoreType.DMA((3,))],
    )
    return pl.pallas_call(
        functools.partial(_experts_kernel, n_tok=n_tok),
        grid_spec=grid_spec,
        out_shape=jax.ShapeDtypeStruct((2 * n_tok + 2 * ROW_BLOCK,) + tile, jnp.uint32),
        compiler_params=_cparams(("arbitrary",)),
        name="experts",
    )(block_expert, next_expert, row_pack, n_used, xn_packed, w_gate, w_up, w_down)


def _combine_kernel(h_ref, y0_ref, y1_ref, gate_ref, g_ref, o_ref):
    gate = gate_ref[...]
    y0 = jnp.concatenate(_unpack_rows(y0_ref[...]), axis=1)
    y1 = jnp.concatenate(_unpack_rows(y1_ref[...]), axis=1)
    hh = h_ref[...] + gate[:, 0:1] * y0 + gate[:, 1:2] * y1
    ms = jnp.mean(hh * hh, axis=-1, keepdims=True)
    o_ref[...] = hh * lax.rsqrt(ms + EPS) * g_ref[...]


def _combine(h, yt, gate, g_final, tm):
    t, d = h.shape
    nt = t // tm
    return pl.pallas_call(
        _combine_kernel,
        grid=(nt,),
        in_specs=[
            pl.BlockSpec((tm, d), lambda i: (i, 0)),
            pl.BlockSpec((tm,) + yt.shape[1:], lambda i: (i, 0, 0)),
            pl.BlockSpec((tm,) + yt.shape[1:], lambda i: (nt + i, 0, 0)),
            pl.BlockSpec((tm, LANES), lambda i: (i, 0)),
            pl.BlockSpec((1, d), lambda i: (0, 0)),
        ],
        out_specs=pl.BlockSpec((tm, d), lambda i: (i, 0)),
        out_shape=jax.ShapeDtypeStruct((t, d), f32),
        compiler_params=_cparams(("parallel",)),
        name="combine",
    )(h, yt, yt, gate, g_final)


def _route_kernel(eid_ref, cnt_ref, rpack_ref, bexp_ref, nexp_ref, nused_ref, *cur_refs, n_tok, n_blocks):
    tok_bits = (n_tok - 1).bit_length()
    chunk = 2 * n_tok // ROUTE_CHAINS

    def no_next(k, carry):
        nexp_ref[k] = -1
        return carry

    lax.fori_loop(0, n_blocks, no_next, 0)

    def per_expert(e, carry):
        blk, prev_blk, prev_nb = carry
        start = blk * ROW_BLOCK
        run = start
        for c in range(ROUTE_CHAINS):
            cur_refs[c][e] = run
            run = run + cnt_ref[c * N_EXPERTS + e]
        nb = (run - start + ROW_BLOCK - 1) // ROW_BLOCK

        def set_block(k, c):
            bexp_ref[blk + k] = e
            return c

        lax.fori_loop(0, nb, set_block, 0)

        def set_next(k, c):
            nexp_ref[prev_blk + k] = e
            return c

        lax.fori_loop(0, jnp.where(nb > 0, prev_nb, 0), set_next, 0)

        def set_pad(r, c):
            rpack_ref[r] = (2 * n_tok + (r & (2 * ROW_BLOCK - 1))) << tok_bits
            return c

        lax.fori_loop(run, start + nb * ROW_BLOCK, set_pad, 0)
        return blk + nb, jnp.where(nb > 0, blk, prev_blk), jnp.where(nb > 0, nb, prev_nb)

    n_used, _, _ = lax.fori_loop(0, N_EXPERTS, per_expert, (0, 0, 0))
    nused_ref[0] = n_used

    def tail_block(k, carry):
        bexp_ref[k] = N_EXPERTS - 1
        return carry

    lax.fori_loop(n_used, n_blocks, tail_block, 0)

    def tail_row(r, carry):
        rpack_ref[r] = (2 * n_tok + (r & (2 * ROW_BLOCK - 1))) << tok_bits
        return carry

    lax.fori_loop(n_used * ROW_BLOCK, n_blocks * ROW_BLOCK, tail_row, 0)

    def place(i, carry):
        for c in range(ROUTE_CHAINS):
            a = c * chunk + i
            e = eid_ref[a]
            p = cur_refs[c][e]
            cur_refs[c][e] = p + 1
            rpack_ref[p] = (a << tok_bits) | (a - (c * chunk // n_tok) * n_tok)
        return carry

    lax.fori_loop(0, chunk, place, 0)


def _route(eid_flat, counts, n_tok):
    n_asg = eid_flat.shape[0]
    n_rows = -(-(n_asg + N_EXPERTS * (ROW_BLOCK - 1)) // ROW_BLOCK) * ROW_BLOCK
    n_blocks = n_rows // ROW_BLOCK
    smem = pl.BlockSpec(memory_space=pltpu.SMEM)
    return pl.pallas_call(
        functools.partial(_route_kernel, n_tok=n_tok, n_blocks=n_blocks),
        in_specs=[smem, smem],
        out_specs=[smem, smem, smem, smem],
        out_shape=[jax.ShapeDtypeStruct((n_rows,), jnp.int32),
                   jax.ShapeDtypeStruct((n_blocks,), jnp.int32),
                   jax.ShapeDtypeStruct((n_blocks,), jnp.int32),
                   jax.ShapeDtypeStruct((1,), jnp.int32)],
        scratch_shapes=[pltpu.SMEM((N_EXPERTS,), jnp.int32)] * ROUTE_CHAINS,
        name="route",
    )(eid_flat, counts)


def _pick(n, prefs):
    for p in prefs:
        if n % p == 0:
            return p
    return n


def kernel(x, g_mix_norm, w_in, g_kv, w_uv, g_ret, w_branch, w_out, g_ffn_norm, w_router_group,
           b_router_group, w_router_expert, b_router_expert, w_expert_gate, w_expert_up,
           w_expert_down, g_final):
    b, s, d = x.shape
    t = b * s
    depth = w_in.shape[0]
    n_sel = min(TOPK_MAX, s // 4)
    assert s % RET_CHUNK == 0 and s % Q_TILE == 0

    cos_t, sin_t, d_in, d_q, d_k, d_c = _retention_tables(s)
    h2 = x.reshape(t, d)
    for l in range(depth):
        w_pt = _relayout(jnp.swapaxes(w_in, 1, 2)[l])

        proj = _proj(h2, g_mix_norm[l].reshape(1, d), w_pt,_pick(t, (512, 256)), 3328)
        proj3 = proj.reshape(b, s, D_IN_P)

        o_a = _attn(proj3, g_kv[l].reshape(1, D_LATENT), w_uv[l].astype(bf16), n_sel)
        o_b = _ret(proj3, cos_t, sin_t, d_in, d_q, d_k, d_c, g_ret[l].reshape(1, H_R * DV_R))

        mixed = _mix(o_a.reshape(t, D_BRANCH), o_b.reshape(t, D_BRANCH), w_branch[l].astype(bf16),
                     proj, _pick(t, (512, 256)))

        w_r = jnp.concatenate([w_router_group[l], w_router_expert[l],
                               jnp.zeros((d, LANES - N_GROUPS - N_EXPERTS), f32)], axis=1)
        b_r = jnp.concatenate([b_router_group[l], b_router_expert[l],
                               jnp.zeros((LANES - N_GROUPS - N_EXPERTS,), f32)]).reshape(1, LANES)
        wr_hi = w_r.astype(bf16)
        wr_lo = (w_r - wr_hi.astype(f32)).astype(bf16)
        h2, xn, eid_t, gate, cnt = _outproj(mixed, h2, w_out[l].astype(bf16), g_ffn_norm[l].reshape(1, d),
                                            jnp.concatenate([wr_hi, wr_lo], axis=1), b_r, _pick(t, (512, 256)))

        row_pack, block_expert, next_expert, n_used = _route(
            eid_t[:2].reshape(-1), cnt[:ROUTE_CHAINS, N_GROUPS:N_GROUPS + N_EXPERTS].reshape(-1), t)
        yt = _experts(block_expert, next_expert, row_pack, n_used, xn,
                      w_expert_gate[l], w_expert_up[l], w_expert_down[l])
        assert depth == 1
        h2 = _combine(h2, yt, gate, g_final.reshape(1, d), _pick(t, (256,)))
    return h2.reshape(b, s, d)
```

```python
import functools

import jax
import jax.numpy as jnp
import numpy as np
from jax import lax
from jax.experimental import pallas as pl
from jax.experimental.pallas import tpu as pltpu

EPS = 1e-6
CHUNK = 64
H_A = 8
D_LATENT = 128
DH_A = 128
H_IDX = 8
D_IDX = 64
TOPK_MAX = 256
H_R = 8
DK_R = 128
DV_R = 128
ROPE_BASE = 10000.0
D_BRANCH = 1024
N_BRANCH = 2
N_GROUPS = 4
EXP_PER_GROUP = 8
N_EXPERTS = N_GROUPS * EXP_PER_GROUP
D_EXPERT = 1024

LANES = 128
KEY_TILE = 256
Q_TILE = 512
RET_CHUNK = 256
ROW_BLOCK = 256
ROUTE_HALVES = 4
ROUTE_CHAINS = 2 * ROUTE_HALVES
VMEM_LIMIT = 56 * 1024 * 1024

C_QLAT = 0
C_GBR = 1024
C_QR = 5120
C_KR = 6144
C_VR = 7168
C_GR = 8192
C_QIDX = 9216
C_CKV = 9728
C_KW = 9856
D_IN_P = 9984

INT_MIN = np.int32(-2 ** 31)
NEG_BIG = -1e30

bf16 = jnp.bfloat16
f32 = jnp.float32


def _cparams(sem):
    return pltpu.CompilerParams(dimension_semantics=sem, vmem_limit_bytes=VMEM_LIMIT)


RELAYOUT_ROWS = 128
RELAYOUT_GROUP = 6


def _relayout_table(d_model):
    sp = np.cumsum([0, H_A * D_LATENT, D_LATENT, H_IDX * D_IDX, D_IDX, H_IDX,
                    H_R * DK_R, H_R * DK_R, H_R * DV_R, H_R * DV_R, N_BRANCH * d_model])
    order = [(sp[0], C_GBR - C_QLAT), (sp[9], C_QR - C_GBR), (sp[5], C_KR - C_QR), (sp[6], C_VR - C_KR),
             (sp[7], C_GR - C_VR), (sp[8], C_QIDX - C_GR), (sp[2], C_CKV - C_QIDX), (sp[1], C_KW - C_CKV),
             (sp[3], D_IN_P - C_KW)]
    rows = [int(off) + k * RELAYOUT_ROWS for off, height in order for k in range(height // RELAYOUT_ROWS)]
    assert len(rows) == D_IN_P // RELAYOUT_ROWS and all(r % 8 == 0 for r in rows)
    return np.asarray(rows, np.int32)


def _relayout_kernel(row_ref, wt_hbm, o_ref, buf, sem):
    j = pl.program_id(0)
    slot = j % 2

    def fetch(step, sl, k):
        src = wt_hbm.at[pl.ds(pl.multiple_of(row_ref[step * RELAYOUT_GROUP + k], 8), RELAYOUT_ROWS)]
        return pltpu.make_async_copy(src, buf.at[sl, pl.ds(k * RELAYOUT_ROWS, RELAYOUT_ROWS)], sem.at[sl])

    @pl.when(j == 0)
    def _():
        for k in range(RELAYOUT_GROUP):
            fetch(0, 0, k).start()

    @pl.when(j + 1 < pl.num_programs(0))
    def _():
        for k in range(RELAYOUT_GROUP):
            fetch(j + 1, 1 - slot, k).start()

    for k in range(RELAYOUT_GROUP):
        fetch(j, slot, k).wait()
    o_ref[...] = buf[slot].astype(o_ref.dtype)


def _relayout(wt):
    d_in, d_model = wt.shape
    rows = _relayout_table(d_model)
    step_rows = RELAYOUT_GROUP * RELAYOUT_ROWS
    assert int(rows.max()) + RELAYOUT_ROWS <= d_in and D_IN_P % step_rows == 0
    grid_spec = pltpu.PrefetchScalarGridSpec(
        num_scalar_prefetch=1,
        grid=(D_IN_P // step_rows,),
        in_specs=[pl.BlockSpec(memory_space=pl.ANY)],
        out_specs=pl.BlockSpec((step_rows, d_model), lambda j, row: (j, 0)),
        scratch_shapes=[pltpu.VMEM((2, step_rows, d_model), f32), pltpu.SemaphoreType.DMA((2,))],
    )
    return pl.pallas_call(
        _relayout_kernel,
        grid_spec=grid_spec,
        out_shape=jax.ShapeDtypeStruct((D_IN_P, d_model), bf16),
        compiler_params=_cparams(("arbitrary",)),
        name="relayout",
    )(jnp.asarray(rows), wt)


def _proj_kernel(x_ref, g_ref, w_ref, o_ref, xn_ref):
    @pl.when(pl.program_id(1) == 0)
    def _():
        x = x_ref[...]
        ms = jnp.mean(x * x, axis=-1, keepdims=True)
        xn_ref[...] = (x * lax.rsqrt(ms + EPS) * g_ref[...]).astype(bf16)

    o_ref[...] = lax.dot_general(xn_ref[...], w_ref[...], (((1,), (1,)), ((), ())),
                                 preferred_element_type=f32).astype(o_ref.dtype)


def _proj(x2, g, w_pt, tm, tn):
    t, d = x2.shape
    n = w_pt.shape[0]
    return pl.pallas_call(
        _proj_kernel,
        grid=(t // tm, n // tn),
        in_specs=[
            pl.BlockSpec((tm, d), lambda i, j: (i, 0)),
            pl.BlockSpec((1, d), lambda i, j: (0, 0)),
            pl.BlockSpec((tn, d), lambda i, j: (j, 0)),
        ],
        out_specs=pl.BlockSpec((tm, tn), lambda i, j: (i, j)),
        out_shape=jax.ShapeDtypeStruct((t, n), bf16),
        scratch_shapes=[pltpu.VMEM((tm, d), bf16)],
        compiler_params=_cparams(("parallel", "arbitrary")),
        name="proj",
    )(x2, g, w_pt)


def _float_key(s):
    bits = pltpu.bitcast(s, jnp.int32)
    key = bits ^ ((bits >> 31) & jnp.int32(0x7FFFFFFF))
    return jnp.where(s == 0.0, jnp.int32(0), key)


def _attn_kernel(qlat_ref, qidx_ref, kwq_ref, ckv_ref, kwk_ref, gkv_ref, wuv_ref, o_ref,
                 kv_s, kvT_s, kidx_s, key_s, bias_s, qT_s, qiT_s, acc_s, *, n_sel, n_kt):
    i = pl.program_id(1)
    idx_scale = (H_IDX ** -0.5) * (D_IDX ** -0.5)
    attn_scale = D_LATENT ** -0.5
    hq = H_A * Q_TILE

    @pl.when(i == 0)
    def _():
        g = gkv_ref[...]
        for t in range(n_kt):
            c = ckv_ref[t * KEY_TILE:(t + 1) * KEY_TILE, :].astype(f32)
            ms = jnp.mean(c * c, axis=-1, keepdims=True)
            kv = c * lax.rsqrt(ms + EPS) * g
            kv_s[t] = kv.astype(bf16)
            kvT_s[t] = kv.T.astype(bf16)
            kidx_s[t] = kwk_ref[t * KEY_TILE:(t + 1) * KEY_TILE, :D_IDX]

    nk = ((i + 1) * Q_TILE + KEY_TILE - 1) // KEY_TILE
    lane = lax.broadcasted_iota(jnp.int32, (1, Q_TILE), 1)
    sub = lax.broadcasted_iota(jnp.int32, (KEY_TILE, 1), 0)
    q_chunk = (i * Q_TILE + lane) // CHUNK

    wT = kwq_ref[...].astype(f32).T
    for h in range(H_A):
        qT_s[:, h * Q_TILE:(h + 1) * Q_TILE] = qlat_ref[:, h * D_LATENT:(h + 1) * D_LATENT].astype(f32).T.astype(bf16)
    for h in range(H_IDX):
        qiT_s[:, h * Q_TILE:(h + 1) * Q_TILE] = qidx_ref[:, h * D_IDX:(h + 1) * D_IDX].astype(f32).T.astype(bf16)

    n_pairs = (nk + 1) // 2

    def score_pair(p, carry):
        for t in (2 * p, 2 * p + 1):
            d_all = jnp.dot(kidx_s[t], qiT_s[...], preferred_element_type=f32)
            acc = jnp.zeros((KEY_TILE, Q_TILE), f32)
            for h in range(H_IDX):
                d = d_all[:, h * Q_TILE:(h + 1) * Q_TILE]
                acc = acc + wT[D_IDX + h:D_IDX + h + 1, :] * jnp.maximum(d, 0.0)
            score = acc * idx_scale
            k_chunk = (t * KEY_TILE + sub) // CHUNK
            key_s[t] = jnp.where(k_chunk <= q_chunk, _float_key(score), INT_MIN)
        return carry

    lax.fori_loop(0, n_pairs, score_pair, 0)


    def count(pred):
        def body(p, c):
            for t in (2 * p, 2 * p + 1):
                m = pred(key_s[t], t).astype(jnp.int32)
                c = c + jnp.sum(m.reshape(KEY_TILE // 8, 8, Q_TILE), axis=0)
            return c
        c8 = lax.fori_loop(0, n_pairs, body, jnp.zeros((8, Q_TILE), jnp.int32))
        return jnp.sum(c8, axis=0, keepdims=True)

    thr0 = jnp.where(count(lambda k, t: k >= 0) >= n_sel, jnp.int32(0), INT_MIN)
    thr0 = jnp.broadcast_to(thr0, (1, Q_TILE)).astype(jnp.int32)

    def bit_step(j, thr):
        cand = thr | (jnp.int32(1) << (jnp.int32(30) - j))
        return jnp.where(count(lambda k, t: k >= cand) >= n_sel, cand, thr)

    thr = lax.fori_loop(0, 31, bit_step, thr0)

    c_gt = count(lambda k, t: k > thr)
    c_ge = count(lambda k, t: k >= thr)
    need = n_sel - c_gt
    has_tie = jnp.max(jnp.where((c_ge > n_sel) & (thr > INT_MIN), 1, 0)) > 0

    def tie_limit():
        def step(j, m):
            cand = m | (jnp.int32(1) << (jnp.int32(14) - j))
            c = count(lambda k, t: (k == thr) & ((t * KEY_TILE + sub) < cand))
            return jnp.where(c < need, cand, m)
        return lax.fori_loop(0, 15, step, jnp.zeros((1, Q_TILE), jnp.int32))

    m_lim = lax.cond(has_tie, tie_limit, lambda: jnp.full((1, Q_TILE), 2 ** 30, jnp.int32))

    def bias_pair(p, carry):
        for t in (2 * p, 2 * p + 1):
            k = key_s[t]
            sel = (k > thr) | ((k == thr) & ((t * KEY_TILE + sub) <= m_lim))
            sel = sel & (k > INT_MIN)
            bias_s[t] = jnp.where(sel, 0.0, NEG_BIG).astype(f32)
        return carry

    lax.fori_loop(0, n_pairs, bias_pair, 0)

    acc_s[...] = jnp.zeros_like(acc_s)

    def att_pair(p, carry):
        m_run, l_run = carry
        kv2 = jnp.concatenate([kv_s[2 * p], kv_s[2 * p + 1]], axis=0)
        kvT2 = jnp.concatenate([kvT_s[2 * p], kvT_s[2 * p + 1]], axis=1)
        bias2 = jnp.concatenate([bias_s[2 * p], bias_s[2 * p + 1]], axis=0)
        logit = jnp.dot(kv2, qT_s[...], preferred_element_type=f32) * attn_scale
        logit = logit + jnp.concatenate([bias2] * H_A, axis=1)
        m_new = jnp.maximum(m_run, jnp.max(logit, axis=0, keepdims=True))
        alpha = jnp.exp(m_run - m_new)
        pr = jnp.exp(logit - m_new)
        l_new = alpha * l_run + jnp.sum(pr, axis=0, keepdims=True)
        acc_s[...] = alpha * acc_s[...] + jnp.dot(kvT2, pr.astype(bf16), preferred_element_type=f32)
        return m_new, l_new

    init = (jnp.full((1, hq), NEG_BIG, f32), jnp.zeros((1, hq), f32))
    _, l_fin = lax.fori_loop(0, n_pairs, att_pair, init)
    inv_l = 1.0 / l_fin
    for h in range(H_A):
        sl = slice(h * Q_TILE, (h + 1) * Q_TILE)
        o_lat = (acc_s[:, sl] * inv_l[:, sl]).T
        o_ref[:, h * DH_A:(h + 1) * DH_A] = jnp.dot(
            o_lat.astype(bf16), wuv_ref[h], preferred_element_type=f32).astype(o_ref.dtype)


def _attn(proj3, g_kv, w_uv_bf, n_sel):
    b, s, _ = proj3.shape
    n_kt = s // KEY_TILE
    kern = functools.partial(_attn_kernel, n_sel=n_sel, n_kt=n_kt)
    return pl.pallas_call(
        kern,
        grid=(b, s // Q_TILE),
        in_specs=[
            pl.BlockSpec((None, Q_TILE, H_A * D_LATENT), lambda bi, i: (bi, i, C_QLAT // 1024)),
            pl.BlockSpec((None, Q_TILE, H_IDX * D_IDX), lambda bi, i: (bi, i, C_QIDX // 512)),
            pl.BlockSpec((None, Q_TILE, LANES), lambda bi, i: (bi, i, C_KW // LANES)),
            pl.BlockSpec((None, s, LANES), lambda bi, i: (bi, 0, C_CKV // LANES)),
            pl.BlockSpec((None, s, LANES), lambda bi, i: (bi, 0, C_KW // LANES)),
            pl.BlockSpec((1, D_LATENT), lambda bi, i: (0, 0)),
            pl.BlockSpec((H_A, D_LATENT, DH_A), lambda bi, i: (0, 0, 0)),
        ],
        out_specs=pl.BlockSpec((None, Q_TILE, D_BRANCH), lambda bi, i: (bi, i, 0)),
        out_shape=jax.ShapeDtypeStruct((b, s, D_BRANCH), bf16),
        scratch_shapes=[
            pltpu.VMEM((n_kt, KEY_TILE, D_LATENT), bf16),
            pltpu.VMEM((n_kt, D_LATENT, KEY_TILE), bf16),
            pltpu.VMEM((n_kt, KEY_TILE, D_IDX), bf16),
            pltpu.VMEM((n_kt, KEY_TILE, Q_TILE), jnp.int32),
            pltpu.VMEM((n_kt, KEY_TILE, Q_TILE), f32),
            pltpu.VMEM((D_LATENT, H_A * Q_TILE), bf16),
            pltpu.VMEM((D_IDX, H_IDX * Q_TILE), bf16),
            pltpu.VMEM((D_LATENT, H_A * Q_TILE), f32),
        ],
        compiler_params=_cparams(("parallel", "arbitrary")),
        name="attn",
    )(proj3, proj3, proj3, proj3, proj3, g_kv, w_uv_bf)


def _ret_kernel(q_ref, k_ref, v_ref, gr_ref, cos_ref, sin_ref, din_ref, dq_ref, dk_ref, dc_ref,
                gret_ref, o_ref, state_s):
    @pl.when(pl.program_id(1) == 0)
    def _():
        state_s[...] = jnp.zeros_like(state_s)

    cos = cos_ref[...]
    sin = sin_ref[...]

    def rot(x):
        return x * cos + pltpu.roll(x, DK_R // 2, axis=1) * sin

    for h in range(H_R):
        sl = slice(h * DK_R, (h + 1) * DK_R)
        q = rot(q_ref[:, sl].astype(f32)).astype(bf16)
        kf = rot(k_ref[:, sl].astype(f32)) * (DK_R ** -0.5)
        k = kf.astype(bf16)
        v = v_ref[:, sl]
        inner = lax.dot_general(q, k, (((1,), (1,)), ((), ())), preferred_element_type=f32) * din_ref[h]
        o = jnp.dot(inner.astype(bf16), v, preferred_element_type=f32)
        st = state_s[h]
        o = o + jnp.dot(q, st.astype(bf16), preferred_element_type=f32) * dq_ref[h]
        kd = (kf * dk_ref[h]).astype(bf16)
        state_s[h] = st * dc_ref[h] + jnp.dot(kd.T, v, preferred_element_type=f32)
        mu = jnp.mean(o, axis=-1, keepdims=True)
        var = jnp.mean(jnp.square(o - mu), axis=-1, keepdims=True)
        y = (o - mu) * lax.rsqrt(var + EPS) * gret_ref[:, sl]
        gate = gr_ref[:, sl].astype(f32)
        o_ref[:, sl] = (gate * jax.nn.sigmoid(gate) * y).astype(o_ref.dtype)


def _ret(proj3, cos_t, sin_t, d_in, d_q, d_k, d_c, g_ret):
    b, s, _ = proj3.shape
    c = RET_CHUNK
    w = H_R * DK_R

    def col(off):
        return pl.BlockSpec((None, c, w), lambda bi, ci: (bi, ci, off // w))

    return pl.pallas_call(
        _ret_kernel,
        grid=(b, s // c),
        in_specs=[
            col(C_QR), col(C_KR), col(C_VR), col(C_GR),
            pl.BlockSpec((c, DK_R), lambda bi, ci: (ci, 0)),
            pl.BlockSpec((c, DK_R), lambda bi, ci: (ci, 0)),
            pl.BlockSpec((H_R, c, c), lambda bi, ci: (0, 0, 0)),
            pl.BlockSpec((H_R, c, DK_R), lambda bi, ci: (0, 0, 0)),
            pl.BlockSpec((H_R, c, DK_R), lambda bi, ci: (0, 0, 0)),
            pl.BlockSpec((H_R, 1, DK_R), lambda bi, ci: (0, 0, 0)),
            pl.BlockSpec((1, w), lambda bi, ci: (0, 0)),
        ],
        out_specs=pl.BlockSpec((None, c, w), lambda bi, ci: (bi, ci, 0)),
        out_shape=jax.ShapeDtypeStruct((b, s, w), bf16),
        scratch_shapes=[pltpu.VMEM((H_R, DK_R, DV_R), f32)],
        compiler_params=_cparams(("parallel", "arbitrary")),
        name="ret",
    )(proj3, proj3, proj3, proj3, cos_t, sin_t, d_in, d_q, d_k, d_c, g_ret)


def _retention_tables(s):
    c = RET_CHUNK
    half = DK_R // 2
    freq = ROPE_BASE ** (-jnp.arange(half, dtype=f32) / half)
    ang = jnp.arange(s, dtype=f32)[:, None] * freq[None, :]
    cos = jnp.cos(ang)
    sin = jnp.sin(ang)
    cos_t = jnp.concatenate([cos, cos], axis=-1)
    sin_t = jnp.concatenate([-sin, sin], axis=-1)
    log_gamma = jnp.log1p(-jnp.exp2(-5.0 - jnp.arange(H_R, dtype=f32)))
    n = jnp.arange(c, dtype=f32)
    diff = n[:, None] - n[None, :]
    d_in = jnp.where(diff >= 0, jnp.exp(log_gamma[:, None, None] * jnp.maximum(diff, 0.0)), 0.0)
    d_q = jnp.broadcast_to(jnp.exp(log_gamma[:, None] * (n + 1.0))[:, :, None], (H_R, c, DK_R))
    d_k = jnp.broadcast_to(jnp.exp(log_gamma[:, None] * (c - 1.0 - n))[:, :, None], (H_R, c, DK_R))
    d_c = jnp.broadcast_to(jnp.exp(log_gamma * c)[:, None, None], (H_R, 1, DK_R))
    return cos_t, sin_t, d_in, d_q, d_k, d_c


MIX_CHUNK = 512


def _mix_kernel(oa_ref, ob_ref, wb_ref, ga0_ref, ga1_ref, gb0_ref, gb1_ref, o_ref):
    oa = oa_ref[...]
    ob = ob_ref[...]
    half = ga0_ref.shape[1]
    for c in range(0, o_ref.shape[1], MIX_CHUNK):
        ga_ref, gb_ref, off = (ga0_ref, gb0_ref, c) if c < half else (ga1_ref, gb1_ref, c - half)
        a = jnp.dot(oa, wb_ref[0, :, c:c + MIX_CHUNK], preferred_element_type=f32)
        b = jnp.dot(ob, wb_ref[1, :, c:c + MIX_CHUNK], preferred_element_type=f32)
        ga = jax.nn.sigmoid(ga_ref[:, off:off + MIX_CHUNK].astype(f32))
        gb = jax.nn.sigmoid(gb_ref[:, off:off + MIX_CHUNK].astype(f32))
        o_ref[:, c:c + MIX_CHUNK] = (ga * a + gb * b).astype(o_ref.dtype)


def _mix(o_a, o_b, w_branch_bf, proj, tm):
    t = o_a.shape[0]
    d = w_branch_bf.shape[2]
    half = d // 2
    assert C_GBR % half == 0 and half % MIX_CHUNK == 0

    def gate(k):
        return pl.BlockSpec((tm, half), lambda i: (i, C_GBR // half + k))

    return pl.pallas_call(
        _mix_kernel,
        grid=(t // tm,),
        in_specs=[
            pl.BlockSpec((tm, D_BRANCH), lambda i: (i, 0)),
            pl.BlockSpec((tm, D_BRANCH), lambda i: (i, 0)),
            pl.BlockSpec((N_BRANCH, D_BRANCH, d), lambda i: (0, 0, 0)),
            gate(0), gate(1), gate(2), gate(3),
        ],
        out_specs=pl.BlockSpec((tm, d), lambda i: (i, 0)),
        out_shape=jax.ShapeDtypeStruct((t, d), bf16),
        compiler_params=_cparams(("parallel",)),
        name="mix",
    )(o_a, o_b, w_branch_bf, proj, proj, proj, proj)


def _pack_rows(v):
    n = v.shape[1] // 2
    r = pltpu.bitcast(v.astype(bf16).astype(f32), jnp.uint32)
    w = (r[:, :n] >> 16) | (r[:, n:] & jnp.uint32(0xFFFF0000))
    return pltpu.einshape("r(ab)->rab", w, b=LANES)


def _unpack_rows(p):
    w = pltpu.einshape("rab->r(ab)", p)
    lo = pltpu.bitcast(w << 16, f32)
    hi = pltpu.bitcast(w & jnp.uint32(0xFFFF0000), f32)
    return lo, hi


def _split_bf16(a):
    hi = a.astype(bf16)
    lo = (a - hi.astype(f32)).astype(bf16)
    return hi, lo


def _outproj_kernel(mixed_ref, x_ref, wo_ref, g_ref, wr_ref, br_ref,
                    h_ref, xn_ref, eid_ref, gate_ref, cnt_ref):
    h = x_ref[...] + jnp.dot(mixed_ref[...], wo_ref[...], preferred_element_type=f32)
    h_ref[...] = h
    ms = jnp.mean(h * h, axis=-1, keepdims=True)
    xn = h * lax.rsqrt(ms + EPS) * g_ref[...]
    xn_ref[...] = _pack_rows(xn)

    x_hi, x_lo = _split_bf16(xn)
    hh_hl = jnp.dot(x_hi, wr_ref[...], preferred_element_type=f32)
    logit = (hh_hl[:, :LANES] + hh_hl[:, LANES:]
             + jnp.dot(x_lo, wr_ref[:, :LANES], preferred_element_type=f32)) + br_ref[...]

    lane = lax.broadcasted_iota(jnp.int32, logit.shape, 1)
    lanef = lane.astype(f32)
    neg = -jnp.inf

    def first_argmax(v, m):
        return jnp.min(jnp.where(v == m, lanef, float(LANES)), axis=-1, keepdims=True)

    lg = jnp.where(lane < N_GROUPS, logit, neg)
    mg = jnp.max(lg, axis=-1, keepdims=True)
    p_grp = 1.0 / jnp.sum(jnp.exp(lg - mg), axis=-1, keepdims=True)
    grp = first_argmax(lg, mg).astype(jnp.int32)

    e_lane = lane - N_GROUPS
    in_grp = (e_lane >= 0) & (e_lane < N_EXPERTS) & ((e_lane // EXP_PER_GROUP) == grp)
    le = jnp.where(in_grp, logit, neg)
    m1 = jnp.max(le, axis=-1, keepdims=True)
    i1 = first_argmax(le, m1)
    le2 = jnp.where(lanef == i1, neg, le)
    m2 = jnp.max(le2, axis=-1, keepdims=True)
    i2 = first_argmax(le2, m2)
    e2 = jnp.exp(m2 - m1)
    g1 = p_grp / (1.0 + e2)
    g2 = p_grp * e2 / (1.0 + e2)

    eid = jnp.where(lane == 0, i1, jnp.where(lane == 1, i2, float(N_GROUPS))) - float(N_GROUPS)
    eid_ref[...] = eid.astype(jnp.int32).T[:8, :]
    gate_ref[...] = jnp.where(lane == 0, g1, jnp.where(lane == 1, g2, 0.0))

    @pl.when(pl.program_id(0) == 0)
    def _():
        cnt_ref[...] = jnp.zeros_like(cnt_ref)

    half = pl.program_id(0) // (pl.num_programs(0) // ROUTE_HALVES)
    sub8 = lax.broadcasted_iota(jnp.int32, (8, LANES), 0)
    for s, idx in enumerate((i1, i2)):
        c = jnp.sum((lanef == idx).astype(jnp.int32), axis=0, keepdims=True)
        cnt_ref[...] += jnp.where(sub8 == s * ROUTE_HALVES + half, c, 0)


def _outproj(mixed, x2, w_out_bf, g_ffn, wr_hi_lo, b_r, tm):
    t, d = x2.shape
    row = lambda i: (i, 0)
    fixed = lambda i: (0, 0)
    return pl.pallas_call(
        _outproj_kernel,
        grid=(t // tm,),
        in_specs=[
            pl.BlockSpec((tm, d), row),
            pl.BlockSpec((tm, d), row),
            pl.BlockSpec((d, d), fixed),
            pl.BlockSpec((1, d), fixed),
            pl.BlockSpec((d, 2 * LANES), fixed),
            pl.BlockSpec((1, LANES), fixed),
        ],
        out_specs=[
            pl.BlockSpec((tm, d), row),
            pl.BlockSpec((tm, d // (2 * LANES), LANES), lambda i: (i, 0, 0)),
            pl.BlockSpec((8, tm), lambda i: (0, i)),
            pl.BlockSpec((tm, LANES), row),
            pl.BlockSpec((8, LANES), fixed),
        ],
        out_shape=[
            jax.ShapeDtypeStruct((t, d), f32),
            jax.ShapeDtypeStruct((t, d // (2 * LANES), LANES), jnp.uint32),
            jax.ShapeDtypeStruct((8, t), jnp.int32),
            jax.ShapeDtypeStruct((t, LANES), f32),
            jax.ShapeDtypeStruct((8, LANES), jnp.int32),
        ],
        compiler_params=_cparams(("arbitrary",)),
        name="outproj",
    )(mixed, x2, w_out_bf, g_ffn, wr_hi_lo, b_r)


ISSUE_UNROLL = 16
CAST_ROWS = 256
GATHER_BUFS = 4


def _experts_kernel(bexp_ref, nexp_ref, rpack_ref, nused_ref, xn_hbm, wg_hbm, wu_hbm, wd_hbm, yt_hbm,
                    xbuf, ybuf, wg_st, wu_st, wd_st, wg_bf, wu_bf, wd_bf, gsem, ssem, wsem, *, n_tok):
    j = pl.program_id(0)
    n_used = nused_ref[0]
    slot = j % 2
    tok_bits = (n_tok - 1).bit_length()

    def rows_of(blk, fn):
        base = blk * ROW_BLOCK

        def body(k, c):
            r0 = pl.multiple_of(k * ISSUE_UNROLL, ISSUE_UNROLL)
            for u in range(ISSUE_UNROLL):
                fn(r0 + u, rpack_ref[base + r0 + u], 1)
            return c

        lax.fori_loop(0, ROW_BLOCK // ISSUE_UNROLL, body, 0)

    def start_gathers(blk, sl):
        def one(r, packed, queue):
            tok = packed & ((1 << tok_bits) - 1)
            pltpu.make_async_copy(xn_hbm.at[tok], xbuf.at[sl, r], gsem.at[sl]).start(priority=queue)
        rows_of(blk, one)

    def start_scatters(blk, sl):
        def one(r, packed, queue):
            row = lax.shift_right_logical(packed, tok_bits)
            pltpu.make_async_copy(ybuf.at[sl, r], yt_hbm.at[row], ssem.at[sl]).start(priority=queue)
        rows_of(blk, one)

    def wait_gathers(sl):
        pltpu.make_async_copy(xn_hbm.at[pl.ds(0, ROW_BLOCK)], xbuf.at[sl], gsem.at[sl]).wait()

    def wait_scatters(sl):
        pltpu.make_async_copy(ybuf.at[sl], yt_hbm.at[pl.ds(0, ROW_BLOCK)], ssem.at[sl]).wait()

    staged = ((wg_hbm, wg_st, wg_bf), (wu_hbm, wu_st, wu_bf), (wd_hbm, wd_st, wd_bf))

    def start_weights(e):
        for q, (src, st, _) in enumerate(staged):
            pltpu.make_async_copy(src.at[e], st, wsem.at[q]).start()

    def wait_and_cast_weights():
        for q, (src, st, dst) in enumerate(staged):
            pltpu.make_async_copy(src.at[0], st, wsem.at[q]).wait()

            def cast(c, carry, st=st, dst=dst):
                r = pl.multiple_of(c * CAST_ROWS, CAST_ROWS)
                dst[pl.ds(r, CAST_ROWS), :] = st[pl.ds(r, CAST_ROWS), :].astype(bf16)
                return carry

            lax.fori_loop(0, st.shape[0] // CAST_ROWS, cast, 0)

    @pl.when(j == 0)
    def _():
        start_weights(bexp_ref[0])
        for b in range(GATHER_BUFS - 1):
            @pl.when(b < n_used)
            def _(b=b):
                start_gathers(b, b)
        ybuf[...] = jnp.zeros_like(ybuf)
        for sl in range(2):
            spare = yt_hbm.at[pl.ds(2 * n_tok + sl * ROW_BLOCK, ROW_BLOCK)]
            pltpu.make_async_copy(ybuf.at[sl], spare, ssem.at[sl]).start()
        for sl in range(2):
            wait_scatters(sl)

    @pl.when(j < n_used)
    def _():
        e = bexp_ref[j]

        @pl.when((j == 0) | (bexp_ref[jnp.maximum(j - 1, 0)] != e))
        def _():
            wait_and_cast_weights()

            @pl.when(nexp_ref[j] >= 0)
            def _():
                start_weights(nexp_ref[j])

        gslot = j % GATHER_BUFS
        wait_gathers(gslot)

        @pl.when(j + GATHER_BUFS - 1 < n_used)
        def _():
            start_gathers(j + GATHER_BUFS - 1, (j + GATHER_BUFS - 1) % GATHER_BUFS)

        @pl.when(j >= 2)
        def _():
            wait_scatters(slot)

        lo, hi = _unpack_rows(xbuf[gslot])
        xb = jnp.concatenate([lo, hi], axis=1).astype(bf16)
        g = jnp.dot(xb, wg_bf[...], preferred_element_type=f32)
        u = jnp.dot(xb, wu_bf[...], preferred_element_type=f32)
        hm = (g * jax.nn.sigmoid(g) * u).astype(bf16)
        ybuf[slot] = _pack_rows(jnp.dot(hm, wd_bf[...], preferred_element_type=f32))
        start_scatters(j, slot)

        @pl.when(j == n_used - 1)
        def _():
            wait_scatters(slot)

            @pl.when(j >= 1)
            def _():
                wait_scatters(1 - slot)


def _experts(block_expert, next_expert, row_pack, n_used, xn_packed, w_gate, w_up, w_down):
    n_rows = row_pack.shape[0]
    n_tok = xn_packed.shape[0]
    tile = xn_packed.shape[1:]
    _, d, f = w_gate.shape
    assert d % CAST_ROWS == 0 and f % CAST_ROWS == 0
    any_space = pl.BlockSpec(memory_space=pl.ANY)
    grid_spec = pltpu.PrefetchScalarGridSpec(
        num_scalar_prefetch=4,
        grid=(n_rows // ROW_BLOCK,),
        in_specs=[any_space, any_space, any_space, any_space],
        out_specs=any_space,
        scratch_shapes=[pltpu.VMEM((GATHER_BUFS, ROW_BLOCK) + tile, jnp.uint32),
                        pltpu.VMEM((2, ROW_BLOCK) + tile, jnp.uint32),
                        pltpu.VMEM((d, f), f32), pltpu.VMEM((d, f), f32), pltpu.VMEM((f, d), f32),
                        pltpu.VMEM((d, f), bf16), pltpu.VMEM((d, f), bf16), pltpu.VMEM((f, d), bf16),
                        pltpu.SemaphoreType.DMA((GATHER_BUFS,)), pltpu.SemaphoreType.DMA((2,)),
                        pltpu.SemaphoreType.DMA((3,))],
    )
    return pl.pallas_call(
        functools.partial(_experts_kernel, n_tok=n_tok),
        grid_spec=grid_spec,
        out_shape=jax.ShapeDtypeStruct((2 * n_tok + 2 * ROW_BLOCK,) + tile, jnp.uint32),
        compiler_params=_cparams(("arbitrary",)),
        name="experts",
    )(block_expert, next_expert, row_pack, n_used, xn_packed, w_gate, w_up, w_down)


def _combine_kernel(h_ref, y0_ref, y1_ref, gate_ref, g_ref, o_ref):
    gate = gate_ref[...]
    y0 = jnp.concatenate(_unpack_rows(y0_ref[...]), axis=1)
    y1 = jnp.concatenate(_unpack_rows(y1_ref[...]), axis=1)
    hh = h_ref[...] + gate[:, 0:1] * y0 + gate[:, 1:2] * y1
    ms = jnp.mean(hh * hh, axis=-1, keepdims=True)
    o_ref[...] = hh * lax.rsqrt(ms + EPS) * g_ref[...]


def _combine(h, yt, gate, g_final, tm):
    t, d = h.shape
    nt = t // tm
    return pl.pallas_call(
        _combine_kernel,
        grid=(nt,),
        in_specs=[
            pl.BlockSpec((tm, d), lambda i: (i, 0)),
            pl.BlockSpec((tm,) + yt.shape[1:], lambda i: (i, 0, 0)),
            pl.BlockSpec((tm,) + yt.shape[1:], lambda i: (nt + i, 0, 0)),
            pl.BlockSpec((tm, LANES), lambda i: (i, 0)),
            pl.BlockSpec((1, d), lambda i: (0, 0)),
        ],
        out_specs=pl.BlockSpec((tm, d), lambda i: (i, 0)),
        out_shape=jax.ShapeDtypeStruct((t, d), f32),
        compiler_params=_cparams(("parallel",)),
        name="combine",
    )(h, yt, yt, gate, g_final)


def _route_kernel(eid_ref, cnt_ref, rpack_ref, bexp_ref, nexp_ref, nused_ref, *cur_refs, n_tok, n_blocks):
    tok_bits = (n_tok - 1).bit_length()
    chunk = 2 * n_tok // ROUTE_CHAINS

    def no_next(k, carry):
        nexp_ref[k] = -1
        return carry

    lax.fori_loop(0, n_blocks, no_next, 0)

    def per_expert(e, carry):
        blk, prev_blk, prev_nb = carry
        start = blk * ROW_BLOCK
        run = start
        for c in range(ROUTE_CHAINS):
            cur_refs[c][e] = run
            run = run + cnt_ref[c * N_EXPERTS + e]
        nb = (run - start + ROW_BLOCK - 1) // ROW_BLOCK

        def set_block(k, c):
            bexp_ref[blk + k] = e
            return c

        lax.fori_loop(0, nb, set_block, 0)

        def set_next(k, c):
            nexp_ref[prev_blk + k] = e
            return c

        lax.fori_loop(0, jnp.where(nb > 0, prev_nb, 0), set_next, 0)

        def set_pad(r, c):
            rpack_ref[r] = (2 * n_tok + (r & (2 * ROW_BLOCK - 1))) << tok_bits
            return c

        lax.fori_loop(run, start + nb * ROW_BLOCK, set_pad, 0)
        return blk + nb, jnp.where(nb > 0, blk, prev_blk), jnp.where(nb > 0, nb, prev_nb)

    n_used, _, _ = lax.fori_loop(0, N_EXPERTS, per_expert, (0, 0, 0))
    nused_ref[0] = n_used

    def tail_block(k, carry):
        bexp_ref[k] = N_EXPERTS - 1
        return carry

    lax.fori_loop(n_used, n_blocks, tail_block, 0)

    def tail_row(r, carry):
        rpack_ref[r] = (2 * n_tok + (r & (2 * ROW_BLOCK - 1))) << tok_bits
        return carry

    lax.fori_loop(n_used * ROW_BLOCK, n_blocks * ROW_BLOCK, tail_row, 0)

    def place(i, carry):
        for c in range(ROUTE_CHAINS):
            a = c * chunk + i
            e = eid_ref[a]
            p = cur_refs[c][e]
            cur_refs[c][e] = p + 1
            rpack_ref[p] = (a << tok_bits) | (a - (c * chunk // n_tok) * n_tok)
        return carry

    lax.fori_loop(0, chunk, place, 0)


def _route(eid_flat, counts, n_tok):
    n_asg = eid_flat.shape[0]
    n_rows = -(-(n_asg + N_EXPERTS * (ROW_BLOCK - 1)) // ROW_BLOCK) * ROW_BLOCK
    n_blocks = n_rows // ROW_BLOCK
    smem = pl.BlockSpec(memory_space=pltpu.SMEM)
    return pl.pallas_call(
        functools.partial(_route_kernel, n_tok=n_tok, n_blocks=n_blocks),
        in_specs=[smem, smem],
        out_specs=[smem, smem, smem, smem],
        out_shape=[jax.ShapeDtypeStruct((n_rows,), jnp.int32),
                   jax.ShapeDtypeStruct((n_blocks,), jnp.int32),
                   jax.ShapeDtypeStruct((n_blocks,), jnp.int32),
                   jax.ShapeDtypeStruct((1,), jnp.int32)],
        scratch_shapes=[pltpu.SMEM((N_EXPERTS,), jnp.int32)] * ROUTE_CHAINS,
        name="route",
    )(eid_flat, counts)


def _pick(n, prefs):
    for p in prefs:
        if n % p == 0:
            return p
    return n


def kernel(x, g_mix_norm, w_in, g_kv, w_uv, g_ret, w_branch, w_out, g_ffn_norm, w_router_group,
           b_router_group, w_router_expert, b_router_expert, w_expert_gate, w_expert_up,
           w_expert_down, g_final):
    b, s, d = x.shape
    t = b * s
    depth = w_in.shape[0]
    n_sel = min(TOPK_MAX, s // 4)
    assert s % RET_CHUNK == 0 and s % Q_TILE == 0

    cos_t, sin_t, d_in, d_q, d_k, d_c = _retention_tables(s)
    h2 = x.reshape(t, d)
    for l in range(depth):
        w_pt = _relayout(jnp.swapaxes(w_in, 1, 2)[l])

        proj = _proj(h2, g_mix_norm[l].reshape(1, d), w_pt,_pick(t, (512, 256)), 3328)
        proj3 = proj.reshape(b, s, D_IN_P)

        o_a = _attn(proj3, g_kv[l].reshape(1, D_LATENT), w_uv[l].astype(bf16), n_sel)
        o_b = _ret(proj3, cos_t, sin_t, d_in, d_q, d_k, d_c, g_ret[l].reshape(1, H_R * DV_R))

        mixed = _mix(o_a.reshape(t, D_BRANCH), o_b.reshape(t, D_BRANCH), w_branch[l].astype(bf16),
                     proj, _pick(t, (512, 256)))

        w_r = jnp.concatenate([w_router_group[l], w_router_expert[l],
                               jnp.zeros((d, LANES - N_GROUPS - N_EXPERTS), f32)], axis=1)
        b_r = jnp.concatenate([b_router_group[l], b_router_expert[l],
                               jnp.zeros((LANES - N_GROUPS - N_EXPERTS,), f32)]).reshape(1, LANES)
        wr_hi = w_r.astype(bf16)
        wr_lo = (w_r - wr_hi.astype(f32)).astype(bf16)
        h2, xn, eid_t, gate, cnt = _outproj(mixed, h2, w_out[l].astype(bf16), g_ffn_norm[l].reshape(1, d),
                                            jnp.concatenate([wr_hi, wr_lo], axis=1), b_r, _pick(t, (512, 256)))

        row_pack, block_expert, next_expert, n_used = _route(
            eid_t[:2].reshape(-1), cnt[:ROUTE_CHAINS, N_GROUPS:N_GROUPS + N_EXPERTS].reshape(-1), t)
        yt = _experts(block_expert, next_expert, row_pack, n_used, xn,
                      w_expert_gate[l], w_expert_up[l], w_expert_down[l])
        assert depth == 1
        h2 = _combine(h2, yt, gate, g_final.reshape(1, d), _pick(t, (512, 256)))
    return h2.reshape(b, s, d)
```

```python
import functools

import jax
import jax.numpy as jnp
import numpy as np
from jax import lax
from jax.experimental import pallas as pl
from jax.experimental.pallas import tpu as pltpu

EPS = 1e-6
CHUNK = 64
H_A = 8
D_LATENT = 128
DH_A = 128
H_IDX = 8
D_IDX = 64
TOPK_MAX = 256
H_R = 8
DK_R = 128
DV_R = 128
ROPE_BASE = 10000.0
D_BRANCH = 1024
N_BRANCH = 2
N_GROUPS = 4
EXP_PER_GROUP = 8
N_EXPERTS = N_GROUPS * EXP_PER_GROUP
D_EXPERT = 1024

LANES = 128
KEY_TILE = 256
Q_TILE = 512
RET_CHUNK = 256
ROW_BLOCK = 256
ROUTE_PARTS = 4
ROUTE_CHAINS = 2 * ROUTE_PARTS
VMEM_LIMIT = 56 * 1024 * 1024

C_QLAT = 0
C_GBR = 1024
C_QR = 5120
C_KR = 6144
C_VR = 7168
C_GR = 8192
C_QIDX = 9216
C_CKV = 9728
C_KW = 9856
D_IN_P = 9984

INT_MIN = np.int32(-2 ** 31)
NEG_BIG = -1e30

bf16 = jnp.bfloat16
f32 = jnp.float32


def _cparams(sem):
    return pltpu.CompilerParams(dimension_semantics=sem, vmem_limit_bytes=VMEM_LIMIT)


RELAYOUT_ROWS = 128
RELAYOUT_GROUP = 6


def _relayout_table(d_model):
    sp = np.cumsum([0, H_A * D_LATENT, D_LATENT, H_IDX * D_IDX, D_IDX, H_IDX,
                    H_R * DK_R, H_R * DK_R, H_R * DV_R, H_R * DV_R, N_BRANCH * d_model])
    order = [(sp[0], C_GBR - C_QLAT), (sp[9], C_QR - C_GBR), (sp[5], C_KR - C_QR), (sp[6], C_VR - C_KR),
             (sp[7], C_GR - C_VR), (sp[8], C_QIDX - C_GR), (sp[2], C_CKV - C_QIDX), (sp[1], C_KW - C_CKV),
             (sp[3], D_IN_P - C_KW)]
    rows = [int(off) + k * RELAYOUT_ROWS for off, height in order for k in range(height // RELAYOUT_ROWS)]
    assert len(rows) == D_IN_P // RELAYOUT_ROWS and all(r % 8 == 0 for r in rows)
    return np.asarray(rows, np.int32)


def _relayout_kernel(row_ref, wt_hbm, o_ref, buf, sem):
    j = pl.program_id(0)
    slot = j % 2

    def fetch(step, sl, k):
        src = wt_hbm.at[pl.ds(pl.multiple_of(row_ref[step * RELAYOUT_GROUP + k], 8), RELAYOUT_ROWS)]
        return pltpu.make_async_copy(src, buf.at[sl, pl.ds(k * RELAYOUT_ROWS, RELAYOUT_ROWS)], sem.at[sl])

    @pl.when(j == 0)
    def _():
        for k in range(RELAYOUT_GROUP):
            fetch(0, 0, k).start()

    @pl.when(j + 1 < pl.num_programs(0))
    def _():
        for k in range(RELAYOUT_GROUP):
            fetch(j + 1, 1 - slot, k).start()

    for k in range(RELAYOUT_GROUP):
        fetch(j, slot, k).wait()
    o_ref[...] = buf[slot].astype(o_ref.dtype)


def _relayout(wt):
    d_in, d_model = wt.shape
    rows = _relayout_table(d_model)
    step_rows = RELAYOUT_GROUP * RELAYOUT_ROWS
    assert int(rows.max()) + RELAYOUT_ROWS <= d_in and D_IN_P % step_rows == 0
    grid_spec = pltpu.PrefetchScalarGridSpec(
        num_scalar_prefetch=1,
        grid=(D_IN_P // step_rows,),
        in_specs=[pl.BlockSpec(memory_space=pl.ANY)],
        out_specs=pl.BlockSpec((step_rows, d_model), lambda j, row: (j, 0)),
        scratch_shapes=[pltpu.VMEM((2, step_rows, d_model), f32), pltpu.SemaphoreType.DMA((2,))],
    )
    return pl.pallas_call(
        _relayout_kernel,
        grid_spec=grid_spec,
        out_shape=jax.ShapeDtypeStruct((D_IN_P, d_model), bf16),
        compiler_params=_cparams(("arbitrary",)),
        name="relayout",
    )(jnp.asarray(rows), wt)


def _proj_kernel(x_ref, g_ref, w_ref, o_ref, xn_ref):
    @pl.when(pl.program_id(1) == 0)
    def _():
        x = x_ref[...]
        ms = jnp.mean(x * x, axis=-1, keepdims=True)
        xn_ref[...] = (x * lax.rsqrt(ms + EPS) * g_ref[...]).astype(bf16)

    o_ref[...] = lax.dot_general(xn_ref[...], w_ref[...], (((1,), (1,)), ((), ())),
                                 preferred_element_type=f32).astype(o_ref.dtype)


def _proj(x2, g, w_pt, tm, tn):
    t, d = x2.shape
    n = w_pt.shape[0]
    return pl.pallas_call(
        _proj_kernel,
        grid=(t // tm, n // tn),
        in_specs=[
            pl.BlockSpec((tm, d), lambda i, j: (i, 0)),
            pl.BlockSpec((1, d), lambda i, j: (0, 0)),
            pl.BlockSpec((tn, d), lambda i, j: (j, 0)),
        ],
        out_specs=pl.BlockSpec((tm, tn), lambda i, j: (i, j)),
        out_shape=jax.ShapeDtypeStruct((t, n), bf16),
        scratch_shapes=[pltpu.VMEM((tm, d), bf16)],
        compiler_params=_cparams(("parallel", "arbitrary")),
        name="proj",
    )(x2, g, w_pt)


def _float_key(s):
    bits = pltpu.bitcast(s, jnp.int32)
    key = bits ^ ((bits >> 31) & jnp.int32(0x7FFFFFFF))
    return jnp.where(s == 0.0, jnp.int32(0), key)


def _attn_kernel(qlat_ref, qidx_ref, kwq_ref, ckv_ref, kwk_ref, gkv_ref, wuv_ref, o_ref,
                 kv_s, kvT_s, kidx_s, key_s, bias_s, qT_s, qiT_s, acc_s, *, n_sel, n_kt):
    i = pl.program_id(1)
    idx_scale = (H_IDX ** -0.5) * (D_IDX ** -0.5)
    attn_scale = D_LATENT ** -0.5
    hq = H_A * Q_TILE

    @pl.when(i == 0)
    def _():
        g = gkv_ref[...]
        for t in range(n_kt):
            c = ckv_ref[t * KEY_TILE:(t + 1) * KEY_TILE, :].astype(f32)
            ms = jnp.mean(c * c, axis=-1, keepdims=True)
            kv = c * lax.rsqrt(ms + EPS) * g
            kv_s[t] = kv.astype(bf16)
            kvT_s[t] = kv.T.astype(bf16)
            kidx_s[t] = kwk_ref[t * KEY_TILE:(t + 1) * KEY_TILE, :D_IDX]

    nk = ((i + 1) * Q_TILE + KEY_TILE - 1) // KEY_TILE
    lane = lax.broadcasted_iota(jnp.int32, (1, Q_TILE), 1)
    sub = lax.broadcasted_iota(jnp.int32, (KEY_TILE, 1), 0)
    q_chunk = (i * Q_TILE + lane) // CHUNK

    wT = kwq_ref[...].astype(f32).T
    for h in range(H_A):
        qT_s[:, h * Q_TILE:(h + 1) * Q_TILE] = qlat_ref[:, h * D_LATENT:(h + 1) * D_LATENT].astype(f32).T.astype(bf16)
    for h in range(H_IDX):
        qiT_s[:, h * Q_TILE:(h + 1) * Q_TILE] = qidx_ref[:, h * D_IDX:(h + 1) * D_IDX].astype(f32).T.astype(bf16)

    n_pairs = (nk + 1) // 2

    def score_pair(p, carry):
        for t in (2 * p, 2 * p + 1):
            d_all = jnp.dot(kidx_s[t], qiT_s[...], preferred_element_type=f32)
            acc = jnp.zeros((KEY_TILE, Q_TILE), f32)
            for h in range(H_IDX):
                d = d_all[:, h * Q_TILE:(h + 1) * Q_TILE]
                acc = acc + wT[D_IDX + h:D_IDX + h + 1, :] * jnp.maximum(d, 0.0)
            score = acc * idx_scale
            k_chunk = (t * KEY_TILE + sub) // CHUNK
            key_s[t] = jnp.where(k_chunk <= q_chunk, _float_key(score), INT_MIN)
        return carry

    lax.fori_loop(0, n_pairs, score_pair, 0)


    def count(pred):
        def body(p, c):
            for t in (2 * p, 2 * p + 1):
                m = pred(key_s[t], t).astype(jnp.int32)
                c = c + jnp.sum(m.reshape(KEY_TILE // 8, 8, Q_TILE), axis=0)
            return c
        c8 = lax.fori_loop(0, n_pairs, body, jnp.zeros((8, Q_TILE), jnp.int32))
        return jnp.sum(c8, axis=0, keepdims=True)

    thr0 = jnp.where(count(lambda k, t: k >= 0) >= n_sel, jnp.int32(0), INT_MIN)
    thr0 = jnp.broadcast_to(thr0, (1, Q_TILE)).astype(jnp.int32)

    def bit_step(j, thr):
        cand = thr | (jnp.int32(1) << (jnp.int32(30) - j))
        return jnp.where(count(lambda k, t: k >= cand) >= n_sel, cand, thr)

    thr = lax.fori_loop(0, 31, bit_step, thr0)

    c_gt = count(lambda k, t: k > thr)
    c_ge = count(lambda k, t: k >= thr)
    need = n_sel - c_gt
    has_tie = jnp.max(jnp.where((c_ge > n_sel) & (thr > INT_MIN), 1, 0)) > 0

    def tie_limit():
        def step(j, m):
            cand = m | (jnp.int32(1) << (jnp.int32(14) - j))
            c = count(lambda k, t: (k == thr) & ((t * KEY_TILE + sub) < cand))
            return jnp.where(c < need, cand, m)
        return lax.fori_loop(0, 15, step, jnp.zeros((1, Q_TILE), jnp.int32))

    m_lim = lax.cond(has_tie, tie_limit, lambda: jnp.full((1, Q_TILE), 2 ** 30, jnp.int32))

    def bias_pair(p, carry):
        for t in (2 * p, 2 * p + 1):
            k = key_s[t]
            sel = (k > thr) | ((k == thr) & ((t * KEY_TILE + sub) <= m_lim))
            sel = sel & (k > INT_MIN)
            bias_s[t] = jnp.where(sel, 0.0, NEG_BIG).astype(f32)
        return carry

    lax.fori_loop(0, n_pairs, bias_pair, 0)

    acc_s[...] = jnp.zeros_like(acc_s)

    def att_pair(p, carry):
        m_run, l_run = carry
        kv2 = jnp.concatenate([kv_s[2 * p], kv_s[2 * p + 1]], axis=0)
        kvT2 = jnp.concatenate([kvT_s[2 * p], kvT_s[2 * p + 1]], axis=1)
        bias2 = jnp.concatenate([bias_s[2 * p], bias_s[2 * p + 1]], axis=0)
        logit = jnp.dot(kv2, qT_s[...], preferred_element_type=f32) * attn_scale
        logit = logit + jnp.concatenate([bias2] * H_A, axis=1)
        m_new = jnp.maximum(m_run, jnp.max(logit, axis=0, keepdims=True))
        alpha = jnp.exp(m_run - m_new)
        pr = jnp.exp(logit - m_new)
        l_new = alpha * l_run + jnp.sum(pr, axis=0, keepdims=True)
        acc_s[...] = alpha * acc_s[...] + jnp.dot(kvT2, pr.astype(bf16), preferred_element_type=f32)
        return m_new, l_new

    init = (jnp.full((1, hq), NEG_BIG, f32), jnp.zeros((1, hq), f32))
    _, l_fin = lax.fori_loop(0, n_pairs, att_pair, init)
    inv_l = 1.0 / l_fin
    for h in range(H_A):
        sl = slice(h * Q_TILE, (h + 1) * Q_TILE)
        o_lat = (acc_s[:, sl] * inv_l[:, sl]).T
        o_ref[:, h * DH_A:(h + 1) * DH_A] = jnp.dot(
            o_lat.astype(bf16), wuv_ref[h], preferred_element_type=f32).astype(o_ref.dtype)


def _attn(proj3, g_kv, w_uv_bf, n_sel):
    b, s, _ = proj3.shape
    n_kt = s // KEY_TILE
    kern = functools.partial(_attn_kernel, n_sel=n_sel, n_kt=n_kt)
    return pl.pallas_call(
        kern,
        grid=(b, s // Q_TILE),
        in_specs=[
            pl.BlockSpec((None, Q_TILE, H_A * D_LATENT), lambda bi, i: (bi, i, C_QLAT // 1024)),
            pl.BlockSpec((None, Q_TILE, H_IDX * D_IDX), lambda bi, i: (bi, i, C_QIDX // 512)),
            pl.BlockSpec((None, Q_TILE, LANES), lambda bi, i: (bi, i, C_KW // LANES)),
            pl.BlockSpec((None, s, LANES), lambda bi, i: (bi, 0, C_CKV // LANES)),
            pl.BlockSpec((None, s, LANES), lambda bi, i: (bi, 0, C_KW // LANES)),
            pl.BlockSpec((1, D_LATENT), lambda bi, i: (0, 0)),
            pl.BlockSpec((H_A, D_LATENT, DH_A), lambda bi, i: (0, 0, 0)),
        ],
        out_specs=pl.BlockSpec((None, Q_TILE, D_BRANCH), lambda bi, i: (bi, i, 0)),
        out_shape=jax.ShapeDtypeStruct((b, s, D_BRANCH), bf16),
        scratch_shapes=[
            pltpu.VMEM((n_kt, KEY_TILE, D_LATENT), bf16),
            pltpu.VMEM((n_kt, D_LATENT, KEY_TILE), bf16),
            pltpu.VMEM((n_kt, KEY_TILE, D_IDX), bf16),
            pltpu.VMEM((n_kt, KEY_TILE, Q_TILE), jnp.int32),
            pltpu.VMEM((n_kt, KEY_TILE, Q_TILE), f32),
            pltpu.VMEM((D_LATENT, H_A * Q_TILE), bf16),
            pltpu.VMEM((D_IDX, H_IDX * Q_TILE), bf16),
            pltpu.VMEM((D_LATENT, H_A * Q_TILE), f32),
        ],
        compiler_params=_cparams(("parallel", "arbitrary")),
        name="attn",
    )(proj3, proj3, proj3, proj3, proj3, g_kv, w_uv_bf)


def _ret_kernel(q_ref, k_ref, v_ref, gr_ref, cos_ref, sin_ref, din_ref, dq_ref, dk_ref, dc_ref,
                gret_ref, o_ref, state_s):
    @pl.when(pl.program_id(1) == 0)
    def _():
        state_s[...] = jnp.zeros_like(state_s)

    cos = cos_ref[...]
    sin = sin_ref[...]

    def rot(x):
        return x * cos + pltpu.roll(x, DK_R // 2, axis=1) * sin

    for h in range(H_R):
        sl = slice(h * DK_R, (h + 1) * DK_R)
        q = rot(q_ref[:, sl].astype(f32)).astype(bf16)
        kf = rot(k_ref[:, sl].astype(f32)) * (DK_R ** -0.5)
        k = kf.astype(bf16)
        v = v_ref[:, sl]
        inner = lax.dot_general(q, k, (((1,), (1,)), ((), ())), preferred_element_type=f32) * din_ref[h]
        o = jnp.dot(inner.astype(bf16), v, preferred_element_type=f32)
        st = state_s[h]
        o = o + jnp.dot(q, st.astype(bf16), preferred_element_type=f32) * dq_ref[h]
        kd = (kf * dk_ref[h]).astype(bf16)
        state_s[h] = st * dc_ref[h] + jnp.dot(kd.T, v, preferred_element_type=f32)
        mu = jnp.mean(o, axis=-1, keepdims=True)
        var = jnp.mean(jnp.square(o - mu), axis=-1, keepdims=True)
        y = (o - mu) * lax.rsqrt(var + EPS) * gret_ref[:, sl]
        gate = gr_ref[:, sl].astype(f32)
        o_ref[:, sl] = (gate * jax.nn.sigmoid(gate) * y).astype(o_ref.dtype)


def _ret(proj3, cos_t, sin_t, d_in, d_q, d_k, d_c, g_ret):
    b, s, _ = proj3.shape
    c = RET_CHUNK
    w = H_R * DK_R

    def col(off):
        return pl.BlockSpec((None, c, w), lambda bi, ci: (bi, ci, off // w))

    return pl.pallas_call(
        _ret_kernel,
        grid=(b, s // c),
        in_specs=[
            col(C_QR), col(C_KR), col(C_VR), col(C_GR),
            pl.BlockSpec((c, DK_R), lambda bi, ci: (ci, 0)),
            pl.BlockSpec((c, DK_R), lambda bi, ci: (ci, 0)),
            pl.BlockSpec((H_R, c, c), lambda bi, ci: (0, 0, 0)),
            pl.BlockSpec((H_R, c, DK_R), lambda bi, ci: (0, 0, 0)),
            pl.BlockSpec((H_R, c, DK_R), lambda bi, ci: (0, 0, 0)),
            pl.BlockSpec((H_R, 1, DK_R), lambda bi, ci: (0, 0, 0)),
            pl.BlockSpec((1, w), lambda bi, ci: (0, 0)),
        ],
        out_specs=pl.BlockSpec((None, c, w), lambda bi, ci: (bi, ci, 0)),
        out_shape=jax.ShapeDtypeStruct((b, s, w), bf16),
        scratch_shapes=[pltpu.VMEM((H_R, DK_R, DV_R), f32)],
        compiler_params=_cparams(("parallel", "arbitrary")),
        name="ret",
    )(proj3, proj3, proj3, proj3, cos_t, sin_t, d_in, d_q, d_k, d_c, g_ret)


def _retention_tables(s):
    c = RET_CHUNK
    half = DK_R // 2
    freq = ROPE_BASE ** (-jnp.arange(half, dtype=f32) / half)
    ang = jnp.arange(s, dtype=f32)[:, None] * freq[None, :]
    cos = jnp.cos(ang)
    sin = jnp.sin(ang)
    cos_t = jnp.concatenate([cos, cos], axis=-1)
    sin_t = jnp.concatenate([-sin, sin], axis=-1)
    log_gamma = jnp.log1p(-jnp.exp2(-5.0 - jnp.arange(H_R, dtype=f32)))
    n = jnp.arange(c, dtype=f32)
    diff = n[:, None] - n[None, :]
    d_in = jnp.where(diff >= 0, jnp.exp(log_gamma[:, None, None] * jnp.maximum(diff, 0.0)), 0.0)
    d_q = jnp.broadcast_to(jnp.exp(log_gamma[:, None] * (n + 1.0))[:, :, None], (H_R, c, DK_R))
    d_k = jnp.broadcast_to(jnp.exp(log_gamma[:, None] * (c - 1.0 - n))[:, :, None], (H_R, c, DK_R))
    d_c = jnp.broadcast_to(jnp.exp(log_gamma * c)[:, None, None], (H_R, 1, DK_R))
    return cos_t, sin_t, d_in, d_q, d_k, d_c


MIX_CHUNK = 512


def _mix_kernel(oa_ref, ob_ref, wb_ref, ga0_ref, ga1_ref, gb0_ref, gb1_ref, o_ref):
    oa = oa_ref[...]
    ob = ob_ref[...]
    half = ga0_ref.shape[1]
    for c in range(0, o_ref.shape[1], MIX_CHUNK):
        ga_ref, gb_ref, off = (ga0_ref, gb0_ref, c) if c < half else (ga1_ref, gb1_ref, c - half)
        a = jnp.dot(oa, wb_ref[0, :, c:c + MIX_CHUNK], preferred_element_type=f32)
        b = jnp.dot(ob, wb_ref[1, :, c:c + MIX_CHUNK], preferred_element_type=f32)
        ga = jax.nn.sigmoid(ga_ref[:, off:off + MIX_CHUNK].astype(f32))
        gb = jax.nn.sigmoid(gb_ref[:, off:off + MIX_CHUNK].astype(f32))
        o_ref[:, c:c + MIX_CHUNK] = (ga * a + gb * b).astype(o_ref.dtype)


def _mix(o_a, o_b, w_branch_bf, proj, tm):
    t = o_a.shape[0]
    d = w_branch_bf.shape[2]
    half = d // 2
    assert C_GBR % half == 0 and half % MIX_CHUNK == 0

    def gate(k):
        return pl.BlockSpec((tm, half), lambda i: (i, C_GBR // half + k))

    return pl.pallas_call(
        _mix_kernel,
        grid=(t // tm,),
        in_specs=[
            pl.BlockSpec((tm, D_BRANCH), lambda i: (i, 0)),
            pl.BlockSpec((tm, D_BRANCH), lambda i: (i, 0)),
            pl.BlockSpec((N_BRANCH, D_BRANCH, d), lambda i: (0, 0, 0)),
            gate(0), gate(1), gate(2), gate(3),
        ],
        out_specs=pl.BlockSpec((tm, d), lambda i: (i, 0)),
        out_shape=jax.ShapeDtypeStruct((t, d), bf16),
        compiler_params=_cparams(("parallel",)),
        name="mix",
    )(o_a, o_b, w_branch_bf, proj, proj, proj, proj)


def _pack_rows(v):
    n = v.shape[1] // 2
    r = pltpu.bitcast(v.astype(bf16).astype(f32), jnp.uint32)
    w = (r[:, :n] >> 16) | (r[:, n:] & jnp.uint32(0xFFFF0000))
    return pltpu.einshape("r(ab)->rab", w, b=LANES)


def _unpack_rows(p):
    w = pltpu.einshape("rab->r(ab)", p)
    lo = pltpu.bitcast(w << 16, f32)
    hi = pltpu.bitcast(w & jnp.uint32(0xFFFF0000), f32)
    return lo, hi


def _split_bf16(a):
    hi = a.astype(bf16)
    lo = (a - hi.astype(f32)).astype(bf16)
    return hi, lo


def _outproj_kernel(mixed_ref, x_ref, wo_ref, g_ref, wr_ref, br_ref,
                    h_ref, xn_ref, eid_ref, gate_ref, cnt_ref):
    h = x_ref[...] + jnp.dot(mixed_ref[...], wo_ref[...], preferred_element_type=f32)
    h_ref[...] = h
    ms = jnp.mean(h * h, axis=-1, keepdims=True)
    xn = h * lax.rsqrt(ms + EPS) * g_ref[...]
    xn_ref[...] = _pack_rows(xn)

    x_hi, x_lo = _split_bf16(xn)
    hh_hl = jnp.dot(x_hi, wr_ref[...], preferred_element_type=f32)
    logit = (hh_hl[:, :LANES] + hh_hl[:, LANES:]
             + jnp.dot(x_lo, wr_ref[:, :LANES], preferred_element_type=f32)) + br_ref[...]

    lane = lax.broadcasted_iota(jnp.int32, logit.shape, 1)
    lanef = lane.astype(f32)
    neg = -jnp.inf

    def first_argmax(v, m):
        return jnp.min(jnp.where(v == m, lanef, float(LANES)), axis=-1, keepdims=True)

    lg = jnp.where(lane < N_GROUPS, logit, neg)
    mg = jnp.max(lg, axis=-1, keepdims=True)
    p_grp = 1.0 / jnp.sum(jnp.exp(lg - mg), axis=-1, keepdims=True)
    grp = first_argmax(lg, mg).astype(jnp.int32)

    e_lane = lane - N_GROUPS
    in_grp = (e_lane >= 0) & (e_lane < N_EXPERTS) & ((e_lane // EXP_PER_GROUP) == grp)
    le = jnp.where(in_grp, logit, neg)
    m1 = jnp.max(le, axis=-1, keepdims=True)
    i1 = first_argmax(le, m1)
    le2 = jnp.where(lanef == i1, neg, le)
    m2 = jnp.max(le2, axis=-1, keepdims=True)
    i2 = first_argmax(le2, m2)
    e2 = jnp.exp(m2 - m1)
    g1 = p_grp / (1.0 + e2)
    g2 = p_grp * e2 / (1.0 + e2)

    eid = jnp.where(lane == 0, i1, jnp.where(lane == 1, i2, float(N_GROUPS))) - float(N_GROUPS)
    eid_ref[...] = eid.astype(jnp.int32).T[:8, :]
    gate_ref[...] = jnp.where(lane == 0, g1, jnp.where(lane == 1, g2, 0.0))

    @pl.when(pl.program_id(0) == 0)
    def _():
        cnt_ref[...] = jnp.zeros_like(cnt_ref)

    part = pl.program_id(0) // (pl.num_programs(0) // ROUTE_PARTS)
    sub8 = lax.broadcasted_iota(jnp.int32, (8, LANES), 0)
    for s, idx in enumerate((i1, i2)):
        c = jnp.sum((lanef == idx).astype(jnp.int32), axis=0, keepdims=True)
        cnt_ref[...] += jnp.where(sub8 == s * ROUTE_PARTS + part, c, 0)


def _outproj(mixed, x2, w_out_bf, g_ffn, wr_hi_lo, b_r, tm):
    t, d = x2.shape
    row = lambda i: (i, 0)
    fixed = lambda i: (0, 0)
    return pl.pallas_call(
        _outproj_kernel,
        grid=(t // tm,),
        in_specs=[
            pl.BlockSpec((tm, d), row),
            pl.BlockSpec((tm, d), row),
            pl.BlockSpec((d, d), fixed),
            pl.BlockSpec((1, d), fixed),
            pl.BlockSpec((d, 2 * LANES), fixed),
            pl.BlockSpec((1, LANES), fixed),
        ],
        out_specs=[
            pl.BlockSpec((tm, d), row),
            pl.BlockSpec((tm, d // (2 * LANES), LANES), lambda i: (i, 0, 0)),
            pl.BlockSpec((8, tm), lambda i: (0, i)),
            pl.BlockSpec((tm, LANES), row),
            pl.BlockSpec((8, LANES), fixed),
        ],
        out_shape=[
            jax.ShapeDtypeStruct((t, d), f32),
            jax.ShapeDtypeStruct((t, d // (2 * LANES), LANES), jnp.uint32),
            jax.ShapeDtypeStruct((8, t), jnp.int32),
            jax.ShapeDtypeStruct((t, LANES), f32),
            jax.ShapeDtypeStruct((8, LANES), jnp.int32),
        ],
        compiler_params=_cparams(("arbitrary",)),
        name="outproj",
    )(mixed, x2, w_out_bf, g_ffn, wr_hi_lo, b_r)


ISSUE_UNROLL = 16
CAST_ROWS = 256
GATHER_BUFS = 4


def _experts_kernel(bexp_ref, nexp_ref, rpack_ref, nused_ref, xn_hbm, wg_hbm, wu_hbm, wd_hbm, yt_hbm,
                    xbuf, ybuf, wg_st, wu_st, wd_st, wg_bf, wu_bf, wd_bf, gsem, ssem, wsem, *, n_tok):
    j = pl.program_id(0)
    n_used = nused_ref[0]
    slot = j % 2
    tok_bits = (n_tok - 1).bit_length()

    def rows_of(blk, fn):
        base = blk * ROW_BLOCK

        def body(k, c):
            r0 = pl.multiple_of(k * ISSUE_UNROLL, ISSUE_UNROLL)
            for u in range(ISSUE_UNROLL):
                fn(r0 + u, rpack_ref[base + r0 + u], 1)
            return c

        lax.fori_loop(0, ROW_BLOCK // ISSUE_UNROLL, body, 0)

    def start_gathers(blk, sl):
        def one(r, packed, queue):
            tok = packed & ((1 << tok_bits) - 1)
            pltpu.make_async_copy(xn_hbm.at[tok], xbuf.at[sl, r], gsem.at[sl]).start(priority=queue)
        rows_of(blk, one)

    def start_scatters(blk, sl):
        def one(r, packed, queue):
            row = lax.shift_right_logical(packed, tok_bits)
            pltpu.make_async_copy(ybuf.at[sl, r], yt_hbm.at[row], ssem.at[sl]).start(priority=queue)
        rows_of(blk, one)

    def wait_gathers(sl):
        pltpu.make_async_copy(xn_hbm.at[pl.ds(0, ROW_BLOCK)], xbuf.at[sl], gsem.at[sl]).wait()

    def wait_scatters(sl):
        pltpu.make_async_copy(ybuf.at[sl], yt_hbm.at[pl.ds(0, ROW_BLOCK)], ssem.at[sl]).wait()

    staged = ((wg_hbm, wg_st, wg_bf), (wu_hbm, wu_st, wu_bf), (wd_hbm, wd_st, wd_bf))

    def start_weights(e):
        for q, (src, st, _) in enumerate(staged):
            pltpu.make_async_copy(src.at[e], st, wsem.at[q]).start()

    def wait_and_cast_weights():
        for q, (src, st, dst) in enumerate(staged):
            pltpu.make_async_copy(src.at[0], st, wsem.at[q]).wait()

            def cast(c, carry, st=st, dst=dst):
                r = pl.multiple_of(c * CAST_ROWS, CAST_ROWS)
                dst[pl.ds(r, CAST_ROWS), :] = st[pl.ds(r, CAST_ROWS), :].astype(bf16)
                return carry

            lax.fori_loop(0, st.shape[0] // CAST_ROWS, cast, 0)

    @pl.when(j == 0)
    def _():
        start_weights(bexp_ref[0])
        for b in range(GATHER_BUFS - 1):
            @pl.when(b < n_used)
            def _(b=b):
                start_gathers(b, b)
        ybuf[...] = jnp.zeros_like(ybuf)
        for sl in range(2):
            spare = yt_hbm.at[pl.ds(2 * n_tok + sl * ROW_BLOCK, ROW_BLOCK)]
            pltpu.make_async_copy(ybuf.at[sl], spare, ssem.at[sl]).start()
        for sl in range(2):
            wait_scatters(sl)

    @pl.when(j < n_used)
    def _():
        e = bexp_ref[j]

        @pl.when((j == 0) | (bexp_ref[jnp.maximum(j - 1, 0)] != e))
        def _():
            wait_and_cast_weights()

            @pl.when(nexp_ref[j] >= 0)
            def _():
                start_weights(nexp_ref[j])

        gslot = j % GATHER_BUFS
        wait_gathers(gslot)

        @pl.when(j + GATHER_BUFS - 1 < n_used)
        def _():
            start_gathers(j + GATHER_BUFS - 1, (j + GATHER_BUFS - 1) % GATHER_BUFS)

        @pl.when(j >= 2)
        def _():
            wait_scatters(slot)

        lo, hi = _unpack_rows(xbuf[gslot])
        xb = jnp.concatenate([lo, hi], axis=1).astype(bf16)
        g = jnp.dot(xb, wg_bf[...], preferred_element_type=f32)
        u = jnp.dot(xb, wu_bf[...], preferred_element_type=f32)
        hm = (g * jax.nn.sigmoid(g) * u).astype(bf16)
        ybuf[slot] = _pack_rows(jnp.dot(hm, wd_bf[...], preferred_element_type=f32))
        start_scatters(j, slot)

        @pl.when(j == n_used - 1)
        def _():
            wait_scatters(slot)

            @pl.when(j >= 1)
            def _():
                wait_scatters(1 - slot)


def _experts(block_expert, next_expert, row_pack, n_used, xn_packed, w_gate, w_up, w_down):
    n_rows = row_pack.shape[0]
    n_tok = xn_packed.shape[0]
    tile = xn_packed.shape[1:]
    _, d, f = w_gate.shape
    assert d % CAST_ROWS == 0 and f % CAST_ROWS == 0
    any_space = pl.BlockSpec(memory_space=pl.ANY)
    grid_spec = pltpu.PrefetchScalarGridSpec(
        num_scalar_prefetch=4,
        grid=(n_rows // ROW_BLOCK,),
        in_specs=[any_space, any_space, any_space, any_space],
        out_specs=any_space,
        scratch_shapes=[pltpu.VMEM((GATHER_BUFS, ROW_BLOCK) + tile, jnp.uint32),
                        pltpu.VMEM((2, ROW_BLOCK) + tile, jnp.uint32),
                        pltpu.VMEM((d, f), f32), pltpu.VMEM((d, f), f32), pltpu.VMEM((f, d), f32),
                        pltpu.VMEM((d, f), bf16), pltpu.VMEM((d, f), bf16), pltpu.VMEM((f, d), bf16),
                        pltpu.SemaphoreType.DMA((GATHER_BUFS,)), pltpu.SemaphoreType.DMA((2,)),
                        pltpu.SemaphoreType.DMA((3,))],
    )
    return pl.pallas_call(
        functools.partial(_experts_kernel, n_tok=n_tok),
        grid_spec=grid_spec,
        out_shape=jax.ShapeDtypeStruct((2 * n_tok + 2 * ROW_BLOCK,) + tile, jnp.uint32),
        compiler_params=_cparams(("arbitrary",)),
        name="experts",
    )(block_expert, next_expert, row_pack, n_used, xn_packed, w_gate, w_up, w_down)


def _combine_kernel(h_ref, y0_ref, y1_ref, gate_ref, g_ref, o_ref):
    gate = gate_ref[...]
    y0 = jnp.concatenate(_unpack_rows(y0_ref[...]), axis=1)
    y1 = jnp.concatenate(_unpack_rows(y1_ref[...]), axis=1)
    hh = h_ref[...] + gate[:, 0:1] * y0 + gate[:, 1:2] * y1
    ms = jnp.mean(hh * hh, axis=-1, keepdims=True)
    o_ref[...] = hh * lax.rsqrt(ms + EPS) * g_ref[...]


def _combine(h, yt, gate, g_final, tm):
    t, d = h.shape
    nt = t // tm
    return pl.pallas_call(
        _combine_kernel,
        grid=(nt,),
        in_specs=[
            pl.BlockSpec((tm, d), lambda i: (i, 0)),
            pl.BlockSpec((tm,) + yt.shape[1:], lambda i: (i, 0, 0)),
            pl.BlockSpec((tm,) + yt.shape[1:], lambda i: (nt + i, 0, 0)),
            pl.BlockSpec((tm, LANES), lambda i: (i, 0)),
            pl.BlockSpec((1, d), lambda i: (0, 0)),
        ],
        out_specs=pl.BlockSpec((tm, d), lambda i: (i, 0)),
        out_shape=jax.ShapeDtypeStruct((t, d), f32),
        compiler_params=_cparams(("parallel",)),
        name="combine",
    )(h, yt, yt, gate, g_final)


def _route_kernel(eid_ref, cnt_ref, rpack_ref, bexp_ref, nexp_ref, nused_ref, *cur_refs, n_tok, n_blocks):
    tok_bits = (n_tok - 1).bit_length()
    chunk = 2 * n_tok // ROUTE_CHAINS

    def no_next(k, carry):
        nexp_ref[k] = -1
        return carry

    lax.fori_loop(0, n_blocks, no_next, 0)

    def per_expert(e, carry):
        blk, prev_blk, prev_nb = carry
        start = blk * ROW_BLOCK
        run = start
        for c in range(ROUTE_CHAINS):
            cur_refs[c][e] = run
            run = run + cnt_ref[c * N_EXPERTS + e]
        nb = (run - start + ROW_BLOCK - 1) // ROW_BLOCK

        def set_block(k, c):
            bexp_ref[blk + k] = e
            return c

        lax.fori_loop(0, nb, set_block, 0)

        def set_next(k, c):
            nexp_ref[prev_blk + k] = e
            return c

        lax.fori_loop(0, jnp.where(nb > 0, prev_nb, 0), set_next, 0)

        def set_pad(r, c):
            rpack_ref[r] = (2 * n_tok + (r & (2 * ROW_BLOCK - 1))) << tok_bits
            return c

        lax.fori_loop(run, start + nb * ROW_BLOCK, set_pad, 0)
        return blk + nb, jnp.where(nb > 0, blk, prev_blk), jnp.where(nb > 0, nb, prev_nb)

    n_used, _, _ = lax.fori_loop(0, N_EXPERTS, per_expert, (0, 0, 0))
    nused_ref[0] = n_used

    def tail_block(k, carry):
        bexp_ref[k] = N_EXPERTS - 1
        return carry

    lax.fori_loop(n_used, n_blocks, tail_block, 0)

    def tail_row(r, carry):
        rpack_ref[r] = (2 * n_tok + (r & (2 * ROW_BLOCK - 1))) << tok_bits
        return carry

    lax.fori_loop(n_used * ROW_BLOCK, n_blocks * ROW_BLOCK, tail_row, 0)

    def place(i, carry):
        for c in range(ROUTE_CHAINS):
            a = c * chunk + i
            e = eid_ref[a]
            p = cur_refs[c][e]
            cur_refs[c][e] = p + 1
            rpack_ref[p] = (a << tok_bits) | (a - (c * chunk // n_tok) * n_tok)
        return carry

    lax.fori_loop(0, chunk, place, 0)


def _route(eid_flat, counts, n_tok):
    n_asg = eid_flat.shape[0]
    n_rows = -(-(n_asg + N_EXPERTS * (ROW_BLOCK - 1)) // ROW_BLOCK) * ROW_BLOCK
    n_blocks = n_rows // ROW_BLOCK
    smem = pl.BlockSpec(memory_space=pltpu.SMEM)
    return pl.pallas_call(
        functools.partial(_route_kernel, n_tok=n_tok, n_blocks=n_blocks),
        in_specs=[smem, smem],
        out_specs=[smem, smem, smem, smem],
        out_shape=[jax.ShapeDtypeStruct((n_rows,), jnp.int32),
                   jax.ShapeDtypeStruct((n_blocks,), jnp.int32),
                   jax.ShapeDtypeStruct((n_blocks,), jnp.int32),
                   jax.ShapeDtypeStruct((1,), jnp.int32)],
        scratch_shapes=[pltpu.SMEM((N_EXPERTS,), jnp.int32)] * ROUTE_CHAINS,
        name="route",
    )(eid_flat, counts)


def _pick(n, prefs):
    for p in prefs:
        if n % p == 0:
            return p
    return n


def kernel(x, g_mix_norm, w_in, g_kv, w_uv, g_ret, w_branch, w_out, g_ffn_norm, w_router_group,
           b_router_group, w_router_expert, b_router_expert, w_expert_gate, w_expert_up,
           w_expert_down, g_final):
    b, s, d = x.shape
    t = b * s
    depth = w_in.shape[0]
    n_sel = min(TOPK_MAX, s // 4)
    assert s % RET_CHUNK == 0 and s % Q_TILE == 0

    cos_t, sin_t, d_in, d_q, d_k, d_c = _retention_tables(s)
    h2 = x.reshape(t, d)
    for l in range(depth):
        w_pt = _relayout(jnp.swapaxes(w_in, 1, 2)[l])

        proj = _proj(h2, g_mix_norm[l].reshape(1, d), w_pt,_pick(t, (512, 256)), 3328)
        proj3 = proj.reshape(b, s, D_IN_P)

        o_a = _attn(proj3, g_kv[l].reshape(1, D_LATENT), w_uv[l].astype(bf16), n_sel)
        o_b = _ret(proj3, cos_t, sin_t, d_in, d_q, d_k, d_c, g_ret[l].reshape(1, H_R * DV_R))

        mixed = _mix(o_a.reshape(t, D_BRANCH), o_b.reshape(t, D_BRANCH), w_branch[l].astype(bf16),
                     proj, _pick(t, (512, 256)))

        w_r = jnp.concatenate([w_router_group[l], w_router_expert[l],
                               jnp.zeros((d, LANES - N_GROUPS - N_EXPERTS), f32)], axis=1)
        b_r = jnp.concatenate([b_router_group[l], b_router_expert[l],
                               jnp.zeros((LANES - N_GROUPS - N_EXPERTS,), f32)]).reshape(1, LANES)
        wr_hi = w_r.astype(bf16)
        wr_lo = (w_r - wr_hi.astype(f32)).astype(bf16)
        h2, xn, eid_t, gate, cnt = _outproj(mixed, h2, w_out[l].astype(bf16), g_ffn_norm[l].reshape(1, d),
                                            jnp.concatenate([wr_hi, wr_lo], axis=1), b_r, _pick(t, (512, 256)))

        row_pack, block_expert, next_expert, n_used = _route(
            eid_t[:2].reshape(-1), cnt[:ROUTE_CHAINS, N_GROUPS:N_GROUPS + N_EXPERTS].reshape(-1), t)
        yt = _experts(block_expert, next_expert, row_pack, n_used, xn,
                      w_expert_gate[l], w_expert_up[l], w_expert_down[l])
        assert depth == 1
        h2 = _combine(h2, yt, gate, g_final.reshape(1, d), _pick(t, (512, 256)))
    return h2.reshape(b, s, d)
```

```python
import functools

import jax
import jax.numpy as jnp
import numpy as np
from jax import lax
from jax.experimental import pallas as pl
from jax.experimental.pallas import tpu as pltpu

EPS = 1e-6
CHUNK = 64
H_A = 8
D_LATENT = 128
DH_A = 128
H_IDX = 8
D_IDX = 64
TOPK_MAX = 256
H_R = 8
DK_R = 128
DV_R = 128
ROPE_BASE = 10000.0
D_BRANCH = 1024
N_BRANCH = 2
N_GROUPS = 4
EXP_PER_GROUP = 8
N_EXPERTS = N_GROUPS * EXP_PER_GROUP
D_EXPERT = 1024

LANES = 128
KEY_TILE = 256
Q_TILE = 512
RET_CHUNK = 512
ROW_BLOCK = 256
ROUTE_PARTS = 4
ROUTE_CHAINS = 2 * ROUTE_PARTS
VMEM_LIMIT = 56 * 1024 * 1024

C_QLAT = 0
C_GBR = 1024
C_QR = 5120
C_KR = 6144
C_VR = 7168
C_GR = 8192
C_QIDX = 9216
C_CKV = 9728
C_KW = 9856
D_IN_P = 9984

INT_MIN = np.int32(-2 ** 31)
NEG_BIG = -1e30

bf16 = jnp.bfloat16
f32 = jnp.float32


def _cparams(sem):
    return pltpu.CompilerParams(dimension_semantics=sem, vmem_limit_bytes=VMEM_LIMIT)


RELAYOUT_ROWS = 128
RELAYOUT_GROUP = 6


def _relayout_table(d_model):
    sp = np.cumsum([0, H_A * D_LATENT, D_LATENT, H_IDX * D_IDX, D_IDX, H_IDX,
                    H_R * DK_R, H_R * DK_R, H_R * DV_R, H_R * DV_R, N_BRANCH * d_model])
    order = [(sp[0], C_GBR - C_QLAT), (sp[9], C_QR - C_GBR), (sp[5], C_KR - C_QR), (sp[6], C_VR - C_KR),
             (sp[7], C_GR - C_VR), (sp[8], C_QIDX - C_GR), (sp[2], C_CKV - C_QIDX), (sp[1], C_KW - C_CKV),
             (sp[3], D_IN_P - C_KW)]
    rows = [int(off) + k * RELAYOUT_ROWS for off, height in order for k in range(height // RELAYOUT_ROWS)]
    assert len(rows) == D_IN_P // RELAYOUT_ROWS and all(r % 8 == 0 for r in rows)
    return np.asarray(rows, np.int32)


def _relayout_kernel(row_ref, wt_hbm, o_ref, buf, sem):
    j = pl.program_id(0)
    slot = j % 2

    def fetch(step, sl, k):
        src = wt_hbm.at[pl.ds(pl.multiple_of(row_ref[step * RELAYOUT_GROUP + k], 8), RELAYOUT_ROWS)]
        return pltpu.make_async_copy(src, buf.at[sl, pl.ds(k * RELAYOUT_ROWS, RELAYOUT_ROWS)], sem.at[sl])

    @pl.when(j == 0)
    def _():
        for k in range(RELAYOUT_GROUP):
            fetch(0, 0, k).start()

    @pl.when(j + 1 < pl.num_programs(0))
    def _():
        for k in range(RELAYOUT_GROUP):
            fetch(j + 1, 1 - slot, k).start()

    for k in range(RELAYOUT_GROUP):
        fetch(j, slot, k).wait()
    o_ref[...] = buf[slot].astype(o_ref.dtype)


def _relayout(wt):
    d_in, d_model = wt.shape
    rows = _relayout_table(d_model)
    step_rows = RELAYOUT_GROUP * RELAYOUT_ROWS
    assert int(rows.max()) + RELAYOUT_ROWS <= d_in and D_IN_P % step_rows == 0
    grid_spec = pltpu.PrefetchScalarGridSpec(
        num_scalar_prefetch=1,
        grid=(D_IN_P // step_rows,),
        in_specs=[pl.BlockSpec(memory_space=pl.ANY)],
        out_specs=pl.BlockSpec((step_rows, d_model), lambda j, row: (j, 0)),
        scratch_shapes=[pltpu.VMEM((2, step_rows, d_model), f32), pltpu.SemaphoreType.DMA((2,))],
    )
    return pl.pallas_call(
        _relayout_kernel,
        grid_spec=grid_spec,
        out_shape=jax.ShapeDtypeStruct((D_IN_P, d_model), bf16),
        compiler_params=_cparams(("arbitrary",)),
        name="relayout",
    )(jnp.asarray(rows), wt)


def _proj_kernel(x_ref, g_ref, w_ref, o_ref, xn_ref):
    @pl.when(pl.program_id(1) == 0)
    def _():
        x = x_ref[...]
        ms = jnp.mean(x * x, axis=-1, keepdims=True)
        xn_ref[...] = (x * lax.rsqrt(ms + EPS) * g_ref[...]).astype(bf16)

    o_ref[...] = lax.dot_general(xn_ref[...], w_ref[...], (((1,), (1,)), ((), ())),
                                 preferred_element_type=f32).astype(o_ref.dtype)


def _proj(x2, g, w_pt, tm, tn):
    t, d = x2.shape
    n = w_pt.shape[0]
    return pl.pallas_call(
        _proj_kernel,
        grid=(t // tm, n // tn),
        in_specs=[
            pl.BlockSpec((tm, d), lambda i, j: (i, 0)),
            pl.BlockSpec((1, d), lambda i, j: (0, 0)),
            pl.BlockSpec((tn, d), lambda i, j: (j, 0)),
        ],
        out_specs=pl.BlockSpec((tm, tn), lambda i, j: (i, j)),
        out_shape=jax.ShapeDtypeStruct((t, n), bf16),
        scratch_shapes=[pltpu.VMEM((tm, d), bf16)],
        compiler_params=_cparams(("parallel", "arbitrary")),
        name="proj",
    )(x2, g, w_pt)


def _float_key(s):
    bits = pltpu.bitcast(s, jnp.int32)
    key = bits ^ ((bits >> 31) & jnp.int32(0x7FFFFFFF))
    return jnp.where(s == 0.0, jnp.int32(0), key)


def _attn_kernel(qlat_ref, qidx_ref, kwq_ref, ckv_ref, kwk_ref, gkv_ref, wuv_ref, o_ref,
                 kv_s, kvT_s, kidx_s, key_s, bias_s, qT_s, qiT_s, acc_s, *, n_sel, n_kt):
    i = pl.program_id(1)
    idx_scale = (H_IDX ** -0.5) * (D_IDX ** -0.5)
    attn_scale = D_LATENT ** -0.5
    hq = H_A * Q_TILE

    @pl.when(i == 0)
    def _():
        g = gkv_ref[...]
        for t in range(n_kt):
            c = ckv_ref[t * KEY_TILE:(t + 1) * KEY_TILE, :].astype(f32)
            ms = jnp.mean(c * c, axis=-1, keepdims=True)
            kv = c * lax.rsqrt(ms + EPS) * g
            kv_s[t] = kv.astype(bf16)
            kvT_s[t] = kv.T.astype(bf16)
            kidx_s[t] = kwk_ref[t * KEY_TILE:(t + 1) * KEY_TILE, :D_IDX]

    nk = ((i + 1) * Q_TILE + KEY_TILE - 1) // KEY_TILE
    lane = lax.broadcasted_iota(jnp.int32, (1, Q_TILE), 1)
    sub = lax.broadcasted_iota(jnp.int32, (KEY_TILE, 1), 0)
    q_chunk = (i * Q_TILE + lane) // CHUNK

    wT = kwq_ref[...].astype(f32).T
    for h in range(H_A):
        qT_s[:, h * Q_TILE:(h + 1) * Q_TILE] = qlat_ref[:, h * D_LATENT:(h + 1) * D_LATENT].astype(f32).T.astype(bf16)
    for h in range(H_IDX):
        qiT_s[:, h * Q_TILE:(h + 1) * Q_TILE] = qidx_ref[:, h * D_IDX:(h + 1) * D_IDX].astype(f32).T.astype(bf16)

    n_pairs = (nk + 1) // 2

    def score_pair(p, carry):
        for t in (2 * p, 2 * p + 1):
            d_all = jnp.dot(kidx_s[t], qiT_s[...], preferred_element_type=f32)
            acc = jnp.zeros((KEY_TILE, Q_TILE), f32)
            for h in range(H_IDX):
                d = d_all[:, h * Q_TILE:(h + 1) * Q_TILE]
                acc = acc + wT[D_IDX + h:D_IDX + h + 1, :] * jnp.maximum(d, 0.0)
            score = acc * idx_scale
            k_chunk = (t * KEY_TILE + sub) // CHUNK
            key_s[t] = jnp.where(k_chunk <= q_chunk, _float_key(score), INT_MIN)
        return carry

    lax.fori_loop(0, n_pairs, score_pair, 0)


    def count(pred):
        def body(p, c):
            for t in (2 * p, 2 * p + 1):
                m = pred(key_s[t], t).astype(jnp.int32)
                c = c + jnp.sum(m.reshape(KEY_TILE // 8, 8, Q_TILE), axis=0)
            return c
        c8 = lax.fori_loop(0, n_pairs, body, jnp.zeros((8, Q_TILE), jnp.int32))
        return jnp.sum(c8, axis=0, keepdims=True)

    thr0 = jnp.where(count(lambda k, t: k >= 0) >= n_sel, jnp.int32(0), INT_MIN)
    thr0 = jnp.broadcast_to(thr0, (1, Q_TILE)).astype(jnp.int32)

    def bit_step(j, thr):
        cand = thr | (jnp.int32(1) << (jnp.int32(30) - j))
        return jnp.where(count(lambda k, t: k >= cand) >= n_sel, cand, thr)

    thr = lax.fori_loop(0, 31, bit_step, thr0)

    c_gt = count(lambda k, t: k > thr)
    c_ge = count(lambda k, t: k >= thr)
    need = n_sel - c_gt
    has_tie = jnp.max(jnp.where((c_ge > n_sel) & (thr > INT_MIN), 1, 0)) > 0

    def tie_limit():
        def step(j, m):
            cand = m | (jnp.int32(1) << (jnp.int32(14) - j))
            c = count(lambda k, t: (k == thr) & ((t * KEY_TILE + sub) < cand))
            return jnp.where(c < need, cand, m)
        return lax.fori_loop(0, 15, step, jnp.zeros((1, Q_TILE), jnp.int32))

    m_lim = lax.cond(has_tie, tie_limit, lambda: jnp.full((1, Q_TILE), 2 ** 30, jnp.int32))

    def bias_pair(p, carry):
        for t in (2 * p, 2 * p + 1):
            k = key_s[t]
            sel = (k > thr) | ((k == thr) & ((t * KEY_TILE + sub) <= m_lim))
            sel = sel & (k > INT_MIN)
            bias_s[t] = jnp.where(sel, 0.0, NEG_BIG).astype(f32)
        return carry

    lax.fori_loop(0, n_pairs, bias_pair, 0)

    acc_s[...] = jnp.zeros_like(acc_s)

    def att_pair(p, carry):
        m_run, l_run = carry
        kv2 = jnp.concatenate([kv_s[2 * p], kv_s[2 * p + 1]], axis=0)
        kvT2 = jnp.concatenate([kvT_s[2 * p], kvT_s[2 * p + 1]], axis=1)
        bias2 = jnp.concatenate([bias_s[2 * p], bias_s[2 * p + 1]], axis=0)
        logit = jnp.dot(kv2, qT_s[...], preferred_element_type=f32) * attn_scale
        logit = logit + jnp.concatenate([bias2] * H_A, axis=1)
        m_new = jnp.maximum(m_run, jnp.max(logit, axis=0, keepdims=True))
        alpha = jnp.exp(m_run - m_new)
        pr = jnp.exp(logit - m_new)
        l_new = alpha * l_run + jnp.sum(pr, axis=0, keepdims=True)
        acc_s[...] = alpha * acc_s[...] + jnp.dot(kvT2, pr.astype(bf16), preferred_element_type=f32)
        return m_new, l_new

    init = (jnp.full((1, hq), NEG_BIG, f32), jnp.zeros((1, hq), f32))
    _, l_fin = lax.fori_loop(0, n_pairs, att_pair, init)
    inv_l = 1.0 / l_fin
    for h in range(H_A):
        sl = slice(h * Q_TILE, (h + 1) * Q_TILE)
        o_lat = (acc_s[:, sl] * inv_l[:, sl]).T
        o_ref[:, h * DH_A:(h + 1) * DH_A] = jnp.dot(
            o_lat.astype(bf16), wuv_ref[h], preferred_element_type=f32).astype(o_ref.dtype)


def _attn(proj3, g_kv, w_uv_bf, n_sel):
    b, s, _ = proj3.shape
    n_kt = s // KEY_TILE
    kern = functools.partial(_attn_kernel, n_sel=n_sel, n_kt=n_kt)
    return pl.pallas_call(
        kern,
        grid=(b, s // Q_TILE),
        in_specs=[
            pl.BlockSpec((None, Q_TILE, H_A * D_LATENT), lambda bi, i: (bi, i, C_QLAT // 1024)),
            pl.BlockSpec((None, Q_TILE, H_IDX * D_IDX), lambda bi, i: (bi, i, C_QIDX // 512)),
            pl.BlockSpec((None, Q_TILE, LANES), lambda bi, i: (bi, i, C_KW // LANES)),
            pl.BlockSpec((None, s, LANES), lambda bi, i: (bi, 0, C_CKV // LANES)),
            pl.BlockSpec((None, s, LANES), lambda bi, i: (bi, 0, C_KW // LANES)),
            pl.BlockSpec((1, D_LATENT), lambda bi, i: (0, 0)),
            pl.BlockSpec((H_A, D_LATENT, DH_A), lambda bi, i: (0, 0, 0)),
        ],
        out_specs=pl.BlockSpec((None, Q_TILE, D_BRANCH), lambda bi, i: (bi, i, 0)),
        out_shape=jax.ShapeDtypeStruct((b, s, D_BRANCH), bf16),
        scratch_shapes=[
            pltpu.VMEM((n_kt, KEY_TILE, D_LATENT), bf16),
            pltpu.VMEM((n_kt, D_LATENT, KEY_TILE), bf16),
            pltpu.VMEM((n_kt, KEY_TILE, D_IDX), bf16),
            pltpu.VMEM((n_kt, KEY_TILE, Q_TILE), jnp.int32),
            pltpu.VMEM((n_kt, KEY_TILE, Q_TILE), f32),
            pltpu.VMEM((D_LATENT, H_A * Q_TILE), bf16),
            pltpu.VMEM((D_IDX, H_IDX * Q_TILE), bf16),
            pltpu.VMEM((D_LATENT, H_A * Q_TILE), f32),
        ],
        compiler_params=_cparams(("parallel", "arbitrary")),
        name="attn",
    )(proj3, proj3, proj3, proj3, proj3, g_kv, w_uv_bf)


def _ret_kernel(q_ref, k_ref, v_ref, gr_ref, cos_ref, sin_ref, din_ref, dq_ref, dk_ref, dc_ref,
                gret_ref, o_ref, state_s):
    @pl.when(pl.program_id(1) == 0)
    def _():
        state_s[...] = jnp.zeros_like(state_s)

    cos = cos_ref[...]
    sin = sin_ref[...]

    def rot(x):
        return x * cos + pltpu.roll(x, DK_R // 2, axis=1) * sin

    for h in range(H_R):
        sl = slice(h * DK_R, (h + 1) * DK_R)
        q = rot(q_ref[:, sl].astype(f32)).astype(bf16)
        kf = rot(k_ref[:, sl].astype(f32)) * (DK_R ** -0.5)
        k = kf.astype(bf16)
        v = v_ref[:, sl]
        inner = lax.dot_general(q, k, (((1,), (1,)), ((), ())), preferred_element_type=f32) * din_ref[h]
        o = jnp.dot(inner.astype(bf16), v, preferred_element_type=f32)
        st = state_s[h]
        o = o + jnp.dot(q, st.astype(bf16), preferred_element_type=f32) * dq_ref[h]
        kd = (kf * dk_ref[h]).astype(bf16)
        state_s[h] = st * dc_ref[h] + jnp.dot(kd.T, v, preferred_element_type=f32)
        mu = jnp.mean(o, axis=-1, keepdims=True)
        var = jnp.mean(jnp.square(o - mu), axis=-1, keepdims=True)
        y = (o - mu) * lax.rsqrt(var + EPS) * gret_ref[:, sl]
        gate = gr_ref[:, sl].astype(f32)
        o_ref[:, sl] = (gate * jax.nn.sigmoid(gate) * y).astype(o_ref.dtype)


def _ret(proj3, cos_t, sin_t, d_in, d_q, d_k, d_c, g_ret):
    b, s, _ = proj3.shape
    c = RET_CHUNK
    w = H_R * DK_R

    def col(off):
        return pl.BlockSpec((None, c, w), lambda bi, ci: (bi, ci, off // w))

    return pl.pallas_call(
        _ret_kernel,
        grid=(b, s // c),
        in_specs=[
            col(C_QR), col(C_KR), col(C_VR), col(C_GR),
            pl.BlockSpec((c, DK_R), lambda bi, ci: (ci, 0)),
            pl.BlockSpec((c, DK_R), lambda bi, ci: (ci, 0)),
            pl.BlockSpec((H_R, c, c), lambda bi, ci: (0, 0, 0)),
            pl.BlockSpec((H_R, c, DK_R), lambda bi, ci: (0, 0, 0)),
            pl.BlockSpec((H_R, c, DK_R), lambda bi, ci: (0, 0, 0)),
            pl.BlockSpec((H_R, 1, DK_R), lambda bi, ci: (0, 0, 0)),
            pl.BlockSpec((1, w), lambda bi, ci: (0, 0)),
        ],
        out_specs=pl.BlockSpec((None, c, w), lambda bi, ci: (bi, ci, 0)),
        out_shape=jax.ShapeDtypeStruct((b, s, w), bf16),
        scratch_shapes=[pltpu.VMEM((H_R, DK_R, DV_R), f32)],
        compiler_params=_cparams(("parallel", "arbitrary")),
        name="ret",
    )(proj3, proj3, proj3, proj3, cos_t, sin_t, d_in, d_q, d_k, d_c, g_ret)


def _retention_tables(s):
    c = RET_CHUNK
    half = DK_R // 2
    freq = ROPE_BASE ** (-jnp.arange(half, dtype=f32) / half)
    ang = jnp.arange(s, dtype=f32)[:, None] * freq[None, :]
    cos = jnp.cos(ang)
    sin = jnp.sin(ang)
    cos_t = jnp.concatenate([cos, cos], axis=-1)
    sin_t = jnp.concatenate([-sin, sin], axis=-1)
    log_gamma = jnp.log1p(-jnp.exp2(-5.0 - jnp.arange(H_R, dtype=f32)))
    n = jnp.arange(c, dtype=f32)
    diff = n[:, None] - n[None, :]
    d_in = jnp.where(diff >= 0, jnp.exp(log_gamma[:, None, None] * jnp.maximum(diff, 0.0)), 0.0)
    d_q = jnp.broadcast_to(jnp.exp(log_gamma[:, None] * (n + 1.0))[:, :, None], (H_R, c, DK_R))
    d_k = jnp.broadcast_to(jnp.exp(log_gamma[:, None] * (c - 1.0 - n))[:, :, None], (H_R, c, DK_R))
    d_c = jnp.broadcast_to(jnp.exp(log_gamma * c)[:, None, None], (H_R, 1, DK_R))
    return cos_t, sin_t, d_in, d_q, d_k, d_c


MIX_CHUNK = 512


def _mix_kernel(oa_ref, ob_ref, wb_ref, ga0_ref, ga1_ref, gb0_ref, gb1_ref, o_ref):
    oa = oa_ref[...]
    ob = ob_ref[...]
    half = ga0_ref.shape[1]
    for c in range(0, o_ref.shape[1], MIX_CHUNK):
        ga_ref, gb_ref, off = (ga0_ref, gb0_ref, c) if c < half else (ga1_ref, gb1_ref, c - half)
        a = jnp.dot(oa, wb_ref[0, :, c:c + MIX_CHUNK], preferred_element_type=f32)
        b = jnp.dot(ob, wb_ref[1, :, c:c + MIX_CHUNK], preferred_element_type=f32)
        ga = jax.nn.sigmoid(ga_ref[:, off:off + MIX_CHUNK].astype(f32))
        gb = jax.nn.sigmoid(gb_ref[:, off:off + MIX_CHUNK].astype(f32))
        o_ref[:, c:c + MIX_CHUNK] = (ga * a + gb * b).astype(o_ref.dtype)


def _mix(o_a, o_b, w_branch_bf, proj, tm):
    t = o_a.shape[0]
    d = w_branch_bf.shape[2]
    half = d // 2
    assert C_GBR % half == 0 and half % MIX_CHUNK == 0

    def gate(k):
        return pl.BlockSpec((tm, half), lambda i: (i, C_GBR // half + k))

    return pl.pallas_call(
        _mix_kernel,
        grid=(t // tm,),
        in_specs=[
            pl.BlockSpec((tm, D_BRANCH), lambda i: (i, 0)),
            pl.BlockSpec((tm, D_BRANCH), lambda i: (i, 0)),
            pl.BlockSpec((N_BRANCH, D_BRANCH, d), lambda i: (0, 0, 0)),
            gate(0), gate(1), gate(2), gate(3),
        ],
        out_specs=pl.BlockSpec((tm, d), lambda i: (i, 0)),
        out_shape=jax.ShapeDtypeStruct((t, d), bf16),
        compiler_params=_cparams(("parallel",)),
        name="mix",
    )(o_a, o_b, w_branch_bf, proj, proj, proj, proj)


def _pack_rows(v):
    n = v.shape[1] // 2
    r = pltpu.bitcast(v.astype(bf16).astype(f32), jnp.uint32)
    w = (r[:, :n] >> 16) | (r[:, n:] & jnp.uint32(0xFFFF0000))
    return pltpu.einshape("r(ab)->rab", w, b=LANES)


def _unpack_rows(p):
    w = pltpu.einshape("rab->r(ab)", p)
    lo = pltpu.bitcast(w << 16, f32)
    hi = pltpu.bitcast(w & jnp.uint32(0xFFFF0000), f32)
    return lo, hi


def _split_bf16(a):
    hi = a.astype(bf16)
    lo = (a - hi.astype(f32)).astype(bf16)
    return hi, lo


def _outproj_kernel(mixed_ref, x_ref, wo_ref, g_ref, wr_ref, br_ref,
                    h_ref, xn_ref, eid_ref, gate_ref, cnt_ref):
    h = x_ref[...] + jnp.dot(mixed_ref[...], wo_ref[...], preferred_element_type=f32)
    h_ref[...] = h
    ms = jnp.mean(h * h, axis=-1, keepdims=True)
    xn = h * lax.rsqrt(ms + EPS) * g_ref[...]
    xn_ref[...] = _pack_rows(xn)

    x_hi, x_lo = _split_bf16(xn)
    hh_hl = jnp.dot(x_hi, wr_ref[...], preferred_element_type=f32)
    logit = (hh_hl[:, :LANES] + hh_hl[:, LANES:]
             + jnp.dot(x_lo, wr_ref[:, :LANES], preferred_element_type=f32)) + br_ref[...]

    lane = lax.broadcasted_iota(jnp.int32, logit.shape, 1)
    lanef = lane.astype(f32)
    neg = -jnp.inf

    def first_argmax(v, m):
        return jnp.min(jnp.where(v == m, lanef, float(LANES)), axis=-1, keepdims=True)

    lg = jnp.where(lane < N_GROUPS, logit, neg)
    mg = jnp.max(lg, axis=-1, keepdims=True)
    p_grp = 1.0 / jnp.sum(jnp.exp(lg - mg), axis=-1, keepdims=True)
    grp = first_argmax(lg, mg).astype(jnp.int32)

    e_lane = lane - N_GROUPS
    in_grp = (e_lane >= 0) & (e_lane < N_EXPERTS) & ((e_lane // EXP_PER_GROUP) == grp)
    le = jnp.where(in_grp, logit, neg)
    m1 = jnp.max(le, axis=-1, keepdims=True)
    i1 = first_argmax(le, m1)
    le2 = jnp.where(lanef == i1, neg, le)
    m2 = jnp.max(le2, axis=-1, keepdims=True)
    i2 = first_argmax(le2, m2)
    e2 = jnp.exp(m2 - m1)
    g1 = p_grp / (1.0 + e2)
    g2 = p_grp * e2 / (1.0 + e2)

    eid = jnp.where(lane == 0, i1, jnp.where(lane == 1, i2, float(N_GROUPS))) - float(N_GROUPS)
    eid_ref[...] = eid.astype(jnp.int32).T[:8, :]
    gate_ref[...] = jnp.where(lane == 0, g1, jnp.where(lane == 1, g2, 0.0))

    @pl.when(pl.program_id(0) == 0)
    def _():
        cnt_ref[...] = jnp.zeros_like(cnt_ref)

    part = pl.program_id(0) // (pl.num_programs(0) // ROUTE_PARTS)
    sub8 = lax.broadcasted_iota(jnp.int32, (8, LANES), 0)
    for s, idx in enumerate((i1, i2)):
        c = jnp.sum((lanef == idx).astype(jnp.int32), axis=0, keepdims=True)
        cnt_ref[...] += jnp.where(sub8 == s * ROUTE_PARTS + part, c, 0)


def _outproj(mixed, x2, w_out_bf, g_ffn, wr_hi_lo, b_r, tm):
    t, d = x2.shape
    row = lambda i: (i, 0)
    fixed = lambda i: (0, 0)
    return pl.pallas_call(
        _outproj_kernel,
        grid=(t // tm,),
        in_specs=[
            pl.BlockSpec((tm, d), row),
            pl.BlockSpec((tm, d), row),
            pl.BlockSpec((d, d), fixed),
            pl.BlockSpec((1, d), fixed),
            pl.BlockSpec((d, 2 * LANES), fixed),
            pl.BlockSpec((1, LANES), fixed),
        ],
        out_specs=[
            pl.BlockSpec((tm, d), row),
            pl.BlockSpec((tm, d // (2 * LANES), LANES), lambda i: (i, 0, 0)),
            pl.BlockSpec((8, tm), lambda i: (0, i)),
            pl.BlockSpec((tm, LANES), row),
            pl.BlockSpec((8, LANES), fixed),
        ],
        out_shape=[
            jax.ShapeDtypeStruct((t, d), f32),
            jax.ShapeDtypeStruct((t, d // (2 * LANES), LANES), jnp.uint32),
            jax.ShapeDtypeStruct((8, t), jnp.int32),
            jax.ShapeDtypeStruct((t, LANES), f32),
            jax.ShapeDtypeStruct((8, LANES), jnp.int32),
        ],
        compiler_params=_cparams(("arbitrary",)),
        name="outproj",
    )(mixed, x2, w_out_bf, g_ffn, wr_hi_lo, b_r)


ISSUE_UNROLL = 16
CAST_ROWS = 256
GATHER_BUFS = 4


def _experts_kernel(bexp_ref, nexp_ref, rpack_ref, nused_ref, xn_hbm, wg_hbm, wu_hbm, wd_hbm, yt_hbm,
                    xbuf, ybuf, wg_st, wu_st, wd_st, wg_bf, wu_bf, wd_bf, gsem, ssem, wsem, *, n_tok):
    j = pl.program_id(0)
    n_used = nused_ref[0]
    slot = j % 2
    tok_bits = (n_tok - 1).bit_length()

    def rows_of(blk, fn):
        base = blk * ROW_BLOCK

        def body(k, c):
            r0 = pl.multiple_of(k * ISSUE_UNROLL, ISSUE_UNROLL)
            for u in range(ISSUE_UNROLL):
                fn(r0 + u, rpack_ref[base + r0 + u], 1)
            return c

        lax.fori_loop(0, ROW_BLOCK // ISSUE_UNROLL, body, 0)

    def start_gathers(blk, sl):
        def one(r, packed, queue):
            tok = packed & ((1 << tok_bits) - 1)
            pltpu.make_async_copy(xn_hbm.at[tok], xbuf.at[sl, r], gsem.at[sl]).start(priority=queue)
        rows_of(blk, one)

    def start_scatters(blk, sl):
        def one(r, packed, queue):
            row = lax.shift_right_logical(packed, tok_bits)
            pltpu.make_async_copy(ybuf.at[sl, r], yt_hbm.at[row], ssem.at[sl]).start(priority=queue)
        rows_of(blk, one)

    def wait_gathers(sl):
        pltpu.make_async_copy(xn_hbm.at[pl.ds(0, ROW_BLOCK)], xbuf.at[sl], gsem.at[sl]).wait()

    def wait_scatters(sl):
        pltpu.make_async_copy(ybuf.at[sl], yt_hbm.at[pl.ds(0, ROW_BLOCK)], ssem.at[sl]).wait()

    staged = ((wg_hbm, wg_st, wg_bf), (wu_hbm, wu_st, wu_bf), (wd_hbm, wd_st, wd_bf))

    def start_weights(e):
        for q, (src, st, _) in enumerate(staged):
            pltpu.make_async_copy(src.at[e], st, wsem.at[q]).start()

    def wait_and_cast_weights():
        for q, (src, st, dst) in enumerate(staged):
            pltpu.make_async_copy(src.at[0], st, wsem.at[q]).wait()

            def cast(c, carry, st=st, dst=dst):
                r = pl.multiple_of(c * CAST_ROWS, CAST_ROWS)
                dst[pl.ds(r, CAST_ROWS), :] = st[pl.ds(r, CAST_ROWS), :].astype(bf16)
                return carry

            lax.fori_loop(0, st.shape[0] // CAST_ROWS, cast, 0)

    @pl.when(j == 0)
    def _():
        start_weights(bexp_ref[0])
        for b in range(GATHER_BUFS - 1):
            @pl.when(b < n_used)
            def _(b=b):
                start_gathers(b, b)
        ybuf[...] = jnp.zeros_like(ybuf)
        for sl in range(2):
            spare = yt_hbm.at[pl.ds(2 * n_tok + sl * ROW_BLOCK, ROW_BLOCK)]
            pltpu.make_async_copy(ybuf.at[sl], spare, ssem.at[sl]).start()
        for sl in range(2):
            wait_scatters(sl)

    @pl.when(j < n_used)
    def _():
        e = bexp_ref[j]

        @pl.when((j == 0) | (bexp_ref[jnp.maximum(j - 1, 0)] != e))
        def _():
            wait_and_cast_weights()

            @pl.when(nexp_ref[j] >= 0)
            def _():
                start_weights(nexp_ref[j])

        gslot = j % GATHER_BUFS
        wait_gathers(gslot)

        @pl.when(j + GATHER_BUFS - 1 < n_used)
        def _():
            start_gathers(j + GATHER_BUFS - 1, (j + GATHER_BUFS - 1) % GATHER_BUFS)

        @pl.when(j >= 2)
        def _():
            wait_scatters(slot)

        lo, hi = _unpack_rows(xbuf[gslot])
        xb = jnp.concatenate([lo, hi], axis=1).astype(bf16)
        g = jnp.dot(xb, wg_bf[...], preferred_element_type=f32)
        u = jnp.dot(xb, wu_bf[...], preferred_element_type=f32)
        hm = (g * jax.nn.sigmoid(g) * u).astype(bf16)
        ybuf[slot] = _pack_rows(jnp.dot(hm, wd_bf[...], preferred_element_type=f32))
        start_scatters(j, slot)

        @pl.when(j == n_used - 1)
        def _():
            wait_scatters(slot)

            @pl.when(j >= 1)
            def _():
                wait_scatters(1 - slot)


def _experts(block_expert, next_expert, row_pack, n_used, xn_packed, w_gate, w_up, w_down):
    n_rows = row_pack.shape[0]
    n_tok = xn_packed.shape[0]
    tile = xn_packed.shape[1:]
    _, d, f = w_gate.shape
    assert d % CAST_ROWS == 0 and f % CAST_ROWS == 0
    any_space = pl.BlockSpec(memory_space=pl.ANY)
    grid_spec = pltpu.PrefetchScalarGridSpec(
        num_scalar_prefetch=4,
        grid=(n_rows // ROW_BLOCK,),
        in_specs=[any_space, any_space, any_space, any_space],
        out_specs=any_space,
        scratch_shapes=[pltpu.VMEM((GATHER_BUFS, ROW_BLOCK) + tile, jnp.uint32),
                        pltpu.VMEM((2, ROW_BLOCK) + tile, jnp.uint32),
                        pltpu.VMEM((d, f), f32), pltpu.VMEM((d, f), f32), pltpu.VMEM((f, d), f32),
                        pltpu.VMEM((d, f), bf16), pltpu.VMEM((d, f), bf16), pltpu.VMEM((f, d), bf16),
                        pltpu.SemaphoreType.DMA((GATHER_BUFS,)), pltpu.SemaphoreType.DMA((2,)),
                        pltpu.SemaphoreType.DMA((3,))],
    )
    return pl.pallas_call(
        functools.partial(_experts_kernel, n_tok=n_tok),
        grid_spec=grid_spec,
        out_shape=jax.ShapeDtypeStruct((2 * n_tok + 2 * ROW_BLOCK,) + tile, jnp.uint32),
        compiler_params=_cparams(("arbitrary",)),
        name="experts",
    )(block_expert, next_expert, row_pack, n_used, xn_packed, w_gate, w_up, w_down)


def _combine_kernel(h_ref, y0_ref, y1_ref, gate_ref, g_ref, o_ref):
    gate = gate_ref[...]
    y0 = jnp.concatenate(_unpack_rows(y0_ref[...]), axis=1)
    y1 = jnp.concatenate(_unpack_rows(y1_ref[...]), axis=1)
    hh = h_ref[...] + gate[:, 0:1] * y0 + gate[:, 1:2] * y1
    ms = jnp.mean(hh * hh, axis=-1, keepdims=True)
    o_ref[...] = hh * lax.rsqrt(ms + EPS) * g_ref[...]


def _combine(h, yt, gate, g_final, tm):
    t, d = h.shape
    nt = t // tm
    return pl.pallas_call(
        _combine_kernel,
        grid=(nt,),
        in_specs=[
            pl.BlockSpec((tm, d), lambda i: (i, 0)),
            pl.BlockSpec((tm,) + yt.shape[1:], lambda i: (i, 0, 0)),
            pl.BlockSpec((tm,) + yt.shape[1:], lambda i: (nt + i, 0, 0)),
            pl.BlockSpec((tm, LANES), lambda i: (i, 0)),
            pl.BlockSpec((1, d), lambda i: (0, 0)),
        ],
        out_specs=pl.BlockSpec((tm, d), lambda i: (i, 0)),
        out_shape=jax.ShapeDtypeStruct((t, d), f32),
        compiler_params=_cparams(("parallel",)),
        name="combine",
    )(h, yt, yt, gate, g_final)


def _route_kernel(eid_ref, cnt_ref, rpack_ref, bexp_ref, nexp_ref, nused_ref, *cur_refs, n_tok, n_blocks):
    tok_bits = (n_tok - 1).bit_length()
    chunk = 2 * n_tok // ROUTE_CHAINS

    def no_next(k, carry):
        nexp_ref[k] = -1
        return carry

    lax.fori_loop(0, n_blocks, no_next, 0)

    def per_expert(e, carry):
        blk, prev_blk, prev_nb = carry
        start = blk * ROW_BLOCK
        run = start
        for c in range(ROUTE_CHAINS):
            cur_refs[c][e] = run
            run = run + cnt_ref[c * N_EXPERTS + e]
        nb = (run - start + ROW_BLOCK - 1) // ROW_BLOCK

        def set_block(k, c):
            bexp_ref[blk + k] = e
            return c

        lax.fori_loop(0, nb, set_block, 0)

        def set_next(k, c):
            nexp_ref[prev_blk + k] = e
            return c

        lax.fori_loop(0, jnp.where(nb > 0, prev_nb, 0), set_next, 0)

        def set_pad(r, c):
            rpack_ref[r] = (2 * n_tok + (r & (2 * ROW_BLOCK - 1))) << tok_bits
            return c

        lax.fori_loop(run, start + nb * ROW_BLOCK, set_pad, 0)
        return blk + nb, jnp.where(nb > 0, blk, prev_blk), jnp.where(nb > 0, nb, prev_nb)

    n_used, _, _ = lax.fori_loop(0, N_EXPERTS, per_expert, (0, 0, 0))
    nused_ref[0] = n_used

    def tail_block(k, carry):
        bexp_ref[k] = N_EXPERTS - 1
        return carry

    lax.fori_loop(n_used, n_blocks, tail_block, 0)

    def tail_row(r, carry):
        rpack_ref[r] = (2 * n_tok + (r & (2 * ROW_BLOCK - 1))) << tok_bits
        return carry

    lax.fori_loop(n_used * ROW_BLOCK, n_blocks * ROW_BLOCK, tail_row, 0)

    def place(i, carry):
        for c in range(ROUTE_CHAINS):
            a = c * chunk + i
            e = eid_ref[a]
            p = cur_refs[c][e]
            cur_refs[c][e] = p + 1
            rpack_ref[p] = (a << tok_bits) | (a - (c * chunk // n_tok) * n_tok)
        return carry

    lax.fori_loop(0, chunk, place, 0)


def _route(eid_flat, counts, n_tok):
    n_asg = eid_flat.shape[0]
    n_rows = -(-(n_asg + N_EXPERTS * (ROW_BLOCK - 1)) // ROW_BLOCK) * ROW_BLOCK
    n_blocks = n_rows // ROW_BLOCK
    smem = pl.BlockSpec(memory_space=pltpu.SMEM)
    return pl.pallas_call(
        functools.partial(_route_kernel, n_tok=n_tok, n_blocks=n_blocks),
        in_specs=[smem, smem],
        out_specs=[smem, smem, smem, smem],
        out_shape=[jax.ShapeDtypeStruct((n_rows,), jnp.int32),
                   jax.ShapeDtypeStruct((n_blocks,), jnp.int32),
                   jax.ShapeDtypeStruct((n_blocks,), jnp.int32),
                   jax.ShapeDtypeStruct((1,), jnp.int32)],
        scratch_shapes=[pltpu.SMEM((N_EXPERTS,), jnp.int32)] * ROUTE_CHAINS,
        name="route",
    )(eid_flat, counts)


def _pick(n, prefs):
    for p in prefs:
        if n % p == 0:
            return p
    return n


def kernel(x, g_mix_norm, w_in, g_kv, w_uv, g_ret, w_branch, w_out, g_ffn_norm, w_router_group,
           b_router_group, w_router_expert, b_router_expert, w_expert_gate, w_expert_up,
           w_expert_down, g_final):
    b, s, d = x.shape
    t = b * s
    depth = w_in.shape[0]
    n_sel = min(TOPK_MAX, s // 4)
    assert s % RET_CHUNK == 0 and s % Q_TILE == 0

    cos_t, sin_t, d_in, d_q, d_k, d_c = _retention_tables(s)
    h2 = x.reshape(t, d)
    for l in range(depth):
        w_pt = _relayout(jnp.swapaxes(w_in, 1, 2)[l])

        proj = _proj(h2, g_mix_norm[l].reshape(1, d), w_pt,_pick(t, (512, 256)), 3328)
        proj3 = proj.reshape(b, s, D_IN_P)

        o_a = _attn(proj3, g_kv[l].reshape(1, D_LATENT), w_uv[l].astype(bf16), n_sel)
        o_b = _ret(proj3, cos_t, sin_t, d_in, d_q, d_k, d_c, g_ret[l].reshape(1, H_R * DV_R))

        mixed = _mix(o_a.reshape(t, D_BRANCH), o_b.reshape(t, D_BRANCH), w_branch[l].astype(bf16),
                     proj, _pick(t, (512, 256)))

        w_r = jnp.concatenate([w_router_group[l], w_router_expert[l],
                               jnp.zeros((d, LANES - N_GROUPS - N_EXPERTS), f32)], axis=1)
        b_r = jnp.concatenate([b_router_group[l], b_router_expert[l],
                               jnp.zeros((LANES - N_GROUPS - N_EXPERTS,), f32)]).reshape(1, LANES)
        wr_hi = w_r.astype(bf16)
        wr_lo = (w_r - wr_hi.astype(f32)).astype(bf16)
        h2, xn, eid_t, gate, cnt = _outproj(mixed, h2, w_out[l].astype(bf16), g_ffn_norm[l].reshape(1, d),
                                            jnp.concatenate([wr_hi, wr_lo], axis=1), b_r, _pick(t, (512, 256)))

        row_pack, block_expert, next_expert, n_used = _route(
            eid_t[:2].reshape(-1), cnt[:ROUTE_CHAINS, N_GROUPS:N_GROUPS + N_EXPERTS].reshape(-1), t)
        yt = _experts(block_expert, next_expert, row_pack, n_used, xn,
                      w_expert_gate[l], w_expert_up[l], w_expert_down[l])
        assert depth == 1
        h2 = _combine(h2, yt, gate, g_final.reshape(1, d), _pick(t, (512, 256)))
    return h2.reshape(b, s, d)
```
